```python
import jax
import jax.numpy as jnp
from jax import lax
import numpy as np

D_MODEL = 1024
BATCH = 4
SEQ = 8192
DEPTH = 1

GLA_HEADS = 4
GLA_DK = 128
GLA_DV = 128
GLA_RANK = 16
GLA_TAU = 16.0
GLA_CHUNK = 16
DIL_PATTERNS = ((128, 1), (512, 4), (2048, 16))
DIL_GROUPS = 3
DIL_HEADS = 4
DIL_DH = 128
DIL_BLOCK = 128
N_GROUPS = 4
EXPERTS_PER_GROUP = 8
N_EXPERTS = N_GROUPS * EXPERTS_PER_GROUP
TOP_K = 2
D_EXPERT = 512
MOE_BLOCK = 128
EPS = 1e-6

GLA_QK_W = GLA_HEADS * GLA_DK
GLA_V_W = GLA_HEADS * GLA_DV
DIL_W = DIL_HEADS * DIL_DH
SPLIT_SIZES = (GLA_QK_W, GLA_QK_W, GLA_V_W, GLA_V_W, GLA_RANK) + (DIL_W,) * (3 * DIL_GROUPS)
IN_COLS = sum(SPLIT_SIZES)
SPLIT_POINTS = tuple(int(v) for v in np.cumsum(SPLIT_SIZES)[:-1])
ALIBI_SLOPES = tuple(2.0 ** (-8.0 * (i + 1) / (DIL_GROUPS * DIL_HEADS)) for i in range(DIL_GROUPS * DIL_HEADS))

kernel_name = 'hybrid_gla_dilated_attn_hier_moe'


def rms_norm(x, g):
    xf = x.astype(jnp.float32)
    y = xf * lax.rsqrt(jnp.mean(xf * xf, axis=-1, keepdims=True) + EPS)
    return (y * g.astype(jnp.float32)).astype(x.dtype)


def gla_chunked(q, k, v, log_a):
    B, S, H, DK = q.shape
    DV = v.shape[-1]
    C = GLA_CHUNK
    N = S // C
    def chunks(t):
        return t.astype(jnp.float32).reshape(B, N, C, H, t.shape[-1]).transpose(1, 0, 3, 2, 4)
    qc = chunks(q) * (DK ** -0.5)
    kc = chunks(k)
    vc = chunks(v)
    b = jnp.cumsum(chunks(log_a), axis=3)
    b_last = b[:, :, :, -1:, :]
    q_in = qc * jnp.exp(b)
    k_in = kc * jnp.exp(-b)
    k_end = kc * jnp.exp(b_last - b)
    causal = jnp.tril(jnp.ones((C, C), dtype=bool))
    att = jnp.where(causal, jnp.einsum('nbhik,nbhjk->nbhij', q_in, k_in), 0.0)
    o_intra = jnp.einsum('nbhij,nbhjv->nbhiv', att, vc)

    def step(state, xs):
        q_n, k_n, v_n, dec_n = xs
        o_n = jnp.einsum('bhik,bhkv->bhiv', q_n, state)
        state = dec_n[:, :, 0, :, None] * state + jnp.einsum('bhjk,bhjv->bhkv', k_n, v_n)
        return state, o_n

    state0 = jnp.zeros((B, H, DK, DV), jnp.float32)
    _, o_inter = lax.scan(step, state0, (q_in, k_end, vc, jnp.exp(b_last)))
    o = o_intra + o_inter
    return o.transpose(1, 0, 3, 2, 4).reshape(B, S, H, DV)


def dilated_group(q, k, v, slopes, window, dilation):
    B, S, H, E = q.shape
    L = S // dilation
    Q = DIL_BLOCK
    span = window // dilation
    nb = -(-L // Q)
    Lp = nb * Q

    def to_sub(t):
        t = jnp.swapaxes(t.reshape(B, L, dilation, H, E), 1, 2)
        return jnp.pad(t, ((0, 0), (0, 0), (0, Lp - L), (0, 0), (0, 0)))

    def kv_blocks(t):
        t = jnp.pad(to_sub(t), ((0, 0), (0, 0), (Q, 0), (0, 0), (0, 0))).reshape(B, dilation, nb + 1, Q, H, E)
        return jnp.concatenate([t[:, :, :-1], t[:, :, 1:]], axis=3)

    def from_sub(t):
        t = t.reshape((B, dilation, Lp) + t.shape[4:])[:, :, :L]
        t = jnp.swapaxes(t, 1, 2)
        return t.reshape((B, S) + t.shape[3:])

    qb = to_sub(q).reshape(B, dilation, nb, Q, H, E)
    kb = kv_blocks(k)
    vb = kv_blocks(v)
    s = jnp.einsum('bdnqhe,bdnkhe->bdnhqk', qb, kb, preferred_element_type=jnp.float32) * (E ** -0.5)
    dist = jnp.arange(Q)[:, None] + Q - jnp.arange(2 * Q)[None, :]
    key_pos = jnp.arange(nb)[:, None, None] * Q + jnp.arange(2 * Q)[None, None, :] - Q
    valid = (dist >= 0) & (dist <= span) & (key_pos >= 0)
    bias = -slopes[:, None, None] * (dilation * dist).astype(jnp.float32)
    s = jnp.where(valid[:, None], s + bias, -jnp.inf)
    m = jnp.max(s, axis=-1)
    p = jnp.exp(s - m[..., None])
    den = jnp.sum(p, axis=-1)
    den_t = jnp.swapaxes(den, -1, -2)
    o = jnp.einsum('bdnhqk,bdnkhe->bdnqhe', p, vb.astype(jnp.float32)) / den_t[..., None]
    return from_sub(o), from_sub(jnp.swapaxes(m, -1, -2)), from_sub(den_t)


def hier_moe(h, w_rg, b_rg, w_re, b_re, w_gate, w_up, w_down):
    B, S, D = h.shape
    T = B * S
    hf = h.reshape(T, D)
    g_logit = (hf @ w_rg).astype(jnp.float32) + b_rg
    g_prob = jax.nn.softmax(g_logit, axis=-1)
    g_sel = jnp.argmax(g_logit, axis=-1)
    g_p = jnp.take_along_axis(g_prob, g_sel[:, None], axis=-1)[:, 0]
    e_logit = ((hf @ w_re).astype(jnp.float32) + b_re).reshape(T, N_GROUPS, EXPERTS_PER_GROUP)
    e_logit = jnp.take_along_axis(e_logit, g_sel[:, None, None], axis=1)[:, 0]
    top_v, top_i = lax.top_k(e_logit, TOP_K)
    gate = jax.nn.softmax(top_v, axis=-1) * g_p[:, None]
    eid = g_sel[:, None] * EXPERTS_PER_GROUP + top_i

    A = T * TOP_K
    flat_e = eid.reshape(A).astype(jnp.int32)
    flat_w = gate.reshape(A)
    flat_t = jnp.repeat(jnp.arange(T, dtype=jnp.int32), TOP_K)
    order = jnp.argsort(flat_e)
    se, st, sw = flat_e[order], flat_t[order], flat_w[order]
    counts = jnp.bincount(flat_e, length=N_EXPERTS)
    starts = jnp.cumsum(counts) - counts
    pcounts = (counts + MOE_BLOCK - 1) // MOE_BLOCK * MOE_BLOCK
    pends = jnp.cumsum(pcounts)
    pstarts = pends - pcounts
    dest = pstarts[se] + jnp.arange(A, dtype=jnp.int32) - starts[se]
    P = A + N_EXPERTS * MOE_BLOCK
    NB = P // MOE_BLOCK
    row_t = jnp.full((P,), T, jnp.int32).at[dest].set(st)
    row_w = jnp.zeros((P,), jnp.float32).at[dest].set(sw)
    blk_e = jnp.minimum(jnp.searchsorted(pends, jnp.arange(NB, dtype=jnp.int32) * MOE_BLOCK, side='right'), N_EXPERTS - 1)
    h_pad = jnp.concatenate([hf, jnp.zeros((1, D), hf.dtype)], axis=0)
    xin = h_pad[row_t].reshape(NB, MOE_BLOCK, D)

    def expert_block(args):
        xb, e = args
        return (jax.nn.silu(xb @ w_gate[e]) * (xb @ w_up[e])) @ w_down[e]

    yb = lax.map(expert_block, (xin, blk_e)).reshape(P, D)
    out = jnp.zeros((T + 1, D), jnp.float32).at[row_t].add(yb.astype(jnp.float32) * row_w[:, None])[:T]
    return out.reshape(B, S, D).astype(h.dtype)


def setup_inputs(seed: int = 0) -> dict:
    key = jax.random.key(seed)
    ks = jax.random.split(key, 22)
    def nrm(k, shape, scale):
        return scale * jax.random.normal(k, shape, jnp.float32)
    L = DEPTH
    return {
        'x': nrm(ks[0], (BATCH, SEQ, D_MODEL), 1.0),
        'norm1_g': 1.0 + nrm(ks[1], (L, D_MODEL), 0.02),
        'w_in': nrm(ks[2], (L, D_MODEL, IN_COLS), D_MODEL ** -0.5),
        'w_gla_a2': nrm(ks[3], (L, GLA_RANK, GLA_QK_W), GLA_RANK ** -0.5),
        'b_gla_a': nrm(ks[4], (L, GLA_QK_W), 0.1),
        'gla_out_norm_g': 1.0 + nrm(ks[5], (L, GLA_DV), 0.02),
        'dil_q_norm_g': 1.0 + nrm(ks[6], (L, DIL_GROUPS, DIL_DH), 0.02),
        'dil_k_norm_g': 1.0 + nrm(ks[7], (L, DIL_GROUPS, DIL_DH), 0.02),
        'w_proj_gla': nrm(ks[8], (L, GLA_V_W, D_MODEL), GLA_V_W ** -0.5),
        'w_proj_attn': nrm(ks[9], (L, DIL_W, D_MODEL), DIL_W ** -0.5),
        'w_branch_gate': nrm(ks[10], (L, D_MODEL, 2 * D_MODEL), D_MODEL ** -0.5),
        'b_branch_gate': nrm(ks[11], (L, 2 * D_MODEL), 0.01),
        'w_out': nrm(ks[12], (L, D_MODEL, D_MODEL), D_MODEL ** -0.5),
        'norm2_g': 1.0 + nrm(ks[13], (L, D_MODEL), 0.02),
        'w_router_group': nrm(ks[14], (L, D_MODEL, N_GROUPS), D_MODEL ** -0.5),
        'b_router_group': nrm(ks[15], (L, N_GROUPS), 0.01),
        'w_router_expert': nrm(ks[16], (L, D_MODEL, N_EXPERTS), D_MODEL ** -0.5),
        'b_router_expert': nrm(ks[17], (L, N_EXPERTS), 0.01),
        'w_gate': nrm(ks[18], (L, N_EXPERTS, D_MODEL, D_EXPERT), D_MODEL ** -0.5),
        'w_up': nrm(ks[19], (L, N_EXPERTS, D_MODEL, D_EXPERT), D_MODEL ** -0.5),
        'w_down': nrm(ks[20], (L, N_EXPERTS, D_EXPERT, D_MODEL), D_EXPERT ** -0.5),
    }


def reference(x, norm1_g, w_in, w_gla_a2, b_gla_a, gla_out_norm_g, dil_q_norm_g, dil_k_norm_g,
              w_proj_gla, w_proj_attn, w_branch_gate, b_branch_gate, w_out, norm2_g,
              w_router_group, b_router_group, w_router_expert, b_router_expert, w_gate, w_up, w_down):
    B, S, D = x.shape
    slopes = jnp.asarray(ALIBI_SLOPES, jnp.float32).reshape(DIL_GROUPS, DIL_HEADS)
    def heads(t, e):
        return t.reshape(B, S, -1, e)
    for l in range(DEPTH):
        h = rms_norm(x, norm1_g[l])
        parts = jnp.split(h @ w_in[l], list(SPLIT_POINTS), axis=-1)
        gq, gk, gv, gr, ga_lr = parts[:5]
        log_a = jax.nn.log_sigmoid((ga_lr @ w_gla_a2[l] + b_gla_a[l]).astype(jnp.float32)) / GLA_TAU
        o_gla = gla_chunked(heads(gq, GLA_DK), heads(gk, GLA_DK), heads(gv, GLA_DV), heads(log_a, GLA_DK))
        o_gla = rms_norm(o_gla, gla_out_norm_g[l]).reshape(B, S, GLA_V_W) * jax.nn.silu(gr.astype(jnp.float32))
        y_gla = o_gla.astype(x.dtype) @ w_proj_gla[l]
        outs, maxes, dens = [], [], []
        for gi, (window, dilation) in enumerate(DIL_PATTERNS):
            q, k, v = parts[5 + 3 * gi: 8 + 3 * gi]
            q = rms_norm(heads(q, DIL_DH), dil_q_norm_g[l, gi])
            k = rms_norm(heads(k, DIL_DH), dil_k_norm_g[l, gi])
            o_g, m_g, d_g = dilated_group(q, k, heads(v, DIL_DH), slopes[gi], window, dilation)
            outs.append(o_g)
            maxes.append(m_g)
            dens.append(d_g)
        m_all = jnp.stack(maxes)
        wts = jnp.stack(dens) * jnp.exp(m_all - jnp.max(m_all, axis=0))
        o_att = jnp.sum(wts[..., None] * jnp.stack(outs), axis=0) / jnp.sum(wts, axis=0)[..., None]
        y_att = o_att.reshape(B, S, DIL_W).astype(x.dtype) @ w_proj_attn[l]
        g_gla, g_att = jnp.split(jax.nn.sigmoid(h @ w_branch_gate[l] + b_branch_gate[l]), 2, axis=-1)
        x = x + ((g_gla * y_gla + g_att * y_att) @ w_out[l]).astype(x.dtype)
        x = x + hier_moe(rms_norm(x, norm2_g[l]), w_router_group[l], b_router_group[l], w_router_expert[l],
                         b_router_expert[l], w_gate[l], w_up[l], w_down[l])
    return x
```

```python
import functools

import jax
import jax.numpy as jnp
import numpy as np
from jax import lax
from jax.experimental import pallas as pl
from jax.experimental.pallas import tpu as pltpu

F32 = jnp.float32
BF16 = jnp.bfloat16

D_MODEL = 1024
EPS = 1e-6
GLA_HEADS = 4
GLA_DK = 128
GLA_RANK = 16
GLA_TAU = 16.0
GLA_SUBCHUNK = 16
GLA_BLOCK = 128
DIL_PATTERNS = ((128, 1), (512, 4), (2048, 16))
DIL_GROUPS = 3
DIL_HEADS = 4
DIL_DH = 128
DIL_BLOCK = 128
ALIBI_SLOPES = tuple(2.0 ** (-8.0 * (i + 1) / (DIL_GROUPS * DIL_HEADS)) for i in range(DIL_GROUPS * DIL_HEADS))
N_GROUPS = 4
EXPERTS_PER_GROUP = 8
N_EXPERTS = N_GROUPS * EXPERTS_PER_GROUP
TOP_K = 2
D_EXPERT = 512

HEAD_W = 512
N_COL_TILES = 13
PROJ_W = N_COL_TILES * HEAD_W
LANES = 128
NEG = -1e30

INPROJ_TM = 1024
MERGE_TM = 512
MOE_TM = 512
ROW_TM = 512
DIL_QBLOCKS = 4
VMEM_LIMIT = 56 * 1024 * 1024


def _dot(a, b):
    return jnp.dot(a, b, preferred_element_type=F32)


def _dot_nt(a, b):
    return lax.dot_general(a, b, (((1,), (1,)), ((), ())), preferred_element_type=F32)


def _dot_tn(a, b):
    return lax.dot_general(a, b, (((0,), (0,)), ((), ())), preferred_element_type=F32)


def _rms(x, g):
    return x * lax.rsqrt(jnp.mean(x * x, axis=-1, keepdims=True) + EPS) * g


def _params(sem):
    return pltpu.CompilerParams(dimension_semantics=sem, vmem_limit_bytes=VMEM_LIMIT)


def _inproj_kernel(x_ref, g1_ref, w_ref, wga_ref, wa2_ref, ba_ref, qkg_ref, out_ref, loga_ref, h_ref):
    j = pl.program_id(1)

    @pl.when(j == 0)
    def _():
        hb = _rms(x_ref[...], g1_ref[...]).astype(BF16)
        h_ref[...] = hb
        ga = _dot(hb, wga_ref[...])
        z = _dot(ga.astype(BF16), wa2_ref[...]) + ba_ref[...]
        log_sig = jnp.minimum(z, 0.0) - jnp.log(1.0 + jnp.exp(-jnp.abs(z)))
        loga_ref[...] = log_sig * (1.0 / GLA_TAU)

    y = _dot(h_ref[...], w_ref[...])
    is_qk = jnp.logical_and(j >= 4, (j - 4) % 3 != 2)

    @pl.when(is_qk)
    def _():
        g = qkg_ref[0]
        for c in range(DIL_HEADS):
            sl = slice(c * DIL_DH, (c + 1) * DIL_DH)
            out_ref[:, sl] = _rms(y[:, sl], g[:, sl]).astype(BF16)

    @pl.when(jnp.logical_not(is_qk))
    def _():
        out_ref[...] = y.astype(BF16)


def _inproj(x2d, g1, w_main, w_ga, w_a2, b_a, qk_gain):
    T = x2d.shape[0]
    tm = min(INPROJ_TM, T)
    return pl.pallas_call(
        _inproj_kernel,
        grid=(T // tm, N_COL_TILES),
        in_specs=[
            pl.BlockSpec((tm, D_MODEL), lambda i, j: (i, 0)),
            pl.BlockSpec((1, D_MODEL), lambda i, j: (0, 0)),
            pl.BlockSpec((D_MODEL, HEAD_W), lambda i, j: (0, j)),
            pl.BlockSpec((D_MODEL, LANES), lambda i, j: (0, 0)),
            pl.BlockSpec((LANES, HEAD_W), lambda i, j: (0, 0)),
            pl.BlockSpec((1, HEAD_W), lambda i, j: (0, 0)),
            pl.BlockSpec((1, 1, HEAD_W), lambda i, j: (j, 0, 0)),
        ],
        out_specs=[
            pl.BlockSpec((tm, HEAD_W), lambda i, j: (i, j)),
            pl.BlockSpec((tm, HEAD_W), lambda i, j: (i, 0)),
        ],
        out_shape=[
            jax.ShapeDtypeStruct((T, PROJ_W), BF16),
            jax.ShapeDtypeStruct((T, HEAD_W), F32),
        ],
        scratch_shapes=[pltpu.VMEM((tm, D_MODEL), BF16)],
        compiler_params=_params(("parallel", "arbitrary")),
        name="inproj",
    )(x2d, g1, w_main, w_ga, w_a2, b_a, qk_gain)


def _gla_kernel(q_ref, k_ref, v_ref, r_ref, la_ref, tri_ref, gn_ref, o_ref, st_ref):
    n = pl.program_id(1)
    L = GLA_BLOCK

    @pl.when(n == 0)
    def _():
        st_ref[...] = jnp.zeros_like(st_ref)

    la = la_ref[...]
    hi = la.astype(BF16)
    r1 = la - hi.astype(F32)
    mid = r1.astype(BF16)
    lo = (r1 - mid.astype(F32)).astype(BF16)
    tri = tri_ref[...]
    b_all = _dot(tri, hi) + _dot(tri, mid) + _dot(tri, lo)

    row = lax.broadcasted_iota(jnp.int32, (L, L), 0)
    col = lax.broadcasted_iota(jnp.int32, (L, L), 1)

    for h in range(GLA_HEADS):
        sl = slice(h * GLA_DK, (h + 1) * GLA_DK)
        b = b_all[:, sl]
        q = q_ref[:, sl].astype(F32) * (GLA_DK ** -0.5)
        k = k_ref[:, sl].astype(F32)
        v = v_ref[:, sl]

        att = jnp.zeros((L, L), F32)
        seg = L
        while seg > GLA_SUBCHUNK:
            half = seg // 2
            beta = jnp.concatenate(
                [jnp.broadcast_to(b[s + half - 1:s + half, :], (seg, GLA_DK)) for s in range(0, L, seg)], axis=0)
            qs = (q * jnp.exp(jnp.minimum(b - beta, 0.0))).astype(BF16)
            ks = (k * jnp.exp(jnp.minimum(beta - b, 0.0))).astype(BF16)
            same = (row // seg) == (col // seg)
            mask = same & ((row % seg) >= half) & ((col % seg) < half)
            att = jnp.where(mask, _dot_nt(qs, ks), att)
            seg = half
        beta = jnp.concatenate(
            [jnp.zeros((GLA_SUBCHUNK, GLA_DK), F32)]
            + [jnp.broadcast_to(b[s - 1:s, :], (GLA_SUBCHUNK, GLA_DK)) for s in range(GLA_SUBCHUNK, L, GLA_SUBCHUNK)],
            axis=0)
        qs = (q * jnp.exp(b - beta)).astype(BF16)
        ks = (k * jnp.exp(beta - b)).astype(BF16)
        mask = ((row // GLA_SUBCHUNK) == (col // GLA_SUBCHUNK)) & (col <= row)
        att = jnp.where(mask, _dot_nt(qs, ks), att)

        state_t = st_ref[h]
        q0 = (q * jnp.exp(b)).astype(BF16)
        o = _dot(att.astype(BF16), v) + _dot_nt(q0, state_t.astype(BF16))

        b_last = b[L - 1:L, :]
        k_end = (k * jnp.exp(b_last - b)).astype(BF16)
        st_ref[h] = state_t * jnp.exp(b_last) + _dot_tn(v, k_end)

        rr = r_ref[:, sl].astype(F32)
        o_ref[:, sl] = (_rms(o, gn_ref[...]) * (rr * jax.nn.sigmoid(rr))).astype(BF16)


def _gla(proj, log_a, tri, gn, B, S):
    L = GLA_BLOCK
    nb = S // L
    return pl.pallas_call(
        _gla_kernel,
        grid=(B, nb),
        in_specs=[
            pl.BlockSpec((L, HEAD_W), lambda b, n: (b * nb + n, 0)),
            pl.BlockSpec((L, HEAD_W), lambda b, n: (b * nb + n, 1)),
            pl.BlockSpec((L, HEAD_W), lambda b, n: (b * nb + n, 2)),
            pl.BlockSpec((L, HEAD_W), lambda b, n: (b * nb + n, 3)),
            pl.BlockSpec((L, HEAD_W), lambda b, n: (b * nb + n, 0)),
            pl.BlockSpec((L, L), lambda b, n: (0, 0)),
            pl.BlockSpec((1, GLA_DK), lambda b, n: (0, 0)),
        ],
        out_specs=pl.BlockSpec((L, HEAD_W), lambda b, n: (b * nb + n, 0)),
        out_shape=jax.ShapeDtypeStruct((B * S, HEAD_W), BF16),
        scratch_shapes=[pltpu.VMEM((GLA_HEADS, GLA_DK, GLA_DK), F32)],
        compiler_params=_params(("parallel", "arbitrary")),
        name="gla",
    )(proj, proj, proj, proj, log_a, tri, gn)


def _dil_kernel(q_ref, kp_ref, kc_ref, vp_ref, vc_ref, bias_ref, o_ref, st_ref, *, nq):
    first = pl.program_id(2) == 0
    Q = DIL_BLOCK
    col = lax.broadcasted_iota(jnp.int32, (Q, 2 * Q), 1)
    lane = lax.broadcasted_iota(jnp.int32, (Q, LANES), 1)
    for blk in range(nq):
        rows = slice(blk * Q, (blk + 1) * Q)
        prev = slice((blk - 1) * Q, blk * Q)
        stats = jnp.zeros((Q, LANES), F32)
        for h in range(DIL_HEADS):
            sl = slice(h * DIL_DH, (h + 1) * DIL_DH)
            if blk == 0:
                kprev, vprev = kp_ref[0, :, sl], vp_ref[0, :, sl]
            else:
                kprev, vprev = kc_ref[0, prev, sl], vc_ref[0, prev, sl]
            kk = jnp.concatenate([kprev, kc_ref[0, rows, sl]], axis=0)
            vv = jnp.concatenate([vprev, vc_ref[0, rows, sl]], axis=0)
            s = _dot_nt(q_ref[0, rows, sl], kk) + bias_ref[h]
            if blk == 0:
                s = jnp.where(jnp.logical_and(first, col < Q), NEG, s)
            m = jnp.max(s, axis=-1, keepdims=True)
            p = jnp.exp(s - m)
            den = jnp.sum(p, axis=-1, keepdims=True)
            o_ref[0, rows, sl] = (_dot(p.astype(BF16), vv) / den).astype(BF16)
            stats = jnp.where((lane // 16) == h, m, stats)
            stats = jnp.where((lane // 16) == DIL_HEADS + h, den, stats)
        st_ref[0, rows, :] = stats


def _dilated(proj, bias, gi, B, S):
    _, d = DIL_PATTERNS[gi]
    Lsub = S // d
    nq = min(DIL_QBLOCKS, Lsub // DIL_BLOCK)
    rows = nq * DIL_BLOCK
    nsteps = Lsub // rows
    view = proj.reshape(B, Lsub, d * PROJ_W)
    cq, ck, cv = 4 + 3 * gi, 5 + 3 * gi, 6 + 3 * gi

    def cur(c):
        return pl.BlockSpec((1, rows, HEAD_W), lambda b, r, i: (b, i, r * N_COL_TILES + c))

    def prev(c):
        return pl.BlockSpec((1, DIL_BLOCK, HEAD_W),
                            lambda b, r, i: (b, jnp.maximum(i * nq - 1, 0), r * N_COL_TILES + c))

    o, st = pl.pallas_call(
        functools.partial(_dil_kernel, nq=nq),
        grid=(B, d, nsteps),
        in_specs=[cur(cq), prev(ck), cur(ck), prev(cv), cur(cv),
                  pl.BlockSpec((DIL_HEADS, DIL_BLOCK, 2 * DIL_BLOCK), lambda b, r, i: (0, 0, 0))],
        out_specs=[
            pl.BlockSpec((1, rows, HEAD_W), lambda b, r, i: (b, i, r)),
            pl.BlockSpec((1, rows, LANES), lambda b, r, i: (b, i, r)),
        ],
        out_shape=[
            jax.ShapeDtypeStruct((B, Lsub, d * HEAD_W), BF16),
            jax.ShapeDtypeStruct((B, Lsub, d * LANES), F32),
        ],
        compiler_params=_params(("parallel", "parallel", "arbitrary")),
        name=f"dilated{gi}",
    )(view, view, view, view, view, bias)
    return o.reshape(B * S, HEAD_W), st.reshape(B * S, LANES)


def _alibi_bias(gi):
    window, d = DIL_PATTERNS[gi]
    Q = DIL_BLOCK
    dist = np.arange(Q)[:, None] + Q - np.arange(2 * Q)[None, :]
    valid = (dist >= 0) & (dist <= window // d)
    slopes = np.asarray(ALIBI_SLOPES, np.float32).reshape(DIL_GROUPS, DIL_HEADS)[gi]
    bias = -slopes[:, None, None] * (d * dist).astype(np.float32)
    return jnp.asarray(np.where(valid[None], bias, np.float32(NEG)), F32)


def _merge_kernel(x_ref, og_ref, o0_ref, o1_ref, o2_ref, s0_ref, s1_ref, s2_ref, g1_ref, wbg_ref, bbg_ref,
                  wpg_ref, wpa_ref, wout_ref, g2_ref, wrh_ref, wrl_ref, br_ref, tri_ref,
                  x2_ref, h2_ref, route_ref, cnt_ref, carry_ref):
    @pl.when(pl.program_id(0) == 0)
    def _():
        carry_ref[...] = jnp.zeros_like(carry_ref)

    x = x_ref[...]
    tm = x.shape[0]
    hb = _rms(x, g1_ref[...]).astype(BF16)
    gates = jax.nn.sigmoid(_dot(hb, wbg_ref[...]) + bbg_ref[...])

    stats = (s0_ref[...], s1_ref[...], s2_ref[...])
    dens = [pltpu.roll(s, 64, 1) for s in stats]
    m_all = jnp.maximum(jnp.maximum(stats[0], stats[1]), stats[2])
    wts = [d * jnp.exp(s - m_all) for s, d in zip(stats, dens)]
    inv = 1.0 / (wts[0] + wts[1] + wts[2])
    coef = [w * inv for w in wts]
    outs = (o0_ref, o1_ref, o2_ref)
    heads = []
    for h in range(DIL_HEADS):
        sl = slice(h * DIL_DH, (h + 1) * DIL_DH)
        acc = jnp.zeros((tm, DIL_DH), F32)
        for g in range(DIL_GROUPS):
            c = jnp.broadcast_to(coef[g][:, 16 * h:16 * h + 1], (tm, DIL_DH))
            acc = acc + c * outs[g][:, sl].astype(F32)
        heads.append(acc.astype(BF16))
    o_att = jnp.concatenate(heads, axis=1)

    y = gates[:, :D_MODEL] * _dot(og_ref[...], wpg_ref[...]) + gates[:, D_MODEL:] * _dot(o_att, wpa_ref[...])
    x2 = x + _dot(y.astype(BF16), wout_ref[...])
    x2_ref[...] = x2

    h2 = _rms(x2, g2_ref[...])
    h2_ref[...] = h2

    h2h = h2.astype(BF16)
    h2l = (h2 - h2h.astype(F32)).astype(BF16)
    wrh = wrh_ref[...]
    logit = _dot(h2h, wrh) + _dot(h2l, wrh) + _dot(h2h, wrl_ref[...]) + br_ref[...]

    lane = lax.broadcasted_iota(jnp.int32, (tm, LANES), 1).astype(F32)
    big = jnp.float32(1e9)
    gl = jnp.where(lane < N_GROUPS, logit, NEG)
    gmax = jnp.max(gl, axis=-1, keepdims=True)
    gsel = jnp.min(jnp.where(gl == gmax, lane, big), axis=-1, keepdims=True)
    g_p = 1.0 / jnp.sum(jnp.exp(gl - gmax), axis=-1, keepdims=True)
    lo = N_GROUPS + EXPERTS_PER_GROUP * gsel
    el = jnp.where((lane >= lo) & (lane < lo + EXPERTS_PER_GROUP), logit, NEG)
    v1 = jnp.max(el, axis=-1, keepdims=True)
    i1 = jnp.min(jnp.where(el == v1, lane, big), axis=-1, keepdims=True)
    el2 = jnp.where(lane == i1, NEG, el)
    v2 = jnp.max(el2, axis=-1, keepdims=True)
    i2 = jnp.min(jnp.where(el2 == v2, lane, big), axis=-1, keepdims=True)
    ex = jnp.exp(v2 - v1)
    w1 = g_p / (1.0 + ex)
    w2 = g_p * ex / (1.0 + ex)
    e1 = i1 - N_GROUPS
    e2 = i2 - N_GROUPS

    oh1 = lane == e1
    oh2 = lane == e2
    onehot = jnp.where(oh1 | oh2, 1.0, 0.0)
    prefix = _dot(tri_ref[...], onehot.astype(BF16)) + carry_ref[...]
    r1 = jnp.sum(jnp.where(oh1, prefix, 0.0), axis=-1, keepdims=True)
    r2 = jnp.sum(jnp.where(oh2, prefix, 0.0), axis=-1, keepdims=True)
    carry = carry_ref[...] + jnp.sum(onehot, axis=0, keepdims=True)
    carry_ref[...] = carry
    cnt_ref[...] = jnp.broadcast_to(carry, cnt_ref.shape)

    route = jnp.zeros((tm, LANES), F32)
    for idx, val in enumerate((e1, e2, w1, w2, r1, r2)):
        route = jnp.where(lane == idx, val, route)
    route_ref[...] = route


def _merge(x2d, o_gla, outs, stats, g1, wbg, bbg, wpg, wpa, wout, g2, wrh, wrl, br, tri):
    T = x2d.shape[0]
    tm = min(MERGE_TM, T)
    tok = lambda w: pl.BlockSpec((tm, w), lambda i: (i, 0))
    full = lambda a: pl.BlockSpec(a.shape, lambda i: (0,) * a.ndim)
    return pl.pallas_call(
        _merge_kernel,
        grid=(T // tm,),
        in_specs=[tok(D_MODEL), tok(HEAD_W), tok(HEAD_W), tok(HEAD_W), tok(HEAD_W),
                  tok(LANES), tok(LANES), tok(LANES),
                  full(g1), full(wbg), full(bbg), full(wpg), full(wpa), full(wout), full(g2),
                  full(wrh), full(wrl), full(br), full(tri)],
        out_specs=[tok(D_MODEL), tok(D_MODEL), tok(LANES), pl.BlockSpec((8, LANES), lambda i: (0, 0))],
        out_shape=[
            jax.ShapeDtypeStruct((T, D_MODEL), F32),
            jax.ShapeDtypeStruct((T, D_MODEL), F32),
            jax.ShapeDtypeStruct((T, LANES), F32),
            jax.ShapeDtypeStruct((8, LANES), F32),
        ],
        scratch_shapes=[pltpu.VMEM((1, LANES), F32)],
        compiler_params=_params(("arbitrary",)),
        name="merge",
    )(x2d, o_gla, *outs, *stats, g1, wbg, bbg, wpg, wpa, wout, g2, wrh, wrl, br, tri)


def _row_copy(src_ref, dst_ref, sem):
    return pltpu.make_async_copy(src_ref, dst_ref, sem)


def _dispatch_kernel(dest_ref, h2_ref, xin_hbm_ref, xin_ref, sem):
    del xin_hbm_ref
    tm = h2_ref.shape[0]

    def start(r, c):
        for k in range(TOP_K):
            d = dest_ref[0, 0, TOP_K * r + k]
            _row_copy(h2_ref.at[pl.ds(r, 1), :], xin_ref.at[pl.ds(d, 1), :], sem).start()
        return c

    lax.fori_loop(0, tm, start, 0)

    def wait(r, c):
        for k in range(TOP_K):
            _row_copy(h2_ref.at[pl.ds(0, 1), :], xin_ref.at[pl.ds(0, 1), :], sem).wait()
        return c

    lax.fori_loop(0, tm, wait, 0)


def _dispatch(dest, h2, xin0):
    T = h2.shape[0]
    tm = min(ROW_TM, T)
    return pl.pallas_call(
        _dispatch_kernel,
        grid=(T // tm,),
        in_specs=[
            pl.BlockSpec((1, 1, TOP_K * tm), lambda i: (i, 0, 0), memory_space=pltpu.SMEM),
            pl.BlockSpec((tm, D_MODEL), lambda i: (i, 0)),
            pl.BlockSpec(memory_space=pl.ANY),
        ],
        out_specs=pl.BlockSpec(memory_space=pl.ANY),
        out_shape=jax.ShapeDtypeStruct(xin0.shape, xin0.dtype),
        scratch_shapes=[pltpu.SemaphoreType.DMA],
        input_output_aliases={2: 0},
        compiler_params=_params(("arbitrary",)),
        name="dispatch",
    )(dest.reshape(T // tm, 1, TOP_K * tm), h2, xin0)


def _combine_kernel(dest_ref, x2_ref, route_ref, y_ref, out_ref, buf_ref, sem):
    tm = x2_ref.shape[0]

    def start(r, c):
        for k in range(TOP_K):
            d = dest_ref[0, 0, TOP_K * r + k]
            _row_copy(y_ref.at[pl.ds(d, 1), :], buf_ref.at[k, pl.ds(r, 1), :], sem).start()
        return c

    lax.fori_loop(0, tm, start, 0)

    def wait(r, c):
        for k in range(TOP_K):
            _row_copy(y_ref.at[pl.ds(0, 1), :], buf_ref.at[k, pl.ds(0, 1), :], sem).wait()
        return c

    lax.fori_loop(0, tm, wait, 0)

    route = route_ref[...]
    w1 = route[:, 2:3]
    w2 = route[:, 3:4]
    out_ref[...] = x2_ref[...] + w1 * buf_ref[0] + w2 * buf_ref[1]


def _combine(dest, x2, route, y):
    T = x2.shape[0]
    tm = min(ROW_TM, T)
    return pl.pallas_call(
        _combine_kernel,
        grid=(T // tm,),
        in_specs=[
            pl.BlockSpec((1, 1, TOP_K * tm), lambda i: (i, 0, 0), memory_space=pltpu.SMEM),
            pl.BlockSpec((tm, D_MODEL), lambda i: (i, 0)),
            pl.BlockSpec((tm, LANES), lambda i: (i, 0)),
            pl.BlockSpec(memory_space=pl.ANY),
        ],
        out_specs=pl.BlockSpec((tm, D_MODEL), lambda i: (i, 0)),
        out_shape=jax.ShapeDtypeStruct((T, D_MODEL), F32),
        scratch_shapes=[pltpu.VMEM((TOP_K, tm, D_MODEL), F32), pltpu.SemaphoreType.DMA],
        compiler_params=_params(("arbitrary",)),
        name="combine",
    )(dest.reshape(T // tm, 1, TOP_K * tm), x2, route, y)


def _expert_kernel(tile_e_ref, nact_ref, x_ref, wg_ref, wu_ref, wd_ref, y_ref):
    del tile_e_ref

    active = pl.program_id(0) < nact_ref[0]

    @pl.when(active)
    def _():
        xb = x_ref[...].astype(BF16)
        g = _dot(xb, wg_ref[0])
        u = _dot(xb, wu_ref[0])
        a = (g * jax.nn.sigmoid(g) * u).astype(BF16)
        y_ref[...] = _dot(a, wd_ref[0])

    @pl.when(jnp.logical_not(active))
    def _():
        y_ref[...] = jnp.zeros_like(y_ref)


def _experts(tile_e, nact, xin, wg, wu, wd):
    P = xin.shape[0]
    ntiles = P // MOE_TM

    def rows(i, te, na):
        return (jnp.minimum(i, na[0] - 1), 0)

    def wsel(i, te, na):
        return (te[jnp.minimum(i, na[0] - 1)], 0, 0)

    return pl.pallas_call(
        _expert_kernel,
        grid_spec=pltpu.PrefetchScalarGridSpec(
            num_scalar_prefetch=2,
            grid=(ntiles,),
            in_specs=[
                pl.BlockSpec((MOE_TM, D_MODEL), rows),
                pl.BlockSpec((1, D_MODEL, D_EXPERT), wsel),
                pl.BlockSpec((1, D_MODEL, D_EXPERT), wsel),
                pl.BlockSpec((1, D_EXPERT, D_MODEL), wsel),
            ],
            out_specs=pl.BlockSpec((MOE_TM, D_MODEL), lambda i, te, na: (i, 0)),
        ),
        out_shape=jax.ShapeDtypeStruct((P, D_MODEL), F32),
        compiler_params=_params(("arbitrary",)),
        name="experts",
    )(tile_e, nact, xin, wg, wu, wd)


def _layer(x2d, B, S, norm1_g, w_in, w_gla_a2, b_gla_a, gla_out_norm_g, dil_q_norm_g, dil_k_norm_g,
           w_proj_gla, w_proj_attn, w_branch_gate, b_branch_gate, w_out, norm2_g,
           w_router_group, b_router_group, w_router_expert, b_router_expert, w_gate, w_up, w_down):
    T = B * S
    n_gla = 4 * HEAD_W
    w_main = jnp.concatenate([w_in[:, :n_gla], w_in[:, n_gla + GLA_RANK:]], axis=1).astype(BF16)
    w_ga = jnp.pad(w_in[:, n_gla:n_gla + GLA_RANK], ((0, 0), (0, LANES - GLA_RANK))).astype(BF16)
    w_a2 = jnp.pad(w_gla_a2, ((0, LANES - GLA_RANK), (0, 0))).astype(BF16)
    gains = [jnp.ones((HEAD_W,), F32)] * 4
    for gi in range(DIL_GROUPS):
        gains += [jnp.tile(dil_q_norm_g[gi], DIL_HEADS) * (DIL_DH ** -0.5),
                  jnp.tile(dil_k_norm_g[gi], DIL_HEADS), jnp.ones((HEAD_W,), F32)]
    qk_gain = jnp.stack(gains).reshape(N_COL_TILES, 1, HEAD_W)

    proj, log_a = _inproj(x2d, norm1_g.reshape(1, -1), w_main, w_ga, w_a2, b_gla_a.reshape(1, -1), qk_gain)

    tri_incl = jnp.asarray(np.tril(np.ones((GLA_BLOCK, GLA_BLOCK), np.float32)), BF16)
    o_gla = _gla(proj, log_a, tri_incl, gla_out_norm_g.reshape(1, -1), B, S)

    outs, stats = [], []
    for gi in range(DIL_GROUPS):
        o, st = _dilated(proj, _alibi_bias(gi), gi, B, S)
        outs.append(o)
        stats.append(st)

    w_r = jnp.pad(jnp.concatenate([w_router_group, w_router_expert], axis=1),
                  ((0, 0), (0, LANES - N_GROUPS - N_EXPERTS)))
    w_rh = w_r.astype(BF16)
    w_rl = (w_r - w_rh.astype(F32)).astype(BF16)
    b_r = jnp.pad(jnp.concatenate([b_router_group, b_router_expert]), (0, LANES - N_GROUPS - N_EXPERTS))
    tm = min(MERGE_TM, T)
    tri_strict = jnp.asarray(np.tril(np.ones((tm, tm), np.float32), -1), BF16)
    x2, h2, route, cnt = _merge(
        x2d, o_gla, outs, stats, norm1_g.reshape(1, -1), w_branch_gate.astype(BF16),
        b_branch_gate.reshape(1, -1), w_proj_gla.astype(BF16), w_proj_attn.astype(BF16), w_out.astype(BF16),
        norm2_g.reshape(1, -1), w_rh, w_rl, b_r.reshape(1, -1), tri_strict)

    counts = cnt[0, :N_EXPERTS].astype(jnp.int32)
    pcounts = (counts + MOE_TM - 1) // MOE_TM * MOE_TM
    pends = jnp.cumsum(pcounts)
    pstarts = pends - pcounts
    eid = route[:, 0:TOP_K].astype(jnp.int32)
    rank = route[:, 4:4 + TOP_K].astype(jnp.int32)
    onehot = eid[:, :, None] == jnp.arange(N_EXPERTS, dtype=jnp.int32)[None, None, :]
    dest = rank + jnp.sum(jnp.where(onehot, pstarts[None, None, :], 0), axis=-1)
    P = T * TOP_K + N_EXPERTS * MOE_TM
    ntiles = P // MOE_TM
    tile_start = jnp.arange(ntiles, dtype=jnp.int32) * MOE_TM
    tile_e = jnp.minimum(jnp.sum(pends[None, :] <= tile_start[:, None], axis=1), N_EXPERTS - 1).astype(jnp.int32)
    nact = (pends[-1:] // MOE_TM).astype(jnp.int32)

    xin = _dispatch(dest, h2, jnp.zeros((P, D_MODEL), F32))
    y = _experts(tile_e, nact, xin, w_gate.astype(BF16), w_up.astype(BF16), w_down.astype(BF16))
    return _combine(dest, x2, route, y)


def kernel(x, norm1_g, w_in, w_gla_a2, b_gla_a, gla_out_norm_g, dil_q_norm_g, dil_k_norm_g, w_proj_gla,
           w_proj_attn, w_branch_gate, b_branch_gate, w_out, norm2_g, w_router_group, b_router_group,
           w_router_expert, b_router_expert, w_gate, w_up, w_down):
    B, S, D = x.shape
    assert D == D_MODEL and S % (DIL_BLOCK * DIL_PATTERNS[-1][1]) == 0 and (B * S) % INPROJ_TM == 0
    x2d = x.reshape(B * S, D)
    params = (norm1_g, w_in, w_gla_a2, b_gla_a, gla_out_norm_g, dil_q_norm_g, dil_k_norm_g, w_proj_gla,
              w_proj_attn, w_branch_gate, b_branch_gate, w_out, norm2_g, w_router_group, b_router_group,
              w_router_expert, b_router_expert, w_gate, w_up, w_down)
    for layer in range(norm1_g.shape[0]):
        x2d = _layer(x2d, B, S, *(p[layer] for p in params))
    return x2d.reshape(B, S, D)
```

```python
import functools

import jax
import jax.numpy as jnp
import numpy as np
from jax import lax
from jax.experimental import pallas as pl
from jax.experimental.pallas import tpu as pltpu

F32 = jnp.float32
BF16 = jnp.bfloat16

D_MODEL = 1024
EPS = 1e-6
GLA_HEADS = 4
GLA_DK = 128
GLA_RANK = 16
GLA_TAU = 16.0
GLA_SUBCHUNK = 16
GLA_BLOCK = 128
DIL_PATTERNS = ((128, 1), (512, 4), (2048, 16))
DIL_GROUPS = 3
DIL_HEADS = 4
DIL_DH = 128
DIL_BLOCK = 128
ALIBI_SLOPES = tuple(2.0 ** (-8.0 * (i + 1) / (DIL_GROUPS * DIL_HEADS)) for i in range(DIL_GROUPS * DIL_HEADS))
N_GROUPS = 4
EXPERTS_PER_GROUP = 8
N_EXPERTS = N_GROUPS * EXPERTS_PER_GROUP
TOP_K = 2
D_EXPERT = 512

HEAD_W = 512
N_GLA_TILES = 4
N_COL_TILES = N_GLA_TILES + 3 * DIL_GROUPS
QKV_W = 3 * HEAD_W
LANES = 128
NEG = -1e30

INPROJ_TM = 1024
MERGE_TM = 512
MOE_TM = 512
ROW_TM = 512
ROW_UNROLL = 8
DIL_QBLOCKS = 4
VMEM_LIMIT = 56 * 1024 * 1024


def _dot(a, b):
    return jnp.dot(a, b, preferred_element_type=F32)


def _dot_nt(a, b):
    return lax.dot_general(a, b, (((1,), (1,)), ((), ())), preferred_element_type=F32)


def _dot_tn(a, b):
    return lax.dot_general(a, b, (((0,), (0,)), ((), ())), preferred_element_type=F32)


def _rms(x, g):
    return x * lax.rsqrt(jnp.mean(x * x, axis=-1, keepdims=True) + EPS) * g


def _params(sem):
    return pltpu.CompilerParams(dimension_semantics=sem, vmem_limit_bytes=VMEM_LIMIT)


def _head(c):
    return slice(c * LANES, (c + 1) * LANES)


def _inproj_kernel(x_ref, g1_ref, w_ref, wga_ref, wa2_ref, ba_ref, qkg_ref,
                   gla_ref, loga_ref, d0_ref, d1_ref, d2_ref, h_ref, ybuf_ref):
    j = pl.program_id(1)
    tm = x_ref.shape[0]

    @pl.when(j == 0)
    def _():
        hb = _rms(x_ref[...], g1_ref[...]).astype(BF16)
        h_ref[...] = hb
        ga = _dot(hb, wga_ref[...])
        z = _dot(ga.astype(BF16), wa2_ref[...]) + ba_ref[...]
        log_sig = jnp.minimum(z, 0.0) - jnp.log(1.0 + jnp.exp(-jnp.abs(z)))
        loga_ref[...] = log_sig * (1.0 / GLA_TAU)

    y = _dot(h_ref[...], w_ref[...])
    dil_refs = (d0_ref, d1_ref, d2_ref)

    for jj in range(N_COL_TILES):
        @pl.when(j == jj)
        def _(jj=jj):
            if jj < N_GLA_TILES:
                gla_ref[...] = y.astype(BF16)
                return
            gi, kind = divmod(jj - N_GLA_TILES, 3)
            d = DIL_PATTERNS[gi][1]
            if kind < 2:
                g = qkg_ref[0]
                slabs = [_rms(y[:, _head(c)], g[:, _head(c)]) for c in range(DIL_HEADS)]
            else:
                slabs = [y[:, _head(c)] for c in range(DIL_HEADS)]
            out = dil_refs[gi]
            if d == 1:
                for c in range(DIL_HEADS):
                    out[:, _head(c)] = slabs[c].astype(BF16)
                return
            for c in range(DIL_HEADS):
                ybuf_ref[c] = slabs[c]
            for r in range(d):
                for c in range(DIL_HEADS):
                    out[0, r, :, _head(c)] = ybuf_ref[c, pl.ds(r, tm // d, stride=d), :].astype(BF16)


def _inproj(x2d, g1, w_main, w_ga, w_a2, b_a, qk_gain, B, S):
    T = x2d.shape[0]
    tm = min(INPROJ_TM, S)
    tpb = S // tm

    def dil_spec(gi):
        d = DIL_PATTERNS[gi][1]
        first = N_GLA_TILES + 3 * gi
        col = lambda j: jnp.clip(j - first, 0, 2)
        if d == 1:
            return pl.BlockSpec((tm, HEAD_W), lambda i, j: (i, col(j)))
        return pl.BlockSpec((1, d, tm // d, HEAD_W), lambda i, j: (i // tpb, 0, i % tpb, col(j)))

    def dil_shape(gi):
        d = DIL_PATTERNS[gi][1]
        shape = (T, QKV_W) if d == 1 else (B, d, S // d, QKV_W)
        return jax.ShapeDtypeStruct(shape, BF16)

    return pl.pallas_call(
        _inproj_kernel,
        grid=(T // tm, N_COL_TILES),
        in_specs=[
            pl.BlockSpec((tm, D_MODEL), lambda i, j: (i, 0)),
            pl.BlockSpec((1, D_MODEL), lambda i, j: (0, 0)),
            pl.BlockSpec((D_MODEL, HEAD_W), lambda i, j: (0, j)),
            pl.BlockSpec((D_MODEL, LANES), lambda i, j: (0, 0)),
            pl.BlockSpec((LANES, HEAD_W), lambda i, j: (0, 0)),
            pl.BlockSpec((1, HEAD_W), lambda i, j: (0, 0)),
            pl.BlockSpec((1, 1, HEAD_W), lambda i, j: (j, 0, 0)),
        ],
        out_specs=[
            pl.BlockSpec((tm, HEAD_W), lambda i, j: (i, jnp.minimum(j, N_GLA_TILES - 1))),
            pl.BlockSpec((tm, HEAD_W), lambda i, j: (i, 0)),
            dil_spec(0), dil_spec(1), dil_spec(2),
        ],
        out_shape=[
            jax.ShapeDtypeStruct((T, N_GLA_TILES * HEAD_W), BF16),
            jax.ShapeDtypeStruct((T, HEAD_W), F32),
            dil_shape(0), dil_shape(1), dil_shape(2),
        ],
        scratch_shapes=[pltpu.VMEM((tm, D_MODEL), BF16), pltpu.VMEM((DIL_HEADS, tm, LANES), F32)],
        compiler_params=_params(("arbitrary", "arbitrary")),
        name="inproj",
    )(x2d, g1, w_main, w_ga, w_a2, b_a, qk_gain)


def _gla_kernel(q_ref, k_ref, v_ref, r_ref, la_ref, tri_ref, gn_ref, o_ref, st_ref):
    n = pl.program_id(1)
    L = GLA_BLOCK

    @pl.when(n == 0)
    def _():
        st_ref[...] = jnp.zeros_like(st_ref)

    la = la_ref[...]
    hi = la.astype(BF16)
    r1 = la - hi.astype(F32)
    mid = r1.astype(BF16)
    lo = (r1 - mid.astype(F32)).astype(BF16)
    tri = tri_ref[...]
    b_all = _dot(tri, hi) + _dot(tri, mid) + _dot(tri, lo)

    row = lax.broadcasted_iota(jnp.int32, (L, L), 0)
    col = lax.broadcasted_iota(jnp.int32, (L, L), 1)

    for h in range(GLA_HEADS):
        sl = _head(h)
        b = b_all[:, sl]
        q = q_ref[:, sl].astype(F32) * (GLA_DK ** -0.5)
        k = k_ref[:, sl].astype(F32)
        v = v_ref[:, sl]

        att = jnp.zeros((L, L), F32)
        seg = L
        while seg > GLA_SUBCHUNK:
            half = seg // 2
            beta = jnp.concatenate(
                [jnp.broadcast_to(b[s + half - 1:s + half, :], (seg, GLA_DK)) for s in range(0, L, seg)], axis=0)
            qs = (q * jnp.exp(jnp.minimum(b - beta, 0.0))).astype(BF16)
            ks = (k * jnp.exp(jnp.minimum(beta - b, 0.0))).astype(BF16)
            same = (row // seg) == (col // seg)
            mask = same & ((row % seg) >= half) & ((col % seg) < half)
            att = jnp.where(mask, _dot_nt(qs, ks), att)
            seg = half
        beta = jnp.concatenate(
            [jnp.zeros((GLA_SUBCHUNK, GLA_DK), F32)]
            + [jnp.broadcast_to(b[s - 1:s, :], (GLA_SUBCHUNK, GLA_DK)) for s in range(GLA_SUBCHUNK, L, GLA_SUBCHUNK)],
            axis=0)
        qs = (q * jnp.exp(b - beta)).astype(BF16)
        ks = (k * jnp.exp(beta - b)).astype(BF16)
        mask = ((row // GLA_SUBCHUNK) == (col // GLA_SUBCHUNK)) & (col <= row)
        att = jnp.where(mask, _dot_nt(qs, ks), att)

        state_t = st_ref[h]
        q0 = (q * jnp.exp(b)).astype(BF16)
        o = _dot(att.astype(BF16), v) + _dot_nt(q0, state_t.astype(BF16))

        b_last = b[L - 1:L, :]
        k_end = (k * jnp.exp(b_last - b)).astype(BF16)
        st_ref[h] = state_t * jnp.exp(b_last) + _dot_tn(v, k_end)

        rr = r_ref[:, sl].astype(F32)
        o_ref[:, sl] = (_rms(o, gn_ref[...]) * (rr * jax.nn.sigmoid(rr))).astype(BF16)


def _gla(proj, log_a, tri, gn, B, S):
    L = GLA_BLOCK
    nb = S // L
    return pl.pallas_call(
        _gla_kernel,
        grid=(B, nb),
        in_specs=[
            pl.BlockSpec((L, HEAD_W), lambda b, n: (b * nb + n, 0)),
            pl.BlockSpec((L, HEAD_W), lambda b, n: (b * nb + n, 1)),
            pl.BlockSpec((L, HEAD_W), lambda b, n: (b * nb + n, 2)),
            pl.BlockSpec((L, HEAD_W), lambda b, n: (b * nb + n, 3)),
            pl.BlockSpec((L, HEAD_W), lambda b, n: (b * nb + n, 0)),
            pl.BlockSpec((L, L), lambda b, n: (0, 0)),
            pl.BlockSpec((1, GLA_DK), lambda b, n: (0, 0)),
        ],
        out_specs=pl.BlockSpec((L, HEAD_W), lambda b, n: (b * nb + n, 0)),
        out_shape=jax.ShapeDtypeStruct((B * S, HEAD_W), BF16),
        scratch_shapes=[pltpu.VMEM((GLA_HEADS, GLA_DK, GLA_DK), F32)],
        compiler_params=_params(("arbitrary", "arbitrary")),
        name="gla",
    )(proj, proj, proj, proj, log_a, tri, gn)


def _dil_kernel(q_ref, kp_ref, kc_ref, vp_ref, vc_ref, bias_ref, o_ref, st_ref, *, nq):
    first = pl.program_id(2) == 0
    Q = DIL_BLOCK
    col = lax.broadcasted_iota(jnp.int32, (Q, 2 * Q), 1)
    lane = lax.broadcasted_iota(jnp.int32, (Q, LANES), 1)
    for blk in range(nq):
        rows = slice(blk * Q, (blk + 1) * Q)
        prev = slice((blk - 1) * Q, blk * Q)
        stats = jnp.zeros((Q, LANES), F32)
        for h in range(DIL_HEADS):
            sl = _head(h)
            if blk == 0:
                kprev, vprev = kp_ref[0, 0, :, sl], vp_ref[0, 0, :, sl]
            else:
                kprev, vprev = kc_ref[0, 0, prev, sl], vc_ref[0, 0, prev, sl]
            kk = jnp.concatenate([kprev, kc_ref[0, 0, rows, sl]], axis=0)
            vv = jnp.concatenate([vprev, vc_ref[0, 0, rows, sl]], axis=0)
            s = _dot_nt(q_ref[0, 0, rows, sl], kk) + bias_ref[h]
            if blk == 0:
                s = jnp.where(jnp.logical_and(first, col < Q), NEG, s)
            m = jnp.max(s, axis=-1, keepdims=True)
            p = jnp.exp(s - m)
            den = jnp.sum(p, axis=-1, keepdims=True)
            o_ref[0, 0, rows, sl] = (_dot(p.astype(BF16), vv) / den).astype(BF16)
            stats = jnp.where((lane // 16) == h, m, stats)
            stats = jnp.where((lane // 16) == DIL_HEADS + h, den, stats)
        st_ref[0, 0, rows, :] = stats


def _dilated(qkv, bias, gi):
    B, d, Lsub, _ = qkv.shape
    nq = min(DIL_QBLOCKS, Lsub // DIL_BLOCK)
    rows = nq * DIL_BLOCK

    def cur(c):
        return pl.BlockSpec((1, 1, rows, HEAD_W), lambda b, r, i: (b, r, i, c))

    def prev(c):
        return pl.BlockSpec((1, 1, DIL_BLOCK, HEAD_W), lambda b, r, i: (b, r, jnp.maximum(i * nq - 1, 0), c))

    return pl.pallas_call(
        functools.partial(_dil_kernel, nq=nq),
        grid=(B, d, Lsub // rows),
        in_specs=[cur(0), prev(1), cur(1), prev(2), cur(2),
                  pl.BlockSpec((DIL_HEADS, DIL_BLOCK, 2 * DIL_BLOCK), lambda b, r, i: (0, 0, 0))],
        out_specs=[
            pl.BlockSpec((1, 1, rows, HEAD_W), lambda b, r, i: (b, r, i, 0)),
            pl.BlockSpec((1, 1, rows, LANES), lambda b, r, i: (b, r, i, 0)),
        ],
        out_shape=[
            jax.ShapeDtypeStruct((B, d, Lsub, HEAD_W), BF16),
            jax.ShapeDtypeStruct((B, d, Lsub, LANES), F32),
        ],
        compiler_params=_params(("arbitrary", "arbitrary", "arbitrary")),
        name=f"dilated{gi}",
    )(qkv, qkv, qkv, qkv, qkv, bias)


def _alibi_bias(gi):
    window, d = DIL_PATTERNS[gi]
    Q = DIL_BLOCK
    dist = np.arange(Q)[:, None] + Q - np.arange(2 * Q)[None, :]
    valid = (dist >= 0) & (dist <= window // d)
    slopes = np.asarray(ALIBI_SLOPES, np.float32).reshape(DIL_GROUPS, DIL_HEADS)[gi]
    bias = -slopes[:, None, None] * (d * dist).astype(np.float32)
    return jnp.asarray(np.where(valid[None], bias, np.float32(NEG)), F32)


def _merge_kernel(x_ref, og_ref, o0_ref, o1_ref, o2_ref, s0_ref, s1_ref, s2_ref, g1_ref, wbg_ref, bbg_ref,
                  wpg_ref, wpa_ref, wout_ref, g2_ref, wrh_ref, wrl_ref, br_ref, tri_ref,
                  x2_ref, h2_ref, route_ref, cnt_ref, carry_ref, obuf_ref, sbuf_ref):
    @pl.when(pl.program_id(0) == 0)
    def _():
        carry_ref[...] = jnp.zeros_like(carry_ref)

    x = x_ref[...]
    tm = x.shape[0]
    hb = _rms(x, g1_ref[...]).astype(BF16)
    gates = jax.nn.sigmoid(_dot(hb, wbg_ref[...]) + bbg_ref[...])

    for slot, (o_ref, s_ref) in enumerate(((o1_ref, s1_ref), (o2_ref, s2_ref))):
        d = DIL_PATTERNS[slot + 1][1]
        for r in range(d):
            sbuf_ref[slot, pl.ds(r, tm // d, stride=d), :] = s_ref[0, r]
            for c in range(DIL_HEADS):
                obuf_ref[slot, c, pl.ds(r, tm // d, stride=d), :] = o_ref[0, r, :, _head(c)].astype(F32)

    stats = (s0_ref[...], sbuf_ref[0], sbuf_ref[1])
    dens = [pltpu.roll(s, 64, 1) for s in stats]
    m_all = jnp.maximum(jnp.maximum(stats[0], stats[1]), stats[2])
    wts = [d * jnp.exp(s - m_all) for s, d in zip(stats, dens)]
    inv = 1.0 / (wts[0] + wts[1] + wts[2])
    coef = [w * inv for w in wts]
    heads = []
    for h in range(DIL_HEADS):
        group_out = (o0_ref[:, _head(h)].astype(F32), obuf_ref[0, h], obuf_ref[1, h])
        acc = jnp.zeros((tm, DIL_DH), F32)
        for g in range(DIL_GROUPS):
            c = jnp.broadcast_to(coef[g][:, 16 * h:16 * h + 1], (tm, DIL_DH))
            acc = acc + c * group_out[g]
        heads.append(acc.astype(BF16))
    o_att = jnp.concatenate(heads, axis=1)

    y = gates[:, :D_MODEL] * _dot(og_ref[...], wpg_ref[...]) + gates[:, D_MODEL:] * _dot(o_att, wpa_ref[...])
    x2 = x + _dot(y.astype(BF16), wout_ref[...])
    x2_ref[...] = x2

    h2 = _rms(x2, g2_ref[...])
    h2_ref[...] = h2

    h2h = h2.astype(BF16)
    h2l = (h2 - h2h.astype(F32)).astype(BF16)
    wrh = wrh_ref[...]
    logit = _dot(h2h, wrh) + _dot(h2l, wrh) + _dot(h2h, wrl_ref[...]) + br_ref[...]

    lane = lax.broadcasted_iota(jnp.int32, (tm, LANES), 1).astype(F32)
    big = jnp.float32(1e9)
    gl = jnp.where(lane < N_GROUPS, logit, NEG)
    gmax = jnp.max(gl, axis=-1, keepdims=True)
    gsel = jnp.min(jnp.where(gl == gmax, lane, big), axis=-1, keepdims=True)
    g_p = 1.0 / jnp.sum(jnp.exp(gl - gmax), axis=-1, keepdims=True)
    lo = N_GROUPS + EXPERTS_PER_GROUP * gsel
    el = jnp.where((lane >= lo) & (lane < lo + EXPERTS_PER_GROUP), logit, NEG)
    v1 = jnp.max(el, axis=-1, keepdims=True)
    i1 = jnp.min(jnp.where(el == v1, lane, big), axis=-1, keepdims=True)
    el2 = jnp.where(lane == i1, NEG, el)
    v2 = jnp.max(el2, axis=-1, keepdims=True)
    i2 = jnp.min(jnp.where(el2 == v2, lane, big), axis=-1, keepdims=True)
    ex = jnp.exp(v2 - v1)
    w1 = g_p / (1.0 + ex)
    w2 = g_p * ex / (1.0 + ex)
    e1 = i1 - N_GROUPS
    e2 = i2 - N_GROUPS

    oh1 = lane == e1
    oh2 = lane == e2
    onehot = jnp.where(oh1 | oh2, 1.0, 0.0)
    prefix = _dot(tri_ref[...], onehot.astype(BF16)) + carry_ref[...]
    r1 = jnp.sum(jnp.where(oh1, prefix, 0.0), axis=-1, keepdims=True)
    r2 = jnp.sum(jnp.where(oh2, prefix, 0.0), axis=-1, keepdims=True)
    carry = carry_ref[...] + jnp.sum(onehot, axis=0, keepdims=True)
    carry_ref[...] = carry
    cnt_ref[...] = jnp.broadcast_to(carry, cnt_ref.shape)

    route = jnp.zeros((tm, LANES), F32)
    for idx, val in enumerate((e1, e2, w1, w2, r1, r2)):
        route = jnp.where(lane == idx, val, route)
    route_ref[...] = route


def _merge(x2d, o_gla, outs, stats, g1, wbg, bbg, wpg, wpa, wout, g2, wrh, wrl, br, tri, S):
    T = x2d.shape[0]
    tm = min(MERGE_TM, S)
    tpb = S // tm
    tok = lambda w: pl.BlockSpec((tm, w), lambda i: (i, 0))
    full = lambda a: pl.BlockSpec(a.shape, lambda i: (0,) * a.ndim)

    def res(gi, w):
        d = DIL_PATTERNS[gi][1]
        return pl.BlockSpec((1, d, tm // d, w), lambda i: (i // tpb, 0, i % tpb, 0))

    return pl.pallas_call(
        _merge_kernel,
        grid=(T // tm,),
        in_specs=[tok(D_MODEL), tok(HEAD_W), tok(HEAD_W), res(1, HEAD_W), res(2, HEAD_W),
                  tok(LANES), res(1, LANES), res(2, LANES),
                  full(g1), full(wbg), full(bbg), full(wpg), full(wpa), full(wout), full(g2),
                  full(wrh), full(wrl), full(br), full(tri)],
        out_specs=[tok(D_MODEL), tok(D_MODEL), tok(LANES), pl.BlockSpec((8, LANES), lambda i: (0, 0))],
        out_shape=[
            jax.ShapeDtypeStruct((T, D_MODEL), F32),
            jax.ShapeDtypeStruct((T, D_MODEL), F32),
            jax.ShapeDtypeStruct((T, LANES), F32),
            jax.ShapeDtypeStruct((8, LANES), F32),
        ],
        scratch_shapes=[pltpu.VMEM((1, LANES), F32),
                        pltpu.VMEM((DIL_GROUPS - 1, DIL_HEADS, tm, LANES), F32),
                        pltpu.VMEM((DIL_GROUPS - 1, tm, LANES), F32)],
        compiler_params=_params(("arbitrary",)),
        name="merge",
    )(x2d, o_gla, *outs, *stats, g1, wbg, bbg, wpg, wpa, wout, g2, wrh, wrl, br, tri)


def _row_copy(src_ref, dst_ref, sem):
    return pltpu.make_async_copy(src_ref, dst_ref, sem)


def _dispatch_kernel(dest_ref, h2_ref, xin_hbm_ref, xin_ref, sem):
    del xin_hbm_ref
    tm = h2_ref.shape[0]

    def start(r, c):
        for k in range(TOP_K):
            d = dest_ref[0, 0, TOP_K * r + k]
            _row_copy(h2_ref.at[pl.ds(r, 1), :], xin_ref.at[pl.ds(d, 1), :], sem).start()
        return c

    lax.fori_loop(0, tm, start, 0, unroll=ROW_UNROLL)
    for k in range(TOP_K):
        _row_copy(h2_ref, xin_ref.at[pl.ds(0, tm), :], sem).wait()


def _dispatch(dest, h2, xin0):
    T = h2.shape[0]
    tm = min(ROW_TM, T)
    return pl.pallas_call(
        _dispatch_kernel,
        grid=(T // tm,),
        in_specs=[
            pl.BlockSpec((1, 1, TOP_K * tm), lambda i: (i, 0, 0), memory_space=pltpu.SMEM),
            pl.BlockSpec((tm, D_MODEL), lambda i: (i, 0)),
            pl.BlockSpec(memory_space=pl.ANY),
        ],
        out_specs=pl.BlockSpec(memory_space=pl.ANY),
        out_shape=jax.ShapeDtypeStruct(xin0.shape, xin0.dtype),
        scratch_shapes=[pltpu.SemaphoreType.DMA],
        input_output_aliases={2: 0},
        compiler_params=_params(("arbitrary",)),
        name="dispatch",
    )(dest.reshape(T // tm, 1, TOP_K * tm), h2, xin0)


def _combine_kernel(dest_ref, x2_ref, route_ref, y_ref, out_ref, buf_ref, sem):
    tm = x2_ref.shape[0]

    def start(r, c):
        for k in range(TOP_K):
            d = dest_ref[0, 0, TOP_K * r + k]
            _row_copy(y_ref.at[pl.ds(d, 1), :], buf_ref.at[k, pl.ds(r, 1), :], sem).start()
        return c

    lax.fori_loop(0, tm, start, 0, unroll=ROW_UNROLL)
    for k in range(TOP_K):
        _row_copy(y_ref.at[pl.ds(0, tm), :], buf_ref.at[k], sem).wait()

    route = route_ref[...]
    w1 = route[:, 2:3]
    w2 = route[:, 3:4]
    out_ref[...] = x2_ref[...] + w1 * buf_ref[0] + w2 * buf_ref[1]


def _combine(dest, x2, route, y):
    T = x2.shape[0]
    tm = min(ROW_TM, T)
    return pl.pallas_call(
        _combine_kernel,
        grid=(T // tm,),
        in_specs=[
            pl.BlockSpec((1, 1, TOP_K * tm), lambda i: (i, 0, 0), memory_space=pltpu.SMEM),
            pl.BlockSpec((tm, D_MODEL), lambda i: (i, 0)),
            pl.BlockSpec((tm, LANES), lambda i: (i, 0)),
            pl.BlockSpec(memory_space=pl.ANY),
        ],
        out_specs=pl.BlockSpec((tm, D_MODEL), lambda i: (i, 0)),
        out_shape=jax.ShapeDtypeStruct((T, D_MODEL), F32),
        scratch_shapes=[pltpu.VMEM((TOP_K, tm, D_MODEL), F32), pltpu.SemaphoreType.DMA],
        compiler_params=_params(("arbitrary",)),
        name="combine",
    )(dest.reshape(T // tm, 1, TOP_K * tm), x2, route, y)


def _expert_kernel(tile_e_ref, nact_ref, x_ref, wg_ref, wu_ref, wd_ref, y_ref):
    del tile_e_ref
    active = pl.program_id(0) < nact_ref[0]

    @pl.when(active)
    def _():
        xb = x_ref[...].astype(BF16)
        g = _dot(xb, wg_ref[0])
        u = _dot(xb, wu_ref[0])
        a = (g * jax.nn.sigmoid(g) * u).astype(BF16)
        y_ref[...] = _dot(a, wd_ref[0])

    @pl.when(jnp.logical_not(active))
    def _():
        y_ref[...] = jnp.zeros_like(y_ref)


def _experts(tile_e, nact, xin, wg, wu, wd):
    P = xin.shape[0]
    ntiles = P // MOE_TM

    def rows(i, te, na):
        return (jnp.minimum(i, na[0] - 1), 0)

    def wsel(i, te, na):
        return (te[jnp.minimum(i, na[0] - 1)], 0, 0)

    return pl.pallas_call(
        _expert_kernel,
        grid_spec=pltpu.PrefetchScalarGridSpec(
            num_scalar_prefetch=2,
            grid=(ntiles,),
            in_specs=[
                pl.BlockSpec((MOE_TM, D_MODEL), rows),
                pl.BlockSpec((1, D_MODEL, D_EXPERT), wsel),
                pl.BlockSpec((1, D_MODEL, D_EXPERT), wsel),
                pl.BlockSpec((1, D_EXPERT, D_MODEL), wsel),
            ],
            out_specs=pl.BlockSpec((MOE_TM, D_MODEL), lambda i, te, na: (i, 0)),
        ),
        out_shape=jax.ShapeDtypeStruct((P, D_MODEL), F32),
        compiler_params=_params(("arbitrary",)),
        name="experts",
    )(tile_e, nact, xin, wg, wu, wd)


def _layer(x2d, B, S, norm1_g, w_in, w_gla_a2, b_gla_a, gla_out_norm_g, dil_q_norm_g, dil_k_norm_g,
           w_proj_gla, w_proj_attn, w_branch_gate, b_branch_gate, w_out, norm2_g,
           w_router_group, b_router_group, w_router_expert, b_router_expert, w_gate, w_up, w_down):
    T = B * S
    n_gla = N_GLA_TILES * HEAD_W
    w_main = jnp.concatenate([w_in[:, :n_gla], w_in[:, n_gla + GLA_RANK:]], axis=1).astype(BF16)
    w_ga = jnp.pad(w_in[:, n_gla:n_gla + GLA_RANK], ((0, 0), (0, LANES - GLA_RANK))).astype(BF16)
    w_a2 = jnp.pad(w_gla_a2, ((0, LANES - GLA_RANK), (0, 0))).astype(BF16)
    gains = [jnp.ones((HEAD_W,), F32)] * N_GLA_TILES
    for gi in range(DIL_GROUPS):
        gains += [jnp.tile(dil_q_norm_g[gi], DIL_HEADS) * (DIL_DH ** -0.5),
                  jnp.tile(dil_k_norm_g[gi], DIL_HEADS), jnp.ones((HEAD_W,), F32)]
    qk_gain = jnp.stack(gains).reshape(N_COL_TILES, 1, HEAD_W)

    gla_in, log_a, qkv0, qkv1, qkv2 = _inproj(
        x2d, norm1_g.reshape(1, -1), w_main, w_ga, w_a2, b_gla_a.reshape(1, -1), qk_gain, B, S)

    tri_incl = jnp.asarray(np.tril(np.ones((GLA_BLOCK, GLA_BLOCK), np.float32)), BF16)
    o_gla = _gla(gla_in, log_a, tri_incl, gla_out_norm_g.reshape(1, -1), B, S)

    outs, stats = [], []
    for gi, qkv in enumerate((qkv0.reshape(B, 1, S, QKV_W), qkv1, qkv2)):
        o, st = _dilated(qkv, _alibi_bias(gi), gi)
        outs.append(o)
        stats.append(st)
    outs[0] = outs[0].reshape(T, HEAD_W)
    stats[0] = stats[0].reshape(T, LANES)

    w_r = jnp.pad(jnp.concatenate([w_router_group, w_router_expert], axis=1),
                  ((0, 0), (0, LANES - N_GROUPS - N_EXPERTS)))
    w_rh = w_r.astype(BF16)
    w_rl = (w_r - w_rh.astype(F32)).astype(BF16)
    b_r = jnp.pad(jnp.concatenate([b_router_group, b_router_expert]), (0, LANES - N_GROUPS - N_EXPERTS))
    tm = min(MERGE_TM, S)
    tri_strict = jnp.asarray(np.tril(np.ones((tm, tm), np.float32), -1), BF16)
    x2, h2, route, cnt = _merge(
        x2d, o_gla, outs, stats, norm1_g.reshape(1, -1), w_branch_gate.astype(BF16),
        b_branch_gate.reshape(1, -1), w_proj_gla.astype(BF16), w_proj_attn.astype(BF16), w_out.astype(BF16),
        norm2_g.reshape(1, -1), w_rh, w_rl, b_r.reshape(1, -1), tri_strict, S)

    counts = cnt[0, :N_EXPERTS].astype(jnp.int32)
    pcounts = (counts + MOE_TM - 1) // MOE_TM * MOE_TM
    pends = jnp.cumsum(pcounts)
    pstarts = pends - pcounts
    eid = route[:, 0:TOP_K].astype(jnp.int32)
    rank = route[:, 4:4 + TOP_K].astype(jnp.int32)
    onehot = eid[:, :, None] == jnp.arange(N_EXPERTS, dtype=jnp.int32)[None, None, :]
    dest = rank + jnp.sum(jnp.where(onehot, pstarts[None, None, :], 0), axis=-1)
    P = T * TOP_K + N_EXPERTS * MOE_TM
    ntiles = P // MOE_TM
    tile_start = jnp.arange(ntiles, dtype=jnp.int32) * MOE_TM
    tile_e = jnp.minimum(jnp.sum(pends[None, :] <= tile_start[:, None], axis=1), N_EXPERTS - 1).astype(jnp.int32)
    nact = (pends[-1:] // MOE_TM).astype(jnp.int32)

    xin = _dispatch(dest, h2, jnp.zeros((P, D_MODEL), F32))
    y = _experts(tile_e, nact, xin, w_gate.astype(BF16), w_up.astype(BF16), w_down.astype(BF16))
    return _combine(dest, x2, route, y)


def kernel(x, norm1_g, w_in, w_gla_a2, b_gla_a, gla_out_norm_g, dil_q_norm_g, dil_k_norm_g, w_proj_gla,
           w_proj_attn, w_branch_gate, b_branch_gate, w_out, norm2_g, w_router_group, b_router_group,
           w_router_expert, b_router_expert, w_gate, w_up, w_down):
    B, S, D = x.shape
    assert D == D_MODEL and S % (DIL_BLOCK * DIL_PATTERNS[-1][1]) == 0
    x2d = x.reshape(B * S, D)
    params = (norm1_g, w_in, w_gla_a2, b_gla_a, gla_out_norm_g, dil_q_norm_g, dil_k_norm_g, w_proj_gla,
              w_proj_attn, w_branch_gate, b_branch_gate, w_out, norm2_g, w_router_group, b_router_group,
              w_router_expert, b_router_expert, w_gate, w_up, w_down)
    for layer in range(norm1_g.shape[0]):
        x2d = _layer(x2d, B, S, *(p[layer] for p in params))
    return x2d.reshape(B, S, D)
```

```python
import functools

import jax
import jax.numpy as jnp
import numpy as np
from jax import lax
from jax.experimental import pallas as pl
from jax.experimental.pallas import tpu as pltpu

F32 = jnp.float32
BF16 = jnp.bfloat16

D_MODEL = 1024
EPS = 1e-6
GLA_HEADS = 4
GLA_DK = 128
GLA_RANK = 16
GLA_TAU = 16.0
GLA_SUBCHUNK = 16
GLA_BLOCK = 128
DIL_PATTERNS = ((128, 1), (512, 4), (2048, 16))
DIL_GROUPS = 3
DIL_HEADS = 4
DIL_DH = 128
DIL_BLOCK = 128
ALIBI_SLOPES = tuple(2.0 ** (-8.0 * (i + 1) / (DIL_GROUPS * DIL_HEADS)) for i in range(DIL_GROUPS * DIL_HEADS))
N_GROUPS = 4
EXPERTS_PER_GROUP = 8
N_EXPERTS = N_GROUPS * EXPERTS_PER_GROUP
TOP_K = 2
D_EXPERT = 512

HEAD_W = 512
N_GLA_TILES = 4
N_COL_TILES = N_GLA_TILES + 3 * DIL_GROUPS
QKV_W = 3 * HEAD_W
LANES = 128
NEG = -1e30

INPROJ_TM = 1024
MERGE_TM = 512
MOE_TM = 512
ROW_TM = 512
ROW_UNROLL = 8
DIL_QBLOCKS = 4
VMEM_LIMIT = 56 * 1024 * 1024


def _dot(a, b):
    return jnp.dot(a, b, preferred_element_type=F32)


def _dot_nt(a, b):
    return lax.dot_general(a, b, (((1,), (1,)), ((), ())), preferred_element_type=F32)


def _dot_tn(a, b):
    return lax.dot_general(a, b, (((0,), (0,)), ((), ())), preferred_element_type=F32)


def _rms(x, g):
    return x * lax.rsqrt(jnp.mean(x * x, axis=-1, keepdims=True) + EPS) * g


def _params(sem):
    return pltpu.CompilerParams(dimension_semantics=sem, vmem_limit_bytes=VMEM_LIMIT)


def _head(c):
    return slice(c * LANES, (c + 1) * LANES)


ROW_SUBLANES = 8


def _load_rows(ref, lead, n):
    return jnp.concatenate([ref[lead + (pl.ds(c, n, stride=ROW_SUBLANES), slice(None))]
                            for c in range(ROW_SUBLANES)], axis=1)


def _store_rows(ref, lead, val):
    n = val.shape[0]
    for c in range(ROW_SUBLANES):
        ref[lead + (pl.ds(c, n, stride=ROW_SUBLANES), slice(None))] = val[:, _head(c)]


def _inproj_kernel(x_ref, g1_ref, w_ref, wga_ref, wa2_ref, ba_ref, qkg_ref,
                   gla_ref, loga_ref, d0_ref, d1_ref, d2_ref, h_ref, ybuf_ref):
    j = pl.program_id(1)
    tm = x_ref.shape[0]

    @pl.when(j == 0)
    def _():
        hb = _rms(x_ref[...], g1_ref[...]).astype(BF16)
        h_ref[...] = hb
        ga = _dot(hb, wga_ref[...])
        z = _dot(ga.astype(BF16), wa2_ref[...]) + ba_ref[...]
        log_sig = jnp.minimum(z, 0.0) - jnp.log(1.0 + jnp.exp(-jnp.abs(z)))
        loga_ref[...] = log_sig * (1.0 / GLA_TAU)

    y = _dot(h_ref[...], w_ref[...])
    dil_refs = (d0_ref, d1_ref, d2_ref)

    for jj in range(N_COL_TILES):
        @pl.when(j == jj)
        def _(jj=jj):
            if jj < N_GLA_TILES:
                gla_ref[...] = y.astype(BF16)
                return
            gi, kind = divmod(jj - N_GLA_TILES, 3)
            d = DIL_PATTERNS[gi][1]
            if kind < 2:
                g = qkg_ref[0]
                slabs = [_rms(y[:, _head(c)], g[:, _head(c)]) for c in range(DIL_HEADS)]
            else:
                slabs = [y[:, _head(c)] for c in range(DIL_HEADS)]
            out = dil_refs[gi]
            if d == 1:
                for c in range(DIL_HEADS):
                    out[:, _head(c)] = slabs[c].astype(BF16)
                return
            for c in range(DIL_HEADS):
                ybuf_ref[c] = slabs[c]
            for r in range(d):
                for c in range(DIL_HEADS):
                    out[0, r, :, _head(c)] = ybuf_ref[c, pl.ds(r, tm // d, stride=d), :].astype(BF16)


def _inproj(x2d, g1, w_main, w_ga, w_a2, b_a, qk_gain, B, S):
    T = x2d.shape[0]
    tm = min(INPROJ_TM, S)
    tpb = S // tm

    def dil_spec(gi):
        d = DIL_PATTERNS[gi][1]
        first = N_GLA_TILES + 3 * gi
        col = lambda j: jnp.clip(j - first, 0, 2)
        if d == 1:
            return pl.BlockSpec((tm, HEAD_W), lambda i, j: (i, col(j)))
        return pl.BlockSpec((1, d, tm // d, HEAD_W), lambda i, j: (i // tpb, 0, i % tpb, col(j)))

    def dil_shape(gi):
        d = DIL_PATTERNS[gi][1]
        shape = (T, QKV_W) if d == 1 else (B, d, S // d, QKV_W)
        return jax.ShapeDtypeStruct(shape, BF16)

    return pl.pallas_call(
        _inproj_kernel,
        grid=(T // tm, N_COL_TILES),
        in_specs=[
            pl.BlockSpec((tm, D_MODEL), lambda i, j: (i, 0)),
            pl.BlockSpec((1, D_MODEL), lambda i, j: (0, 0)),
            pl.BlockSpec((D_MODEL, HEAD_W), lambda i, j: (0, j)),
            pl.BlockSpec((D_MODEL, LANES), lambda i, j: (0, 0)),
            pl.BlockSpec((LANES, HEAD_W), lambda i, j: (0, 0)),
            pl.BlockSpec((1, HEAD_W), lambda i, j: (0, 0)),
            pl.BlockSpec((1, 1, HEAD_W), lambda i, j: (j, 0, 0)),
        ],
        out_specs=[
            pl.BlockSpec((tm, HEAD_W), lambda i, j: (i, jnp.minimum(j, N_GLA_TILES - 1))),
            pl.BlockSpec((tm, HEAD_W), lambda i, j: (i, 0)),
            dil_spec(0), dil_spec(1), dil_spec(2),
        ],
        out_shape=[
            jax.ShapeDtypeStruct((T, N_GLA_TILES * HEAD_W), BF16),
            jax.ShapeDtypeStruct((T, HEAD_W), F32),
            dil_shape(0), dil_shape(1), dil_shape(2),
        ],
        scratch_shapes=[pltpu.VMEM((tm, D_MODEL), BF16), pltpu.VMEM((DIL_HEADS, tm, LANES), F32)],
        compiler_params=_params(("arbitrary", "arbitrary")),
        name="inproj",
    )(x2d, g1, w_main, w_ga, w_a2, b_a, qk_gain)


def _gla_kernel(q_ref, k_ref, v_ref, r_ref, la_ref, tri_ref, gn_ref, o_ref, st_ref):
    n = pl.program_id(1)
    L = GLA_BLOCK

    @pl.when(n == 0)
    def _():
        st_ref[...] = jnp.zeros_like(st_ref)

    la = la_ref[...]
    hi = la.astype(BF16)
    r1 = la - hi.astype(F32)
    mid = r1.astype(BF16)
    lo = (r1 - mid.astype(F32)).astype(BF16)
    tri = tri_ref[...]
    b_all = _dot(tri, hi) + _dot(tri, mid) + _dot(tri, lo)

    row = lax.broadcasted_iota(jnp.int32, (L, L), 0)
    col = lax.broadcasted_iota(jnp.int32, (L, L), 1)

    for h in range(GLA_HEADS):
        sl = _head(h)
        b = b_all[:, sl]
        q = q_ref[:, sl].astype(F32) * (GLA_DK ** -0.5)
        k = k_ref[:, sl].astype(F32)
        v = v_ref[:, sl]

        att = jnp.zeros((L, L), F32)
        seg = L
        while seg > GLA_SUBCHUNK:
            half = seg // 2
            beta = jnp.concatenate(
                [jnp.broadcast_to(b[s + half - 1:s + half, :], (seg, GLA_DK)) for s in range(0, L, seg)], axis=0)
            qs = (q * jnp.exp(jnp.minimum(b - beta, 0.0))).astype(BF16)
            ks = (k * jnp.exp(jnp.minimum(beta - b, 0.0))).astype(BF16)
            same = (row // seg) == (col // seg)
            mask = same & ((row % seg) >= half) & ((col % seg) < half)
            att = jnp.where(mask, _dot_nt(qs, ks), att)
            seg = half
        beta = jnp.concatenate(
            [jnp.zeros((GLA_SUBCHUNK, GLA_DK), F32)]
            + [jnp.broadcast_to(b[s - 1:s, :], (GLA_SUBCHUNK, GLA_DK)) for s in range(GLA_SUBCHUNK, L, GLA_SUBCHUNK)],
            axis=0)
        qs = (q * jnp.exp(b - beta)).astype(BF16)
        ks = (k * jnp.exp(beta - b)).astype(BF16)
        mask = ((row // GLA_SUBCHUNK) == (col // GLA_SUBCHUNK)) & (col <= row)
        att = jnp.where(mask, _dot_nt(qs, ks), att)

        state_t = st_ref[h]
        q0 = (q * jnp.exp(b)).astype(BF16)
        o = _dot(att.astype(BF16), v) + _dot_nt(q0, state_t.astype(BF16))

        b_last = b[L - 1:L, :]
        k_end = (k * jnp.exp(b_last - b)).astype(BF16)
        st_ref[h] = state_t * jnp.exp(b_last) + _dot_tn(v, k_end)

        rr = r_ref[:, sl].astype(F32)
        o_ref[:, sl] = (_rms(o, gn_ref[...]) * (rr * jax.nn.sigmoid(rr))).astype(BF16)


def _gla(proj, log_a, tri, gn, B, S):
    L = GLA_BLOCK
    nb = S // L
    return pl.pallas_call(
        _gla_kernel,
        grid=(B, nb),
        in_specs=[
            pl.BlockSpec((L, HEAD_W), lambda b, n: (b * nb + n, 0)),
            pl.BlockSpec((L, HEAD_W), lambda b, n: (b * nb + n, 1)),
            pl.BlockSpec((L, HEAD_W), lambda b, n: (b * nb + n, 2)),
            pl.BlockSpec((L, HEAD_W), lambda b, n: (b * nb + n, 3)),
            pl.BlockSpec((L, HEAD_W), lambda b, n: (b * nb + n, 0)),
            pl.BlockSpec((L, L), lambda b, n: (0, 0)),
            pl.BlockSpec((1, GLA_DK), lambda b, n: (0, 0)),
        ],
        out_specs=pl.BlockSpec((L, HEAD_W), lambda b, n: (b * nb + n, 0)),
        out_shape=jax.ShapeDtypeStruct((B * S, HEAD_W), BF16),
        scratch_shapes=[pltpu.VMEM((GLA_HEADS, GLA_DK, GLA_DK), F32)],
        compiler_params=_params(("arbitrary", "arbitrary")),
        name="gla",
    )(proj, proj, proj, proj, log_a, tri, gn)


def _dil_kernel(q_ref, kp_ref, kc_ref, vp_ref, vc_ref, bias_ref, o_ref, st_ref, *, nq):
    first = pl.program_id(2) == 0
    Q = DIL_BLOCK
    col = lax.broadcasted_iota(jnp.int32, (Q, 2 * Q), 1)
    lane = lax.broadcasted_iota(jnp.int32, (Q, LANES), 1)
    for blk in range(nq):
        rows = slice(blk * Q, (blk + 1) * Q)
        prev = slice((blk - 1) * Q, blk * Q)
        stats = jnp.zeros((Q, LANES), F32)
        for h in range(DIL_HEADS):
            sl = _head(h)
            if blk == 0:
                kprev, vprev = kp_ref[0, 0, :, sl], vp_ref[0, 0, :, sl]
            else:
                kprev, vprev = kc_ref[0, 0, prev, sl], vc_ref[0, 0, prev, sl]
            kk = jnp.concatenate([kprev, kc_ref[0, 0, rows, sl]], axis=0)
            vv = jnp.concatenate([vprev, vc_ref[0, 0, rows, sl]], axis=0)
            s = _dot_nt(q_ref[0, 0, rows, sl], kk) + bias_ref[h]
            if blk == 0:
                s = jnp.where(jnp.logical_and(first, col < Q), NEG, s)
            m = jnp.max(s, axis=-1, keepdims=True)
            p = jnp.exp(s - m)
            den = jnp.sum(p, axis=-1, keepdims=True)
            o_ref[0, 0, rows, sl] = (_dot(p.astype(BF16), vv) / den).astype(BF16)
            stats = jnp.where((lane // 16) == h, m, stats)
            stats = jnp.where((lane // 16) == DIL_HEADS + h, den, stats)
        st_ref[0, 0, rows, :] = stats


def _dilated(qkv, bias, gi):
    B, d, Lsub, _ = qkv.shape
    nq = min(DIL_QBLOCKS, Lsub // DIL_BLOCK)
    rows = nq * DIL_BLOCK

    def cur(c):
        return pl.BlockSpec((1, 1, rows, HEAD_W), lambda b, r, i: (b, r, i, c))

    def prev(c):
        return pl.BlockSpec((1, 1, DIL_BLOCK, HEAD_W), lambda b, r, i: (b, r, jnp.maximum(i * nq - 1, 0), c))

    return pl.pallas_call(
        functools.partial(_dil_kernel, nq=nq),
        grid=(B, d, Lsub // rows),
        in_specs=[cur(0), prev(1), cur(1), prev(2), cur(2),
                  pl.BlockSpec((DIL_HEADS, DIL_BLOCK, 2 * DIL_BLOCK), lambda b, r, i: (0, 0, 0))],
        out_specs=[
            pl.BlockSpec((1, 1, rows, HEAD_W), lambda b, r, i: (b, r, i, 0)),
            pl.BlockSpec((1, 1, rows, LANES), lambda b, r, i: (b, r, i, 0)),
        ],
        out_shape=[
            jax.ShapeDtypeStruct((B, d, Lsub, HEAD_W), BF16),
            jax.ShapeDtypeStruct((B, d, Lsub, LANES), F32),
        ],
        compiler_params=_params(("arbitrary", "arbitrary", "arbitrary")),
        name=f"dilated{gi}",
    )(qkv, qkv, qkv, qkv, qkv, bias)


def _alibi_bias(gi):
    window, d = DIL_PATTERNS[gi]
    Q = DIL_BLOCK
    dist = np.arange(Q)[:, None] + Q - np.arange(2 * Q)[None, :]
    valid = (dist >= 0) & (dist <= window // d)
    slopes = np.asarray(ALIBI_SLOPES, np.float32).reshape(DIL_GROUPS, DIL_HEADS)[gi]
    bias = -slopes[:, None, None] * (d * dist).astype(np.float32)
    return jnp.asarray(np.where(valid[None], bias, np.float32(NEG)), F32)


def _merge_kernel(x_ref, og_ref, o0_ref, o1_ref, o2_ref, s0_ref, s1_ref, s2_ref, g1_ref, wbg_ref, bbg_ref,
                  wpg_ref, wpa_ref, wout_ref, g2_ref, wrh_ref, wrl_ref, br_ref, tri_ref,
                  x2_ref, h2_ref, route_ref, cnt_ref, carry_ref, obuf_ref, sbuf_ref):
    @pl.when(pl.program_id(0) == 0)
    def _():
        carry_ref[...] = jnp.zeros_like(carry_ref)

    x = x_ref[...]
    tm = x.shape[0]
    hb = _rms(x, g1_ref[...]).astype(BF16)
    gates = jax.nn.sigmoid(_dot(hb, wbg_ref[...]) + bbg_ref[...])

    for slot, (o_ref, s_ref) in enumerate(((o1_ref, s1_ref), (o2_ref, s2_ref))):
        d = DIL_PATTERNS[slot + 1][1]
        for r in range(d):
            sbuf_ref[slot, pl.ds(r, tm // d, stride=d), :] = s_ref[0, r]
            for c in range(DIL_HEADS):
                obuf_ref[slot, c, pl.ds(r, tm // d, stride=d), :] = o_ref[0, r, :, _head(c)].astype(F32)

    stats = (s0_ref[...], sbuf_ref[0], sbuf_ref[1])
    dens = [pltpu.roll(s, 64, 1) for s in stats]
    m_all = jnp.maximum(jnp.maximum(stats[0], stats[1]), stats[2])
    wts = [d * jnp.exp(s - m_all) for s, d in zip(stats, dens)]
    inv = 1.0 / (wts[0] + wts[1] + wts[2])
    coef = [w * inv for w in wts]
    heads = []
    for h in range(DIL_HEADS):
        group_out = (o0_ref[:, _head(h)].astype(F32), obuf_ref[0, h], obuf_ref[1, h])
        acc = jnp.zeros((tm, DIL_DH), F32)
        for g in range(DIL_GROUPS):
            c = jnp.broadcast_to(coef[g][:, 16 * h:16 * h + 1], (tm, DIL_DH))
            acc = acc + c * group_out[g]
        heads.append(acc.astype(BF16))
    o_att = jnp.concatenate(heads, axis=1)

    y = gates[:, :D_MODEL] * _dot(og_ref[...], wpg_ref[...]) + gates[:, D_MODEL:] * _dot(o_att, wpa_ref[...])
    x2 = x + _dot(y.astype(BF16), wout_ref[...])
    x2_ref[...] = x2

    h2 = _rms(x2, g2_ref[...])
    _store_rows(h2_ref, (), h2)

    h2h = h2.astype(BF16)
    h2l = (h2 - h2h.astype(F32)).astype(BF16)
    wrh = wrh_ref[...]
    logit = _dot(h2h, wrh) + _dot(h2l, wrh) + _dot(h2h, wrl_ref[...]) + br_ref[...]

    lane = lax.broadcasted_iota(jnp.int32, (tm, LANES), 1).astype(F32)
    big = jnp.float32(1e9)
    gl = jnp.where(lane < N_GROUPS, logit, NEG)
    gmax = jnp.max(gl, axis=-1, keepdims=True)
    gsel = jnp.min(jnp.where(gl == gmax, lane, big), axis=-1, keepdims=True)
    g_p = 1.0 / jnp.sum(jnp.exp(gl - gmax), axis=-1, keepdims=True)
    lo = N_GROUPS + EXPERTS_PER_GROUP * gsel
    el = jnp.where((lane >= lo) & (lane < lo + EXPERTS_PER_GROUP), logit, NEG)
    v1 = jnp.max(el, axis=-1, keepdims=True)
    i1 = jnp.min(jnp.where(el == v1, lane, big), axis=-1, keepdims=True)
    el2 = jnp.where(lane == i1, NEG, el)
    v2 = jnp.max(el2, axis=-1, keepdims=True)
    i2 = jnp.min(jnp.where(el2 == v2, lane, big), axis=-1, keepdims=True)
    ex = jnp.exp(v2 - v1)
    w1 = g_p / (1.0 + ex)
    w2 = g_p * ex / (1.0 + ex)
    e1 = i1 - N_GROUPS
    e2 = i2 - N_GROUPS

    oh1 = lane == e1
    oh2 = lane == e2
    onehot = jnp.where(oh1 | oh2, 1.0, 0.0)
    prefix = _dot(tri_ref[...], onehot.astype(BF16)) + carry_ref[...]
    r1 = jnp.sum(jnp.where(oh1, prefix, 0.0), axis=-1, keepdims=True)
    r2 = jnp.sum(jnp.where(oh2, prefix, 0.0), axis=-1, keepdims=True)
    carry = carry_ref[...] + jnp.sum(onehot, axis=0, keepdims=True)
    carry_ref[...] = carry
    cnt_ref[...] = jnp.broadcast_to(carry, cnt_ref.shape)

    route = jnp.zeros((tm, LANES), F32)
    for idx, val in enumerate((e1, e2, w1, w2, r1, r2)):
        route = jnp.where(lane == idx, val, route)
    route_ref[...] = route


def _merge(x2d, o_gla, outs, stats, g1, wbg, bbg, wpg, wpa, wout, g2, wrh, wrl, br, tri, S):
    T = x2d.shape[0]
    tm = min(MERGE_TM, S)
    tpb = S // tm
    tok = lambda w: pl.BlockSpec((tm, w), lambda i: (i, 0))
    full = lambda a: pl.BlockSpec(a.shape, lambda i: (0,) * a.ndim)

    def res(gi, w):
        d = DIL_PATTERNS[gi][1]
        return pl.BlockSpec((1, d, tm // d, w), lambda i: (i // tpb, 0, i % tpb, 0))

    return pl.pallas_call(
        _merge_kernel,
        grid=(T // tm,),
        in_specs=[tok(D_MODEL), tok(HEAD_W), tok(HEAD_W), res(1, HEAD_W), res(2, HEAD_W),
                  tok(LANES), res(1, LANES), res(2, LANES),
                  full(g1), full(wbg), full(bbg), full(wpg), full(wpa), full(wout), full(g2),
                  full(wrh), full(wrl), full(br), full(tri)],
        out_specs=[tok(D_MODEL), pl.BlockSpec((tm * ROW_SUBLANES, LANES), lambda i: (i, 0)), tok(LANES),
                   pl.BlockSpec((8, LANES), lambda i: (0, 0))],
        out_shape=[
            jax.ShapeDtypeStruct((T, D_MODEL), F32),
            jax.ShapeDtypeStruct((T * ROW_SUBLANES, LANES), F32),
            jax.ShapeDtypeStruct((T, LANES), F32),
            jax.ShapeDtypeStruct((8, LANES), F32),
        ],
        scratch_shapes=[pltpu.VMEM((1, LANES), F32),
                        pltpu.VMEM((DIL_GROUPS - 1, DIL_HEADS, tm, LANES), F32),
                        pltpu.VMEM((DIL_GROUPS - 1, tm, LANES), F32)],
        compiler_params=_params(("arbitrary",)),
        name="merge",
    )(x2d, o_gla, *outs, *stats, g1, wbg, bbg, wpg, wpa, wout, g2, wrh, wrl, br, tri)


def _expert_kernel(tile_e_ref, nact_ref, src_cur_ref, src_next_ref, dst_prev_ref, h2_ref, wg_ref, wu_ref, wd_ref,
                   slots_ref, xbuf_ref, ybuf_ref, gsem, ssem):
    del tile_e_ref
    i = pl.program_id(0)
    nact = nact_ref[0]
    slot = lax.rem(i, 2)
    other = 1 - slot
    R = MOE_TM
    RS = R * ROW_SUBLANES

    def tile_rows(q):
        return pl.ds(q * ROW_SUBLANES, ROW_SUBLANES)

    def gather(src_ref, s):
        for q in range(R):
            row = pl.multiple_of(src_ref[0, 0, q], ROW_SUBLANES)
            pltpu.make_async_copy(h2_ref.at[pl.ds(row, ROW_SUBLANES), :], xbuf_ref.at[s, tile_rows(q), :],
                                  gsem.at[s]).start()

    def scatter(dst_ref, s):
        for q in range(R):
            row = pl.multiple_of(dst_ref[0, 0, q], ROW_SUBLANES)
            pltpu.make_async_copy(ybuf_ref.at[s, tile_rows(q), :], slots_ref.at[pl.ds(row, ROW_SUBLANES), :],
                                  ssem.at[s]).start()

    def wait_gather(s):
        pltpu.make_async_copy(h2_ref.at[pl.ds(0, RS), :], xbuf_ref.at[s], gsem.at[s]).wait()

    def wait_scatter(s):
        pltpu.make_async_copy(ybuf_ref.at[s], slots_ref.at[pl.ds(0, RS), :], ssem.at[s]).wait()

    @pl.when(i == 0)
    def _():
        ybuf_ref[...] = jnp.zeros_like(ybuf_ref)
        gather(src_cur_ref, 0)

    @pl.when(i <= nact)
    def _():
        wait_gather(slot)

        @pl.when(i >= 1)
        def _():
            wait_scatter(slot)

        xb = _load_rows(xbuf_ref, (slot,), R).astype(BF16)
        gather(src_next_ref, other)
        scatter(dst_prev_ref, other)
        g = _dot(xb, wg_ref[0])
        u = _dot(xb, wu_ref[0])
        a = (g * jax.nn.sigmoid(g) * u).astype(BF16)
        _store_rows(ybuf_ref, (slot,), _dot(a, wd_ref[0]))

    @pl.when(i == nact + 1)
    def _():
        wait_gather(slot)
        wait_scatter(slot)


def _experts(tile_e, nact, src, dst, h2_rows, wg, wu, wd, n_slot_rows):
    ntiles = tile_e.shape[0]
    R = MOE_TM

    def wsel(i, te, na):
        return (te[jnp.minimum(i, na[0] - 1)], 0, 0)

    def meta(shift):
        return pl.BlockSpec((1, 1, R), lambda i, te, na: (jnp.minimum(i + shift, ntiles + 1), 0, 0),
                            memory_space=pltpu.SMEM)

    return pl.pallas_call(
        _expert_kernel,
        grid_spec=pltpu.PrefetchScalarGridSpec(
            num_scalar_prefetch=2,
            grid=(ntiles + 2,),
            in_specs=[
                meta(0), meta(1), meta(0),
                pl.BlockSpec(memory_space=pl.ANY),
                pl.BlockSpec((1, D_MODEL, D_EXPERT), wsel),
                pl.BlockSpec((1, D_MODEL, D_EXPERT), wsel),
                pl.BlockSpec((1, D_EXPERT, D_MODEL), wsel),
            ],
            out_specs=pl.BlockSpec(memory_space=pl.ANY),
            scratch_shapes=[
                pltpu.VMEM((2, R * ROW_SUBLANES, LANES), F32),
                pltpu.VMEM((2, R * ROW_SUBLANES, LANES), F32),
                pltpu.SemaphoreType.DMA((2,)),
                pltpu.SemaphoreType.DMA((2,)),
            ],
        ),
        out_shape=jax.ShapeDtypeStruct((n_slot_rows * ROW_SUBLANES, LANES), F32),
        compiler_params=_params(("arbitrary",)),
        name="experts",
    )(tile_e, nact, src, src, dst, h2_rows, wg, wu, wd)


def _combine_kernel(x2_ref, route_ref, s0_ref, s1_ref, out_ref):
    tm = x2_ref.shape[0]
    route = route_ref[...]
    w1 = route[:, 2:3]
    w2 = route[:, 3:4]
    out_ref[...] = x2_ref[...] + w1 * _load_rows(s0_ref, (), tm) + w2 * _load_rows(s1_ref, (), tm)


def _combine(x2, route, slots):
    T = x2.shape[0]
    tm = min(ROW_TM, T)
    nt = T // tm
    return pl.pallas_call(
        _combine_kernel,
        grid=(nt,),
        in_specs=[
            pl.BlockSpec((tm, D_MODEL), lambda i: (i, 0)),
            pl.BlockSpec((tm, LANES), lambda i: (i, 0)),
            pl.BlockSpec((tm * ROW_SUBLANES, LANES), lambda i: (i, 0)),
            pl.BlockSpec((tm * ROW_SUBLANES, LANES), lambda i: (nt + i, 0)),
        ],
        out_specs=pl.BlockSpec((tm, D_MODEL), lambda i: (i, 0)),
        out_shape=jax.ShapeDtypeStruct((T, D_MODEL), F32),
        compiler_params=_params(("arbitrary",)),
        name="combine",
    )(x2, route, slots, slots)


def _layer(x2d, B, S, norm1_g, w_in, w_gla_a2, b_gla_a, gla_out_norm_g, dil_q_norm_g, dil_k_norm_g,
           w_proj_gla, w_proj_attn, w_branch_gate, b_branch_gate, w_out, norm2_g,
           w_router_group, b_router_group, w_router_expert, b_router_expert, w_gate, w_up, w_down):
    T = B * S
    n_gla = N_GLA_TILES * HEAD_W
    w_main = jnp.concatenate([w_in[:, :n_gla], w_in[:, n_gla + GLA_RANK:]], axis=1).astype(BF16)
    w_ga = jnp.pad(w_in[:, n_gla:n_gla + GLA_RANK], ((0, 0), (0, LANES - GLA_RANK))).astype(BF16)
    w_a2 = jnp.pad(w_gla_a2, ((0, LANES - GLA_RANK), (0, 0))).astype(BF16)
    gains = [jnp.ones((HEAD_W,), F32)] * N_GLA_TILES
    for gi in range(DIL_GROUPS):
        gains += [jnp.tile(dil_q_norm_g[gi], DIL_HEADS) * (DIL_DH ** -0.5),
                  jnp.tile(dil_k_norm_g[gi], DIL_HEADS), jnp.ones((HEAD_W,), F32)]
    qk_gain = jnp.stack(gains).reshape(N_COL_TILES, 1, HEAD_W)

    gla_in, log_a, qkv0, qkv1, qkv2 = _inproj(
        x2d, norm1_g.reshape(1, -1), w_main, w_ga, w_a2, b_gla_a.reshape(1, -1), qk_gain, B, S)

    tri_incl = jnp.asarray(np.tril(np.ones((GLA_BLOCK, GLA_BLOCK), np.float32)), BF16)
    o_gla = _gla(gla_in, log_a, tri_incl, gla_out_norm_g.reshape(1, -1), B, S)

    outs, stats = [], []
    for gi, qkv in enumerate((qkv0.reshape(B, 1, S, QKV_W), qkv1, qkv2)):
        o, st = _dilated(qkv, _alibi_bias(gi), gi)
        outs.append(o)
        stats.append(st)
    outs[0] = outs[0].reshape(T, HEAD_W)
    stats[0] = stats[0].reshape(T, LANES)

    w_r = jnp.pad(jnp.concatenate([w_router_group, w_router_expert], axis=1),
                  ((0, 0), (0, LANES - N_GROUPS - N_EXPERTS)))
    w_rh = w_r.astype(BF16)
    w_rl = (w_r - w_rh.astype(F32)).astype(BF16)
    b_r = jnp.pad(jnp.concatenate([b_router_group, b_router_expert]), (0, LANES - N_GROUPS - N_EXPERTS))
    tm = min(MERGE_TM, S)
    tri_strict = jnp.asarray(np.tril(np.ones((tm, tm), np.float32), -1), BF16)
    x2, h2, route, cnt = _merge(
        x2d, o_gla, outs, stats, norm1_g.reshape(1, -1), w_branch_gate.astype(BF16),
        b_branch_gate.reshape(1, -1), w_proj_gla.astype(BF16), w_proj_attn.astype(BF16), w_out.astype(BF16),
        norm2_g.reshape(1, -1), w_rh, w_rl, b_r.reshape(1, -1), tri_strict, S)

    counts = cnt[0, :N_EXPERTS].astype(jnp.int32)
    pcounts = (counts + MOE_TM - 1) // MOE_TM * MOE_TM
    pends = jnp.cumsum(pcounts)
    pstarts = pends - pcounts
    eid = route[:, 0:TOP_K].astype(jnp.int32)
    rank = route[:, 4:4 + TOP_K].astype(jnp.int32)
    onehot = eid[:, :, None] == jnp.arange(N_EXPERTS, dtype=jnp.int32)[None, None, :]
    dest = rank + jnp.sum(jnp.where(onehot, pstarts[None, None, :], 0), axis=-1)
    P = T * TOP_K + N_EXPERTS * MOE_TM
    ntiles = P // MOE_TM
    tile_start = jnp.arange(ntiles, dtype=jnp.int32) * MOE_TM
    tile_e = jnp.minimum(jnp.sum(pends[None, :] <= tile_start[:, None], axis=1), N_EXPERTS - 1).astype(jnp.int32)
    nact = (pends[-1:] // MOE_TM).astype(jnp.int32)

    inv = jnp.full((P,), -1, jnp.int32).at[dest.reshape(-1)].set(
        jnp.arange(T * TOP_K, dtype=jnp.int32), unique_indices=True)
    pad_tile = jnp.full((MOE_TM,), -1, jnp.int32)
    inv_src = jnp.concatenate([inv, pad_tile, pad_tile])
    inv_dst = jnp.concatenate([pad_tile, inv, pad_tile])
    src = jnp.where(inv_src >= 0, inv_src // TOP_K, 0) * ROW_SUBLANES
    spare = T * TOP_K + jnp.tile(jnp.arange(MOE_TM, dtype=jnp.int32), ntiles + 2)
    dst = jnp.where(inv_dst >= 0, (inv_dst % TOP_K) * T + inv_dst // TOP_K, spare) * ROW_SUBLANES
    slots = _experts(tile_e, nact, src.reshape(ntiles + 2, 1, MOE_TM), dst.reshape(ntiles + 2, 1, MOE_TM), h2,
                     w_gate.astype(BF16), w_up.astype(BF16), w_down.astype(BF16), T * TOP_K + MOE_TM)
    return _combine(x2, route, slots)


def kernel(x, norm1_g, w_in, w_gla_a2, b_gla_a, gla_out_norm_g, dil_q_norm_g, dil_k_norm_g, w_proj_gla,
           w_proj_attn, w_branch_gate, b_branch_gate, w_out, norm2_g, w_router_group, b_router_group,
           w_router_expert, b_router_expert, w_gate, w_up, w_down):
    B, S, D = x.shape
    assert D == D_MODEL and S % (DIL_BLOCK * DIL_PATTERNS[-1][1]) == 0
    x2d = x.reshape(B * S, D)
    params = (norm1_g, w_in, w_gla_a2, b_gla_a, gla_out_norm_g, dil_q_norm_g, dil_k_norm_g, w_proj_gla,
              w_proj_attn, w_branch_gate, b_branch_gate, w_out, norm2_g, w_router_group, b_router_group,
              w_router_expert, b_router_expert, w_gate, w_up, w_down)
    for layer in range(norm1_g.shape[0]):
        x2d = _layer(x2d, B, S, *(p[layer] for p in params))
    return x2d.reshape(B, S, D)
```

```python
import functools

import jax
import jax.numpy as jnp
import numpy as np
from jax import lax
from jax.experimental import pallas as pl
from jax.experimental.pallas import tpu as pltpu

F32 = jnp.float32
BF16 = jnp.bfloat16

D_MODEL = 1024
EPS = 1e-6
GLA_HEADS = 4
GLA_DK = 128
GLA_RANK = 16
GLA_TAU = 16.0
GLA_SUBCHUNK = 16
GLA_BLOCK = 128
DIL_PATTERNS = ((128, 1), (512, 4), (2048, 16))
DIL_GROUPS = 3
DIL_HEADS = 4
DIL_DH = 128
DIL_BLOCK = 128
ALIBI_SLOPES = tuple(2.0 ** (-8.0 * (i + 1) / (DIL_GROUPS * DIL_HEADS)) for i in range(DIL_GROUPS * DIL_HEADS))
N_GROUPS = 4
EXPERTS_PER_GROUP = 8
N_EXPERTS = N_GROUPS * EXPERTS_PER_GROUP
TOP_K = 2
D_EXPERT = 512

HEAD_W = 512
N_GLA_TILES = 4
N_COL_TILES = N_GLA_TILES + 3 * DIL_GROUPS
QKV_W = 3 * HEAD_W
LANES = 128
NEG = -1e30

INPROJ_TM = 512
MERGE_TM = 512
MOE_TM = 512
ROW_TM = 512
ROW_UNROLL = 8
DIL_QBLOCKS = 4
VMEM_LIMIT = 56 * 1024 * 1024


def _dot(a, b):
    return jnp.dot(a, b, preferred_element_type=F32)


def _dot_nt(a, b):
    return lax.dot_general(a, b, (((1,), (1,)), ((), ())), preferred_element_type=F32)


def _dot_tn(a, b):
    return lax.dot_general(a, b, (((0,), (0,)), ((), ())), preferred_element_type=F32)


def _rms(x, g):
    return x * lax.rsqrt(jnp.mean(x * x, axis=-1, keepdims=True) + EPS) * g


def _params(sem):
    return pltpu.CompilerParams(dimension_semantics=sem, vmem_limit_bytes=VMEM_LIMIT)


def _head(c):
    return slice(c * LANES, (c + 1) * LANES)


ROW_SUBLANES = 8


def _load_rows(ref, lead, n):
    return jnp.concatenate([ref[lead + (pl.ds(c, n, stride=ROW_SUBLANES), slice(None))]
                            for c in range(ROW_SUBLANES)], axis=1)


def _store_rows(ref, lead, val):
    n = val.shape[0]
    for c in range(ROW_SUBLANES):
        ref[lead + (pl.ds(c, n, stride=ROW_SUBLANES), slice(None))] = val[:, _head(c)]


def _inproj_kernel(x_ref, g1_ref, w_ref, wga_ref, wa2_ref, ba_ref, qkg_ref,
                   gla_ref, loga_ref, d0_ref, d1_ref, d2_ref, h_ref, ybuf_ref):
    tm = x_ref.shape[0]
    hb = _rms(x_ref[...], g1_ref[...]).astype(BF16)
    h_ref[...] = hb
    ga = _dot(hb, wga_ref[...])
    z = _dot(ga.astype(BF16), wa2_ref[...]) + ba_ref[...]
    log_sig = jnp.minimum(z, 0.0) - jnp.log(1.0 + jnp.exp(-jnp.abs(z)))
    loga_ref[...] = log_sig * (1.0 / GLA_TAU)

    dil_refs = (d0_ref, d1_ref, d2_ref)
    n_strided = 0
    for jj in range(N_COL_TILES):
        cols = slice(jj * HEAD_W, (jj + 1) * HEAD_W)
        y = _dot(h_ref[...], w_ref[:, cols])
        if jj < N_GLA_TILES:
            gla_ref[:, cols] = y.astype(BF16)
            continue
        gi, kind = divmod(jj - N_GLA_TILES, 3)
        d = DIL_PATTERNS[gi][1]
        ocols = [slice(kind * HEAD_W + c * LANES, kind * HEAD_W + (c + 1) * LANES) for c in range(DIL_HEADS)]
        if kind < 2:
            g = qkg_ref[jj]
            slabs = [_rms(y[:, _head(c)], g[:, _head(c)]) for c in range(DIL_HEADS)]
        else:
            slabs = [y[:, _head(c)] for c in range(DIL_HEADS)]
        out = dil_refs[gi]
        if d == 1:
            for c in range(DIL_HEADS):
                out[:, ocols[c]] = slabs[c].astype(BF16)
            continue
        buf = n_strided % ybuf_ref.shape[0]
        n_strided += 1
        for c in range(DIL_HEADS):
            ybuf_ref[buf, c] = slabs[c]
        for r in range(d):
            for c in range(DIL_HEADS):
                out[0, r, :, ocols[c]] = ybuf_ref[buf, c, pl.ds(r, tm // d, stride=d), :].astype(BF16)


def _resident(shape):
    return pl.BlockSpec(shape, lambda i: (0,) * len(shape), pipeline_mode=pl.Buffered(1))


def _inproj(x2d, g1, w_main, w_ga, w_a2, b_a, qk_gain, B, S):
    T = x2d.shape[0]
    tm = min(INPROJ_TM, S)
    tpb = S // tm

    def dil_spec(gi):
        d = DIL_PATTERNS[gi][1]
        if d == 1:
            return pl.BlockSpec((tm, QKV_W), lambda i: (i, 0))
        return pl.BlockSpec((1, d, tm // d, QKV_W), lambda i: (i // tpb, 0, i % tpb, 0))

    def dil_shape(gi):
        d = DIL_PATTERNS[gi][1]
        shape = (T, QKV_W) if d == 1 else (B, d, S // d, QKV_W)
        return jax.ShapeDtypeStruct(shape, BF16)

    return pl.pallas_call(
        _inproj_kernel,
        grid=(T // tm,),
        in_specs=[
            pl.BlockSpec((tm, D_MODEL), lambda i: (i, 0)),
            _resident(g1.shape), _resident(w_main.shape), _resident(w_ga.shape), _resident(w_a2.shape),
            _resident(b_a.shape), _resident(qk_gain.shape),
        ],
        out_specs=[
            pl.BlockSpec((tm, N_GLA_TILES * HEAD_W), lambda i: (i, 0)),
            pl.BlockSpec((tm, HEAD_W), lambda i: (i, 0)),
            dil_spec(0), dil_spec(1), dil_spec(2),
        ],
        out_shape=[
            jax.ShapeDtypeStruct((T, N_GLA_TILES * HEAD_W), BF16),
            jax.ShapeDtypeStruct((T, HEAD_W), F32),
            dil_shape(0), dil_shape(1), dil_shape(2),
        ],
        scratch_shapes=[pltpu.VMEM((tm, D_MODEL), BF16), pltpu.VMEM((2, DIL_HEADS, tm, LANES), F32)],
        compiler_params=_params(("arbitrary",)),
        name="inproj",
    )(x2d, g1, w_main, w_ga, w_a2, b_a, qk_gain)


def _gla_kernel(q_ref, k_ref, v_ref, r_ref, la_ref, tri_ref, gn_ref, o_ref, st_ref):
    n = pl.program_id(1)
    L = GLA_BLOCK

    @pl.when(n == 0)
    def _():
        st_ref[...] = jnp.zeros_like(st_ref)

    la = la_ref[...]
    hi = la.astype(BF16)
    r1 = la - hi.astype(F32)
    mid = r1.astype(BF16)
    lo = (r1 - mid.astype(F32)).astype(BF16)
    tri = tri_ref[...]
    b_all = _dot(tri, hi) + _dot(tri, mid) + _dot(tri, lo)

    row = lax.broadcasted_iota(jnp.int32, (L, L), 0)
    col = lax.broadcasted_iota(jnp.int32, (L, L), 1)

    for h in range(GLA_HEADS):
        sl = _head(h)
        b = b_all[:, sl]
        q = q_ref[:, sl].astype(F32) * (GLA_DK ** -0.5)
        k = k_ref[:, sl].astype(F32)
        v = v_ref[:, sl]

        att = jnp.zeros((L, L), F32)
        seg = L
        while seg > GLA_SUBCHUNK:
            half = seg // 2
            beta = jnp.concatenate(
                [jnp.broadcast_to(b[s + half - 1:s + half, :], (seg, GLA_DK)) for s in range(0, L, seg)], axis=0)
            qs = (q * jnp.exp(jnp.minimum(b - beta, 0.0))).astype(BF16)
            ks = (k * jnp.exp(jnp.minimum(beta - b, 0.0))).astype(BF16)
            same = (row // seg) == (col // seg)
            mask = same & ((row % seg) >= half) & ((col % seg) < half)
            att = jnp.where(mask, _dot_nt(qs, ks), att)
            seg = half
        beta = jnp.concatenate(
            [jnp.zeros((GLA_SUBCHUNK, GLA_DK), F32)]
            + [jnp.broadcast_to(b[s - 1:s, :], (GLA_SUBCHUNK, GLA_DK)) for s in range(GLA_SUBCHUNK, L, GLA_SUBCHUNK)],
            axis=0)
        qs = (q * jnp.exp(b - beta)).astype(BF16)
        ks = (k * jnp.exp(beta - b)).astype(BF16)
        mask = ((row // GLA_SUBCHUNK) == (col // GLA_SUBCHUNK)) & (col <= row)
        att = jnp.where(mask, _dot_nt(qs, ks), att)

        state_t = st_ref[h]
        q0 = (q * jnp.exp(b)).astype(BF16)
        o = _dot(att.astype(BF16), v) + _dot_nt(q0, state_t.astype(BF16))

        b_last = b[L - 1:L, :]
        k_end = (k * jnp.exp(b_last - b)).astype(BF16)
        st_ref[h] = state_t * jnp.exp(b_last) + _dot_tn(v, k_end)

        rr = r_ref[:, sl].astype(F32)
        o_ref[:, sl] = (_rms(o, gn_ref[...]) * (rr * jax.nn.sigmoid(rr))).astype(BF16)


def _gla(proj, log_a, tri, gn, B, S):
    L = GLA_BLOCK
    nb = S // L
    return pl.pallas_call(
        _gla_kernel,
        grid=(B, nb),
        in_specs=[
            pl.BlockSpec((L, HEAD_W), lambda b, n: (b * nb + n, 0)),
            pl.BlockSpec((L, HEAD_W), lambda b, n: (b * nb + n, 1)),
            pl.BlockSpec((L, HEAD_W), lambda b, n: (b * nb + n, 2)),
            pl.BlockSpec((L, HEAD_W), lambda b, n: (b * nb + n, 3)),
            pl.BlockSpec((L, HEAD_W), lambda b, n: (b * nb + n, 0)),
            pl.BlockSpec((L, L), lambda b, n: (0, 0)),
            pl.BlockSpec((1, GLA_DK), lambda b, n: (0, 0)),
        ],
        out_specs=pl.BlockSpec((L, HEAD_W), lambda b, n: (b * nb + n, 0)),
        out_shape=jax.ShapeDtypeStruct((B * S, HEAD_W), BF16),
        scratch_shapes=[pltpu.VMEM((GLA_HEADS, GLA_DK, GLA_DK), F32)],
        compiler_params=_params(("arbitrary", "arbitrary")),
        name="gla",
    )(proj, proj, proj, proj, log_a, tri, gn)


def _dil_kernel(q_ref, kp_ref, kc_ref, vp_ref, vc_ref, bias_ref, o_ref, st_ref, *, nq):
    first = pl.program_id(2) == 0
    Q = DIL_BLOCK
    col = lax.broadcasted_iota(jnp.int32, (Q, 2 * Q), 1)
    lane = lax.broadcasted_iota(jnp.int32, (Q, LANES), 1)
    for blk in range(nq):
        rows = slice(blk * Q, (blk + 1) * Q)
        prev = slice((blk - 1) * Q, blk * Q)
        stats = jnp.zeros((Q, LANES), F32)
        for h in range(DIL_HEADS):
            sl = _head(h)
            if blk == 0:
                kprev, vprev = kp_ref[0, 0, :, sl], vp_ref[0, 0, :, sl]
            else:
                kprev, vprev = kc_ref[0, 0, prev, sl], vc_ref[0, 0, prev, sl]
            kk = jnp.concatenate([kprev, kc_ref[0, 0, rows, sl]], axis=0)
            vv = jnp.concatenate([vprev, vc_ref[0, 0, rows, sl]], axis=0)
            s = _dot_nt(q_ref[0, 0, rows, sl], kk) + bias_ref[h]
            if blk == 0:
                s = jnp.where(jnp.logical_and(first, col < Q), NEG, s)
            m = jnp.max(s, axis=-1, keepdims=True)
            p = jnp.exp(s - m)
            den = jnp.sum(p, axis=-1, keepdims=True)
            o_ref[0, 0, rows, sl] = (_dot(p.astype(BF16), vv) / den).astype(BF16)
            stats = jnp.where((lane // 16) == h, m, stats)
            stats = jnp.where((lane // 16) == DIL_HEADS + h, den, stats)
        st_ref[0, 0, rows, :] = stats


def _dilated(qkv, bias, gi):
    B, d, Lsub, _ = qkv.shape
    nq = min(DIL_QBLOCKS, Lsub // DIL_BLOCK)
    rows = nq * DIL_BLOCK

    def cur(c):
        return pl.BlockSpec((1, 1, rows, HEAD_W), lambda b, r, i: (b, r, i, c))

    def prev(c):
        return pl.BlockSpec((1, 1, DIL_BLOCK, HEAD_W), lambda b, r, i: (b, r, jnp.maximum(i * nq - 1, 0), c))

    return pl.pallas_call(
        functools.partial(_dil_kernel, nq=nq),
        grid=(B, d, Lsub // rows),
        in_specs=[cur(0), prev(1), cur(1), prev(2), cur(2),
                  pl.BlockSpec((DIL_HEADS, DIL_BLOCK, 2 * DIL_BLOCK), lambda b, r, i: (0, 0, 0))],
        out_specs=[
            pl.BlockSpec((1, 1, rows, HEAD_W), lambda b, r, i: (b, r, i, 0)),
            pl.BlockSpec((1, 1, rows, LANES), lambda b, r, i: (b, r, i, 0)),
        ],
        out_shape=[
            jax.ShapeDtypeStruct((B, d, Lsub, HEAD_W), BF16),
            jax.ShapeDtypeStruct((B, d, Lsub, LANES), F32),
        ],
        compiler_params=_params(("arbitrary", "arbitrary", "arbitrary")),
        name=f"dilated{gi}",
    )(qkv, qkv, qkv, qkv, qkv, bias)


def _alibi_bias(gi):
    window, d = DIL_PATTERNS[gi]
    Q = DIL_BLOCK
    dist = np.arange(Q)[:, None] + Q - np.arange(2 * Q)[None, :]
    valid = (dist >= 0) & (dist <= window // d)
    slopes = np.asarray(ALIBI_SLOPES, np.float32).reshape(DIL_GROUPS, DIL_HEADS)[gi]
    bias = -slopes[:, None, None] * (d * dist).astype(np.float32)
    return jnp.asarray(np.where(valid[None], bias, np.float32(NEG)), F32)


def _merge_kernel(x_ref, og_ref, o0_ref, o1_ref, o2_ref, s0_ref, s1_ref, s2_ref, g1_ref, wbg_ref, bbg_ref,
                  wpg_ref, wpa_ref, wout_ref, g2_ref, wrh_ref, wrl_ref, br_ref, tri_ref,
                  x2_ref, h2_ref, route_ref, cnt_ref, carry_ref, obuf_ref, sbuf_ref):
    @pl.when(pl.program_id(0) == 0)
    def _():
        carry_ref[...] = jnp.zeros_like(carry_ref)

    x = x_ref[...]
    tm = x.shape[0]
    hb = _rms(x, g1_ref[...]).astype(BF16)
    gates = jax.nn.sigmoid(_dot(hb, wbg_ref[...]) + bbg_ref[...])

    for slot, (o_ref, s_ref) in enumerate(((o1_ref, s1_ref), (o2_ref, s2_ref))):
        d = DIL_PATTERNS[slot + 1][1]
        for r in range(d):
            sbuf_ref[slot, pl.ds(r, tm // d, stride=d), :] = s_ref[0, r]
            for c in range(DIL_HEADS):
                obuf_ref[slot, c, pl.ds(r, tm // d, stride=d), :] = o_ref[0, r, :, _head(c)].astype(F32)

    stats = (s0_ref[...], sbuf_ref[0], sbuf_ref[1])
    dens = [pltpu.roll(s, 64, 1) for s in stats]
    m_all = jnp.maximum(jnp.maximum(stats[0], stats[1]), stats[2])
    wts = [d * jnp.exp(s - m_all) for s, d in zip(stats, dens)]
    inv = 1.0 / (wts[0] + wts[1] + wts[2])
    coef = [w * inv for w in wts]
    heads = []
    for h in range(DIL_HEADS):
        group_out = (o0_ref[:, _head(h)].astype(F32), obuf_ref[0, h], obuf_ref[1, h])
        acc = jnp.zeros((tm, DIL_DH), F32)
        for g in range(DIL_GROUPS):
            c = jnp.broadcast_to(coef[g][:, 16 * h:16 * h + 1], (tm, DIL_DH))
            acc = acc + c * group_out[g]
        heads.append(acc.astype(BF16))
    o_att = jnp.concatenate(heads, axis=1)

    y = gates[:, :D_MODEL] * _dot(og_ref[...], wpg_ref[...]) + gates[:, D_MODEL:] * _dot(o_att, wpa_ref[...])
    x2 = x + _dot(y.astype(BF16), wout_ref[...])
    x2_ref[...] = x2

    h2 = _rms(x2, g2_ref[...])
    _store_rows(h2_ref, (), h2)

    h2h = h2.astype(BF16)
    h2l = (h2 - h2h.astype(F32)).astype(BF16)
    wrh = wrh_ref[...]
    logit = _dot(h2h, wrh) + _dot(h2l, wrh) + _dot(h2h, wrl_ref[...]) + br_ref[...]

    lane = lax.broadcasted_iota(jnp.int32, (tm, LANES), 1).astype(F32)
    big = jnp.float32(1e9)
    gl = jnp.where(lane < N_GROUPS, logit, NEG)
    gmax = jnp.max(gl, axis=-1, keepdims=True)
    gsel = jnp.min(jnp.where(gl == gmax, lane, big), axis=-1, keepdims=True)
    g_p = 1.0 / jnp.sum(jnp.exp(gl - gmax), axis=-1, keepdims=True)
    lo = N_GROUPS + EXPERTS_PER_GROUP * gsel
    el = jnp.where((lane >= lo) & (lane < lo + EXPERTS_PER_GROUP), logit, NEG)
    v1 = jnp.max(el, axis=-1, keepdims=True)
    i1 = jnp.min(jnp.where(el == v1, lane, big), axis=-1, keepdims=True)
    el2 = jnp.where(lane == i1, NEG, el)
    v2 = jnp.max(el2, axis=-1, keepdims=True)
    i2 = jnp.min(jnp.where(el2 == v2, lane, big), axis=-1, keepdims=True)
    ex = jnp.exp(v2 - v1)
    w1 = g_p / (1.0 + ex)
    w2 = g_p * ex / (1.0 + ex)
    e1 = i1 - N_GROUPS
    e2 = i2 - N_GROUPS

    oh1 = lane == e1
    oh2 = lane == e2
    onehot = jnp.where(oh1 | oh2, 1.0, 0.0)
    prefix = _dot(tri_ref[...], onehot.astype(BF16)) + carry_ref[...]
    r1 = jnp.sum(jnp.where(oh1, prefix, 0.0), axis=-1, keepdims=True)
    r2 = jnp.sum(jnp.where(oh2, prefix, 0.0), axis=-1, keepdims=True)
    carry = carry_ref[...] + jnp.sum(onehot, axis=0, keepdims=True)
    carry_ref[...] = carry
    cnt_ref[...] = jnp.broadcast_to(carry, cnt_ref.shape)

    route = jnp.zeros((tm, LANES), F32)
    for idx, val in enumerate((e1, e2, w1, w2, r1, r2)):
        route = jnp.where(lane == idx, val, route)
    route_ref[...] = route


def _merge(x2d, o_gla, outs, stats, g1, wbg, bbg, wpg, wpa, wout, g2, wrh, wrl, br, tri, S):
    T = x2d.shape[0]
    tm = min(MERGE_TM, S)
    tpb = S // tm
    tok = lambda w: pl.BlockSpec((tm, w), lambda i: (i, 0))
    full = lambda a: pl.BlockSpec(a.shape, lambda i: (0,) * a.ndim)

    def res(gi, w):
        d = DIL_PATTERNS[gi][1]
        return pl.BlockSpec((1, d, tm // d, w), lambda i: (i // tpb, 0, i % tpb, 0))

    return pl.pallas_call(
        _merge_kernel,
        grid=(T // tm,),
        in_specs=[tok(D_MODEL), tok(HEAD_W), tok(HEAD_W), res(1, HEAD_W), res(2, HEAD_W),
                  tok(LANES), res(1, LANES), res(2, LANES),
                  full(g1), full(wbg), full(bbg), full(wpg), full(wpa), full(wout), full(g2),
                  full(wrh), full(wrl), full(br), full(tri)],
        out_specs=[tok(D_MODEL), pl.BlockSpec((tm * ROW_SUBLANES, LANES), lambda i: (i, 0)), tok(LANES),
                   pl.BlockSpec((8, LANES), lambda i: (0, 0))],
        out_shape=[
            jax.ShapeDtypeStruct((T, D_MODEL), F32),
            jax.ShapeDtypeStruct((T * ROW_SUBLANES, LANES), F32),
            jax.ShapeDtypeStruct((T, LANES), F32),
            jax.ShapeDtypeStruct((8, LANES), F32),
        ],
        scratch_shapes=[pltpu.VMEM((1, LANES), F32),
                        pltpu.VMEM((DIL_GROUPS - 1, DIL_HEADS, tm, LANES), F32),
                        pltpu.VMEM((DIL_GROUPS - 1, tm, LANES), F32)],
        compiler_params=_params(("arbitrary",)),
        name="merge",
    )(x2d, o_gla, *outs, *stats, g1, wbg, bbg, wpg, wpa, wout, g2, wrh, wrl, br, tri)


def _expert_kernel(tile_e_ref, nact_ref, src_cur_ref, src_next_ref, dst_prev_ref, h2_ref, wg_ref, wu_ref, wd_ref,
                   slots_ref, xbuf_ref, ybuf_ref, gsem, ssem):
    del tile_e_ref
    i = pl.program_id(0)
    nact = nact_ref[0]
    slot = lax.rem(i, 2)
    other = 1 - slot
    R = MOE_TM
    RS = R * ROW_SUBLANES

    def tile_rows(q):
        return pl.ds(q * ROW_SUBLANES, ROW_SUBLANES)

    def gather(src_ref, s):
        for q in range(R):
            row = pl.multiple_of(src_ref[0, 0, q], ROW_SUBLANES)
            pltpu.make_async_copy(h2_ref.at[pl.ds(row, ROW_SUBLANES), :], xbuf_ref.at[s, tile_rows(q), :],
                                  gsem.at[s]).start()

    def scatter(dst_ref, s):
        for q in range(R):
            row = pl.multiple_of(dst_ref[0, 0, q], ROW_SUBLANES)
            pltpu.make_async_copy(ybuf_ref.at[s, tile_rows(q), :], slots_ref.at[pl.ds(row, ROW_SUBLANES), :],
                                  ssem.at[s]).start(priority=1)

    def wait_gather(s):
        pltpu.make_async_copy(h2_ref.at[pl.ds(0, RS), :], xbuf_ref.at[s], gsem.at[s]).wait()

    def wait_scatter(s):
        pltpu.make_async_copy(ybuf_ref.at[s], slots_ref.at[pl.ds(0, RS), :], ssem.at[s]).wait()

    @pl.when(i == 0)
    def _():
        ybuf_ref[...] = jnp.zeros_like(ybuf_ref)
        gather(src_cur_ref, 0)

    @pl.when(i <= nact)
    def _():
        wait_gather(slot)

        @pl.when(i >= 1)
        def _():
            wait_scatter(slot)

        xb = _load_rows(xbuf_ref, (slot,), R).astype(BF16)
        gather(src_next_ref, other)
        scatter(dst_prev_ref, other)
        g = _dot(xb, wg_ref[0])
        u = _dot(xb, wu_ref[0])
        a = (g * jax.nn.sigmoid(g) * u).astype(BF16)
        _store_rows(ybuf_ref, (slot,), _dot(a, wd_ref[0]))

    @pl.when(i == nact + 1)
    def _():
        wait_gather(slot)
        wait_scatter(slot)


def _experts(tile_e, nact, src, dst, h2_rows, wg, wu, wd, n_slot_rows):
    ntiles = tile_e.shape[0]
    R = MOE_TM

    def wsel(i, te, na):
        return (te[jnp.minimum(i, na[0] - 1)], 0, 0)

    def meta(shift):
        return pl.BlockSpec((1, 1, R), lambda i, te, na: (jnp.minimum(i + shift, ntiles + 1), 0, 0),
                            memory_space=pltpu.SMEM)

    return pl.pallas_call(
        _expert_kernel,
        grid_spec=pltpu.PrefetchScalarGridSpec(
            num_scalar_prefetch=2,
            grid=(ntiles + 2,),
            in_specs=[
                meta(0), meta(1), meta(0),
                pl.BlockSpec(memory_space=pl.ANY),
                pl.BlockSpec((1, D_MODEL, D_EXPERT), wsel),
                pl.BlockSpec((1, D_MODEL, D_EXPERT), wsel),
                pl.BlockSpec((1, D_EXPERT, D_MODEL), wsel),
            ],
            out_specs=pl.BlockSpec(memory_space=pl.ANY),
            scratch_shapes=[
                pltpu.VMEM((2, R * ROW_SUBLANES, LANES), F32),
                pltpu.VMEM((2, R * ROW_SUBLANES, LANES), F32),
                pltpu.SemaphoreType.DMA((2,)),
                pltpu.SemaphoreType.DMA((2,)),
            ],
        ),
        out_shape=jax.ShapeDtypeStruct((n_slot_rows * ROW_SUBLANES, LANES), F32),
        compiler_params=_params(("arbitrary",)),
        name="experts",
    )(tile_e, nact, src, src, dst, h2_rows, wg, wu, wd)


def _combine_kernel(x2_ref, route_ref, s0_ref, s1_ref, out_ref):
    tm = x2_ref.shape[0]
    route = route_ref[...]
    w1 = route[:, 2:3]
    w2 = route[:, 3:4]
    out_ref[...] = x2_ref[...] + w1 * _load_rows(s0_ref, (), tm) + w2 * _load_rows(s1_ref, (), tm)


def _combine(x2, route, slots):
    T = x2.shape[0]
    tm = min(ROW_TM, T)
    nt = T // tm
    return pl.pallas_call(
        _combine_kernel,
        grid=(nt,),
        in_specs=[
            pl.BlockSpec((tm, D_MODEL), lambda i: (i, 0)),
            pl.BlockSpec((tm, LANES), lambda i: (i, 0)),
            pl.BlockSpec((tm * ROW_SUBLANES, LANES), lambda i: (i, 0)),
            pl.BlockSpec((tm * ROW_SUBLANES, LANES), lambda i: (nt + i, 0)),
        ],
        out_specs=pl.BlockSpec((tm, D_MODEL), lambda i: (i, 0)),
        out_shape=jax.ShapeDtypeStruct((T, D_MODEL), F32),
        compiler_params=_params(("arbitrary",)),
        name="combine",
    )(x2, route, slots, slots)


def _layer(x2d, B, S, norm1_g, w_in, w_gla_a2, b_gla_a, gla_out_norm_g, dil_q_norm_g, dil_k_norm_g,
           w_proj_gla, w_proj_attn, w_branch_gate, b_branch_gate, w_out, norm2_g,
           w_router_group, b_router_group, w_router_expert, b_router_expert, w_gate, w_up, w_down):
    T = B * S
    n_gla = N_GLA_TILES * HEAD_W
    w_main = jnp.concatenate([w_in[:, :n_gla], w_in[:, n_gla + GLA_RANK:]], axis=1).astype(BF16)
    w_ga = jnp.pad(w_in[:, n_gla:n_gla + GLA_RANK], ((0, 0), (0, LANES - GLA_RANK))).astype(BF16)
    w_a2 = jnp.pad(w_gla_a2, ((0, LANES - GLA_RANK), (0, 0))).astype(BF16)
    gains = [jnp.ones((HEAD_W,), F32)] * N_GLA_TILES
    for gi in range(DIL_GROUPS):
        gains += [jnp.tile(dil_q_norm_g[gi], DIL_HEADS) * (DIL_DH ** -0.5),
                  jnp.tile(dil_k_norm_g[gi], DIL_HEADS), jnp.ones((HEAD_W,), F32)]
    qk_gain = jnp.stack(gains).reshape(N_COL_TILES, 1, HEAD_W)

    gla_in, log_a, qkv0, qkv1, qkv2 = _inproj(
        x2d, norm1_g.reshape(1, -1), w_main, w_ga, w_a2, b_gla_a.reshape(1, -1), qk_gain, B, S)

    tri_incl = jnp.asarray(np.tril(np.ones((GLA_BLOCK, GLA_BLOCK), np.float32)), BF16)
    o_gla = _gla(gla_in, log_a, tri_incl, gla_out_norm_g.reshape(1, -1), B, S)

    outs, stats = [], []
    for gi, qkv in enumerate((qkv0.reshape(B, 1, S, QKV_W), qkv1, qkv2)):
        o, st = _dilated(qkv, _alibi_bias(gi), gi)
        outs.append(o)
        stats.append(st)
    outs[0] = outs[0].reshape(T, HEAD_W)
    stats[0] = stats[0].reshape(T, LANES)

    w_r = jnp.pad(jnp.concatenate([w_router_group, w_router_expert], axis=1),
                  ((0, 0), (0, LANES - N_GROUPS - N_EXPERTS)))
    w_rh = w_r.astype(BF16)
    w_rl = (w_r - w_rh.astype(F32)).astype(BF16)
    b_r = jnp.pad(jnp.concatenate([b_router_group, b_router_expert]), (0, LANES - N_GROUPS - N_EXPERTS))
    tm = min(MERGE_TM, S)
    tri_strict = jnp.asarray(np.tril(np.ones((tm, tm), np.float32), -1), BF16)
    x2, h2, route, cnt = _merge(
        x2d, o_gla, outs, stats, norm1_g.reshape(1, -1), w_branch_gate.astype(BF16),
        b_branch_gate.reshape(1, -1), w_proj_gla.astype(BF16), w_proj_attn.astype(BF16), w_out.astype(BF16),
        norm2_g.reshape(1, -1), w_rh, w_rl, b_r.reshape(1, -1), tri_strict, S)

    counts = cnt[0, :N_EXPERTS].astype(jnp.int32)
    pcounts = (counts + MOE_TM - 1) // MOE_TM * MOE_TM
    pends = jnp.cumsum(pcounts)
    pstarts = pends - pcounts
    eid = route[:, 0:TOP_K].astype(jnp.int32)
    rank = route[:, 4:4 + TOP_K].astype(jnp.int32)
    onehot = eid[:, :, None] == jnp.arange(N_EXPERTS, dtype=jnp.int32)[None, None, :]
    dest = rank + jnp.sum(jnp.where(onehot, pstarts[None, None, :], 0), axis=-1)
    P = T * TOP_K + N_EXPERTS * MOE_TM
    ntiles = P // MOE_TM
    tile_start = jnp.arange(ntiles, dtype=jnp.int32) * MOE_TM
    tile_e = jnp.minimum(jnp.sum(pends[None, :] <= tile_start[:, None], axis=1), N_EXPERTS - 1).astype(jnp.int32)
    nact = (pends[-1:] // MOE_TM).astype(jnp.int32)

    inv = jnp.full((P,), -1, jnp.int32).at[dest.reshape(-1)].set(
        jnp.arange(T * TOP_K, dtype=jnp.int32), unique_indices=True)
    pad_tile = jnp.full((MOE_TM,), -1, jnp.int32)
    inv_src = jnp.concatenate([inv, pad_tile, pad_tile])
    inv_dst = jnp.concatenate([pad_tile, inv, pad_tile])
    src = jnp.where(inv_src >= 0, inv_src // TOP_K, 0) * ROW_SUBLANES
    spare = T * TOP_K + jnp.tile(jnp.arange(MOE_TM, dtype=jnp.int32), ntiles + 2)
    dst = jnp.where(inv_dst >= 0, (inv_dst % TOP_K) * T + inv_dst // TOP_K, spare) * ROW_SUBLANES
    slots = _experts(tile_e, nact, src.reshape(ntiles + 2, 1, MOE_TM), dst.reshape(ntiles + 2, 1, MOE_TM), h2,
                     w_gate.astype(BF16), w_up.astype(BF16), w_down.astype(BF16), T * TOP_K + MOE_TM)
    return _combine(x2, route, slots)


def kernel(x, norm1_g, w_in, w_gla_a2, b_gla_a, gla_out_norm_g, dil_q_norm_g, dil_k_norm_g, w_proj_gla,
           w_proj_attn, w_branch_gate, b_branch_gate, w_out, norm2_g, w_router_group, b_router_group,
           w_router_expert, b_router_expert, w_gate, w_up, w_down):
    B, S, D = x.shape
    assert D == D_MODEL and S % (DIL_BLOCK * DIL_PATTERNS[-1][1]) == 0
    x2d = x.reshape(B * S, D)
    params = (norm1_g, w_in, w_gla_a2, b_gla_a, gla_out_norm_g, dil_q_norm_g, dil_k_norm_g, w_proj_gla,
              w_proj_attn, w_branch_gate, b_branch_gate, w_out, norm2_g, w_router_group, b_router_group,
              w_router_expert, b_router_expert, w_gate, w_up, w_down)
    for layer in range(norm1_g.shape[0]):
        x2d = _layer(x2d, B, S, *(p[layer] for p in params))
    return x2d.reshape(B, S, D)
```

```python
import functools

import jax
import jax.numpy as jnp
import numpy as np
from jax import lax
from jax.experimental import pallas as pl
from jax.experimental.pallas import tpu as pltpu

F32 = jnp.float32
BF16 = jnp.bfloat16

D_MODEL = 1024
EPS = 1e-6
GLA_HEADS = 4
GLA_DK = 128
GLA_RANK = 16
GLA_TAU = 16.0
GLA_SUBCHUNK = 16
GLA_BLOCK = 128
DIL_PATTERNS = ((128, 1), (512, 4), (2048, 16))
DIL_GROUPS = 3
DIL_HEADS = 4
DIL_DH = 128
DIL_BLOCK = 128
ALIBI_SLOPES = tuple(2.0 ** (-8.0 * (i + 1) / (DIL_GROUPS * DIL_HEADS)) for i in range(DIL_GROUPS * DIL_HEADS))
N_GROUPS = 4
EXPERTS_PER_GROUP = 8
N_EXPERTS = N_GROUPS * EXPERTS_PER_GROUP
TOP_K = 2
D_EXPERT = 512

HEAD_W = 512
N_GLA_TILES = 4
N_COL_TILES = N_GLA_TILES + 3 * DIL_GROUPS
QKV_W = 3 * HEAD_W
LANES = 128
NEG = -1e30

INPROJ_TM = 512
MERGE_TM = 512
MOE_TM = 512
ROW_TM = 512
ROW_UNROLL = 8
DIL_QBLOCKS = 4
VMEM_LIMIT = 56 * 1024 * 1024


def _dot(a, b):
    return jnp.dot(a, b, preferred_element_type=F32)


def _dot_nt(a, b):
    return lax.dot_general(a, b, (((1,), (1,)), ((), ())), preferred_element_type=F32)


def _dot_tn(a, b):
    return lax.dot_general(a, b, (((0,), (0,)), ((), ())), preferred_element_type=F32)


def _rms(x, g):
    return x * lax.rsqrt(jnp.mean(x * x, axis=-1, keepdims=True) + EPS) * g


def _params(sem):
    return pltpu.CompilerParams(dimension_semantics=sem, vmem_limit_bytes=VMEM_LIMIT)


def _head(c):
    return slice(c * LANES, (c + 1) * LANES)


ROW_SUBLANES = 8


def _load_rows(ref, lead, n):
    return jnp.concatenate([ref[lead + (pl.ds(c, n, stride=ROW_SUBLANES), slice(None))]
                            for c in range(ROW_SUBLANES)], axis=1)


def _store_rows(ref, lead, val):
    n = val.shape[0]
    for c in range(ROW_SUBLANES):
        ref[lead + (pl.ds(c, n, stride=ROW_SUBLANES), slice(None))] = val[:, _head(c)]


def _inproj_kernel(x_ref, g1_ref, w_ref, wga_ref, wa2_ref, ba_ref, qkg_ref,
                   gla_ref, loga_ref, d0_ref, d1_ref, d2_ref, h_ref, ybuf_ref):
    tm = x_ref.shape[0]
    hb = _rms(x_ref[...], g1_ref[...]).astype(BF16)
    h_ref[...] = hb
    ga = _dot(hb, wga_ref[...])
    z = _dot(ga.astype(BF16), wa2_ref[...]) + ba_ref[...]
    log_sig = jnp.minimum(z, 0.0) - jnp.log(1.0 + jnp.exp(-jnp.abs(z)))
    loga_ref[...] = log_sig * (1.0 / GLA_TAU)

    dil_refs = (d0_ref, d1_ref, d2_ref)
    n_strided = 0
    for jj in range(N_COL_TILES):
        cols = slice(jj * HEAD_W, (jj + 1) * HEAD_W)
        y = _dot(h_ref[...], w_ref[:, cols])
        if jj < N_GLA_TILES:
            gla_ref[:, cols] = y.astype(BF16)
            continue
        gi, kind = divmod(jj - N_GLA_TILES, 3)
        d = DIL_PATTERNS[gi][1]
        ocols = [slice(kind * HEAD_W + c * LANES, kind * HEAD_W + (c + 1) * LANES) for c in range(DIL_HEADS)]
        if kind < 2:
            g = qkg_ref[jj]
            slabs = [_rms(y[:, _head(c)], g[:, _head(c)]) for c in range(DIL_HEADS)]
        else:
            slabs = [y[:, _head(c)] for c in range(DIL_HEADS)]
        out = dil_refs[gi]
        if d == 1:
            for c in range(DIL_HEADS):
                out[:, ocols[c]] = slabs[c].astype(BF16)
            continue
        buf = n_strided % ybuf_ref.shape[0]
        n_strided += 1
        for c in range(DIL_HEADS):
            ybuf_ref[buf, c] = slabs[c]
        for r in range(d):
            for c in range(DIL_HEADS):
                out[0, r, :, ocols[c]] = ybuf_ref[buf, c, pl.ds(r, tm // d, stride=d), :].astype(BF16)


def _resident(shape):
    return pl.BlockSpec(shape, lambda i: (0,) * len(shape), pipeline_mode=pl.Buffered(1))


def _inproj(x2d, g1, w_main, w_ga, w_a2, b_a, qk_gain, B, S):
    T = x2d.shape[0]
    tm = min(INPROJ_TM, S)
    tpb = S // tm

    def dil_spec(gi):
        d = DIL_PATTERNS[gi][1]
        if d == 1:
            return pl.BlockSpec((tm, QKV_W), lambda i: (i, 0))
        return pl.BlockSpec((1, d, tm // d, QKV_W), lambda i: (i // tpb, 0, i % tpb, 0))

    def dil_shape(gi):
        d = DIL_PATTERNS[gi][1]
        shape = (T, QKV_W) if d == 1 else (B, d, S // d, QKV_W)
        return jax.ShapeDtypeStruct(shape, BF16)

    return pl.pallas_call(
        _inproj_kernel,
        grid=(T // tm,),
        in_specs=[
            pl.BlockSpec((tm, D_MODEL), lambda i: (i, 0)),
            _resident(g1.shape), _resident(w_main.shape), _resident(w_ga.shape), _resident(w_a2.shape),
            _resident(b_a.shape), _resident(qk_gain.shape),
        ],
        out_specs=[
            pl.BlockSpec((tm, N_GLA_TILES * HEAD_W), lambda i: (i, 0)),
            pl.BlockSpec((tm, HEAD_W), lambda i: (i, 0)),
            dil_spec(0), dil_spec(1), dil_spec(2),
        ],
        out_shape=[
            jax.ShapeDtypeStruct((T, N_GLA_TILES * HEAD_W), BF16),
            jax.ShapeDtypeStruct((T, HEAD_W), F32),
            dil_shape(0), dil_shape(1), dil_shape(2),
        ],
        scratch_shapes=[pltpu.VMEM((tm, D_MODEL), BF16), pltpu.VMEM((2, DIL_HEADS, tm, LANES), F32)],
        compiler_params=_params(("arbitrary",)),
        name="inproj",
    )(x2d, g1, w_main, w_ga, w_a2, b_a, qk_gain)


def _gla_kernel(q_ref, k_ref, v_ref, r_ref, la_ref, tri_ref, gn_ref, o_ref, st_ref):
    n = pl.program_id(1)
    L = GLA_BLOCK

    @pl.when(n == 0)
    def _():
        st_ref[...] = jnp.zeros_like(st_ref)

    la = la_ref[...]
    hi = la.astype(BF16)
    r1 = la - hi.astype(F32)
    mid = r1.astype(BF16)
    lo = (r1 - mid.astype(F32)).astype(BF16)
    tri = tri_ref[...]
    b_all = _dot(tri, hi) + _dot(tri, mid) + _dot(tri, lo)

    row = lax.broadcasted_iota(jnp.int32, (L, L), 0)
    col = lax.broadcasted_iota(jnp.int32, (L, L), 1)

    for h in range(GLA_HEADS):
        sl = _head(h)
        b = b_all[:, sl]
        q = q_ref[:, sl].astype(F32) * (GLA_DK ** -0.5)
        k = k_ref[:, sl].astype(F32)
        v = v_ref[:, sl]

        att = jnp.zeros((L, L), F32)
        seg = L
        while seg > GLA_SUBCHUNK:
            half = seg // 2
            beta = jnp.concatenate(
                [jnp.broadcast_to(b[s + half - 1:s + half, :], (seg, GLA_DK)) for s in range(0, L, seg)], axis=0)
            qs = (q * jnp.exp(jnp.minimum(b - beta, 0.0))).astype(BF16)
            ks = (k * jnp.exp(jnp.minimum(beta - b, 0.0))).astype(BF16)
            same = (row // seg) == (col // seg)
            mask = same & ((row % seg) >= half) & ((col % seg) < half)
            att = jnp.where(mask, _dot_nt(qs, ks), att)
            seg = half
        beta = jnp.concatenate(
            [jnp.zeros((GLA_SUBCHUNK, GLA_DK), F32)]
            + [jnp.broadcast_to(b[s - 1:s, :], (GLA_SUBCHUNK, GLA_DK)) for s in range(GLA_SUBCHUNK, L, GLA_SUBCHUNK)],
            axis=0)
        qs = (q * jnp.exp(b - beta)).astype(BF16)
        ks = (k * jnp.exp(beta - b)).astype(BF16)
        mask = ((row // GLA_SUBCHUNK) == (col // GLA_SUBCHUNK)) & (col <= row)
        att = jnp.where(mask, _dot_nt(qs, ks), att)

        state_t = st_ref[h]
        q0 = (q * jnp.exp(b)).astype(BF16)
        o = _dot(att.astype(BF16), v) + _dot_nt(q0, state_t.astype(BF16))

        b_last = b[L - 1:L, :]
        k_end = (k * jnp.exp(b_last - b)).astype(BF16)
        st_ref[h] = state_t * jnp.exp(b_last) + _dot_tn(v, k_end)

        rr = r_ref[:, sl].astype(F32)
        o_ref[:, sl] = (_rms(o, gn_ref[...]) * (rr * jax.nn.sigmoid(rr))).astype(BF16)


def _gla(proj, log_a, tri, gn, B, S):
    L = GLA_BLOCK
    nb = S // L
    return pl.pallas_call(
        _gla_kernel,
        grid=(B, nb),
        in_specs=[
            pl.BlockSpec((L, HEAD_W), lambda b, n: (b * nb + n, 0)),
            pl.BlockSpec((L, HEAD_W), lambda b, n: (b * nb + n, 1)),
            pl.BlockSpec((L, HEAD_W), lambda b, n: (b * nb + n, 2)),
            pl.BlockSpec((L, HEAD_W), lambda b, n: (b * nb + n, 3)),
            pl.BlockSpec((L, HEAD_W), lambda b, n: (b * nb + n, 0)),
            pl.BlockSpec((L, L), lambda b, n: (0, 0)),
            pl.BlockSpec((1, GLA_DK), lambda b, n: (0, 0)),
        ],
        out_specs=pl.BlockSpec((L, HEAD_W), lambda b, n: (b * nb + n, 0)),
        out_shape=jax.ShapeDtypeStruct((B * S, HEAD_W), BF16),
        scratch_shapes=[pltpu.VMEM((GLA_HEADS, GLA_DK, GLA_DK), F32)],
        compiler_params=_params(("arbitrary", "arbitrary")),
        name="gla",
    )(proj, proj, proj, proj, log_a, tri, gn)


def _dil_kernel(q_ref, kp_ref, kc_ref, vp_ref, vc_ref, bias_ref, o_ref, st_ref, *, nq):
    first = pl.program_id(2) == 0
    Q = DIL_BLOCK
    col = lax.broadcasted_iota(jnp.int32, (Q, 2 * Q), 1)
    lane = lax.broadcasted_iota(jnp.int32, (Q, LANES), 1)
    for blk in range(nq):
        rows = slice(blk * Q, (blk + 1) * Q)
        prev = slice((blk - 1) * Q, blk * Q)
        stats = jnp.zeros((Q, LANES), F32)
        for h in range(DIL_HEADS):
            sl = _head(h)
            if blk == 0:
                kprev, vprev = kp_ref[0, 0, :, sl], vp_ref[0, 0, :, sl]
            else:
                kprev, vprev = kc_ref[0, 0, prev, sl], vc_ref[0, 0, prev, sl]
            kk = jnp.concatenate([kprev, kc_ref[0, 0, rows, sl]], axis=0)
            vv = jnp.concatenate([vprev, vc_ref[0, 0, rows, sl]], axis=0)
            s = _dot_nt(q_ref[0, 0, rows, sl], kk) + bias_ref[h]
            if blk == 0:
                s = jnp.where(jnp.logical_and(first, col < Q), NEG, s)
            m = jnp.max(s, axis=-1, keepdims=True)
            p = jnp.exp(s - m)
            den = jnp.sum(p, axis=-1, keepdims=True)
            o_ref[0, 0, rows, sl] = (_dot(p.astype(BF16), vv) / den).astype(BF16)
            stats = jnp.where((lane // 16) == h, m, stats)
            stats = jnp.where((lane // 16) == DIL_HEADS + h, den, stats)
        st_ref[0, 0, rows, :] = stats


def _dilated(qkv, bias, gi):
    B, d, Lsub, _ = qkv.shape
    nq = min(DIL_QBLOCKS, Lsub // DIL_BLOCK)
    rows = nq * DIL_BLOCK

    def cur(c):
        return pl.BlockSpec((1, 1, rows, HEAD_W), lambda b, r, i: (b, r, i, c))

    def prev(c):
        return pl.BlockSpec((1, 1, DIL_BLOCK, HEAD_W), lambda b, r, i: (b, r, jnp.maximum(i * nq - 1, 0), c))

    return pl.pallas_call(
        functools.partial(_dil_kernel, nq=nq),
        grid=(B, d, Lsub // rows),
        in_specs=[cur(0), prev(1), cur(1), prev(2), cur(2),
                  pl.BlockSpec((DIL_HEADS, DIL_BLOCK, 2 * DIL_BLOCK), lambda b, r, i: (0, 0, 0))],
        out_specs=[
            pl.BlockSpec((1, 1, rows, HEAD_W), lambda b, r, i: (b, r, i, 0)),
            pl.BlockSpec((1, 1, rows, LANES), lambda b, r, i: (b, r, i, 0)),
        ],
        out_shape=[
            jax.ShapeDtypeStruct((B, d, Lsub, HEAD_W), BF16),
            jax.ShapeDtypeStruct((B, d, Lsub, LANES), F32),
        ],
        compiler_params=_params(("arbitrary", "arbitrary", "arbitrary")),
        name=f"dilated{gi}",
    )(qkv, qkv, qkv, qkv, qkv, bias)


def _alibi_bias(gi):
    window, d = DIL_PATTERNS[gi]
    Q = DIL_BLOCK
    dist = np.arange(Q)[:, None] + Q - np.arange(2 * Q)[None, :]
    valid = (dist >= 0) & (dist <= window // d)
    slopes = np.asarray(ALIBI_SLOPES, np.float32).reshape(DIL_GROUPS, DIL_HEADS)[gi]
    bias = -slopes[:, None, None] * (d * dist).astype(np.float32)
    return jnp.asarray(np.where(valid[None], bias, np.float32(NEG)), F32)


def _merge_kernel(x_ref, og_ref, o0_ref, o1_ref, o2_ref, s0_ref, s1_ref, s2_ref, g1_ref, wbg_ref, bbg_ref,
                  wpg_ref, wpa_ref, wout_ref, g2_ref, wrh_ref, wrl_ref, br_ref, tri_ref,
                  x2_ref, h2_ref, route_ref, cnt_ref, carry_ref, obuf_ref, sbuf_ref):
    @pl.when(pl.program_id(0) == 0)
    def _():
        carry_ref[...] = jnp.zeros_like(carry_ref)

    x = x_ref[...]
    tm = x.shape[0]
    hb = _rms(x, g1_ref[...]).astype(BF16)
    gates = jax.nn.sigmoid(_dot(hb, wbg_ref[...]) + bbg_ref[...])

    for slot, (o_ref, s_ref) in enumerate(((o1_ref, s1_ref), (o2_ref, s2_ref))):
        d = DIL_PATTERNS[slot + 1][1]
        for r in range(d):
            sbuf_ref[slot, pl.ds(r, tm // d, stride=d), :] = s_ref[0, r]
            for c in range(DIL_HEADS):
                obuf_ref[slot, c, pl.ds(r, tm // d, stride=d), :] = o_ref[0, r, :, _head(c)].astype(F32)

    stats = (s0_ref[...], sbuf_ref[0], sbuf_ref[1])
    dens = [pltpu.roll(s, 64, 1) for s in stats]
    m_all = jnp.maximum(jnp.maximum(stats[0], stats[1]), stats[2])
    wts = [d * jnp.exp(s - m_all) for s, d in zip(stats, dens)]
    inv = 1.0 / (wts[0] + wts[1] + wts[2])
    coef = [w * inv for w in wts]
    heads = []
    for h in range(DIL_HEADS):
        group_out = (o0_ref[:, _head(h)].astype(F32), obuf_ref[0, h], obuf_ref[1, h])
        acc = jnp.zeros((tm, DIL_DH), F32)
        for g in range(DIL_GROUPS):
            c = jnp.broadcast_to(coef[g][:, 16 * h:16 * h + 1], (tm, DIL_DH))
            acc = acc + c * group_out[g]
        heads.append(acc.astype(BF16))
    o_att = jnp.concatenate(heads, axis=1)

    y = gates[:, :D_MODEL] * _dot(og_ref[...], wpg_ref[...]) + gates[:, D_MODEL:] * _dot(o_att, wpa_ref[...])
    x2 = x + _dot(y.astype(BF16), wout_ref[...])
    x2_ref[...] = x2

    h2 = _rms(x2, g2_ref[...])
    h2_ref[...] = h2

    h2h = h2.astype(BF16)
    h2l = (h2 - h2h.astype(F32)).astype(BF16)
    wrh = wrh_ref[...]
    logit = _dot(h2h, wrh) + _dot(h2l, wrh) + _dot(h2h, wrl_ref[...]) + br_ref[...]

    lane = lax.broadcasted_iota(jnp.int32, (tm, LANES), 1).astype(F32)
    big = jnp.float32(1e9)
    gl = jnp.where(lane < N_GROUPS, logit, NEG)
    gmax = jnp.max(gl, axis=-1, keepdims=True)
    gsel = jnp.min(jnp.where(gl == gmax, lane, big), axis=-1, keepdims=True)
    g_p = 1.0 / jnp.sum(jnp.exp(gl - gmax), axis=-1, keepdims=True)
    lo = N_GROUPS + EXPERTS_PER_GROUP * gsel
    el = jnp.where((lane >= lo) & (lane < lo + EXPERTS_PER_GROUP), logit, NEG)
    v1 = jnp.max(el, axis=-1, keepdims=True)
    i1 = jnp.min(jnp.where(el == v1, lane, big), axis=-1, keepdims=True)
    el2 = jnp.where(lane == i1, NEG, el)
    v2 = jnp.max(el2, axis=-1, keepdims=True)
    i2 = jnp.min(jnp.where(el2 == v2, lane, big), axis=-1, keepdims=True)
    ex = jnp.exp(v2 - v1)
    w1 = g_p / (1.0 + ex)
    w2 = g_p * ex / (1.0 + ex)
    e1 = i1 - N_GROUPS
    e2 = i2 - N_GROUPS

    oh1 = lane == e1
    oh2 = lane == e2
    onehot = jnp.where(oh1 | oh2, 1.0, 0.0)
    prefix = _dot(tri_ref[...], onehot.astype(BF16)) + carry_ref[...]
    r1 = jnp.sum(jnp.where(oh1, prefix, 0.0), axis=-1, keepdims=True)
    r2 = jnp.sum(jnp.where(oh2, prefix, 0.0), axis=-1, keepdims=True)
    carry = carry_ref[...] + jnp.sum(onehot, axis=0, keepdims=True)
    carry_ref[...] = carry
    cnt_ref[...] = jnp.broadcast_to(carry, cnt_ref.shape)

    route = jnp.zeros((tm, LANES), F32)
    for idx, val in enumerate((e1, e2, w1, w2, r1, r2)):
        route = jnp.where(lane == idx, val, route)
    route_ref[...] = route


def _merge(x2d, o_gla, outs, stats, g1, wbg, bbg, wpg, wpa, wout, g2, wrh, wrl, br, tri, S):
    T = x2d.shape[0]
    tm = min(MERGE_TM, S)
    tpb = S // tm
    tok = lambda w: pl.BlockSpec((tm, w), lambda i: (i, 0))
    full = lambda a: pl.BlockSpec(a.shape, lambda i: (0,) * a.ndim)

    def res(gi, w):
        d = DIL_PATTERNS[gi][1]
        return pl.BlockSpec((1, d, tm // d, w), lambda i: (i // tpb, 0, i % tpb, 0))

    return pl.pallas_call(
        _merge_kernel,
        grid=(T // tm,),
        in_specs=[tok(D_MODEL), tok(HEAD_W), tok(HEAD_W), res(1, HEAD_W), res(2, HEAD_W),
                  tok(LANES), res(1, LANES), res(2, LANES),
                  full(g1), full(wbg), full(bbg), full(wpg), full(wpa), full(wout), full(g2),
                  full(wrh), full(wrl), full(br), full(tri)],
        out_specs=[tok(D_MODEL), tok(D_MODEL), tok(LANES), pl.BlockSpec((8, LANES), lambda i: (0, 0))],
        out_shape=[
            jax.ShapeDtypeStruct((T, D_MODEL), F32),
            jax.ShapeDtypeStruct((T, D_MODEL), F32),
            jax.ShapeDtypeStruct((T, LANES), F32),
            jax.ShapeDtypeStruct((8, LANES), F32),
        ],
        scratch_shapes=[pltpu.VMEM((1, LANES), F32),
                        pltpu.VMEM((DIL_GROUPS - 1, DIL_HEADS, tm, LANES), F32),
                        pltpu.VMEM((DIL_GROUPS - 1, tm, LANES), F32)],
        compiler_params=_params(("arbitrary",)),
        name="merge",
    )(x2d, o_gla, *outs, *stats, g1, wbg, bbg, wpg, wpa, wout, g2, wrh, wrl, br, tri)


def _dispatch_kernel(pad_lo_ref, pad_hi_ref, dest_ref, h2_ref, xin_hbm_ref, xin_ref, inv_ref, sem):
    del xin_hbm_ref
    i = pl.program_id(0)
    tm = h2_ref.shape[0]

    def start(r, c):
        for k in range(TOP_K):
            d = dest_ref[0, 0, TOP_K * r + k]
            pltpu.make_async_copy(h2_ref.at[pl.ds(r, 1), :], xin_ref.at[pl.ds(d, 1), :], sem).start()
            inv_ref[d] = (i * tm + r) * TOP_K + k
        return c

    lax.fori_loop(0, tm, start, 0, unroll=ROW_UNROLL)
    for k in range(TOP_K):
        pltpu.make_async_copy(h2_ref, xin_ref.at[pl.ds(0, tm), :], sem).wait()

    @pl.when(i == pl.num_programs(0) - 1)
    def _():
        def mark(p, c):
            inv_ref[p] = -1
            return c

        for e in range(N_EXPERTS + 1):
            lax.fori_loop(pad_lo_ref[e], pad_hi_ref[e], mark, 0)


def _dispatch(pad_lo, pad_hi, dest, h2, xin0):
    T = h2.shape[0]
    P = xin0.shape[0]
    tm = min(ROW_TM, T)
    return pl.pallas_call(
        _dispatch_kernel,
        grid_spec=pltpu.PrefetchScalarGridSpec(
            num_scalar_prefetch=2,
            grid=(T // tm,),
            in_specs=[
                pl.BlockSpec((1, 1, TOP_K * tm), lambda i, lo, hi: (i, 0, 0), memory_space=pltpu.SMEM),
                pl.BlockSpec((tm, D_MODEL), lambda i, lo, hi: (i, 0)),
                pl.BlockSpec(memory_space=pl.ANY),
            ],
            out_specs=[pl.BlockSpec(memory_space=pl.ANY), pl.BlockSpec(memory_space=pltpu.SMEM)],
            scratch_shapes=[pltpu.SemaphoreType.DMA],
        ),
        out_shape=[jax.ShapeDtypeStruct(xin0.shape, xin0.dtype), jax.ShapeDtypeStruct((P,), jnp.int32)],
        input_output_aliases={4: 0},
        compiler_params=_params(("arbitrary",)),
        name="dispatch",
    )(pad_lo, pad_hi, dest.reshape(T // tm, 1, TOP_K * tm), h2, xin0)


def _expert_kernel(tile_e_ref, nact_ref, dst_prev_ref, x_ref, wg_ref, wu_ref, wd_ref, slots_ref, ybuf_ref, ssem):
    del tile_e_ref
    i = pl.program_id(0)
    nact = nact_ref[0]
    slot = lax.rem(i, 2)
    other = 1 - slot
    R = MOE_TM

    def scatter(dst_ref, s):
        for q in range(R):
            row = pl.multiple_of(dst_ref[0, 0, q], ROW_SUBLANES)
            pltpu.make_async_copy(ybuf_ref.at[s, pl.ds(q * ROW_SUBLANES, ROW_SUBLANES), :],
                                  slots_ref.at[pl.ds(row, ROW_SUBLANES), :], ssem.at[s]).start()

    def wait_scatter(s):
        pltpu.make_async_copy(ybuf_ref.at[s], slots_ref.at[pl.ds(0, R * ROW_SUBLANES), :], ssem.at[s]).wait()

    @pl.when(i == 0)
    def _():
        ybuf_ref[...] = jnp.zeros_like(ybuf_ref)

    @pl.when(i <= nact)
    def _():
        @pl.when(i >= 1)
        def _():
            wait_scatter(slot)

        scatter(dst_prev_ref, other)
        xb = x_ref[...].astype(BF16)
        g = _dot(xb, wg_ref[0])
        u = _dot(xb, wu_ref[0])
        a = (g * jax.nn.sigmoid(g) * u).astype(BF16)
        _store_rows(ybuf_ref, (slot,), _dot(a, wd_ref[0]))

    @pl.when(i == nact + 1)
    def _():
        wait_scatter(slot)


def _experts(tile_e, nact, dst, xin, wg, wu, wd, n_slot_rows):
    ntiles = tile_e.shape[0]
    R = MOE_TM

    def rows(i, te, na):
        return (jnp.minimum(i, na[0] - 1), 0)

    def wsel(i, te, na):
        return (te[jnp.minimum(i, na[0] - 1)], 0, 0)

    return pl.pallas_call(
        _expert_kernel,
        grid_spec=pltpu.PrefetchScalarGridSpec(
            num_scalar_prefetch=2,
            grid=(ntiles + 2,),
            in_specs=[
                pl.BlockSpec((1, 1, R), lambda i, te, na: (i, 0, 0), memory_space=pltpu.SMEM),
                pl.BlockSpec((R, D_MODEL), rows),
                pl.BlockSpec((1, D_MODEL, D_EXPERT), wsel),
                pl.BlockSpec((1, D_MODEL, D_EXPERT), wsel),
                pl.BlockSpec((1, D_EXPERT, D_MODEL), wsel),
            ],
            out_specs=pl.BlockSpec(memory_space=pl.ANY),
            scratch_shapes=[
                pltpu.VMEM((2, R * ROW_SUBLANES, LANES), F32),
                pltpu.SemaphoreType.DMA((2,)),
            ],
        ),
        out_shape=jax.ShapeDtypeStruct((n_slot_rows * ROW_SUBLANES, LANES), F32),
        compiler_params=_params(("arbitrary",)),
        name="experts",
    )(tile_e, nact, dst, xin, wg, wu, wd)


def _combine_kernel(x2_ref, route_ref, s0_ref, s1_ref, out_ref):
    tm = x2_ref.shape[0]
    route = route_ref[...]
    w1 = route[:, 2:3]
    w2 = route[:, 3:4]
    out_ref[...] = x2_ref[...] + w1 * _load_rows(s0_ref, (), tm) + w2 * _load_rows(s1_ref, (), tm)


def _combine(x2, route, slots):
    T = x2.shape[0]
    tm = min(ROW_TM, T)
    nt = T // tm
    return pl.pallas_call(
        _combine_kernel,
        grid=(nt,),
        in_specs=[
            pl.BlockSpec((tm, D_MODEL), lambda i: (i, 0)),
            pl.BlockSpec((tm, LANES), lambda i: (i, 0)),
            pl.BlockSpec((tm * ROW_SUBLANES, LANES), lambda i: (i, 0)),
            pl.BlockSpec((tm * ROW_SUBLANES, LANES), lambda i: (nt + i, 0)),
        ],
        out_specs=pl.BlockSpec((tm, D_MODEL), lambda i: (i, 0)),
        out_shape=jax.ShapeDtypeStruct((T, D_MODEL), F32),
        compiler_params=_params(("arbitrary",)),
        name="combine",
    )(x2, route, slots, slots)


def _layer(x2d, B, S, norm1_g, w_in, w_gla_a2, b_gla_a, gla_out_norm_g, dil_q_norm_g, dil_k_norm_g,
           w_proj_gla, w_proj_attn, w_branch_gate, b_branch_gate, w_out, norm2_g,
           w_router_group, b_router_group, w_router_expert, b_router_expert, w_gate, w_up, w_down):
    T = B * S
    n_gla = N_GLA_TILES * HEAD_W
    w_main = jnp.concatenate([w_in[:, :n_gla], w_in[:, n_gla + GLA_RANK:]], axis=1).astype(BF16)
    w_ga = jnp.pad(w_in[:, n_gla:n_gla + GLA_RANK], ((0, 0), (0, LANES - GLA_RANK))).astype(BF16)
    w_a2 = jnp.pad(w_gla_a2, ((0, LANES - GLA_RANK), (0, 0))).astype(BF16)
    gains = [jnp.ones((HEAD_W,), F32)] * N_GLA_TILES
    for gi in range(DIL_GROUPS):
        gains += [jnp.tile(dil_q_norm_g[gi], DIL_HEADS) * (DIL_DH ** -0.5),
                  jnp.tile(dil_k_norm_g[gi], DIL_HEADS), jnp.ones((HEAD_W,), F32)]
    qk_gain = jnp.stack(gains).reshape(N_COL_TILES, 1, HEAD_W)

    gla_in, log_a, qkv0, qkv1, qkv2 = _inproj(
        x2d, norm1_g.reshape(1, -1), w_main, w_ga, w_a2, b_gla_a.reshape(1, -1), qk_gain, B, S)

    tri_incl = jnp.asarray(np.tril(np.ones((GLA_BLOCK, GLA_BLOCK), np.float32)), BF16)
    o_gla = _gla(gla_in, log_a, tri_incl, gla_out_norm_g.reshape(1, -1), B, S)

    outs, stats = [], []
    for gi, qkv in enumerate((qkv0.reshape(B, 1, S, QKV_W), qkv1, qkv2)):
        o, st = _dilated(qkv, _alibi_bias(gi), gi)
        outs.append(o)
        stats.append(st)
    outs[0] = outs[0].reshape(T, HEAD_W)
    stats[0] = stats[0].reshape(T, LANES)

    w_r = jnp.pad(jnp.concatenate([w_router_group, w_router_expert], axis=1),
                  ((0, 0), (0, LANES - N_GROUPS - N_EXPERTS)))
    w_rh = w_r.astype(BF16)
    w_rl = (w_r - w_rh.astype(F32)).astype(BF16)
    b_r = jnp.pad(jnp.concatenate([b_router_group, b_router_expert]), (0, LANES - N_GROUPS - N_EXPERTS))
    tm = min(MERGE_TM, S)
    tri_strict = jnp.asarray(np.tril(np.ones((tm, tm), np.float32), -1), BF16)
    x2, h2, route, cnt = _merge(
        x2d, o_gla, outs, stats, norm1_g.reshape(1, -1), w_branch_gate.astype(BF16),
        b_branch_gate.reshape(1, -1), w_proj_gla.astype(BF16), w_proj_attn.astype(BF16), w_out.astype(BF16),
        norm2_g.reshape(1, -1), w_rh, w_rl, b_r.reshape(1, -1), tri_strict, S)

    counts = cnt[0, :N_EXPERTS].astype(jnp.int32)
    pcounts = (counts + MOE_TM - 1) // MOE_TM * MOE_TM
    pends = jnp.cumsum(pcounts)
    pstarts = pends - pcounts
    eid = route[:, 0:TOP_K].astype(jnp.int32)
    rank = route[:, 4:4 + TOP_K].astype(jnp.int32)
    onehot = eid[:, :, None] == jnp.arange(N_EXPERTS, dtype=jnp.int32)[None, None, :]
    dest = rank + jnp.sum(jnp.where(onehot, pstarts[None, None, :], 0), axis=-1)
    P = T * TOP_K + N_EXPERTS * MOE_TM
    ntiles = P // MOE_TM
    tile_start = jnp.arange(ntiles, dtype=jnp.int32) * MOE_TM
    tile_e = jnp.minimum(jnp.sum(pends[None, :] <= tile_start[:, None], axis=1), N_EXPERTS - 1).astype(jnp.int32)
    nact = (pends[-1:] // MOE_TM).astype(jnp.int32)

    pad_lo = jnp.concatenate([pstarts + counts, pends[-1:]]).astype(jnp.int32)
    pad_hi = jnp.concatenate([pends, jnp.full((1,), P, jnp.int32)]).astype(jnp.int32)
    xin, inv = _dispatch(pad_lo, pad_hi, dest, h2, jnp.zeros((P, D_MODEL), F32))

    pad_tile = jnp.full((MOE_TM,), -1, jnp.int32)
    inv_dst = jnp.concatenate([pad_tile, inv, pad_tile])
    spare = T * TOP_K + jnp.tile(jnp.arange(MOE_TM, dtype=jnp.int32), ntiles + 2)
    dst = jnp.where(inv_dst >= 0, (inv_dst % TOP_K) * T + inv_dst // TOP_K, spare) * ROW_SUBLANES
    slots = _experts(tile_e, nact, dst.reshape(ntiles + 2, 1, MOE_TM), xin,
                     w_gate.astype(BF16), w_up.astype(BF16), w_down.astype(BF16), T * TOP_K + MOE_TM)
    return _combine(x2, route, slots)


def kernel(x, norm1_g, w_in, w_gla_a2, b_gla_a, gla_out_norm_g, dil_q_norm_g, dil_k_norm_g, w_proj_gla,
           w_proj_attn, w_branch_gate, b_branch_gate, w_out, norm2_g, w_router_group, b_router_group,
           w_router_expert, b_router_expert, w_gate, w_up, w_down):
    B, S, D = x.shape
    assert D == D_MODEL and S % (DIL_BLOCK * DIL_PATTERNS[-1][1]) == 0
    x2d = x.reshape(B * S, D)
    params = (norm1_g, w_in, w_gla_a2, b_gla_a, gla_out_norm_g, dil_q_norm_g, dil_k_norm_g, w_proj_gla,
              w_proj_attn, w_branch_gate, b_branch_gate, w_out, norm2_g, w_router_group, b_router_group,
              w_router_expert, b_router_expert, w_gate, w_up, w_down)
    for layer in range(norm1_g.shape[0]):
        x2d = _layer(x2d, B, S, *(p[layer] for p in params))
    return x2d.reshape(B, S, D)
```

```python
import functools

import jax
import jax.numpy as jnp
import numpy as np
from jax import lax
from jax.experimental import pallas as pl
from jax.experimental.pallas import tpu as pltpu

F32 = jnp.float32
BF16 = jnp.bfloat16

D_MODEL = 1024
EPS = 1e-6
GLA_HEADS = 4
GLA_DK = 128
GLA_RANK = 16
GLA_TAU = 16.0
GLA_SUBCHUNK = 16
GLA_BLOCK = 128
GLA_STEP_BLOCKS = 4
DIL_PATTERNS = ((128, 1), (512, 4), (2048, 16))
DIL_GROUPS = 3
DIL_HEADS = 4
DIL_DH = 128
DIL_BLOCK = 128
ALIBI_SLOPES = tuple(2.0 ** (-8.0 * (i + 1) / (DIL_GROUPS * DIL_HEADS)) for i in range(DIL_GROUPS * DIL_HEADS))
N_GROUPS = 4
EXPERTS_PER_GROUP = 8
N_EXPERTS = N_GROUPS * EXPERTS_PER_GROUP
TOP_K = 2
D_EXPERT = 512

HEAD_W = 512
N_GLA_TILES = 4
N_COL_TILES = N_GLA_TILES + 3 * DIL_GROUPS
QKV_W = 3 * HEAD_W
LANES = 128
NEG = -1e30

INPROJ_TM = 512
MERGE_TM = 512
MOE_TM = 512
ROW_TM = 512
ROW_UNROLL = 8
DIL_QBLOCKS = 4
VMEM_LIMIT = 56 * 1024 * 1024


def _dot(a, b):
    return jnp.dot(a, b, preferred_element_type=F32)


def _dot_nt(a, b):
    return lax.dot_general(a, b, (((1,), (1,)), ((), ())), preferred_element_type=F32)


def _dot_tn(a, b):
    return lax.dot_general(a, b, (((0,), (0,)), ((), ())), preferred_element_type=F32)


def _rms(x, g):
    return x * lax.rsqrt(jnp.mean(x * x, axis=-1, keepdims=True) + EPS) * g


def _params(sem):
    return pltpu.CompilerParams(dimension_semantics=sem, vmem_limit_bytes=VMEM_LIMIT)


def _head(c):
    return slice(c * LANES, (c + 1) * LANES)


ROW_SUBLANES = 8


def _load_rows(ref, lead, n):
    return jnp.concatenate([ref[lead + (pl.ds(c, n, stride=ROW_SUBLANES), slice(None))]
                            for c in range(ROW_SUBLANES)], axis=1)


def _store_rows(ref, lead, val):
    n = val.shape[0]
    for c in range(ROW_SUBLANES):
        ref[lead + (pl.ds(c, n, stride=ROW_SUBLANES), slice(None))] = val[:, _head(c)]


def _inproj_kernel(x_ref, g1_ref, w_ref, wga_ref, wa2_ref, ba_ref, qkg_ref,
                   gla_ref, loga_ref, d0_ref, d1_ref, d2_ref, h_ref, ybuf_ref):
    tm = x_ref.shape[0]
    hb = _rms(x_ref[...], g1_ref[...]).astype(BF16)
    h_ref[...] = hb
    ga = _dot(hb, wga_ref[...])
    z = _dot(ga.astype(BF16), wa2_ref[...]) + ba_ref[...]
    log_sig = jnp.minimum(z, 0.0) - jnp.log(1.0 + jnp.exp(-jnp.abs(z)))
    loga_ref[...] = log_sig * (1.0 / GLA_TAU)

    dil_refs = (d0_ref, d1_ref, d2_ref)
    n_strided = 0
    for jj in range(N_COL_TILES):
        cols = slice(jj * HEAD_W, (jj + 1) * HEAD_W)
        y = _dot(h_ref[...], w_ref[:, cols])
        if jj < N_GLA_TILES:
            gla_ref[:, cols] = y.astype(BF16)
            continue
        gi, kind = divmod(jj - N_GLA_TILES, 3)
        d = DIL_PATTERNS[gi][1]
        ocols = [slice(kind * HEAD_W + c * LANES, kind * HEAD_W + (c + 1) * LANES) for c in range(DIL_HEADS)]
        if kind < 2:
            g = qkg_ref[jj]
            slabs = [_rms(y[:, _head(c)], g[:, _head(c)]) for c in range(DIL_HEADS)]
        else:
            slabs = [y[:, _head(c)] for c in range(DIL_HEADS)]
        out = dil_refs[gi]
        if d == 1:
            for c in range(DIL_HEADS):
                out[:, ocols[c]] = slabs[c].astype(BF16)
            continue
        buf = n_strided % ybuf_ref.shape[0]
        n_strided += 1
        for c in range(DIL_HEADS):
            ybuf_ref[buf, c] = slabs[c]
        for r in range(d):
            for c in range(DIL_HEADS):
                out[0, r, :, ocols[c]] = ybuf_ref[buf, c, pl.ds(r, tm // d, stride=d), :].astype(BF16)


def _resident(shape):
    return pl.BlockSpec(shape, lambda i: (0,) * len(shape), pipeline_mode=pl.Buffered(1))


def _inproj(x2d, g1, w_main, w_ga, w_a2, b_a, qk_gain, B, S):
    T = x2d.shape[0]
    tm = min(INPROJ_TM, S)
    tpb = S // tm

    def dil_spec(gi):
        d = DIL_PATTERNS[gi][1]
        if d == 1:
            return pl.BlockSpec((tm, QKV_W), lambda i: (i, 0))
        return pl.BlockSpec((1, d, tm // d, QKV_W), lambda i: (i // tpb, 0, i % tpb, 0))

    def dil_shape(gi):
        d = DIL_PATTERNS[gi][1]
        shape = (T, QKV_W) if d == 1 else (B, d, S // d, QKV_W)
        return jax.ShapeDtypeStruct(shape, BF16)

    return pl.pallas_call(
        _inproj_kernel,
        grid=(T // tm,),
        in_specs=[
            pl.BlockSpec((tm, D_MODEL), lambda i: (i, 0)),
            _resident(g1.shape), _resident(w_main.shape), _resident(w_ga.shape), _resident(w_a2.shape),
            _resident(b_a.shape), _resident(qk_gain.shape),
        ],
        out_specs=[
            pl.BlockSpec((tm, N_GLA_TILES * HEAD_W), lambda i: (i, 0)),
            pl.BlockSpec((tm, HEAD_W), lambda i: (i, 0)),
            dil_spec(0), dil_spec(1), dil_spec(2),
        ],
        out_shape=[
            jax.ShapeDtypeStruct((T, N_GLA_TILES * HEAD_W), BF16),
            jax.ShapeDtypeStruct((T, HEAD_W), F32),
            dil_shape(0), dil_shape(1), dil_shape(2),
        ],
        scratch_shapes=[pltpu.VMEM((tm, D_MODEL), BF16), pltpu.VMEM((2, DIL_HEADS, tm, LANES), F32)],
        compiler_params=_params(("arbitrary",)),
        name="inproj",
    )(x2d, g1, w_main, w_ga, w_a2, b_a, qk_gain)


def _gla_kernel(q_ref, k_ref, v_ref, r_ref, la_ref, tri_ref, gn_ref, o_ref, st_ref):
    n = pl.program_id(1)

    @pl.when(n == 0)
    def _():
        st_ref[...] = jnp.zeros_like(st_ref)

    for blk in range(q_ref.shape[0] // GLA_BLOCK):
        _gla_block(q_ref, k_ref, v_ref, r_ref, la_ref, tri_ref, gn_ref, o_ref, st_ref,
                   slice(blk * GLA_BLOCK, (blk + 1) * GLA_BLOCK))


def _gla_block(q_ref, k_ref, v_ref, r_ref, la_ref, tri_ref, gn_ref, o_ref, st_ref, rows):
    L = GLA_BLOCK
    la = la_ref[rows, :]
    hi = la.astype(BF16)
    r1 = la - hi.astype(F32)
    mid = r1.astype(BF16)
    lo = (r1 - mid.astype(F32)).astype(BF16)
    tri = tri_ref[...]
    b_all = _dot(tri, hi) + _dot(tri, mid) + _dot(tri, lo)

    row = lax.broadcasted_iota(jnp.int32, (L, L), 0)
    col = lax.broadcasted_iota(jnp.int32, (L, L), 1)

    for h in range(GLA_HEADS):
        sl = _head(h)
        b = b_all[:, sl]
        q = q_ref[rows, sl].astype(F32) * (GLA_DK ** -0.5)
        k = k_ref[rows, sl].astype(F32)
        v = v_ref[rows, sl]

        att = jnp.zeros((L, L), F32)
        seg = L
        while seg > GLA_SUBCHUNK:
            half = seg // 2
            beta = jnp.concatenate(
                [jnp.broadcast_to(b[s + half - 1:s + half, :], (seg, GLA_DK)) for s in range(0, L, seg)], axis=0)
            qs = (q * jnp.exp(jnp.minimum(b - beta, 0.0))).astype(BF16)
            ks = (k * jnp.exp(jnp.minimum(beta - b, 0.0))).astype(BF16)
            same = (row // seg) == (col // seg)
            mask = same & ((row % seg) >= half) & ((col % seg) < half)
            att = jnp.where(mask, _dot_nt(qs, ks), att)
            seg = half
        beta = jnp.concatenate(
            [jnp.zeros((GLA_SUBCHUNK, GLA_DK), F32)]
            + [jnp.broadcast_to(b[s - 1:s, :], (GLA_SUBCHUNK, GLA_DK)) for s in range(GLA_SUBCHUNK, L, GLA_SUBCHUNK)],
            axis=0)
        qs = (q * jnp.exp(b - beta)).astype(BF16)
        ks = (k * jnp.exp(beta - b)).astype(BF16)
        mask = ((row // GLA_SUBCHUNK) == (col // GLA_SUBCHUNK)) & (col <= row)
        att = jnp.where(mask, _dot_nt(qs, ks), att)

        state_t = st_ref[h]
        q0 = (q * jnp.exp(b)).astype(BF16)
        o = _dot(att.astype(BF16), v) + _dot_nt(q0, state_t.astype(BF16))

        b_last = b[L - 1:L, :]
        k_end = (k * jnp.exp(b_last - b)).astype(BF16)
        st_ref[h] = state_t * jnp.exp(b_last) + _dot_tn(v, k_end)

        rr = r_ref[rows, sl].astype(F32)
        o_ref[rows, sl] = (_rms(o, gn_ref[...]) * (rr * jax.nn.sigmoid(rr))).astype(BF16)


def _gla(proj, log_a, tri, gn, B, S):
    L = GLA_STEP_BLOCKS * GLA_BLOCK
    nb = S // L
    return pl.pallas_call(
        _gla_kernel,
        grid=(B, nb),
        in_specs=[
            pl.BlockSpec((L, HEAD_W), lambda b, n: (b * nb + n, 0)),
            pl.BlockSpec((L, HEAD_W), lambda b, n: (b * nb + n, 1)),
            pl.BlockSpec((L, HEAD_W), lambda b, n: (b * nb + n, 2)),
            pl.BlockSpec((L, HEAD_W), lambda b, n: (b * nb + n, 3)),
            pl.BlockSpec((L, HEAD_W), lambda b, n: (b * nb + n, 0)),
            pl.BlockSpec((GLA_BLOCK, GLA_BLOCK), lambda b, n: (0, 0)),
            pl.BlockSpec((1, GLA_DK), lambda b, n: (0, 0)),
        ],
        out_specs=pl.BlockSpec((L, HEAD_W), lambda b, n: (b * nb + n, 0)),
        out_shape=jax.ShapeDtypeStruct((B * S, HEAD_W), BF16),
        scratch_shapes=[pltpu.VMEM((GLA_HEADS, GLA_DK, GLA_DK), F32)],
        compiler_params=_params(("arbitrary", "arbitrary")),
        name="gla",
    )(proj, proj, proj, proj, log_a, tri, gn)


def _dil_kernel(q_ref, kp_ref, kc_ref, vp_ref, vc_ref, bias_ref, o_ref, st_ref, *, nq):
    first = pl.program_id(2) == 0
    Q = DIL_BLOCK
    col = lax.broadcasted_iota(jnp.int32, (Q, 2 * Q), 1)
    lane = lax.broadcasted_iota(jnp.int32, (Q, LANES), 1)
    for blk in range(nq):
        rows = slice(blk * Q, (blk + 1) * Q)
        prev = slice((blk - 1) * Q, blk * Q)
        stats = jnp.zeros((Q, LANES), F32)
        for h in range(DIL_HEADS):
            sl = _head(h)
            if blk == 0:
                kprev, vprev = kp_ref[0, 0, :, sl], vp_ref[0, 0, :, sl]
            else:
                kprev, vprev = kc_ref[0, 0, prev, sl], vc_ref[0, 0, prev, sl]
            kk = jnp.concatenate([kprev, kc_ref[0, 0, rows, sl]], axis=0)
            vv = jnp.concatenate([vprev, vc_ref[0, 0, rows, sl]], axis=0)
            s = _dot_nt(q_ref[0, 0, rows, sl], kk) + bias_ref[h]
            if blk == 0:
                s = jnp.where(jnp.logical_and(first, col < Q), NEG, s)
            m = jnp.max(s, axis=-1, keepdims=True)
            p = jnp.exp(s - m)
            den = jnp.sum(p, axis=-1, keepdims=True)
            o_ref[0, 0, rows, sl] = (_dot(p.astype(BF16), vv) / den).astype(BF16)
            stats = jnp.where((lane // 16) == h, m, stats)
            stats = jnp.where((lane // 16) == DIL_HEADS + h, den, stats)
        st_ref[0, 0, rows, :] = stats


def _dilated(qkv, bias, gi):
    B, d, Lsub, _ = qkv.shape
    nq = min(DIL_QBLOCKS, Lsub // DIL_BLOCK)
    rows = nq * DIL_BLOCK

    def cur(c):
        return pl.BlockSpec((1, 1, rows, HEAD_W), lambda b, r, i: (b, r, i, c))

    def prev(c):
        return pl.BlockSpec((1, 1, DIL_BLOCK, HEAD_W), lambda b, r, i: (b, r, jnp.maximum(i * nq - 1, 0), c))

    return pl.pallas_call(
        functools.partial(_dil_kernel, nq=nq),
        grid=(B, d, Lsub // rows),
        in_specs=[cur(0), prev(1), cur(1), prev(2), cur(2),
                  pl.BlockSpec((DIL_HEADS, DIL_BLOCK, 2 * DIL_BLOCK), lambda b, r, i: (0, 0, 0))],
        out_specs=[
            pl.BlockSpec((1, 1, rows, HEAD_W), lambda b, r, i: (b, r, i, 0)),
            pl.BlockSpec((1, 1, rows, LANES), lambda b, r, i: (b, r, i, 0)),
        ],
        out_shape=[
            jax.ShapeDtypeStruct((B, d, Lsub, HEAD_W), BF16),
            jax.ShapeDtypeStruct((B, d, Lsub, LANES), F32),
        ],
        compiler_params=_params(("arbitrary", "arbitrary", "arbitrary")),
        name=f"dilated{gi}",
    )(qkv, qkv, qkv, qkv, qkv, bias)


def _alibi_bias(gi):
    window, d = DIL_PATTERNS[gi]
    Q = DIL_BLOCK
    dist = np.arange(Q)[:, None] + Q - np.arange(2 * Q)[None, :]
    valid = (dist >= 0) & (dist <= window // d)
    slopes = np.asarray(ALIBI_SLOPES, np.float32).reshape(DIL_GROUPS, DIL_HEADS)[gi]
    bias = -slopes[:, None, None] * (d * dist).astype(np.float32)
    return jnp.asarray(np.where(valid[None], bias, np.float32(NEG)), F32)


def _merge_kernel(x_ref, og_ref, o0_ref, o1_ref, o2_ref, s0_ref, s1_ref, s2_ref, g1_ref, wbg_ref, bbg_ref,
                  wpg_ref, wpa_ref, wout_ref, g2_ref, wrh_ref, wrl_ref, br_ref, tri_ref,
                  x2_ref, h2_ref, route_ref, cnt_ref, carry_ref, obuf_ref, sbuf_ref):
    @pl.when(pl.program_id(0) == 0)
    def _():
        carry_ref[...] = jnp.zeros_like(carry_ref)

    x = x_ref[...]
    tm = x.shape[0]
    hb = _rms(x, g1_ref[...]).astype(BF16)
    gates = jax.nn.sigmoid(_dot(hb, wbg_ref[...]) + bbg_ref[...])

    for slot, (o_ref, s_ref) in enumerate(((o1_ref, s1_ref), (o2_ref, s2_ref))):
        d = DIL_PATTERNS[slot + 1][1]
        for r in range(d):
            sbuf_ref[slot, pl.ds(r, tm // d, stride=d), :] = s_ref[0, r]
            for c in range(DIL_HEADS):
                obuf_ref[slot, c, pl.ds(r, tm // d, stride=d), :] = o_ref[0, r, :, _head(c)].astype(F32)

    stats = (s0_ref[...], sbuf_ref[0], sbuf_ref[1])
    dens = [pltpu.roll(s, 64, 1) for s in stats]
    m_all = jnp.maximum(jnp.maximum(stats[0], stats[1]), stats[2])
    wts = [d * jnp.exp(s - m_all) for s, d in zip(stats, dens)]
    inv = 1.0 / (wts[0] + wts[1] + wts[2])
    coef = [w * inv for w in wts]
    heads = []
    for h in range(DIL_HEADS):
        group_out = (o0_ref[:, _head(h)].astype(F32), obuf_ref[0, h], obuf_ref[1, h])
        acc = jnp.zeros((tm, DIL_DH), F32)
        for g in range(DIL_GROUPS):
            c = jnp.broadcast_to(coef[g][:, 16 * h:16 * h + 1], (tm, DIL_DH))
            acc = acc + c * group_out[g]
        heads.append(acc.astype(BF16))
    o_att = jnp.concatenate(heads, axis=1)

    y = gates[:, :D_MODEL] * _dot(og_ref[...], wpg_ref[...]) + gates[:, D_MODEL:] * _dot(o_att, wpa_ref[...])
    x2 = x + _dot(y.astype(BF16), wout_ref[...])
    x2_ref[...] = x2

    h2 = _rms(x2, g2_ref[...])
    _store_rows(h2_ref, (), h2)

    h2h = h2.astype(BF16)
    h2l = (h2 - h2h.astype(F32)).astype(BF16)
    wrh = wrh_ref[...]
    logit = _dot(h2h, wrh) + _dot(h2l, wrh) + _dot(h2h, wrl_ref[...]) + br_ref[...]

    lane = lax.broadcasted_iota(jnp.int32, (tm, LANES), 1).astype(F32)
    big = jnp.float32(1e9)
    gl = jnp.where(lane < N_GROUPS, logit, NEG)
    gmax = jnp.max(gl, axis=-1, keepdims=True)
    gsel = jnp.min(jnp.where(gl == gmax, lane, big), axis=-1, keepdims=True)
    g_p = 1.0 / jnp.sum(jnp.exp(gl - gmax), axis=-1, keepdims=True)
    lo = N_GROUPS + EXPERTS_PER_GROUP * gsel
    el = jnp.where((lane >= lo) & (lane < lo + EXPERTS_PER_GROUP), logit, NEG)
    v1 = jnp.max(el, axis=-1, keepdims=True)
    i1 = jnp.min(jnp.where(el == v1, lane, big), axis=-1, keepdims=True)
    el2 = jnp.where(lane == i1, NEG, el)
    v2 = jnp.max(el2, axis=-1, keepdims=True)
    i2 = jnp.min(jnp.where(el2 == v2, lane, big), axis=-1, keepdims=True)
    ex = jnp.exp(v2 - v1)
    w1 = g_p / (1.0 + ex)
    w2 = g_p * ex / (1.0 + ex)
    e1 = i1 - N_GROUPS
    e2 = i2 - N_GROUPS

    oh1 = lane == e1
    oh2 = lane == e2
    onehot = jnp.where(oh1 | oh2, 1.0, 0.0)
    prefix = _dot(tri_ref[...], onehot.astype(BF16)) + carry_ref[...]
    r1 = jnp.sum(jnp.where(oh1, prefix, 0.0), axis=-1, keepdims=True)
    r2 = jnp.sum(jnp.where(oh2, prefix, 0.0), axis=-1, keepdims=True)
    carry = carry_ref[...] + jnp.sum(onehot, axis=0, keepdims=True)
    carry_ref[...] = carry
    cnt_ref[...] = jnp.broadcast_to(carry, cnt_ref.shape)

    route = jnp.zeros((tm, LANES), F32)
    for idx, val in enumerate((e1, e2, w1, w2, r1, r2)):
        route = jnp.where(lane == idx, val, route)
    route_ref[...] = route


def _merge(x2d, o_gla, outs, stats, g1, wbg, bbg, wpg, wpa, wout, g2, wrh, wrl, br, tri, S):
    T = x2d.shape[0]
    tm = min(MERGE_TM, S)
    tpb = S // tm
    tok = lambda w: pl.BlockSpec((tm, w), lambda i: (i, 0))
    full = lambda a: pl.BlockSpec(a.shape, lambda i: (0,) * a.ndim)

    def res(gi, w):
        d = DIL_PATTERNS[gi][1]
        return pl.BlockSpec((1, d, tm // d, w), lambda i: (i // tpb, 0, i % tpb, 0))

    return pl.pallas_call(
        _merge_kernel,
        grid=(T // tm,),
        in_specs=[tok(D_MODEL), tok(HEAD_W), tok(HEAD_W), res(1, HEAD_W), res(2, HEAD_W),
                  tok(LANES), res(1, LANES), res(2, LANES),
                  full(g1), full(wbg), full(bbg), full(wpg), full(wpa), full(wout), full(g2),
                  full(wrh), full(wrl), full(br), full(tri)],
        out_specs=[tok(D_MODEL), pl.BlockSpec((tm * ROW_SUBLANES, LANES), lambda i: (i, 0)), tok(LANES),
                   pl.BlockSpec((8, LANES), lambda i: (0, 0))],
        out_shape=[
            jax.ShapeDtypeStruct((T, D_MODEL), F32),
            jax.ShapeDtypeStruct((T * ROW_SUBLANES, LANES), F32),
            jax.ShapeDtypeStruct((T, LANES), F32),
            jax.ShapeDtypeStruct((8, LANES), F32),
        ],
        scratch_shapes=[pltpu.VMEM((1, LANES), F32),
                        pltpu.VMEM((DIL_GROUPS - 1, DIL_HEADS, tm, LANES), F32),
                        pltpu.VMEM((DIL_GROUPS - 1, tm, LANES), F32)],
        compiler_params=_params(("arbitrary",)),
        name="merge",
    )(x2d, o_gla, *outs, *stats, g1, wbg, bbg, wpg, wpa, wout, g2, wrh, wrl, br, tri)


def _row_tile(ref, row):
    return ref.at[pl.ds(pl.multiple_of(row * ROW_SUBLANES, ROW_SUBLANES), ROW_SUBLANES), :]


def _dispatch_kernel(pad_lo_ref, pad_hi_ref, dest_ref, h2_ref, xin_ref, inv_ref, zero_ref, sem, zsem):
    i = pl.program_id(0)
    tm = h2_ref.shape[0] // ROW_SUBLANES
    R = MOE_TM

    def start(r, c):
        for k in range(TOP_K):
            d = dest_ref[0, 0, TOP_K * r + k]
            pltpu.make_async_copy(_row_tile(h2_ref, r), _row_tile(xin_ref, d), sem).start()
            inv_ref[d] = (i * tm + r) * TOP_K + k
        return c

    lax.fori_loop(0, tm, start, 0, unroll=ROW_UNROLL)
    for k in range(TOP_K):
        pltpu.make_async_copy(h2_ref, xin_ref.at[pl.ds(0, tm * ROW_SUBLANES), :], sem).wait()

    @pl.when(i == pl.num_programs(0) - 1)
    def _():
        zero_ref[...] = jnp.zeros_like(zero_ref)

        def fill_row(p, c):
            pltpu.make_async_copy(_row_tile(zero_ref, 0), _row_tile(xin_ref, p), zsem).start()
            inv_ref[p] = -1
            return c

        def drain_row(p, c):
            pltpu.make_async_copy(_row_tile(zero_ref, 0), _row_tile(xin_ref, 0), zsem).wait()
            return c

        for e in range(N_EXPERTS):
            lax.fori_loop(pad_lo_ref[e], pad_hi_ref[e], fill_row, 0)
            lax.fori_loop(pad_lo_ref[e], pad_hi_ref[e], drain_row, 0)

        def tail_tile(j):
            return xin_ref.at[pl.ds(pl.multiple_of(j * R * ROW_SUBLANES, R * ROW_SUBLANES), R * ROW_SUBLANES), :]

        def fill_tile(j, c):
            pltpu.make_async_copy(zero_ref, tail_tile(j), zsem).start()
            return c

        def drain_tile(j, c):
            pltpu.make_async_copy(zero_ref, tail_tile(0), zsem).wait()
            return c

        def mark(p, c):
            inv_ref[p] = -1
            return c

        lo, hi = pad_lo_ref[N_EXPERTS], pad_hi_ref[N_EXPERTS]
        lax.fori_loop(lo // R, hi // R, fill_tile, 0)
        lax.fori_loop(lo // R, hi // R, drain_tile, 0)
        lax.fori_loop(lo, hi, mark, 0)


def _dispatch(pad_lo, pad_hi, dest, h2_rows, P):
    T = h2_rows.shape[0] // ROW_SUBLANES
    tm = min(ROW_TM, T)
    return pl.pallas_call(
        _dispatch_kernel,
        grid_spec=pltpu.PrefetchScalarGridSpec(
            num_scalar_prefetch=2,
            grid=(T // tm,),
            in_specs=[
                pl.BlockSpec((1, 1, TOP_K * tm), lambda i, lo, hi: (i, 0, 0), memory_space=pltpu.SMEM),
                pl.BlockSpec((tm * ROW_SUBLANES, LANES), lambda i, lo, hi: (i, 0)),
            ],
            out_specs=[pl.BlockSpec(memory_space=pl.ANY), pl.BlockSpec(memory_space=pltpu.SMEM)],
            scratch_shapes=[pltpu.VMEM((MOE_TM * ROW_SUBLANES, LANES), F32),
                            pltpu.SemaphoreType.DMA, pltpu.SemaphoreType.DMA],
        ),
        out_shape=[jax.ShapeDtypeStruct((P * ROW_SUBLANES, LANES), F32), jax.ShapeDtypeStruct((P,), jnp.int32)],
        compiler_params=_params(("arbitrary",)),
        name="dispatch",
    )(pad_lo, pad_hi, dest.reshape(T // tm, 1, TOP_K * tm), h2_rows)


def _expert_kernel(tile_e_ref, nact_ref, dst_prev_ref, x_ref, wg_ref, wu_ref, wd_ref, slots_ref, ybuf_ref, ssem):
    del tile_e_ref
    i = pl.program_id(0)
    nact = nact_ref[0]
    slot = lax.rem(i, 2)
    other = 1 - slot
    R = MOE_TM

    def scatter(dst_ref, s):
        for q in range(R):
            row = pl.multiple_of(dst_ref[0, 0, q], ROW_SUBLANES)
            pltpu.make_async_copy(ybuf_ref.at[s, pl.ds(q * ROW_SUBLANES, ROW_SUBLANES), :],
                                  slots_ref.at[pl.ds(row, ROW_SUBLANES), :], ssem.at[s]).start()

    def wait_scatter(s):
        pltpu.make_async_copy(ybuf_ref.at[s], slots_ref.at[pl.ds(0, R * ROW_SUBLANES), :], ssem.at[s]).wait()

    @pl.when(i == 0)
    def _():
        ybuf_ref[...] = jnp.zeros_like(ybuf_ref)

    @pl.when(i <= nact)
    def _():
        @pl.when(i >= 1)
        def _():
            wait_scatter(slot)

        scatter(dst_prev_ref, other)
        xb = _load_rows(x_ref, (), R).astype(BF16)
        g = _dot(xb, wg_ref[0].astype(BF16))
        u = _dot(xb, wu_ref[0].astype(BF16))
        a = (g * jax.nn.sigmoid(g) * u).astype(BF16)
        _store_rows(ybuf_ref, (slot,), _dot(a, wd_ref[0].astype(BF16)))

    @pl.when(i == nact + 1)
    def _():
        wait_scatter(slot)


def _experts(tile_e, nact, dst, xin, wg, wu, wd, n_slot_rows):
    ntiles = tile_e.shape[0]
    R = MOE_TM

    def rows(i, te, na):
        return (jnp.minimum(i, na[0] - 1), 0)

    def wsel(i, te, na):
        return (te[jnp.minimum(i, na[0] - 1)], 0, 0)

    return pl.pallas_call(
        _expert_kernel,
        grid_spec=pltpu.PrefetchScalarGridSpec(
            num_scalar_prefetch=2,
            grid=(ntiles + 2,),
            in_specs=[
                pl.BlockSpec((1, 1, R), lambda i, te, na: (i, 0, 0), memory_space=pltpu.SMEM),
                pl.BlockSpec((R * ROW_SUBLANES, LANES), rows),
                pl.BlockSpec((1, D_MODEL, D_EXPERT), wsel),
                pl.BlockSpec((1, D_MODEL, D_EXPERT), wsel),
                pl.BlockSpec((1, D_EXPERT, D_MODEL), wsel),
            ],
            out_specs=pl.BlockSpec(memory_space=pl.ANY),
            scratch_shapes=[
                pltpu.VMEM((2, R * ROW_SUBLANES, LANES), F32),
                pltpu.SemaphoreType.DMA((2,)),
            ],
        ),
        out_shape=jax.ShapeDtypeStruct((n_slot_rows * ROW_SUBLANES, LANES), F32),
        compiler_params=_params(("arbitrary",)),
        name="experts",
    )(tile_e, nact, dst, xin, wg, wu, wd)


def _combine_kernel(x2_ref, route_ref, s0_ref, s1_ref, out_ref):
    tm = x2_ref.shape[0]
    route = route_ref[...]
    w1 = route[:, 2:3]
    w2 = route[:, 3:4]
    out_ref[...] = x2_ref[...] + w1 * _load_rows(s0_ref, (), tm) + w2 * _load_rows(s1_ref, (), tm)


def _combine(x2, route, slots):
    T = x2.shape[0]
    tm = min(ROW_TM, T)
    nt = T // tm
    return pl.pallas_call(
        _combine_kernel,
        grid=(nt,),
        in_specs=[
            pl.BlockSpec((tm, D_MODEL), lambda i: (i, 0)),
            pl.BlockSpec((tm, LANES), lambda i: (i, 0)),
            pl.BlockSpec((tm * ROW_SUBLANES, LANES), lambda i: (i, 0)),
            pl.BlockSpec((tm * ROW_SUBLANES, LANES), lambda i: (nt + i, 0)),
        ],
        out_specs=pl.BlockSpec((tm, D_MODEL), lambda i: (i, 0)),
        out_shape=jax.ShapeDtypeStruct((T, D_MODEL), F32),
        compiler_params=_params(("arbitrary",)),
        name="combine",
    )(x2, route, slots, slots)


def _layer(x2d, B, S, norm1_g, w_in, w_gla_a2, b_gla_a, gla_out_norm_g, dil_q_norm_g, dil_k_norm_g,
           w_proj_gla, w_proj_attn, w_branch_gate, b_branch_gate, w_out, norm2_g,
           w_router_group, b_router_group, w_router_expert, b_router_expert, w_gate, w_up, w_down):
    T = B * S
    n_gla = N_GLA_TILES * HEAD_W
    w_main = jnp.concatenate([w_in[:, :n_gla], w_in[:, n_gla + GLA_RANK:]], axis=1).astype(BF16)
    w_ga = jnp.pad(w_in[:, n_gla:n_gla + GLA_RANK], ((0, 0), (0, LANES - GLA_RANK))).astype(BF16)
    w_a2 = jnp.pad(w_gla_a2, ((0, LANES - GLA_RANK), (0, 0))).astype(BF16)
    gains = [jnp.ones((HEAD_W,), F32)] * N_GLA_TILES
    for gi in range(DIL_GROUPS):
        gains += [jnp.tile(dil_q_norm_g[gi], DIL_HEADS) * (DIL_DH ** -0.5),
                  jnp.tile(dil_k_norm_g[gi], DIL_HEADS), jnp.ones((HEAD_W,), F32)]
    qk_gain = jnp.stack(gains).reshape(N_COL_TILES, 1, HEAD_W)

    gla_in, log_a, qkv0, qkv1, qkv2 = _inproj(
        x2d, norm1_g.reshape(1, -1), w_main, w_ga, w_a2, b_gla_a.reshape(1, -1), qk_gain, B, S)

    tri_incl = jnp.asarray(np.tril(np.ones((GLA_BLOCK, GLA_BLOCK), np.float32)), BF16)
    o_gla = _gla(gla_in, log_a, tri_incl, gla_out_norm_g.reshape(1, -1), B, S)

    outs, stats = [], []
    for gi, qkv in enumerate((qkv0.reshape(B, 1, S, QKV_W), qkv1, qkv2)):
        o, st = _dilated(qkv, _alibi_bias(gi), gi)
        outs.append(o)
        stats.append(st)
    outs[0] = outs[0].reshape(T, HEAD_W)
    stats[0] = stats[0].reshape(T, LANES)

    w_r = jnp.pad(jnp.concatenate([w_router_group, w_router_expert], axis=1),
                  ((0, 0), (0, LANES - N_GROUPS - N_EXPERTS)))
    w_rh = w_r.astype(BF16)
    w_rl = (w_r - w_rh.astype(F32)).astype(BF16)
    b_r = jnp.pad(jnp.concatenate([b_router_group, b_router_expert]), (0, LANES - N_GROUPS - N_EXPERTS))
    tm = min(MERGE_TM, S)
    tri_strict = jnp.asarray(np.tril(np.ones((tm, tm), np.float32), -1), BF16)
    x2, h2, route, cnt = _merge(
        x2d, o_gla, outs, stats, norm1_g.reshape(1, -1), w_branch_gate.astype(BF16),
        b_branch_gate.reshape(1, -1), w_proj_gla.astype(BF16), w_proj_attn.astype(BF16), w_out.astype(BF16),
        norm2_g.reshape(1, -1), w_rh, w_rl, b_r.reshape(1, -1), tri_strict, S)

    counts = cnt[0, :N_EXPERTS].astype(jnp.int32)
    pcounts = (counts + MOE_TM - 1) // MOE_TM * MOE_TM
    pends = jnp.cumsum(pcounts)
    pstarts = pends - pcounts
    eid = route[:, 0:TOP_K].astype(jnp.int32)
    rank = route[:, 4:4 + TOP_K].astype(jnp.int32)
    onehot = eid[:, :, None] == jnp.arange(N_EXPERTS, dtype=jnp.int32)[None, None, :]
    dest = rank + jnp.sum(jnp.where(onehot, pstarts[None, None, :], 0), axis=-1)
    P = T * TOP_K + N_EXPERTS * MOE_TM
    ntiles = P // MOE_TM
    tile_start = jnp.arange(ntiles, dtype=jnp.int32) * MOE_TM
    tile_e = jnp.minimum(jnp.sum(pends[None, :] <= tile_start[:, None], axis=1), N_EXPERTS - 1).astype(jnp.int32)
    nact = (pends[-1:] // MOE_TM).astype(jnp.int32)

    pad_lo = jnp.concatenate([pstarts + counts, pends[-1:]]).astype(jnp.int32)
    pad_hi = jnp.concatenate([pends, jnp.full((1,), P, jnp.int32)]).astype(jnp.int32)
    xin, inv = _dispatch(pad_lo, pad_hi, dest, h2, P)

    pad_tile = jnp.full((MOE_TM,), -1, jnp.int32)
    inv_dst = jnp.concatenate([pad_tile, inv, pad_tile])
    spare = T * TOP_K + jnp.tile(jnp.arange(MOE_TM, dtype=jnp.int32), ntiles + 2)
    dst = jnp.where(inv_dst >= 0, (inv_dst % TOP_K) * T + inv_dst // TOP_K, spare) * ROW_SUBLANES
    slots = _experts(tile_e, nact, dst.reshape(ntiles + 2, 1, MOE_TM), xin,
                     w_gate, w_up, w_down, T * TOP_K + MOE_TM)
    return _combine(x2, route, slots)


def kernel(x, norm1_g, w_in, w_gla_a2, b_gla_a, gla_out_norm_g, dil_q_norm_g, dil_k_norm_g, w_proj_gla,
           w_proj_attn, w_branch_gate, b_branch_gate, w_out, norm2_g, w_router_group, b_router_group,
           w_router_expert, b_router_expert, w_gate, w_up, w_down):
    B, S, D = x.shape
    assert D == D_MODEL and S % (DIL_BLOCK * DIL_PATTERNS[-1][1]) == 0
    x2d = x.reshape(B * S, D)
    params = (norm1_g, w_in, w_gla_a2, b_gla_a, gla_out_norm_g, dil_q_norm_g, dil_k_norm_g, w_proj_gla,
              w_proj_attn, w_branch_gate, b_branch_gate, w_out, norm2_g, w_router_group, b_router_group,
              w_router_expert, b_router_expert, w_gate, w_up, w_down)
    for layer in range(norm1_g.shape[0]):
        x2d = _layer(x2d, B, S, *(p[layer] for p in params))
    return x2d.reshape(B, S, D)
```

```python
import functools

import jax
import jax.numpy as jnp
import numpy as np
from jax import lax
from jax.experimental import pallas as pl
from jax.experimental.pallas import tpu as pltpu

F32 = jnp.float32
BF16 = jnp.bfloat16

D_MODEL = 1024
EPS = 1e-6
GLA_HEADS = 4
GLA_DK = 128
GLA_RANK = 16
GLA_TAU = 16.0
GLA_SUBCHUNK = 16
GLA_BLOCK = 128
GLA_STEP_BLOCKS = 4
DIL_PATTERNS = ((128, 1), (512, 4), (2048, 16))
DIL_GROUPS = 3
DIL_HEADS = 4
DIL_DH = 128
DIL_BLOCK = 128
ALIBI_SLOPES = tuple(2.0 ** (-8.0 * (i + 1) / (DIL_GROUPS * DIL_HEADS)) for i in range(DIL_GROUPS * DIL_HEADS))
N_GROUPS = 4
EXPERTS_PER_GROUP = 8
N_EXPERTS = N_GROUPS * EXPERTS_PER_GROUP
TOP_K = 2
D_EXPERT = 512

HEAD_W = 512
N_GLA_TILES = 4
N_COL_TILES = N_GLA_TILES + 3 * DIL_GROUPS
QKV_W = 3 * HEAD_W
LANES = 128
NEG = -1e30

INPROJ_TM = 512
MERGE_TM = 512
MERGE_SPLIT = 1
MOE_TM = 512
ROW_TM = 512
ROW_UNROLL = 8
DIL_QBLOCKS = 4
VMEM_LIMIT = 56 * 1024 * 1024


def _dot(a, b):
    return jnp.dot(a, b, preferred_element_type=F32)


def _dot_nt(a, b):
    return lax.dot_general(a, b, (((1,), (1,)), ((), ())), preferred_element_type=F32)


def _dot_tn(a, b):
    return lax.dot_general(a, b, (((0,), (0,)), ((), ())), preferred_element_type=F32)


def _rms(x, g):
    return x * lax.rsqrt(jnp.mean(x * x, axis=-1, keepdims=True) + EPS) * g


def _params(sem):
    return pltpu.CompilerParams(dimension_semantics=sem, vmem_limit_bytes=VMEM_LIMIT)


def _head(c):
    return slice(c * LANES, (c + 1) * LANES)


ROW_SUBLANES = 8


def _load_rows(ref, lead, n):
    return jnp.concatenate([ref[lead + (pl.ds(c, n, stride=ROW_SUBLANES), slice(None))]
                            for c in range(ROW_SUBLANES)], axis=1)


def _store_rows(ref, lead, val, row0=0):
    n = val.shape[0]
    for c in range(ROW_SUBLANES):
        ref[lead + (pl.ds(row0 * ROW_SUBLANES + c, n, stride=ROW_SUBLANES), slice(None))] = val[:, _head(c)]


def _inproj_kernel(x_ref, g1_ref, w_ref, wga_ref, wa2_ref, ba_ref, qkg_ref,
                   gla_ref, loga_ref, d0_ref, d1_ref, d2_ref, h_ref, ybuf_ref):
    tm = x_ref.shape[0]
    hb = _rms(x_ref[...], g1_ref[...]).astype(BF16)
    h_ref[...] = hb
    ga = _dot(hb, wga_ref[...])
    z = _dot(ga.astype(BF16), wa2_ref[...]) + ba_ref[...]
    log_sig = jnp.minimum(z, 0.0) - jnp.log(1.0 + jnp.exp(-jnp.abs(z)))
    loga_ref[...] = log_sig * (1.0 / GLA_TAU)

    dil_refs = (d0_ref, d1_ref, d2_ref)
    n_strided = 0
    for jj in range(N_COL_TILES):
        cols = slice(jj * HEAD_W, (jj + 1) * HEAD_W)
        y = _dot(h_ref[...], w_ref[:, cols])
        if jj < N_GLA_TILES:
            gla_ref[:, cols] = y.astype(BF16)
            continue
        gi, kind = divmod(jj - N_GLA_TILES, 3)
        d = DIL_PATTERNS[gi][1]
        ocols = [slice(kind * HEAD_W + c * LANES, kind * HEAD_W + (c + 1) * LANES) for c in range(DIL_HEADS)]
        if kind < 2:
            g = qkg_ref[jj]
            slabs = [_rms(y[:, _head(c)], g[:, _head(c)]) for c in range(DIL_HEADS)]
        else:
            slabs = [y[:, _head(c)] for c in range(DIL_HEADS)]
        out = dil_refs[gi]
        if d == 1:
            for c in range(DIL_HEADS):
                out[:, ocols[c]] = slabs[c].astype(BF16)
            continue
        buf = n_strided % ybuf_ref.shape[0]
        n_strided += 1
        for c in range(DIL_HEADS):
            ybuf_ref[buf, c] = slabs[c]
        for r in range(d):
            for c in range(DIL_HEADS):
                out[0, r, :, ocols[c]] = ybuf_ref[buf, c, pl.ds(r, tm // d, stride=d), :].astype(BF16)


def _resident(shape):
    return pl.BlockSpec(shape, lambda i: (0,) * len(shape), pipeline_mode=pl.Buffered(1))


def _inproj(x2d, g1, w_main, w_ga, w_a2, b_a, qk_gain, B, S):
    T = x2d.shape[0]
    tm = min(INPROJ_TM, S)
    tpb = S // tm

    def dil_spec(gi):
        d = DIL_PATTERNS[gi][1]
        if d == 1:
            return pl.BlockSpec((tm, QKV_W), lambda i: (i, 0))
        return pl.BlockSpec((1, d, tm // d, QKV_W), lambda i: (i // tpb, 0, i % tpb, 0))

    def dil_shape(gi):
        d = DIL_PATTERNS[gi][1]
        shape = (T, QKV_W) if d == 1 else (B, d, S // d, QKV_W)
        return jax.ShapeDtypeStruct(shape, BF16)

    return pl.pallas_call(
        _inproj_kernel,
        grid=(T // tm,),
        in_specs=[
            pl.BlockSpec((tm, D_MODEL), lambda i: (i, 0)),
            _resident(g1.shape), _resident(w_main.shape), _resident(w_ga.shape), _resident(w_a2.shape),
            _resident(b_a.shape), _resident(qk_gain.shape),
        ],
        out_specs=[
            pl.BlockSpec((tm, N_GLA_TILES * HEAD_W), lambda i: (i, 0)),
            pl.BlockSpec((tm, HEAD_W), lambda i: (i, 0)),
            dil_spec(0), dil_spec(1), dil_spec(2),
        ],
        out_shape=[
            jax.ShapeDtypeStruct((T, N_GLA_TILES * HEAD_W), BF16),
            jax.ShapeDtypeStruct((T, HEAD_W), F32),
            dil_shape(0), dil_shape(1), dil_shape(2),
        ],
        scratch_shapes=[pltpu.VMEM((tm, D_MODEL), BF16), pltpu.VMEM((2, DIL_HEADS, tm, LANES), F32)],
        compiler_params=_params(("arbitrary",)),
        name="inproj",
    )(x2d, g1, w_main, w_ga, w_a2, b_a, qk_gain)


def _gla_kernel(q_ref, k_ref, v_ref, r_ref, la_ref, tri_ref, gn_ref, o_ref, st_ref):
    n = pl.program_id(1)

    @pl.when(n == 0)
    def _():
        st_ref[...] = jnp.zeros_like(st_ref)

    for blk in range(q_ref.shape[0] // GLA_BLOCK):
        _gla_block(q_ref, k_ref, v_ref, r_ref, la_ref, tri_ref, gn_ref, o_ref, st_ref,
                   slice(blk * GLA_BLOCK, (blk + 1) * GLA_BLOCK))


def _gla_block(q_ref, k_ref, v_ref, r_ref, la_ref, tri_ref, gn_ref, o_ref, st_ref, rows):
    L = GLA_BLOCK
    la = la_ref[rows, :]
    hi = la.astype(BF16)
    r1 = la - hi.astype(F32)
    mid = r1.astype(BF16)
    lo = (r1 - mid.astype(F32)).astype(BF16)
    tri = tri_ref[...]
    b_all = _dot(tri, hi) + _dot(tri, mid) + _dot(tri, lo)

    row = lax.broadcasted_iota(jnp.int32, (L, L), 0)
    col = lax.broadcasted_iota(jnp.int32, (L, L), 1)

    for h in range(GLA_HEADS):
        sl = _head(h)
        b = b_all[:, sl]
        q = q_ref[rows, sl].astype(F32) * (GLA_DK ** -0.5)
        k = k_ref[rows, sl].astype(F32)
        v = v_ref[rows, sl]

        att = jnp.zeros((L, L), F32)
        seg = L
        while seg > GLA_SUBCHUNK:
            half = seg // 2
            beta = jnp.concatenate(
                [jnp.broadcast_to(b[s + half - 1:s + half, :], (seg, GLA_DK)) for s in range(0, L, seg)], axis=0)
            qs = (q * jnp.exp(jnp.minimum(b - beta, 0.0))).astype(BF16)
            ks = (k * jnp.exp(jnp.minimum(beta - b, 0.0))).astype(BF16)
            same = (row // seg) == (col // seg)
            mask = same & ((row % seg) >= half) & ((col % seg) < half)
            att = jnp.where(mask, _dot_nt(qs, ks), att)
            seg = half
        beta = jnp.concatenate(
            [jnp.zeros((GLA_SUBCHUNK, GLA_DK), F32)]
            + [jnp.broadcast_to(b[s - 1:s, :], (GLA_SUBCHUNK, GLA_DK)) for s in range(GLA_SUBCHUNK, L, GLA_SUBCHUNK)],
            axis=0)
        qs = (q * jnp.exp(b - beta)).astype(BF16)
        ks = (k * jnp.exp(beta - b)).astype(BF16)
        mask = ((row // GLA_SUBCHUNK) == (col // GLA_SUBCHUNK)) & (col <= row)
        att = jnp.where(mask, _dot_nt(qs, ks), att)

        state_t = st_ref[h]
        q0 = (q * jnp.exp(b)).astype(BF16)
        o = _dot(att.astype(BF16), v) + _dot_nt(q0, state_t.astype(BF16))

        b_last = b[L - 1:L, :]
        k_end = (k * jnp.exp(b_last - b)).astype(BF16)
        st_ref[h] = state_t * jnp.exp(b_last) + _dot_tn(v, k_end)

        rr = r_ref[rows, sl].astype(F32)
        o_ref[rows, sl] = (_rms(o, gn_ref[...]) * (rr * jax.nn.sigmoid(rr))).astype(BF16)


def _gla(proj, log_a, tri, gn, B, S):
    L = GLA_STEP_BLOCKS * GLA_BLOCK
    nb = S // L
    return pl.pallas_call(
        _gla_kernel,
        grid=(B, nb),
        in_specs=[
            pl.BlockSpec((L, HEAD_W), lambda b, n: (b * nb + n, 0)),
            pl.BlockSpec((L, HEAD_W), lambda b, n: (b * nb + n, 1)),
            pl.BlockSpec((L, HEAD_W), lambda b, n: (b * nb + n, 2)),
            pl.BlockSpec((L, HEAD_W), lambda b, n: (b * nb + n, 3)),
            pl.BlockSpec((L, HEAD_W), lambda b, n: (b * nb + n, 0)),
            pl.BlockSpec((GLA_BLOCK, GLA_BLOCK), lambda b, n: (0, 0)),
            pl.BlockSpec((1, GLA_DK), lambda b, n: (0, 0)),
        ],
        out_specs=pl.BlockSpec((L, HEAD_W), lambda b, n: (b * nb + n, 0)),
        out_shape=jax.ShapeDtypeStruct((B * S, HEAD_W), BF16),
        scratch_shapes=[pltpu.VMEM((GLA_HEADS, GLA_DK, GLA_DK), F32)],
        compiler_params=_params(("arbitrary", "arbitrary")),
        name="gla",
    )(proj, proj, proj, proj, log_a, tri, gn)


def _dil_kernel(q_ref, kp_ref, kc_ref, vp_ref, vc_ref, bias_ref, o_ref, st_ref, *, nq):
    first = pl.program_id(2) == 0
    Q = DIL_BLOCK
    col = lax.broadcasted_iota(jnp.int32, (Q, 2 * Q), 1)
    lane = lax.broadcasted_iota(jnp.int32, (Q, LANES), 1)
    for blk in range(nq):
        rows = slice(blk * Q, (blk + 1) * Q)
        prev = slice((blk - 1) * Q, blk * Q)
        stats = jnp.zeros((Q, LANES), F32)
        for h in range(DIL_HEADS):
            sl = _head(h)
            if blk == 0:
                kprev, vprev = kp_ref[0, 0, :, sl], vp_ref[0, 0, :, sl]
            else:
                kprev, vprev = kc_ref[0, 0, prev, sl], vc_ref[0, 0, prev, sl]
            kk = jnp.concatenate([kprev, kc_ref[0, 0, rows, sl]], axis=0)
            vv = jnp.concatenate([vprev, vc_ref[0, 0, rows, sl]], axis=0)
            s = _dot_nt(q_ref[0, 0, rows, sl], kk) + bias_ref[h]
            if blk == 0:
                s = jnp.where(jnp.logical_and(first, col < Q), NEG, s)
            m = jnp.max(s, axis=-1, keepdims=True)
            p = jnp.exp(s - m)
            den = jnp.sum(p, axis=-1, keepdims=True)
            o_ref[0, 0, rows, sl] = (_dot(p.astype(BF16), vv) / den).astype(BF16)
            stats = jnp.where((lane // 16) == h, m, stats)
            stats = jnp.where((lane // 16) == DIL_HEADS + h, den, stats)
        st_ref[0, 0, rows, :] = stats


def _dilated(qkv, bias, gi):
    B, d, Lsub, _ = qkv.shape
    nq = min(DIL_QBLOCKS, Lsub // DIL_BLOCK)
    rows = nq * DIL_BLOCK

    def cur(c):
        return pl.BlockSpec((1, 1, rows, HEAD_W), lambda b, r, i: (b, r, i, c))

    def prev(c):
        return pl.BlockSpec((1, 1, DIL_BLOCK, HEAD_W), lambda b, r, i: (b, r, jnp.maximum(i * nq - 1, 0), c))

    return pl.pallas_call(
        functools.partial(_dil_kernel, nq=nq),
        grid=(B, d, Lsub // rows),
        in_specs=[cur(0), prev(1), cur(1), prev(2), cur(2),
                  pl.BlockSpec((DIL_HEADS, DIL_BLOCK, 2 * DIL_BLOCK), lambda b, r, i: (0, 0, 0))],
        out_specs=[
            pl.BlockSpec((1, 1, rows, HEAD_W), lambda b, r, i: (b, r, i, 0)),
            pl.BlockSpec((1, 1, rows, LANES), lambda b, r, i: (b, r, i, 0)),
        ],
        out_shape=[
            jax.ShapeDtypeStruct((B, d, Lsub, HEAD_W), BF16),
            jax.ShapeDtypeStruct((B, d, Lsub, LANES), F32),
        ],
        compiler_params=_params(("arbitrary", "arbitrary", "arbitrary")),
        name=f"dilated{gi}",
    )(qkv, qkv, qkv, qkv, qkv, bias)


def _alibi_bias(gi):
    window, d = DIL_PATTERNS[gi]
    Q = DIL_BLOCK
    dist = np.arange(Q)[:, None] + Q - np.arange(2 * Q)[None, :]
    valid = (dist >= 0) & (dist <= window // d)
    slopes = np.asarray(ALIBI_SLOPES, np.float32).reshape(DIL_GROUPS, DIL_HEADS)[gi]
    bias = -slopes[:, None, None] * (d * dist).astype(np.float32)
    return jnp.asarray(np.where(valid[None], bias, np.float32(NEG)), F32)


def _merge_rows(x_ref, og_ref, o0_ref, s0_ref, g1_ref, wbg_ref, bbg_ref, wpg_ref, wpa_ref, wout_ref,
                g2_ref, wr_ref, br_ref, x2_ref, h2_ref, obuf_ref, sbuf_ref, row0, n):
    rows = slice(row0, row0 + n)
    x = x_ref[rows, :]
    hb = _rms(x, g1_ref[...]).astype(BF16)
    gates = jax.nn.sigmoid(_dot(hb, wbg_ref[...]) + bbg_ref[...])

    stats = (s0_ref[rows, :], sbuf_ref[0, rows, :], sbuf_ref[1, rows, :])
    dens = [pltpu.roll(s, 64, 1) for s in stats]
    m_all = jnp.maximum(jnp.maximum(stats[0], stats[1]), stats[2])
    wts = [d * jnp.exp(s - m_all) for s, d in zip(stats, dens)]
    inv = 1.0 / (wts[0] + wts[1] + wts[2])
    coef = [w * inv for w in wts]
    heads = []
    for h in range(DIL_HEADS):
        group_out = (o0_ref[rows, _head(h)].astype(F32), obuf_ref[0, h, rows, :], obuf_ref[1, h, rows, :])
        acc = jnp.zeros((n, DIL_DH), F32)
        for g in range(DIL_GROUPS):
            c = jnp.broadcast_to(coef[g][:, 16 * h:16 * h + 1], (n, DIL_DH))
            acc = acc + c * group_out[g]
        heads.append(acc.astype(BF16))
    o_att = jnp.concatenate(heads, axis=1)

    y = gates[:, :D_MODEL] * _dot(og_ref[rows, :], wpg_ref[...]) + gates[:, D_MODEL:] * _dot(o_att, wpa_ref[...])
    x2 = x + _dot(y.astype(BF16), wout_ref[...])
    x2_ref[rows, :] = x2

    h2 = _rms(x2, g2_ref[...])
    _store_rows(h2_ref, (), h2, row0)

    h2h = h2.astype(BF16)
    h2l = (h2 - h2h.astype(F32)).astype(BF16)
    wr = wr_ref[...]
    pa = _dot(h2h, wr)
    pb = _dot(h2l, wr)
    logit = pa[:, :LANES] + pa[:, LANES:] + pb[:, :LANES] + pb[:, LANES:] + br_ref[...]

    lane = lax.broadcasted_iota(jnp.int32, (n, LANES), 1).astype(F32)
    big = jnp.float32(1e9)
    gl = jnp.where(lane < N_GROUPS, logit, NEG)
    gmax = jnp.max(gl, axis=-1, keepdims=True)
    gsel = jnp.min(jnp.where(gl == gmax, lane, big), axis=-1, keepdims=True)
    g_p = 1.0 / jnp.sum(jnp.exp(gl - gmax), axis=-1, keepdims=True)
    lo = N_GROUPS + EXPERTS_PER_GROUP * gsel
    el = jnp.where((lane >= lo) & (lane < lo + EXPERTS_PER_GROUP), logit, NEG)
    v1 = jnp.max(el, axis=-1, keepdims=True)
    i1 = jnp.min(jnp.where(el == v1, lane, big), axis=-1, keepdims=True)
    el2 = jnp.where(lane == i1, NEG, el)
    v2 = jnp.max(el2, axis=-1, keepdims=True)
    i2 = jnp.min(jnp.where(el2 == v2, lane, big), axis=-1, keepdims=True)
    ex = jnp.exp(v2 - v1)
    w1 = g_p / (1.0 + ex)
    w2 = g_p * ex / (1.0 + ex)
    return i1 - N_GROUPS, i2 - N_GROUPS, w1, w2


def _merge_kernel(x_ref, og_ref, o0_ref, o1_ref, o2_ref, s0_ref, s1_ref, s2_ref, g1_ref, wbg_ref, bbg_ref,
                  wpg_ref, wpa_ref, wout_ref, g2_ref, wr_ref, br_ref, tri_ref,
                  x2_ref, h2_ref, route_ref, cnt_ref, carry_ref, obuf_ref, sbuf_ref):
    @pl.when(pl.program_id(0) == 0)
    def _():
        carry_ref[...] = jnp.zeros_like(carry_ref)

    tm = x_ref.shape[0]

    for slot, (o_ref, s_ref) in enumerate(((o1_ref, s1_ref), (o2_ref, s2_ref))):
        d = DIL_PATTERNS[slot + 1][1]
        for r in range(d):
            sbuf_ref[slot, pl.ds(r, tm // d, stride=d), :] = s_ref[0, r]
            for c in range(DIL_HEADS):
                obuf_ref[slot, c, pl.ds(r, tm // d, stride=d), :] = o_ref[0, r, :, _head(c)].astype(F32)

    sub = tm // MERGE_SPLIT
    parts = [_merge_rows(x_ref, og_ref, o0_ref, s0_ref, g1_ref, wbg_ref, bbg_ref, wpg_ref, wpa_ref, wout_ref,
                         g2_ref, wr_ref, br_ref, x2_ref, h2_ref, obuf_ref, sbuf_ref, p * sub, sub)
             for p in range(MERGE_SPLIT)]
    e1, e2, w1, w2 = (jnp.concatenate([part[j] for part in parts], axis=0) for j in range(4))

    lane = lax.broadcasted_iota(jnp.int32, (tm, LANES), 1).astype(F32)
    oh1 = lane == e1
    oh2 = lane == e2
    onehot = jnp.where(oh1 | oh2, 1.0, 0.0)
    prefix = _dot(tri_ref[...], onehot.astype(BF16)) + carry_ref[...]
    r1 = jnp.sum(jnp.where(oh1, prefix, 0.0), axis=-1, keepdims=True)
    r2 = jnp.sum(jnp.where(oh2, prefix, 0.0), axis=-1, keepdims=True)
    carry = carry_ref[...] + jnp.sum(onehot, axis=0, keepdims=True)
    carry_ref[...] = carry
    cnt_ref[...] = jnp.broadcast_to(carry, cnt_ref.shape)

    route = jnp.zeros((tm, LANES), F32)
    for idx, val in enumerate((e1, e2, w1, w2, r1, r2)):
        route = jnp.where(lane == idx, val, route)
    route_ref[...] = route


def _merge(x2d, o_gla, outs, stats, g1, wbg, bbg, wpg, wpa, wout, g2, wr, br, tri, S):
    T = x2d.shape[0]
    tm = min(MERGE_TM, S)
    tpb = S // tm
    tok = lambda w: pl.BlockSpec((tm, w), lambda i: (i, 0))
    full = lambda a: pl.BlockSpec(a.shape, lambda i: (0,) * a.ndim)

    def res(gi, w):
        d = DIL_PATTERNS[gi][1]
        return pl.BlockSpec((1, d, tm // d, w), lambda i: (i // tpb, 0, i % tpb, 0))

    return pl.pallas_call(
        _merge_kernel,
        grid=(T // tm,),
        in_specs=[tok(D_MODEL), tok(HEAD_W), tok(HEAD_W), res(1, HEAD_W), res(2, HEAD_W),
                  tok(LANES), res(1, LANES), res(2, LANES),
                  full(g1), full(wbg), full(bbg), full(wpg), full(wpa), full(wout), full(g2),
                  full(wr), full(br), full(tri)],
        out_specs=[tok(D_MODEL), pl.BlockSpec((tm * ROW_SUBLANES, LANES), lambda i: (i, 0)), tok(LANES),
                   pl.BlockSpec((8, LANES), lambda i: (0, 0))],
        out_shape=[
            jax.ShapeDtypeStruct((T, D_MODEL), F32),
            jax.ShapeDtypeStruct((T * ROW_SUBLANES, LANES), F32),
            jax.ShapeDtypeStruct((T, LANES), F32),
            jax.ShapeDtypeStruct((8, LANES), F32),
        ],
        scratch_shapes=[pltpu.VMEM((1, LANES), F32),
                        pltpu.VMEM((DIL_GROUPS - 1, DIL_HEADS, tm, LANES), F32),
                        pltpu.VMEM((DIL_GROUPS - 1, tm, LANES), F32)],
        compiler_params=_params(("arbitrary",)),
        name="merge",
    )(x2d, o_gla, *outs, *stats, g1, wbg, bbg, wpg, wpa, wout, g2, wr, br, tri)


def _row_tile(ref, row):
    return ref.at[pl.ds(pl.multiple_of(row * ROW_SUBLANES, ROW_SUBLANES), ROW_SUBLANES), :]


def _dispatch_kernel(pad_lo_ref, pad_hi_ref, dest_ref, h2_ref, xin_ref, inv_ref, zero_ref, sem, zsem):
    i = pl.program_id(0)
    tm = h2_ref.shape[0] // ROW_SUBLANES
    R = MOE_TM

    def start(r, c):
        for k in range(TOP_K):
            d = dest_ref[0, 0, TOP_K * r + k]
            pltpu.make_async_copy(_row_tile(h2_ref, r), _row_tile(xin_ref, d), sem).start(priority=k % 2)
            inv_ref[d] = (i * tm + r) * TOP_K + k
        return c

    lax.fori_loop(0, tm, start, 0, unroll=ROW_UNROLL)
    for k in range(TOP_K):
        pltpu.make_async_copy(h2_ref, xin_ref.at[pl.ds(0, tm * ROW_SUBLANES), :], sem).wait()

    @pl.when(i == pl.num_programs(0) - 1)
    def _():
        zero_ref[...] = jnp.zeros_like(zero_ref)

        def fill_row(p, c):
            pltpu.make_async_copy(_row_tile(zero_ref, 0), _row_tile(xin_ref, p), zsem).start()
            inv_ref[p] = -1
            return c

        def drain_row(p, c):
            pltpu.make_async_copy(_row_tile(zero_ref, 0), _row_tile(xin_ref, 0), zsem).wait()
            return c

        for e in range(N_EXPERTS):
            lax.fori_loop(pad_lo_ref[e], pad_hi_ref[e], fill_row, 0)
            lax.fori_loop(pad_lo_ref[e], pad_hi_ref[e], drain_row, 0)

        def tail_tile(j):
            return xin_ref.at[pl.ds(pl.multiple_of(j * R * ROW_SUBLANES, R * ROW_SUBLANES), R * ROW_SUBLANES), :]

        def fill_tile(j, c):
            pltpu.make_async_copy(zero_ref, tail_tile(j), zsem).start()
            return c

        def drain_tile(j, c):
            pltpu.make_async_copy(zero_ref, tail_tile(0), zsem).wait()
            return c

        def mark(p, c):
            inv_ref[p] = -1
            return c

        lo, hi = pad_lo_ref[N_EXPERTS], pad_hi_ref[N_EXPERTS]
        lax.fori_loop(lo // R, hi // R, fill_tile, 0)
        lax.fori_loop(lo // R, hi // R, drain_tile, 0)
        lax.fori_loop(lo, hi, mark, 0)


def _dispatch(pad_lo, pad_hi, dest, h2_rows, P):
    T = h2_rows.shape[0] // ROW_SUBLANES
    tm = min(ROW_TM, T)
    return pl.pallas_call(
        _dispatch_kernel,
        grid_spec=pltpu.PrefetchScalarGridSpec(
            num_scalar_prefetch=2,
            grid=(T // tm,),
            in_specs=[
                pl.BlockSpec((1, 1, TOP_K * tm), lambda i, lo, hi: (i, 0, 0), memory_space=pltpu.SMEM),
                pl.BlockSpec((tm * ROW_SUBLANES, LANES), lambda i, lo, hi: (i, 0)),
            ],
            out_specs=[pl.BlockSpec(memory_space=pl.ANY), pl.BlockSpec(memory_space=pltpu.SMEM)],
            scratch_shapes=[pltpu.VMEM((MOE_TM * ROW_SUBLANES, LANES), F32),
                            pltpu.SemaphoreType.DMA, pltpu.SemaphoreType.DMA],
        ),
        out_shape=[jax.ShapeDtypeStruct((P * ROW_SUBLANES, LANES), F32), jax.ShapeDtypeStruct((P,), jnp.int32)],
        compiler_params=_params(("arbitrary",)),
        name="dispatch",
    )(pad_lo, pad_hi, dest.reshape(T // tm, 1, TOP_K * tm), h2_rows)


def _expert_kernel(tile_e_ref, nact_ref, dst_prev_ref, x_ref, wg_ref, wu_ref, wd_ref, slots_ref, ybuf_ref, ssem):
    del tile_e_ref
    i = pl.program_id(0)
    nact = nact_ref[0]
    slot = lax.rem(i, 2)
    other = 1 - slot
    R = MOE_TM

    def scatter(dst_ref, s):
        for q in range(R):
            row = pl.multiple_of(dst_ref[0, 0, q], ROW_SUBLANES)
            pltpu.make_async_copy(ybuf_ref.at[s, pl.ds(q * ROW_SUBLANES, ROW_SUBLANES), :],
                                  slots_ref.at[pl.ds(row, ROW_SUBLANES), :], ssem.at[s]).start(priority=q % 2)

    def wait_scatter(s):
        pltpu.make_async_copy(ybuf_ref.at[s], slots_ref.at[pl.ds(0, R * ROW_SUBLANES), :], ssem.at[s]).wait()

    @pl.when(i == 0)
    def _():
        ybuf_ref[...] = jnp.zeros_like(ybuf_ref)

    @pl.when(i <= nact)
    def _():
        @pl.when(i >= 1)
        def _():
            wait_scatter(slot)

        scatter(dst_prev_ref, other)
        xb = _load_rows(x_ref, (), R).astype(BF16)
        g = _dot(xb, wg_ref[0].astype(BF16))
        u = _dot(xb, wu_ref[0].astype(BF16))
        a = (g * jax.nn.sigmoid(g) * u).astype(BF16)
        _store_rows(ybuf_ref, (slot,), _dot(a, wd_ref[0].astype(BF16)))

    @pl.when(i == nact + 1)
    def _():
        wait_scatter(slot)


def _experts(tile_e, nact, dst, xin, wg, wu, wd, n_slot_rows):
    ntiles = tile_e.shape[0]
    R = MOE_TM

    def rows(i, te, na):
        return (jnp.minimum(i, na[0] - 1), 0)

    def wsel(i, te, na):
        return (te[jnp.minimum(i, na[0] - 1)], 0, 0)

    return pl.pallas_call(
        _expert_kernel,
        grid_spec=pltpu.PrefetchScalarGridSpec(
            num_scalar_prefetch=2,
            grid=(ntiles + 2,),
            in_specs=[
                pl.BlockSpec((1, 1, R), lambda i, te, na: (i, 0, 0), memory_space=pltpu.SMEM),
                pl.BlockSpec((R * ROW_SUBLANES, LANES), rows),
                pl.BlockSpec((1, D_MODEL, D_EXPERT), wsel),
                pl.BlockSpec((1, D_MODEL, D_EXPERT), wsel),
                pl.BlockSpec((1, D_EXPERT, D_MODEL), wsel),
            ],
            out_specs=pl.BlockSpec(memory_space=pl.ANY),
            scratch_shapes=[
                pltpu.VMEM((2, R * ROW_SUBLANES, LANES), F32),
                pltpu.SemaphoreType.DMA((2,)),
            ],
        ),
        out_shape=jax.ShapeDtypeStruct((n_slot_rows * ROW_SUBLANES, LANES), F32),
        compiler_params=_params(("arbitrary",)),
        name="experts",
    )(tile_e, nact, dst, xin, wg, wu, wd)


def _combine_kernel(x2_ref, route_ref, s0_ref, s1_ref, out_ref):
    tm = x2_ref.shape[0]
    route = route_ref[...]
    w1 = route[:, 2:3]
    w2 = route[:, 3:4]
    out_ref[...] = x2_ref[...] + w1 * _load_rows(s0_ref, (), tm) + w2 * _load_rows(s1_ref, (), tm)


def _combine(x2, route, slots):
    T = x2.shape[0]
    tm = min(ROW_TM, T)
    nt = T // tm
    return pl.pallas_call(
        _combine_kernel,
        grid=(nt,),
        in_specs=[
            pl.BlockSpec((tm, D_MODEL), lambda i: (i, 0)),
            pl.BlockSpec((tm, LANES), lambda i: (i, 0)),
            pl.BlockSpec((tm * ROW_SUBLANES, LANES), lambda i: (i, 0)),
            pl.BlockSpec((tm * ROW_SUBLANES, LANES), lambda i: (nt + i, 0)),
        ],
        out_specs=pl.BlockSpec((tm, D_MODEL), lambda i: (i, 0)),
        out_shape=jax.ShapeDtypeStruct((T, D_MODEL), F32),
        compiler_params=_params(("arbitrary",)),
        name="combine",
    )(x2, route, slots, slots)


def _layer(x2d, B, S, norm1_g, w_in, w_gla_a2, b_gla_a, gla_out_norm_g, dil_q_norm_g, dil_k_norm_g,
           w_proj_gla, w_proj_attn, w_branch_gate, b_branch_gate, w_out, norm2_g,
           w_router_group, b_router_group, w_router_expert, b_router_expert, w_gate, w_up, w_down):
    T = B * S
    n_gla = N_GLA_TILES * HEAD_W
    w_main = jnp.concatenate([w_in[:, :n_gla], w_in[:, n_gla + GLA_RANK:]], axis=1).astype(BF16)
    w_ga = jnp.pad(w_in[:, n_gla:n_gla + GLA_RANK], ((0, 0), (0, LANES - GLA_RANK))).astype(BF16)
    w_a2 = jnp.pad(w_gla_a2, ((0, LANES - GLA_RANK), (0, 0))).astype(BF16)
    gains = [jnp.ones((HEAD_W,), F32)] * N_GLA_TILES
    for gi in range(DIL_GROUPS):
        gains += [jnp.tile(dil_q_norm_g[gi], DIL_HEADS) * (DIL_DH ** -0.5),
                  jnp.tile(dil_k_norm_g[gi], DIL_HEADS), jnp.ones((HEAD_W,), F32)]
    qk_gain = jnp.stack(gains).reshape(N_COL_TILES, 1, HEAD_W)

    gla_in, log_a, qkv0, qkv1, qkv2 = _inproj(
        x2d, norm1_g.reshape(1, -1), w_main, w_ga, w_a2, b_gla_a.reshape(1, -1), qk_gain, B, S)

    tri_incl = jnp.asarray(np.tril(np.ones((GLA_BLOCK, GLA_BLOCK), np.float32)), BF16)
    o_gla = _gla(gla_in, log_a, tri_incl, gla_out_norm_g.reshape(1, -1), B, S)

    outs, stats = [], []
    for gi, qkv in enumerate((qkv0.reshape(B, 1, S, QKV_W), qkv1, qkv2)):
        o, st = _dilated(qkv, _alibi_bias(gi), gi)
        outs.append(o)
        stats.append(st)
    outs[0] = outs[0].reshape(T, HEAD_W)
    stats[0] = stats[0].reshape(T, LANES)

    w_r = jnp.pad(jnp.concatenate([w_router_group, w_router_expert], axis=1),
                  ((0, 0), (0, LANES - N_GROUPS - N_EXPERTS)))
    w_rh = w_r.astype(BF16)
    w_rl = (w_r - w_rh.astype(F32)).astype(BF16)
    b_r = jnp.pad(jnp.concatenate([b_router_group, b_router_expert]), (0, LANES - N_GROUPS - N_EXPERTS))
    tm = min(MERGE_TM, S)
    tri_strict = jnp.asarray(np.tril(np.ones((tm, tm), np.float32), -1), BF16)
    x2, h2, route, cnt = _merge(
        x2d, o_gla, outs, stats, norm1_g.reshape(1, -1), w_branch_gate.astype(BF16),
        b_branch_gate.reshape(1, -1), w_proj_gla.astype(BF16), w_proj_attn.astype(BF16), w_out.astype(BF16),
        norm2_g.reshape(1, -1), jnp.concatenate([w_rh, w_rl], axis=1), b_r.reshape(1, -1), tri_strict, S)

    counts = cnt[0, :N_EXPERTS].astype(jnp.int32)
    pcounts = (counts + MOE_TM - 1) // MOE_TM * MOE_TM
    pends = jnp.cumsum(pcounts)
    pstarts = pends - pcounts
    eid = route[:, 0:TOP_K].astype(jnp.int32)
    rank = route[:, 4:4 + TOP_K].astype(jnp.int32)
    onehot = eid[:, :, None] == jnp.arange(N_EXPERTS, dtype=jnp.int32)[None, None, :]
    dest = rank + jnp.sum(jnp.where(onehot, pstarts[None, None, :], 0), axis=-1)
    P = T * TOP_K + N_EXPERTS * MOE_TM
    ntiles = P // MOE_TM
    tile_start = jnp.arange(ntiles, dtype=jnp.int32) * MOE_TM
    tile_e = jnp.minimum(jnp.sum(pends[None, :] <= tile_start[:, None], axis=1), N_EXPERTS - 1).astype(jnp.int32)
    nact = (pends[-1:] // MOE_TM).astype(jnp.int32)

    pad_lo = jnp.concatenate([pstarts + counts, pends[-1:]]).astype(jnp.int32)
    pad_hi = jnp.concatenate([pends, jnp.full((1,), P, jnp.int32)]).astype(jnp.int32)
    xin, inv = _dispatch(pad_lo, pad_hi, dest, h2, P)

    pad_tile = jnp.full((MOE_TM,), -1, jnp.int32)
    inv_dst = jnp.concatenate([pad_tile, inv, pad_tile])
    spare = T * TOP_K + jnp.tile(jnp.arange(MOE_TM, dtype=jnp.int32), ntiles + 2)
    dst = jnp.where(inv_dst >= 0, (inv_dst % TOP_K) * T + inv_dst // TOP_K, spare) * ROW_SUBLANES
    slots = _experts(tile_e, nact, dst.reshape(ntiles + 2, 1, MOE_TM), xin,
                     w_gate, w_up, w_down, T * TOP_K + MOE_TM)
    return _combine(x2, route, slots)


def kernel(x, norm1_g, w_in, w_gla_a2, b_gla_a, gla_out_norm_g, dil_q_norm_g, dil_k_norm_g, w_proj_gla,
           w_proj_attn, w_branch_gate, b_branch_gate, w_out, norm2_g, w_router_group, b_router_group,
           w_router_expert, b_router_expert, w_gate, w_up, w_down):
    B, S, D = x.shape
    assert D == D_MODEL and S % (DIL_BLOCK * DIL_PATTERNS[-1][1]) == 0
    x2d = x.reshape(B * S, D)
    params = (norm1_g, w_in, w_gla_a2, b_gla_a, gla_out_norm_g, dil_q_norm_g, dil_k_norm_g, w_proj_gla,
              w_proj_attn, w_branch_gate, b_branch_gate, w_out, norm2_g, w_router_group, b_router_group,
              w_router_expert, b_router_expert, w_gate, w_up, w_down)
    for layer in range(norm1_g.shape[0]):
        x2d = _layer(x2d, B, S, *(p[layer] for p in params))
    return x2d.reshape(B, S, D)
```

```python
import functools

import jax
import jax.numpy as jnp
import numpy as np
from jax import lax
from jax.experimental import pallas as pl
from jax.experimental.pallas import tpu as pltpu

F32 = jnp.float32
BF16 = jnp.bfloat16

D_MODEL = 1024
EPS = 1e-6
GLA_HEADS = 4
GLA_DK = 128
GLA_RANK = 16
GLA_TAU = 16.0
LOG2_E = 1.4426950408889634
GLA_SUBCHUNK = 16
GLA_BLOCK = 128
GLA_STEP_BLOCKS = 16
DIL_PATTERNS = ((128, 1), (512, 4), (2048, 16))
DIL_GROUPS = 3
DIL_HEADS = 4
DIL_DH = 128
DIL_BLOCK = 128
ALIBI_SLOPES = tuple(2.0 ** (-8.0 * (i + 1) / (DIL_GROUPS * DIL_HEADS)) for i in range(DIL_GROUPS * DIL_HEADS))
N_GROUPS = 4
EXPERTS_PER_GROUP = 8
N_EXPERTS = N_GROUPS * EXPERTS_PER_GROUP
TOP_K = 2
D_EXPERT = 512

HEAD_W = 512
N_GLA_TILES = 4
N_COL_TILES = N_GLA_TILES + 3 * DIL_GROUPS
QKV_W = 3 * HEAD_W
LANES = 128
NEG = -1e30

INPROJ_TM = 512
MERGE_TM = 512
MERGE_SPLIT = 1
MOE_TM = 512
ROW_TM = 512
ROW_UNROLL = 8
DIL_STEP_BLOCKS = 16
VMEM_LIMIT = 56 * 1024 * 1024


def _dot(a, b):
    return jnp.dot(a, b, preferred_element_type=F32)


def _dot_nt(a, b):
    return lax.dot_general(a, b, (((1,), (1,)), ((), ())), preferred_element_type=F32)


def _dot_tn(a, b):
    return lax.dot_general(a, b, (((0,), (0,)), ((), ())), preferred_element_type=F32)


def _rms(x, g):
    return x * lax.rsqrt(jnp.mean(x * x, axis=-1, keepdims=True) + EPS) * g


def _params(sem):
    return pltpu.CompilerParams(dimension_semantics=sem, vmem_limit_bytes=VMEM_LIMIT)


def _head(c):
    return slice(c * LANES, (c + 1) * LANES)


ROW_SUBLANES = 8


def _load_rows(ref, lead, n):
    return jnp.concatenate([ref[lead + (pl.ds(c, n, stride=ROW_SUBLANES), slice(None))]
                            for c in range(ROW_SUBLANES)], axis=1)


def _store_rows(ref, lead, val, row0=0):
    n = val.shape[0]
    for c in range(ROW_SUBLANES):
        ref[lead + (pl.ds(row0 * ROW_SUBLANES + c, n, stride=ROW_SUBLANES), slice(None))] = val[:, _head(c)]


def _inproj_kernel(x_ref, g1_ref, wgla_ref, wdil_ref, wga_ref, wa2_ref, ba_ref, qkg_ref,
                   gla_ref, loga_ref, d0_ref, d1_ref, d2_ref, h_ref, ybuf_ref):
    tm = x_ref.shape[0]
    hb = _rms(x_ref[...], g1_ref[...]).astype(BF16)
    h_ref[...] = hb
    ga = _dot(hb, wga_ref[...])
    z = _dot(ga.astype(BF16), wa2_ref[...]) + ba_ref[...]
    log_sig = jnp.minimum(z, 0.0) - jnp.log(1.0 + jnp.exp(-jnp.abs(z)))
    loga_ref[...] = log_sig * (LOG2_E / GLA_TAU)

    dil_refs = (d0_ref, d1_ref, d2_ref)
    n_strided = 0
    for jj in range(N_COL_TILES):
        cols = slice(jj * HEAD_W, (jj + 1) * HEAD_W)
        if jj < N_GLA_TILES:
            w = wgla_ref[:, cols]
        else:
            w = wdil_ref[:, (jj - N_GLA_TILES) * HEAD_W:(jj - N_GLA_TILES + 1) * HEAD_W]
        y = _dot(h_ref[...], w)
        if jj < N_GLA_TILES:
            gla_ref[:, cols] = y.astype(BF16)
            continue
        gi, kind = divmod(jj - N_GLA_TILES, 3)
        d = DIL_PATTERNS[gi][1]
        ocols = [slice(kind * HEAD_W + c * LANES, kind * HEAD_W + (c + 1) * LANES) for c in range(DIL_HEADS)]
        if kind < 2:
            g = qkg_ref[jj]
            slabs = [_rms(y[:, _head(c)], g[:, _head(c)]) for c in range(DIL_HEADS)]
        else:
            slabs = [y[:, _head(c)] for c in range(DIL_HEADS)]
        out = dil_refs[gi]
        if d == 1:
            for c in range(DIL_HEADS):
                out[:, ocols[c]] = slabs[c].astype(BF16)
            continue
        buf = n_strided % ybuf_ref.shape[0]
        n_strided += 1
        for c in range(DIL_HEADS):
            ybuf_ref[buf, c] = slabs[c]
        for r in range(d):
            for c in range(DIL_HEADS):
                out[0, r, :, ocols[c]] = ybuf_ref[buf, c, pl.ds(r, tm // d, stride=d), :].astype(BF16)


def _resident(shape):
    return pl.BlockSpec(shape, lambda i: (0,) * len(shape), pipeline_mode=pl.Buffered(1))


def _inproj(x2d, g1, w_gla, w_dil, w_ga, w_a2, b_a, qk_gain, B, S):
    T = x2d.shape[0]
    tm = min(INPROJ_TM, S)
    tpb = S // tm

    def dil_spec(gi):
        d = DIL_PATTERNS[gi][1]
        if d == 1:
            return pl.BlockSpec((tm, QKV_W), lambda i: (i, 0))
        return pl.BlockSpec((1, d, tm // d, QKV_W), lambda i: (i // tpb, 0, i % tpb, 0))

    def dil_shape(gi):
        d = DIL_PATTERNS[gi][1]
        shape = (T, QKV_W) if d == 1 else (B, d, S // d, QKV_W)
        return jax.ShapeDtypeStruct(shape, BF16)

    return pl.pallas_call(
        _inproj_kernel,
        grid=(T // tm,),
        in_specs=[
            pl.BlockSpec((tm, D_MODEL), lambda i: (i, 0)),
            _resident(g1.shape), _resident(w_gla.shape), _resident(w_dil.shape), _resident(w_ga.shape),
            _resident(w_a2.shape),
            _resident(b_a.shape), _resident(qk_gain.shape),
        ],
        out_specs=[
            pl.BlockSpec((tm, N_GLA_TILES * HEAD_W), lambda i: (i, 0)),
            pl.BlockSpec((tm, HEAD_W), lambda i: (i, 0)),
            dil_spec(0), dil_spec(1), dil_spec(2),
        ],
        out_shape=[
            jax.ShapeDtypeStruct((T, N_GLA_TILES * HEAD_W), BF16),
            jax.ShapeDtypeStruct((T, HEAD_W), F32),
            dil_shape(0), dil_shape(1), dil_shape(2),
        ],
        scratch_shapes=[pltpu.VMEM((tm, D_MODEL), BF16), pltpu.VMEM((2, DIL_HEADS, tm, LANES), F32)],
        compiler_params=_params(("arbitrary",)),
        name="inproj",
    )(x2d, g1, w_gla, w_dil, w_ga, w_a2, b_a, qk_gain)


def _gla_kernel(q_ref, k_ref, v_ref, r_ref, la_ref, tri_ref, gn_ref, o_ref, st_ref):
    n = pl.program_id(1)

    @pl.when(n == 0)
    def _():
        st_ref[...] = jnp.zeros_like(st_ref)

    for blk in range(q_ref.shape[0] // GLA_BLOCK):
        _gla_block(q_ref, k_ref, v_ref, r_ref, la_ref, tri_ref, gn_ref, o_ref, st_ref,
                   slice(blk * GLA_BLOCK, (blk + 1) * GLA_BLOCK))


def _gla_block(q_ref, k_ref, v_ref, r_ref, la_ref, tri_ref, gn_ref, o_ref, st_ref, rows):
    L = GLA_BLOCK
    la = la_ref[rows, :]
    hi = la.astype(BF16)
    r1 = la - hi.astype(F32)
    mid = r1.astype(BF16)
    lo = (r1 - mid.astype(F32)).astype(BF16)
    tri = tri_ref[...]
    b_all = _dot(tri, hi) + _dot(tri, mid) + _dot(tri, lo)

    row = lax.broadcasted_iota(jnp.int32, (L, L), 0)
    col = lax.broadcasted_iota(jnp.int32, (L, L), 1)

    for h in range(GLA_HEADS):
        sl = _head(h)
        b = b_all[:, sl]
        q = q_ref[rows, sl].astype(F32) * (GLA_DK ** -0.5)
        k = k_ref[rows, sl].astype(F32)
        v = v_ref[rows, sl]

        att = jnp.zeros((L, L), F32)
        seg = L
        while seg > GLA_SUBCHUNK:
            half = seg // 2
            beta = jnp.concatenate(
                [jnp.broadcast_to(b[s + half - 1:s + half, :], (seg, GLA_DK)) for s in range(0, L, seg)], axis=0)
            rel = b - beta
            neg = jnp.minimum(rel, 0.0)
            qs = (q * jnp.exp2(neg)).astype(BF16)
            ks = (k * jnp.exp2(neg - rel)).astype(BF16)
            same = (row // seg) == (col // seg)
            mask = same & ((row % seg) >= half) & ((col % seg) < half)
            att = jnp.where(mask, _dot_nt(qs, ks), att)
            seg = half
        beta = jnp.concatenate(
            [jnp.zeros((GLA_SUBCHUNK, GLA_DK), F32)]
            + [jnp.broadcast_to(b[s - 1:s, :], (GLA_SUBCHUNK, GLA_DK)) for s in range(GLA_SUBCHUNK, L, GLA_SUBCHUNK)],
            axis=0)
        qs = (q * jnp.exp2(b - beta)).astype(BF16)
        ks = (k * jnp.exp2(beta - b)).astype(BF16)
        mask = ((row // GLA_SUBCHUNK) == (col // GLA_SUBCHUNK)) & (col <= row)
        att = jnp.where(mask, _dot_nt(qs, ks), att)

        state_t = st_ref[h]
        q0 = (q * jnp.exp2(b)).astype(BF16)
        o = _dot(att.astype(BF16), v) + _dot_nt(q0, state_t.astype(BF16))

        b_last = b[L - 1:L, :]
        k_end = (k * jnp.exp2(b_last - b)).astype(BF16)
        st_ref[h] = state_t * jnp.exp2(b_last) + _dot_tn(v, k_end)

        rr = r_ref[rows, sl].astype(F32)
        o_ref[rows, sl] = (_rms(o, gn_ref[...]) * (rr * jax.nn.sigmoid(rr))).astype(BF16)


def _gla(proj, log_a, tri, gn, B, S):
    L = GLA_STEP_BLOCKS * GLA_BLOCK
    nb = S // L
    return pl.pallas_call(
        _gla_kernel,
        grid=(B, nb),
        in_specs=[
            pl.BlockSpec((L, HEAD_W), lambda b, n: (b * nb + n, 0)),
            pl.BlockSpec((L, HEAD_W), lambda b, n: (b * nb + n, 1)),
            pl.BlockSpec((L, HEAD_W), lambda b, n: (b * nb + n, 2)),
            pl.BlockSpec((L, HEAD_W), lambda b, n: (b * nb + n, 3)),
            pl.BlockSpec((L, HEAD_W), lambda b, n: (b * nb + n, 0)),
            pl.BlockSpec((GLA_BLOCK, GLA_BLOCK), lambda b, n: (0, 0)),
            pl.BlockSpec((1, GLA_DK), lambda b, n: (0, 0)),
        ],
        out_specs=pl.BlockSpec((L, HEAD_W), lambda b, n: (b * nb + n, 0)),
        out_shape=jax.ShapeDtypeStruct((B * S, HEAD_W), BF16),
        scratch_shapes=[pltpu.VMEM((GLA_HEADS, GLA_DK, GLA_DK), F32)],
        compiler_params=_params(("arbitrary", "arbitrary")),
        name="gla",
    )(proj, proj, proj, proj, log_a, tri, gn)


def _dil_kernel(q_ref, kp_ref, kc_ref, vp_ref, vc_ref, bias_ref, o_ref, st_ref, *, nr, nq):
    first = pl.program_id(2) == 0
    Q = DIL_BLOCK
    col = lax.broadcasted_iota(jnp.int32, (Q, 2 * Q), 1)
    lane = lax.broadcasted_iota(jnp.int32, (Q, LANES), 1)
    for res in range(nr):
        for blk in range(nq):
            rows = slice(blk * Q, (blk + 1) * Q)
            prev = slice((blk - 1) * Q, blk * Q)
            stats = jnp.zeros((Q, LANES), F32)
            for h in range(DIL_HEADS):
                sl = _head(h)
                if blk == 0:
                    kprev, vprev = kp_ref[0, res, :, sl], vp_ref[0, res, :, sl]
                else:
                    kprev, vprev = kc_ref[0, res, prev, sl], vc_ref[0, res, prev, sl]
                kk = jnp.concatenate([kprev, kc_ref[0, res, rows, sl]], axis=0)
                vv = jnp.concatenate([vprev, vc_ref[0, res, rows, sl]], axis=0)
                s = _dot_nt(q_ref[0, res, rows, sl], kk) + bias_ref[h]
                if blk == 0:
                    s = jnp.where(jnp.logical_and(first, col < Q), NEG, s)
                m = jnp.max(s, axis=-1, keepdims=True)
                p = jnp.exp2(s - m)
                den = jnp.sum(p, axis=-1, keepdims=True)
                o_ref[0, res, rows, sl] = (_dot(p.astype(BF16), vv) / den).astype(BF16)
                stats = jnp.where((lane // 16) == h, m, stats)
                stats = jnp.where((lane // 16) == DIL_HEADS + h, den, stats)
            st_ref[0, res, rows, :] = stats


def _dilated(qkv, bias, gi):
    B, d, Lsub, _ = qkv.shape
    nq = min(DIL_STEP_BLOCKS, Lsub // DIL_BLOCK)
    nr = min(DIL_STEP_BLOCKS // nq, d)
    rows = nq * DIL_BLOCK

    def cur(c):
        return pl.BlockSpec((1, nr, rows, HEAD_W), lambda b, r, i: (b, r, i, c))

    def prev(c):
        return pl.BlockSpec((1, nr, DIL_BLOCK, HEAD_W), lambda b, r, i: (b, r, jnp.maximum(i * nq - 1, 0), c))

    return pl.pallas_call(
        functools.partial(_dil_kernel, nr=nr, nq=nq),
        grid=(B, d // nr, Lsub // rows),
        in_specs=[cur(0), prev(1), cur(1), prev(2), cur(2),
                  pl.BlockSpec((DIL_HEADS, DIL_BLOCK, 2 * DIL_BLOCK), lambda b, r, i: (0, 0, 0))],
        out_specs=[
            pl.BlockSpec((1, nr, rows, HEAD_W), lambda b, r, i: (b, r, i, 0)),
            pl.BlockSpec((1, nr, rows, LANES), lambda b, r, i: (b, r, i, 0)),
        ],
        out_shape=[
            jax.ShapeDtypeStruct((B, d, Lsub, HEAD_W), BF16),
            jax.ShapeDtypeStruct((B, d, Lsub, LANES), F32),
        ],
        compiler_params=_params(("arbitrary", "arbitrary", "arbitrary")),
        name=f"dilated{gi}",
    )(qkv, qkv, qkv, qkv, qkv, bias)


def _alibi_bias(gi):
    window, d = DIL_PATTERNS[gi]
    Q = DIL_BLOCK
    dist = np.arange(Q)[:, None] + Q - np.arange(2 * Q)[None, :]
    valid = (dist >= 0) & (dist <= window // d)
    slopes = np.asarray(ALIBI_SLOPES, np.float32).reshape(DIL_GROUPS, DIL_HEADS)[gi]
    bias = -slopes[:, None, None] * (d * dist).astype(np.float32) * np.float32(LOG2_E)
    return jnp.asarray(np.where(valid[None], bias, np.float32(NEG)), F32)


def _merge_rows(x_ref, og_ref, o0_ref, s0_ref, g1_ref, wbg_ref, bbg_ref, wpg_ref, wpa_ref, wout_ref,
                g2_ref, wr_ref, br_ref, x2_ref, h2_ref, obuf_ref, sbuf_ref, row0, n):
    rows = slice(row0, row0 + n)
    x = x_ref[rows, :]
    hb = _rms(x, g1_ref[...]).astype(BF16)
    gates = jax.nn.sigmoid(_dot(hb, wbg_ref[...]) + bbg_ref[...])

    stats = (s0_ref[rows, :], sbuf_ref[0, rows, :], sbuf_ref[1, rows, :])
    dens = [pltpu.roll(s, 64, 1) for s in stats]
    m_all = jnp.maximum(jnp.maximum(stats[0], stats[1]), stats[2])
    wts = [d * jnp.exp2(s - m_all) for s, d in zip(stats, dens)]
    inv = 1.0 / (wts[0] + wts[1] + wts[2])
    coef = [w * inv for w in wts]
    heads = []
    for h in range(DIL_HEADS):
        group_out = (o0_ref[rows, _head(h)].astype(F32), obuf_ref[0, h, rows, :], obuf_ref[1, h, rows, :])
        acc = jnp.zeros((n, DIL_DH), F32)
        for g in range(DIL_GROUPS):
            c = jnp.broadcast_to(coef[g][:, 16 * h:16 * h + 1], (n, DIL_DH))
            acc = acc + c * group_out[g]
        heads.append(acc.astype(BF16))
    o_att = jnp.concatenate(heads, axis=1)

    y = gates[:, :D_MODEL] * _dot(og_ref[rows, :], wpg_ref[...]) + gates[:, D_MODEL:] * _dot(o_att, wpa_ref[...])
    x2 = x + _dot(y.astype(BF16), wout_ref[...])
    x2_ref[rows, :] = x2

    h2 = _rms(x2, g2_ref[...])
    _store_rows(h2_ref, (), h2, row0)

    h2h = h2.astype(BF16)
    h2l = (h2 - h2h.astype(F32)).astype(BF16)
    wr = wr_ref[...]
    pa = _dot(h2h, wr)
    pb = _dot(h2l, wr)
    logit = pa[:, :LANES] + pa[:, LANES:] + pb[:, :LANES] + pb[:, LANES:] + br_ref[...]

    lane = lax.broadcasted_iota(jnp.int32, (n, LANES), 1).astype(F32)
    big = jnp.float32(1e9)
    gl = jnp.where(lane < N_GROUPS, logit, NEG)
    gmax = jnp.max(gl, axis=-1, keepdims=True)
    gsel = jnp.min(jnp.where(gl == gmax, lane, big), axis=-1, keepdims=True)
    g_p = 1.0 / jnp.sum(jnp.exp(gl - gmax), axis=-1, keepdims=True)
    lo = N_GROUPS + EXPERTS_PER_GROUP * gsel
    el = jnp.where((lane >= lo) & (lane < lo + EXPERTS_PER_GROUP), logit, NEG)
    v1 = jnp.max(el, axis=-1, keepdims=True)
    i1 = jnp.min(jnp.where(el == v1, lane, big), axis=-1, keepdims=True)
    el2 = jnp.where(lane == i1, NEG, el)
    v2 = jnp.max(el2, axis=-1, keepdims=True)
    i2 = jnp.min(jnp.where(el2 == v2, lane, big), axis=-1, keepdims=True)
    ex = jnp.exp(v2 - v1)
    w1 = g_p / (1.0 + ex)
    w2 = g_p * ex / (1.0 + ex)
    return i1 - N_GROUPS, i2 - N_GROUPS, w1, w2


def _merge_kernel(x_ref, og_ref, o0_ref, o1_ref, o2_ref, s0_ref, s1_ref, s2_ref, g1_ref, wbg_ref, bbg_ref,
                  wpg_ref, wpa_ref, wout_ref, g2_ref, wr_ref, br_ref, tri_ref,
                  x2_ref, h2_ref, route_ref, cnt_ref, carry_ref, obuf_ref, sbuf_ref):
    @pl.when(pl.program_id(0) == 0)
    def _():
        carry_ref[...] = jnp.zeros_like(carry_ref)

    tm = x_ref.shape[0]

    for slot, (o_ref, s_ref) in enumerate(((o1_ref, s1_ref), (o2_ref, s2_ref))):
        d = DIL_PATTERNS[slot + 1][1]
        for r in range(d):
            sbuf_ref[slot, pl.ds(r, tm // d, stride=d), :] = s_ref[0, r]
            for c in range(DIL_HEADS):
                obuf_ref[slot, c, pl.ds(r, tm // d, stride=d), :] = o_ref[0, r, :, _head(c)].astype(F32)

    sub = tm // MERGE_SPLIT
    parts = [_merge_rows(x_ref, og_ref, o0_ref, s0_ref, g1_ref, wbg_ref, bbg_ref, wpg_ref, wpa_ref, wout_ref,
                         g2_ref, wr_ref, br_ref, x2_ref, h2_ref, obuf_ref, sbuf_ref, p * sub, sub)
             for p in range(MERGE_SPLIT)]
    e1, e2, w1, w2 = (jnp.concatenate([part[j] for part in parts], axis=0) for j in range(4))

    lane = lax.broadcasted_iota(jnp.int32, (tm, LANES), 1).astype(F32)
    oh1 = lane == e1
    oh2 = lane == e2
    onehot = jnp.where(oh1 | oh2, 1.0, 0.0)
    prefix = _dot(tri_ref[...], onehot.astype(BF16)) + carry_ref[...]
    r1 = jnp.sum(jnp.where(oh1, prefix, 0.0), axis=-1, keepdims=True)
    r2 = jnp.sum(jnp.where(oh2, prefix, 0.0), axis=-1, keepdims=True)
    carry = carry_ref[...] + jnp.sum(onehot, axis=0, keepdims=True)
    carry_ref[...] = carry
    cnt_ref[...] = jnp.broadcast_to(carry, cnt_ref.shape)

    route = jnp.zeros((tm, LANES), F32)
    for idx, val in enumerate((e1, e2, w1, w2, r1, r2)):
        route = jnp.where(lane == idx, val, route)
    route_ref[...] = route


def _merge(x2d, o_gla, outs, stats, g1, wbg, bbg, wpg, wpa, wout, g2, wr, br, tri, S):
    T = x2d.shape[0]
    tm = min(MERGE_TM, S)
    tpb = S // tm
    tok = lambda w: pl.BlockSpec((tm, w), lambda i: (i, 0))
    full = lambda a: pl.BlockSpec(a.shape, lambda i: (0,) * a.ndim)

    def res(gi, w):
        d = DIL_PATTERNS[gi][1]
        return pl.BlockSpec((1, d, tm // d, w), lambda i: (i // tpb, 0, i % tpb, 0))

    return pl.pallas_call(
        _merge_kernel,
        grid=(T // tm,),
        in_specs=[tok(D_MODEL), tok(HEAD_W), tok(HEAD_W), res(1, HEAD_W), res(2, HEAD_W),
                  tok(LANES), res(1, LANES), res(2, LANES),
                  full(g1), full(wbg), full(bbg), full(wpg), full(wpa), full(wout), full(g2),
                  full(wr), full(br), full(tri)],
        out_specs=[tok(D_MODEL), pl.BlockSpec((tm * ROW_SUBLANES, LANES), lambda i: (i, 0)), tok(LANES),
                   pl.BlockSpec((8, LANES), lambda i: (0, 0))],
        out_shape=[
            jax.ShapeDtypeStruct((T, D_MODEL), F32),
            jax.ShapeDtypeStruct((T * ROW_SUBLANES, LANES), F32),
            jax.ShapeDtypeStruct((T, LANES), F32),
            jax.ShapeDtypeStruct((8, LANES), F32),
        ],
        scratch_shapes=[pltpu.VMEM((1, LANES), F32),
                        pltpu.VMEM((DIL_GROUPS - 1, DIL_HEADS, tm, LANES), F32),
                        pltpu.VMEM((DIL_GROUPS - 1, tm, LANES), F32)],
        compiler_params=_params(("arbitrary",)),
        name="merge",
    )(x2d, o_gla, *outs, *stats, g1, wbg, bbg, wpg, wpa, wout, g2, wr, br, tri)


def _row_tile(ref, row):
    return ref.at[pl.ds(pl.multiple_of(row * ROW_SUBLANES, ROW_SUBLANES), ROW_SUBLANES), :]


def _dispatch_kernel(pad_lo_ref, pad_hi_ref, dest_ref, h2_ref, xin_ref, inv_ref, zero_ref, sem, zsem):
    i = pl.program_id(0)
    tm = h2_ref.shape[0] // ROW_SUBLANES
    R = MOE_TM

    def start(r, c):
        for k in range(TOP_K):
            d = dest_ref[0, 0, TOP_K * r + k]
            pltpu.make_async_copy(_row_tile(h2_ref, r), _row_tile(xin_ref, d), sem).start(priority=k % 2)
            inv_ref[d] = (i * tm + r) * TOP_K + k
        return c

    lax.fori_loop(0, tm, start, 0, unroll=ROW_UNROLL)
    for k in range(TOP_K):
        pltpu.make_async_copy(h2_ref, xin_ref.at[pl.ds(0, tm * ROW_SUBLANES), :], sem).wait()

    @pl.when(i == pl.num_programs(0) - 1)
    def _():
        zero_ref[...] = jnp.zeros_like(zero_ref)

        def fill_row(p, c):
            pltpu.make_async_copy(_row_tile(zero_ref, 0), _row_tile(xin_ref, p), zsem).start()
            inv_ref[p] = -1
            return c

        def drain_row(p, c):
            pltpu.make_async_copy(_row_tile(zero_ref, 0), _row_tile(xin_ref, 0), zsem).wait()
            return c

        for e in range(N_EXPERTS):
            lax.fori_loop(pad_lo_ref[e], pad_hi_ref[e], fill_row, 0)
            lax.fori_loop(pad_lo_ref[e], pad_hi_ref[e], drain_row, 0)

        def tail_tile(j):
            return xin_ref.at[pl.ds(pl.multiple_of(j * R * ROW_SUBLANES, R * ROW_SUBLANES), R * ROW_SUBLANES), :]

        def fill_tile(j, c):
            pltpu.make_async_copy(zero_ref, tail_tile(j), zsem).start()
            return c

        def drain_tile(j, c):
            pltpu.make_async_copy(zero_ref, tail_tile(0), zsem).wait()
            return c

        def mark(p, c):
            inv_ref[p] = -1
            return c

        lo, hi = pad_lo_ref[N_EXPERTS], pad_hi_ref[N_EXPERTS]
        lax.fori_loop(lo // R, hi // R, fill_tile, 0)
        lax.fori_loop(lo // R, hi // R, drain_tile, 0)
        lax.fori_loop(lo, hi, mark, 0)


def _dispatch(pad_lo, pad_hi, dest, h2_rows, P):
    T = h2_rows.shape[0] // ROW_SUBLANES
    tm = min(ROW_TM, T)
    return pl.pallas_call(
        _dispatch_kernel,
        grid_spec=pltpu.PrefetchScalarGridSpec(
            num_scalar_prefetch=2,
            grid=(T // tm,),
            in_specs=[
                pl.BlockSpec((1, 1, TOP_K * tm), lambda i, lo, hi: (i, 0, 0), memory_space=pltpu.SMEM),
                pl.BlockSpec((tm * ROW_SUBLANES, LANES), lambda i, lo, hi: (i, 0)),
            ],
            out_specs=[pl.BlockSpec(memory_space=pl.ANY), pl.BlockSpec(memory_space=pltpu.SMEM)],
            scratch_shapes=[pltpu.VMEM((MOE_TM * ROW_SUBLANES, LANES), F32),
                            pltpu.SemaphoreType.DMA, pltpu.SemaphoreType.DMA],
        ),
        out_shape=[jax.ShapeDtypeStruct((P * ROW_SUBLANES, LANES), F32), jax.ShapeDtypeStruct((P,), jnp.int32)],
        compiler_params=_params(("arbitrary",)),
        name="dispatch",
    )(pad_lo, pad_hi, dest.reshape(T // tm, 1, TOP_K * tm), h2_rows)


def _expert_kernel(tile_e_ref, nact_ref, dst_prev_ref, x_ref, wg_ref, wu_ref, wd_ref, slots_ref, ybuf_ref, ssem):
    del tile_e_ref
    i = pl.program_id(0)
    nact = nact_ref[0]
    slot = lax.rem(i, 2)
    other = 1 - slot
    R = MOE_TM

    def scatter(dst_ref, s):
        for q in range(R):
            row = pl.multiple_of(dst_ref[0, 0, q], ROW_SUBLANES)
            pltpu.make_async_copy(ybuf_ref.at[s, pl.ds(q * ROW_SUBLANES, ROW_SUBLANES), :],
                                  slots_ref.at[pl.ds(row, ROW_SUBLANES), :], ssem.at[s]).start(priority=q % 2)

    def wait_scatter(s):
        pltpu.make_async_copy(ybuf_ref.at[s], slots_ref.at[pl.ds(0, R * ROW_SUBLANES), :], ssem.at[s]).wait()

    @pl.when(i == 0)
    def _():
        ybuf_ref[...] = jnp.zeros_like(ybuf_ref)

    @pl.when(i <= nact)
    def _():
        @pl.when(i >= 1)
        def _():
            wait_scatter(slot)

        scatter(dst_prev_ref, other)
        xb = _load_rows(x_ref, (), R).astype(BF16)
        g = _dot(xb, wg_ref[0].astype(BF16))
        u = _dot(xb, wu_ref[0].astype(BF16))
        a = (g * jax.nn.sigmoid(g) * u).astype(BF16)
        _store_rows(ybuf_ref, (slot,), _dot(a, wd_ref[0].astype(BF16)))

    @pl.when(i == nact + 1)
    def _():
        wait_scatter(slot)


def _experts(tile_e, nact, dst, xin, wg, wu, wd, n_slot_rows):
    ntiles = tile_e.shape[0]
    R = MOE_TM

    def rows(i, te, na):
        return (jnp.minimum(i, na[0] - 1), 0)

    def wsel(i, te, na):
        return (te[jnp.minimum(i, na[0] - 1)], 0, 0)

    return pl.pallas_call(
        _expert_kernel,
        grid_spec=pltpu.PrefetchScalarGridSpec(
            num_scalar_prefetch=2,
            grid=(ntiles + 2,),
            in_specs=[
                pl.BlockSpec((1, 1, R), lambda i, te, na: (i, 0, 0), memory_space=pltpu.SMEM),
                pl.BlockSpec((R * ROW_SUBLANES, LANES), rows),
                pl.BlockSpec((1, D_MODEL, D_EXPERT), wsel),
                pl.BlockSpec((1, D_MODEL, D_EXPERT), wsel),
                pl.BlockSpec((1, D_EXPERT, D_MODEL), wsel),
            ],
            out_specs=pl.BlockSpec(memory_space=pl.ANY),
            scratch_shapes=[
                pltpu.VMEM((2, R * ROW_SUBLANES, LANES), F32),
                pltpu.SemaphoreType.DMA((2,)),
            ],
        ),
        out_shape=jax.ShapeDtypeStruct((n_slot_rows * ROW_SUBLANES, LANES), F32),
        compiler_params=_params(("arbitrary",)),
        name="experts",
    )(tile_e, nact, dst, xin, wg, wu, wd)


def _combine_kernel(x2_ref, route_ref, s0_ref, s1_ref, out_ref):
    tm = x2_ref.shape[0]
    route = route_ref[...]
    w1 = route[:, 2:3]
    w2 = route[:, 3:4]
    out_ref[...] = x2_ref[...] + w1 * _load_rows(s0_ref, (), tm) + w2 * _load_rows(s1_ref, (), tm)


def _combine(x2, route, slots):
    T = x2.shape[0]
    tm = min(ROW_TM, T)
    nt = T // tm
    return pl.pallas_call(
        _combine_kernel,
        grid=(nt,),
        in_specs=[
            pl.BlockSpec((tm, D_MODEL), lambda i: (i, 0)),
            pl.BlockSpec((tm, LANES), lambda i: (i, 0)),
            pl.BlockSpec((tm * ROW_SUBLANES, LANES), lambda i: (i, 0)),
            pl.BlockSpec((tm * ROW_SUBLANES, LANES), lambda i: (nt + i, 0)),
        ],
        out_specs=pl.BlockSpec((tm, D_MODEL), lambda i: (i, 0)),
        out_shape=jax.ShapeDtypeStruct((T, D_MODEL), F32),
        compiler_params=_params(("arbitrary",)),
        name="combine",
    )(x2, route, slots, slots)


def _layer(x2d, B, S, norm1_g, w_in, w_gla_a2, b_gla_a, gla_out_norm_g, dil_q_norm_g, dil_k_norm_g,
           w_proj_gla, w_proj_attn, w_branch_gate, b_branch_gate, w_out, norm2_g,
           w_router_group, b_router_group, w_router_expert, b_router_expert, w_gate, w_up, w_down):
    T = B * S
    n_gla = N_GLA_TILES * HEAD_W
    w_gla = w_in[:, :n_gla].astype(BF16)
    w_dil = w_in[:, n_gla + GLA_RANK:].astype(BF16)
    w_ga = jnp.pad(w_in[:, n_gla:n_gla + GLA_RANK], ((0, 0), (0, LANES - GLA_RANK))).astype(BF16)
    w_a2 = jnp.pad(w_gla_a2, ((0, LANES - GLA_RANK), (0, 0))).astype(BF16)
    gains = [jnp.ones((HEAD_W,), F32)] * N_GLA_TILES
    for gi in range(DIL_GROUPS):
        gains += [jnp.tile(dil_q_norm_g[gi], DIL_HEADS) * (DIL_DH ** -0.5 * LOG2_E),
                  jnp.tile(dil_k_norm_g[gi], DIL_HEADS), jnp.ones((HEAD_W,), F32)]
    qk_gain = jnp.stack(gains).reshape(N_COL_TILES, 1, HEAD_W)

    gla_in, log_a, qkv0, qkv1, qkv2 = _inproj(
        x2d, norm1_g.reshape(1, -1), w_gla, w_dil, w_ga, w_a2, b_gla_a.reshape(1, -1), qk_gain, B, S)

    tri_incl = jnp.asarray(np.tril(np.ones((GLA_BLOCK, GLA_BLOCK), np.float32)), BF16)
    o_gla = _gla(gla_in, log_a, tri_incl, gla_out_norm_g.reshape(1, -1), B, S)

    outs, stats = [], []
    for gi, qkv in enumerate((qkv0.reshape(B, 1, S, QKV_W), qkv1, qkv2)):
        o, st = _dilated(qkv, _alibi_bias(gi), gi)
        outs.append(o)
        stats.append(st)
    outs[0] = outs[0].reshape(T, HEAD_W)
    stats[0] = stats[0].reshape(T, LANES)

    w_r = jnp.pad(jnp.concatenate([w_router_group, w_router_expert], axis=1),
                  ((0, 0), (0, LANES - N_GROUPS - N_EXPERTS)))
    w_rh = w_r.astype(BF16)
    w_rl = (w_r - w_rh.astype(F32)).astype(BF16)
    b_r = jnp.pad(jnp.concatenate([b_router_group, b_router_expert]), (0, LANES - N_GROUPS - N_EXPERTS))
    tm = min(MERGE_TM, S)
    tri_strict = jnp.asarray(np.tril(np.ones((tm, tm), np.float32), -1), BF16)
    x2, h2, route, cnt = _merge(
        x2d, o_gla, outs, stats, norm1_g.reshape(1, -1), w_branch_gate.astype(BF16),
        b_branch_gate.reshape(1, -1), w_proj_gla.astype(BF16), w_proj_attn.astype(BF16), w_out.astype(BF16),
        norm2_g.reshape(1, -1), jnp.concatenate([w_rh, w_rl], axis=1), b_r.reshape(1, -1), tri_strict, S)

    counts = cnt[0, :N_EXPERTS].astype(jnp.int32)
    pcounts = (counts + MOE_TM - 1) // MOE_TM * MOE_TM
    pends = jnp.cumsum(pcounts)
    pstarts = pends - pcounts
    eid = route[:, 0:TOP_K].astype(jnp.int32)
    rank = route[:, 4:4 + TOP_K].astype(jnp.int32)
    onehot = eid[:, :, None] == jnp.arange(N_EXPERTS, dtype=jnp.int32)[None, None, :]
    dest = rank + jnp.sum(jnp.where(onehot, pstarts[None, None, :], 0), axis=-1)
    P = T * TOP_K + N_EXPERTS * MOE_TM
    ntiles = P // MOE_TM
    tile_start = jnp.arange(ntiles, dtype=jnp.int32) * MOE_TM
    tile_e = jnp.minimum(jnp.sum(pends[None, :] <= tile_start[:, None], axis=1), N_EXPERTS - 1).astype(jnp.int32)
    nact = (pends[-1:] // MOE_TM).astype(jnp.int32)

    pad_lo = jnp.concatenate([pstarts + counts, pends[-1:]]).astype(jnp.int32)
    pad_hi = jnp.concatenate([pends, jnp.full((1,), P, jnp.int32)]).astype(jnp.int32)
    xin, inv = _dispatch(pad_lo, pad_hi, dest, h2, P)

    pad_tile = jnp.full((MOE_TM,), -1, jnp.int32)
    inv_dst = jnp.concatenate([pad_tile, inv, pad_tile])
    spare = T * TOP_K + jnp.tile(jnp.arange(MOE_TM, dtype=jnp.int32), ntiles + 2)
    dst = jnp.where(inv_dst >= 0, (inv_dst % TOP_K) * T + inv_dst // TOP_K, spare) * ROW_SUBLANES
    slots = _experts(tile_e, nact, dst.reshape(ntiles + 2, 1, MOE_TM), xin,
                     w_gate, w_up, w_down, T * TOP_K + MOE_TM)
    return _combine(x2, route, slots)


def kernel(x, norm1_g, w_in, w_gla_a2, b_gla_a, gla_out_norm_g, dil_q_norm_g, dil_k_norm_g, w_proj_gla,
           w_proj_attn, w_branch_gate, b_branch_gate, w_out, norm2_g, w_router_group, b_router_group,
           w_router_expert, b_router_expert, w_gate, w_up, w_down):
    B, S, D = x.shape
    assert D == D_MODEL and S % (DIL_BLOCK * DIL_PATTERNS[-1][1]) == 0
    x2d = x.reshape(B * S, D)
    params = (norm1_g, w_in, w_gla_a2, b_gla_a, gla_out_norm_g, dil_q_norm_g, dil_k_norm_g, w_proj_gla,
              w_proj_attn, w_branch_gate, b_branch_gate, w_out, norm2_g, w_router_group, b_router_group,
              w_router_expert, b_router_expert, w_gate, w_up, w_down)
    for layer in range(norm1_g.shape[0]):
        x2d = _layer(x2d, B, S, *(p[layer] for p in params))
    return x2d.reshape(B, S, D)
```

```python
import functools

import jax
import jax.numpy as jnp
import numpy as np
from jax import lax
from jax.experimental import pallas as pl
from jax.experimental.pallas import tpu as pltpu

F32 = jnp.float32
BF16 = jnp.bfloat16

D_MODEL = 1024
EPS = 1e-6
GLA_HEADS = 4
GLA_DK = 128
GLA_RANK = 16
GLA_TAU = 16.0
LOG2_E = 1.4426950408889634
GLA_SUBCHUNK = 16
GLA_BLOCK = 128
GLA_STEP_BLOCKS = 16
DIL_PATTERNS = ((128, 1), (512, 4), (2048, 16))
DIL_GROUPS = 3
DIL_HEADS = 4
DIL_DH = 128
DIL_BLOCK = 128
ALIBI_SLOPES = tuple(2.0 ** (-8.0 * (i + 1) / (DIL_GROUPS * DIL_HEADS)) for i in range(DIL_GROUPS * DIL_HEADS))
N_GROUPS = 4
EXPERTS_PER_GROUP = 8
N_EXPERTS = N_GROUPS * EXPERTS_PER_GROUP
TOP_K = 2
D_EXPERT = 512

HEAD_W = 512
N_GLA_TILES = 4
N_COL_TILES = N_GLA_TILES + 3 * DIL_GROUPS
QKV_W = 3 * HEAD_W
LANES = 128
NEG = -1e30

INPROJ_TM = 512
MERGE_TM = 512
MERGE_SPLIT = 1
MOE_TM = 512
CHUNK = 8
TILE_CHUNKS = MOE_TM // CHUNK
LOCAL_ROWS = 1280
LOCAL_CHUNKS = LOCAL_ROWS // CHUNK
DIL_STEP_BLOCKS = 16
VMEM_LIMIT = 56 * 1024 * 1024


def _dot(a, b):
    return jnp.dot(a, b, preferred_element_type=F32)


def _dot_nt(a, b):
    return lax.dot_general(a, b, (((1,), (1,)), ((), ())), preferred_element_type=F32)


def _dot_tn(a, b):
    return lax.dot_general(a, b, (((0,), (0,)), ((), ())), preferred_element_type=F32)


def _rms(x, g):
    return x * lax.rsqrt(jnp.mean(x * x, axis=-1, keepdims=True) + EPS) * g


def _params(sem):
    return pltpu.CompilerParams(dimension_semantics=sem, vmem_limit_bytes=VMEM_LIMIT)


def _head(c):
    return slice(c * LANES, (c + 1) * LANES)


ROW_SUBLANES = 8


def _load_rows(ref, lead, n):
    return jnp.concatenate([ref[lead + (pl.ds(c, n, stride=ROW_SUBLANES), slice(None))]
                            for c in range(ROW_SUBLANES)], axis=1)


def _store_rows(ref, lead, val, row0=0):
    n = val.shape[0]
    for c in range(ROW_SUBLANES):
        ref[lead + (pl.ds(row0 * ROW_SUBLANES + c, n, stride=ROW_SUBLANES), slice(None))] = val[:, _head(c)]


def _inproj_kernel(x_ref, g1_ref, wgla_ref, wdil_ref, wga_ref, wa2_ref, ba_ref, qkg_ref,
                   gla_ref, loga_ref, d0_ref, d1_ref, d2_ref, h_ref, ybuf_ref):
    tm = x_ref.shape[0]
    hb = _rms(x_ref[...], g1_ref[...]).astype(BF16)
    h_ref[...] = hb
    ga = _dot(hb, wga_ref[...])
    z = _dot(ga.astype(BF16), wa2_ref[...]) + ba_ref[...]
    log_sig = jnp.minimum(z, 0.0) - jnp.log(1.0 + jnp.exp(-jnp.abs(z)))
    loga_ref[...] = log_sig * (LOG2_E / GLA_TAU)

    dil_refs = (d0_ref, d1_ref, d2_ref)
    n_strided = 0
    for jj in range(N_COL_TILES):
        cols = slice(jj * HEAD_W, (jj + 1) * HEAD_W)
        if jj < N_GLA_TILES:
            w = wgla_ref[:, cols]
        else:
            w = wdil_ref[:, (jj - N_GLA_TILES) * HEAD_W:(jj - N_GLA_TILES + 1) * HEAD_W]
        y = _dot(h_ref[...], w)
        if jj < N_GLA_TILES:
            gla_ref[:, cols] = y.astype(BF16)
            continue
        gi, kind = divmod(jj - N_GLA_TILES, 3)
        d = DIL_PATTERNS[gi][1]
        ocols = [slice(kind * HEAD_W + c * LANES, kind * HEAD_W + (c + 1) * LANES) for c in range(DIL_HEADS)]
        if kind < 2:
            g = qkg_ref[jj]
            slabs = [_rms(y[:, _head(c)], g[:, _head(c)]) for c in range(DIL_HEADS)]
        else:
            slabs = [y[:, _head(c)] for c in range(DIL_HEADS)]
        out = dil_refs[gi]
        if d == 1:
            for c in range(DIL_HEADS):
                out[:, ocols[c]] = slabs[c].astype(BF16)
            continue
        buf = n_strided % ybuf_ref.shape[0]
        n_strided += 1
        for c in range(DIL_HEADS):
            ybuf_ref[buf, c] = slabs[c]
        for r in range(d):
            for c in range(DIL_HEADS):
                out[0, r, :, ocols[c]] = ybuf_ref[buf, c, pl.ds(r, tm // d, stride=d), :].astype(BF16)


def _resident(shape):
    return pl.BlockSpec(shape, lambda i: (0,) * len(shape), pipeline_mode=pl.Buffered(1))


def _inproj(x2d, g1, w_gla, w_dil, w_ga, w_a2, b_a, qk_gain, B, S):
    T = x2d.shape[0]
    tm = min(INPROJ_TM, S)
    tpb = S // tm

    def dil_spec(gi):
        d = DIL_PATTERNS[gi][1]
        if d == 1:
            return pl.BlockSpec((tm, QKV_W), lambda i: (i, 0))
        return pl.BlockSpec((1, d, tm // d, QKV_W), lambda i: (i // tpb, 0, i % tpb, 0))

    def dil_shape(gi):
        d = DIL_PATTERNS[gi][1]
        shape = (T, QKV_W) if d == 1 else (B, d, S // d, QKV_W)
        return jax.ShapeDtypeStruct(shape, BF16)

    return pl.pallas_call(
        _inproj_kernel,
        grid=(T // tm,),
        in_specs=[
            pl.BlockSpec((tm, D_MODEL), lambda i: (i, 0)),
            _resident(g1.shape), _resident(w_gla.shape), _resident(w_dil.shape), _resident(w_ga.shape),
            _resident(w_a2.shape),
            _resident(b_a.shape), _resident(qk_gain.shape),
        ],
        out_specs=[
            pl.BlockSpec((tm, N_GLA_TILES * HEAD_W), lambda i: (i, 0)),
            pl.BlockSpec((tm, HEAD_W), lambda i: (i, 0)),
            dil_spec(0), dil_spec(1), dil_spec(2),
        ],
        out_shape=[
            jax.ShapeDtypeStruct((T, N_GLA_TILES * HEAD_W), BF16),
            jax.ShapeDtypeStruct((T, HEAD_W), F32),
            dil_shape(0), dil_shape(1), dil_shape(2),
        ],
        scratch_shapes=[pltpu.VMEM((tm, D_MODEL), BF16), pltpu.VMEM((2, DIL_HEADS, tm, LANES), F32)],
        compiler_params=_params(("arbitrary",)),
        name="inproj",
    )(x2d, g1, w_gla, w_dil, w_ga, w_a2, b_a, qk_gain)


def _gla_kernel(q_ref, k_ref, v_ref, r_ref, la_ref, tri_ref, gn_ref, o_ref, st_ref):
    n = pl.program_id(1)

    @pl.when(n == 0)
    def _():
        st_ref[...] = jnp.zeros_like(st_ref)

    for blk in range(q_ref.shape[0] // GLA_BLOCK):
        _gla_block(q_ref, k_ref, v_ref, r_ref, la_ref, tri_ref, gn_ref, o_ref, st_ref,
                   slice(blk * GLA_BLOCK, (blk + 1) * GLA_BLOCK))


def _gla_block(q_ref, k_ref, v_ref, r_ref, la_ref, tri_ref, gn_ref, o_ref, st_ref, rows):
    L = GLA_BLOCK
    la = la_ref[rows, :]
    hi = la.astype(BF16)
    r1 = la - hi.astype(F32)
    mid = r1.astype(BF16)
    lo = (r1 - mid.astype(F32)).astype(BF16)
    tri = tri_ref[...]
    b_all = _dot(tri, hi) + _dot(tri, mid) + _dot(tri, lo)

    row = lax.broadcasted_iota(jnp.int32, (L, L), 0)
    col = lax.broadcasted_iota(jnp.int32, (L, L), 1)

    for h in range(GLA_HEADS):
        sl = _head(h)
        b = b_all[:, sl]
        q = q_ref[rows, sl].astype(F32) * (GLA_DK ** -0.5)
        k = k_ref[rows, sl].astype(F32)
        v = v_ref[rows, sl]

        att = jnp.zeros((L, L), F32)
        seg = L
        while seg > GLA_SUBCHUNK:
            half = seg // 2
            beta = jnp.concatenate(
                [jnp.broadcast_to(b[s + half - 1:s + half, :], (seg, GLA_DK)) for s in range(0, L, seg)], axis=0)
            rel = b - beta
            neg = jnp.minimum(rel, 0.0)
            qs = (q * jnp.exp2(neg)).astype(BF16)
            ks = (k * jnp.exp2(neg - rel)).astype(BF16)
            same = (row // seg) == (col // seg)
            mask = same & ((row % seg) >= half) & ((col % seg) < half)
            att = jnp.where(mask, _dot_nt(qs, ks), att)
            seg = half
        beta = jnp.concatenate(
            [jnp.zeros((GLA_SUBCHUNK, GLA_DK), F32)]
            + [jnp.broadcast_to(b[s - 1:s, :], (GLA_SUBCHUNK, GLA_DK)) for s in range(GLA_SUBCHUNK, L, GLA_SUBCHUNK)],
            axis=0)
        qs = (q * jnp.exp2(b - beta)).astype(BF16)
        ks = (k * jnp.exp2(beta - b)).astype(BF16)
        mask = ((row // GLA_SUBCHUNK) == (col // GLA_SUBCHUNK)) & (col <= row)
        att = jnp.where(mask, _dot_nt(qs, ks), att)

        state_t = st_ref[h]
        q0 = (q * jnp.exp2(b)).astype(BF16)
        o = _dot(att.astype(BF16), v) + _dot_nt(q0, state_t.astype(BF16))

        b_last = b[L - 1:L, :]
        k_end = (k * jnp.exp2(b_last - b)).astype(BF16)
        st_ref[h] = state_t * jnp.exp2(b_last) + _dot_tn(v, k_end)

        rr = r_ref[rows, sl].astype(F32)
        o_ref[rows, sl] = (_rms(o, gn_ref[...]) * (rr * jax.nn.sigmoid(rr))).astype(BF16)


def _gla(proj, log_a, tri, gn, B, S):
    L = GLA_STEP_BLOCKS * GLA_BLOCK
    nb = S // L
    return pl.pallas_call(
        _gla_kernel,
        grid=(B, nb),
        in_specs=[
            pl.BlockSpec((L, HEAD_W), lambda b, n: (b * nb + n, 0)),
            pl.BlockSpec((L, HEAD_W), lambda b, n: (b * nb + n, 1)),
            pl.BlockSpec((L, HEAD_W), lambda b, n: (b * nb + n, 2)),
            pl.BlockSpec((L, HEAD_W), lambda b, n: (b * nb + n, 3)),
            pl.BlockSpec((L, HEAD_W), lambda b, n: (b * nb + n, 0)),
            pl.BlockSpec((GLA_BLOCK, GLA_BLOCK), lambda b, n: (0, 0)),
            pl.BlockSpec((1, GLA_DK), lambda b, n: (0, 0)),
        ],
        out_specs=pl.BlockSpec((L, HEAD_W), lambda b, n: (b * nb + n, 0)),
        out_shape=jax.ShapeDtypeStruct((B * S, HEAD_W), BF16),
        scratch_shapes=[pltpu.VMEM((GLA_HEADS, GLA_DK, GLA_DK), F32)],
        compiler_params=_params(("arbitrary", "arbitrary")),
        name="gla",
    )(proj, proj, proj, proj, log_a, tri, gn)


def _dil_kernel(q_ref, kp_ref, kc_ref, vp_ref, vc_ref, bias_ref, o_ref, st_ref, *, nr, nq):
    first = pl.program_id(2) == 0
    Q = DIL_BLOCK
    col = lax.broadcasted_iota(jnp.int32, (Q, 2 * Q), 1)
    lane = lax.broadcasted_iota(jnp.int32, (Q, LANES), 1)
    for res in range(nr):
        for blk in range(nq):
            rows = slice(blk * Q, (blk + 1) * Q)
            prev = slice((blk - 1) * Q, blk * Q)
            stats = jnp.zeros((Q, LANES), F32)
            for h in range(DIL_HEADS):
                sl = _head(h)
                if blk == 0:
                    kprev, vprev = kp_ref[0, res, :, sl], vp_ref[0, res, :, sl]
                else:
                    kprev, vprev = kc_ref[0, res, prev, sl], vc_ref[0, res, prev, sl]
                kk = jnp.concatenate([kprev, kc_ref[0, res, rows, sl]], axis=0)
                vv = jnp.concatenate([vprev, vc_ref[0, res, rows, sl]], axis=0)
                s = _dot_nt(q_ref[0, res, rows, sl], kk) + bias_ref[h]
                if blk == 0:
                    s = jnp.where(jnp.logical_and(first, col < Q), NEG, s)
                m = jnp.max(s, axis=-1, keepdims=True)
                p = jnp.exp2(s - m)
                den = jnp.sum(p, axis=-1, keepdims=True)
                o_ref[0, res, rows, sl] = (_dot(p.astype(BF16), vv) / den).astype(BF16)
                stats = jnp.where((lane // 16) == h, m, stats)
                stats = jnp.where((lane // 16) == DIL_HEADS + h, den, stats)
            st_ref[0, res, rows, :] = stats


def _dilated(qkv, bias, gi):
    B, d, Lsub, _ = qkv.shape
    nq = min(DIL_STEP_BLOCKS, Lsub // DIL_BLOCK)
    nr = min(DIL_STEP_BLOCKS // nq, d)
    rows = nq * DIL_BLOCK

    def cur(c):
        return pl.BlockSpec((1, nr, rows, HEAD_W), lambda b, r, i: (b, r, i, c))

    def prev(c):
        return pl.BlockSpec((1, nr, DIL_BLOCK, HEAD_W), lambda b, r, i: (b, r, jnp.maximum(i * nq - 1, 0), c))

    return pl.pallas_call(
        functools.partial(_dil_kernel, nr=nr, nq=nq),
        grid=(B, d // nr, Lsub // rows),
        in_specs=[cur(0), prev(1), cur(1), prev(2), cur(2),
                  pl.BlockSpec((DIL_HEADS, DIL_BLOCK, 2 * DIL_BLOCK), lambda b, r, i: (0, 0, 0))],
        out_specs=[
            pl.BlockSpec((1, nr, rows, HEAD_W), lambda b, r, i: (b, r, i, 0)),
            pl.BlockSpec((1, nr, rows, LANES), lambda b, r, i: (b, r, i, 0)),
        ],
        out_shape=[
            jax.ShapeDtypeStruct((B, d, Lsub, HEAD_W), BF16),
            jax.ShapeDtypeStruct((B, d, Lsub, LANES), F32),
        ],
        compiler_params=_params(("arbitrary", "arbitrary", "arbitrary")),
        name=f"dilated{gi}",
    )(qkv, qkv, qkv, qkv, qkv, bias)


def _alibi_bias(gi):
    window, d = DIL_PATTERNS[gi]
    Q = DIL_BLOCK
    dist = np.arange(Q)[:, None] + Q - np.arange(2 * Q)[None, :]
    valid = (dist >= 0) & (dist <= window // d)
    slopes = np.asarray(ALIBI_SLOPES, np.float32).reshape(DIL_GROUPS, DIL_HEADS)[gi]
    bias = -slopes[:, None, None] * (d * dist).astype(np.float32) * np.float32(LOG2_E)
    return jnp.asarray(np.where(valid[None], bias, np.float32(NEG)), F32)


def _merge_rows(x_ref, og_ref, o0_ref, s0_ref, g1_ref, wbg_ref, bbg_ref, wpg_ref, wpa_ref, wout_ref,
                g2_ref, wr_ref, br_ref, x2_ref, obuf_ref, sbuf_ref, row0, n):
    rows = slice(row0, row0 + n)
    x = x_ref[rows, :]
    hb = _rms(x, g1_ref[...]).astype(BF16)
    gates = jax.nn.sigmoid(_dot(hb, wbg_ref[...]) + bbg_ref[...])

    stats = (s0_ref[rows, :], sbuf_ref[0, rows, :], sbuf_ref[1, rows, :])
    dens = [pltpu.roll(s, 64, 1) for s in stats]
    m_all = jnp.maximum(jnp.maximum(stats[0], stats[1]), stats[2])
    wts = [d * jnp.exp2(s - m_all) for s, d in zip(stats, dens)]
    inv = 1.0 / (wts[0] + wts[1] + wts[2])
    coef = [w * inv for w in wts]
    heads = []
    for h in range(DIL_HEADS):
        group_out = (o0_ref[rows, _head(h)].astype(F32), obuf_ref[0, h, rows, :], obuf_ref[1, h, rows, :])
        acc = jnp.zeros((n, DIL_DH), F32)
        for g in range(DIL_GROUPS):
            c = jnp.broadcast_to(coef[g][:, 16 * h:16 * h + 1], (n, DIL_DH))
            acc = acc + c * group_out[g]
        heads.append(acc.astype(BF16))
    o_att = jnp.concatenate(heads, axis=1)

    y = gates[:, :D_MODEL] * _dot(og_ref[rows, :], wpg_ref[...]) + gates[:, D_MODEL:] * _dot(o_att, wpa_ref[...])
    x2 = x + _dot(y.astype(BF16), wout_ref[...])
    x2_ref[rows, :] = x2

    h2 = _rms(x2, g2_ref[...])

    h2h = h2.astype(BF16)
    h2l = (h2 - h2h.astype(F32)).astype(BF16)
    wr = wr_ref[...]
    pa = _dot(h2h, wr)
    pb = _dot(h2l, wr)
    logit = pa[:, :LANES] + pa[:, LANES:] + pb[:, :LANES] + pb[:, LANES:] + br_ref[...]

    lane = lax.broadcasted_iota(jnp.int32, (n, LANES), 1).astype(F32)
    big = jnp.float32(1e9)
    gl = jnp.where(lane < N_GROUPS, logit, NEG)
    gmax = jnp.max(gl, axis=-1, keepdims=True)
    gsel = jnp.min(jnp.where(gl == gmax, lane, big), axis=-1, keepdims=True)
    g_p = 1.0 / jnp.sum(jnp.exp(gl - gmax), axis=-1, keepdims=True)
    lo = N_GROUPS + EXPERTS_PER_GROUP * gsel
    el = jnp.where((lane >= lo) & (lane < lo + EXPERTS_PER_GROUP), logit, NEG)
    v1 = jnp.max(el, axis=-1, keepdims=True)
    i1 = jnp.min(jnp.where(el == v1, lane, big), axis=-1, keepdims=True)
    el2 = jnp.where(lane == i1, NEG, el)
    v2 = jnp.max(el2, axis=-1, keepdims=True)
    i2 = jnp.min(jnp.where(el2 == v2, lane, big), axis=-1, keepdims=True)
    ex = jnp.exp(v2 - v1)
    w1 = g_p / (1.0 + ex)
    w2 = g_p * ex / (1.0 + ex)
    return i1 - N_GROUPS, i2 - N_GROUPS, w1, w2, h2h


def _merge_kernel(x_ref, og_ref, o0_ref, o1_ref, o2_ref, s0_ref, s1_ref, s2_ref, g1_ref, wbg_ref, bbg_ref,
                  wpg_ref, wpa_ref, wout_ref, g2_ref, wr_ref, br_ref, tri_ref, upper_ref,
                  x2_ref, xloc_ref, route_ref, cnt_ref, obuf_ref, sbuf_ref):
    tm = x_ref.shape[0]

    for slot, (o_ref, s_ref) in enumerate(((o1_ref, s1_ref), (o2_ref, s2_ref))):
        d = DIL_PATTERNS[slot + 1][1]
        for r in range(d):
            sbuf_ref[slot, pl.ds(r, tm // d, stride=d), :] = s_ref[0, r]
            for c in range(DIL_HEADS):
                obuf_ref[slot, c, pl.ds(r, tm // d, stride=d), :] = o_ref[0, r, :, _head(c)].astype(F32)

    sub = tm // MERGE_SPLIT
    parts = [_merge_rows(x_ref, og_ref, o0_ref, s0_ref, g1_ref, wbg_ref, bbg_ref, wpg_ref, wpa_ref, wout_ref,
                         g2_ref, wr_ref, br_ref, x2_ref, obuf_ref, sbuf_ref, p * sub, sub)
             for p in range(MERGE_SPLIT)]
    e1, e2, w1, w2, h2b = (jnp.concatenate([part[j] for part in parts], axis=0) for j in range(5))

    lane = lax.broadcasted_iota(jnp.int32, (tm, LANES), 1).astype(F32)
    oh1 = lane == e1
    oh2 = lane == e2
    onehot = jnp.where(oh1 | oh2, 1.0, 0.0)
    prefix = _dot(tri_ref[...], onehot.astype(BF16))
    cnt = jnp.sum(onehot, axis=0, keepdims=True)
    cnt_pad = jnp.floor((cnt + (CHUNK - 1)) * (1.0 / CHUNK)) * CHUNK
    seg_off = _dot(jnp.broadcast_to(cnt_pad, (8, LANES)).astype(BF16), upper_ref[...])[0:1, :]
    base = seg_off + prefix
    pos1 = jnp.sum(jnp.where(oh1, base, 0.0), axis=-1, keepdims=True)
    pos2 = jnp.sum(jnp.where(oh2, base, 0.0), axis=-1, keepdims=True)
    cnt_ref[...] = jnp.broadcast_to(cnt, cnt_ref.shape)

    slot = lax.broadcasted_iota(jnp.int32, (tm, LOCAL_ROWS), 1).astype(F32)
    perm = jnp.where((slot == pos1) | (slot == pos2), 1.0, 0.0).astype(BF16)
    xloc_ref[...] = _dot_tn(perm, h2b)

    route = jnp.zeros((tm, LANES), F32)
    for idx, val in enumerate((e1, e2, w1, w2, pos1, pos2)):
        route = jnp.where(lane == idx, val, route)
    route_ref[...] = route


def _merge(x2d, o_gla, outs, stats, g1, wbg, bbg, wpg, wpa, wout, g2, wr, br, tri, upper, S):
    T = x2d.shape[0]
    tm = min(MERGE_TM, S)
    assert TOP_K * tm + N_EXPERTS * CHUNK <= LOCAL_ROWS
    tpb = S // tm
    tok = lambda w: pl.BlockSpec((tm, w), lambda i: (i, 0))
    full = lambda a: pl.BlockSpec(a.shape, lambda i: (0,) * a.ndim)

    def res(gi, w):
        d = DIL_PATTERNS[gi][1]
        return pl.BlockSpec((1, d, tm // d, w), lambda i: (i // tpb, 0, i % tpb, 0))

    return pl.pallas_call(
        _merge_kernel,
        grid=(T // tm,),
        in_specs=[tok(D_MODEL), tok(HEAD_W), tok(HEAD_W), res(1, HEAD_W), res(2, HEAD_W),
                  tok(LANES), res(1, LANES), res(2, LANES),
                  full(g1), full(wbg), full(bbg), full(wpg), full(wpa), full(wout), full(g2),
                  full(wr), full(br), full(tri), full(upper)],
        out_specs=[tok(D_MODEL), pl.BlockSpec((LOCAL_ROWS, D_MODEL), lambda i: (i, 0)), tok(LANES),
                   pl.BlockSpec((8, LANES), lambda i: (i, 0))],
        out_shape=[
            jax.ShapeDtypeStruct((T, D_MODEL), F32),
            jax.ShapeDtypeStruct((T // tm * LOCAL_ROWS, D_MODEL), F32),
            jax.ShapeDtypeStruct((T, LANES), F32),
            jax.ShapeDtypeStruct((T // tm * 8, LANES), F32),
        ],
        scratch_shapes=[pltpu.VMEM((DIL_GROUPS - 1, DIL_HEADS, tm, LANES), F32),
                        pltpu.VMEM((DIL_GROUPS - 1, tm, LANES), F32)],
        compiler_params=_params(("arbitrary",)),
        name="merge",
    )(x2d, o_gla, *outs, *stats, g1, wbg, bbg, wpg, wpa, wout, g2, wr, br, tri, upper)


class _ChunkGather:
    def __init__(self, src_hbm, buf_ref, sem, nchunks):
        self.src, self.buf, self.sem, self.n = src_hbm, buf_ref, sem, nchunks

    def start(self, offs_ref, s):
        for j in range(self.n):
            row = pl.multiple_of(offs_ref[0, 0, j], CHUNK)
            pltpu.make_async_copy(self.src.at[pl.ds(row, CHUNK), :], self.buf.at[s, pl.ds(j * CHUNK, CHUNK), :],
                                  self.sem.at[s]).start(priority=j % 2)

    def wait(self, s):
        pltpu.make_async_copy(self.src.at[pl.ds(0, self.n * CHUNK), :], self.buf.at[s], self.sem.at[s]).wait()

    def step(self, offs_cur_ref, offs_next_ref):
        i = pl.program_id(0)
        slot = lax.rem(i, 2)

        @pl.when(i == 0)
        def _():
            self.start(offs_cur_ref, 0)

        self.wait(slot)

        @pl.when(i + 1 < pl.num_programs(0))
        def _():
            self.start(offs_next_ref, 1 - slot)

        return slot


def _expert_kernel(tile_e_ref, nact_ref, src_cur_ref, src_next_ref, xloc_ref, wg_ref, wu_ref, wd_ref,
                   y_ref, xbuf_ref, gsem):
    del tile_e_ref
    slot = _ChunkGather(xloc_ref, xbuf_ref, gsem, TILE_CHUNKS).step(src_cur_ref, src_next_ref)
    active = pl.program_id(0) < nact_ref[0]

    @pl.when(active)
    def _():
        xb = xbuf_ref[slot].astype(BF16)
        g = _dot(xb, wg_ref[0].astype(BF16))
        u = _dot(xb, wu_ref[0].astype(BF16))
        a = (g * jax.nn.sigmoid(g) * u).astype(BF16)
        y_ref[...] = _dot(a, wd_ref[0].astype(BF16))

    @pl.when(jnp.logical_not(active))
    def _():
        y_ref[...] = jnp.zeros_like(y_ref)


def _experts(tile_e, nact, csrc, xloc, wg, wu, wd):
    ntiles = tile_e.shape[0]
    R = MOE_TM

    def wsel(i, te, na):
        return (te[jnp.minimum(i, na[0] - 1)], 0, 0)

    def offs(shift):
        return pl.BlockSpec((1, 1, TILE_CHUNKS), lambda i, te, na: (jnp.minimum(i + shift, ntiles - 1), 0, 0),
                            memory_space=pltpu.SMEM)

    return pl.pallas_call(
        _expert_kernel,
        grid_spec=pltpu.PrefetchScalarGridSpec(
            num_scalar_prefetch=2,
            grid=(ntiles,),
            in_specs=[
                offs(0), offs(1),
                pl.BlockSpec(memory_space=pl.ANY),
                pl.BlockSpec((1, D_MODEL, D_EXPERT), wsel),
                pl.BlockSpec((1, D_MODEL, D_EXPERT), wsel),
                pl.BlockSpec((1, D_EXPERT, D_MODEL), wsel),
            ],
            out_specs=pl.BlockSpec((R, D_MODEL), lambda i, te, na: (i, 0)),
            scratch_shapes=[pltpu.VMEM((2, R, D_MODEL), F32), pltpu.SemaphoreType.DMA((2,))],
        ),
        out_shape=jax.ShapeDtypeStruct((ntiles * R, D_MODEL), F32),
        compiler_params=_params(("arbitrary",)),
        name="experts",
    )(tile_e, nact, csrc, csrc, xloc, wg, wu, wd)


def _combine_kernel(dst_cur_ref, dst_next_ref, x2_ref, route_ref, y_ref, out_ref, ybuf_ref, gsem):
    slot = _ChunkGather(y_ref, ybuf_ref, gsem, LOCAL_CHUNKS).step(dst_cur_ref, dst_next_ref)
    tm = x2_ref.shape[0]
    route = route_ref[...]
    w1, w2, pos1, pos2 = (route[:, c:c + 1] for c in range(2, 6))
    yl = ybuf_ref[slot].astype(BF16)
    cols = lax.broadcasted_iota(jnp.int32, (tm, LOCAL_ROWS), 1).astype(F32)
    pick1 = jnp.where(cols == pos1, 1.0, 0.0).astype(BF16)
    pick2 = jnp.where(cols == pos2, 1.0, 0.0).astype(BF16)
    out_ref[...] = x2_ref[...] + w1 * _dot(pick1, yl) + w2 * _dot(pick2, yl)


def _combine(cdst, x2, route, y):
    T = x2.shape[0]
    nt = cdst.shape[0]
    tm = T // nt

    def offs(shift):
        return pl.BlockSpec((1, 1, LOCAL_CHUNKS), lambda i: (jnp.minimum(i + shift, nt - 1), 0, 0),
                            memory_space=pltpu.SMEM)

    return pl.pallas_call(
        _combine_kernel,
        grid=(nt,),
        in_specs=[
            offs(0), offs(1),
            pl.BlockSpec((tm, D_MODEL), lambda i: (i, 0)),
            pl.BlockSpec((tm, LANES), lambda i: (i, 0)),
            pl.BlockSpec(memory_space=pl.ANY),
        ],
        out_specs=pl.BlockSpec((tm, D_MODEL), lambda i: (i, 0)),
        out_shape=jax.ShapeDtypeStruct((T, D_MODEL), F32),
        scratch_shapes=[pltpu.VMEM((2, LOCAL_ROWS, D_MODEL), F32), pltpu.SemaphoreType.DMA((2,))],
        compiler_params=_params(("arbitrary",)),
        name="combine",
    )(cdst, cdst, x2, route, y)


def _layer(x2d, B, S, norm1_g, w_in, w_gla_a2, b_gla_a, gla_out_norm_g, dil_q_norm_g, dil_k_norm_g,
           w_proj_gla, w_proj_attn, w_branch_gate, b_branch_gate, w_out, norm2_g,
           w_router_group, b_router_group, w_router_expert, b_router_expert, w_gate, w_up, w_down):
    T = B * S
    n_gla = N_GLA_TILES * HEAD_W
    w_gla = w_in[:, :n_gla].astype(BF16)
    w_dil = w_in[:, n_gla + GLA_RANK:].astype(BF16)
    w_ga = jnp.pad(w_in[:, n_gla:n_gla + GLA_RANK], ((0, 0), (0, LANES - GLA_RANK))).astype(BF16)
    w_a2 = jnp.pad(w_gla_a2, ((0, LANES - GLA_RANK), (0, 0))).astype(BF16)
    gains = [jnp.ones((HEAD_W,), F32)] * N_GLA_TILES
    for gi in range(DIL_GROUPS):
        gains += [jnp.tile(dil_q_norm_g[gi], DIL_HEADS) * (DIL_DH ** -0.5 * LOG2_E),
                  jnp.tile(dil_k_norm_g[gi], DIL_HEADS), jnp.ones((HEAD_W,), F32)]
    qk_gain = jnp.stack(gains).reshape(N_COL_TILES, 1, HEAD_W)

    gla_in, log_a, qkv0, qkv1, qkv2 = _inproj(
        x2d, norm1_g.reshape(1, -1), w_gla, w_dil, w_ga, w_a2, b_gla_a.reshape(1, -1), qk_gain, B, S)

    tri_incl = jnp.asarray(np.tril(np.ones((GLA_BLOCK, GLA_BLOCK), np.float32)), BF16)
    o_gla = _gla(gla_in, log_a, tri_incl, gla_out_norm_g.reshape(1, -1), B, S)

    outs, stats = [], []
    for gi, qkv in enumerate((qkv0.reshape(B, 1, S, QKV_W), qkv1, qkv2)):
        o, st = _dilated(qkv, _alibi_bias(gi), gi)
        outs.append(o)
        stats.append(st)
    outs[0] = outs[0].reshape(T, HEAD_W)
    stats[0] = stats[0].reshape(T, LANES)

    w_r = jnp.pad(jnp.concatenate([w_router_group, w_router_expert], axis=1),
                  ((0, 0), (0, LANES - N_GROUPS - N_EXPERTS)))
    w_rh = w_r.astype(BF16)
    w_rl = (w_r - w_rh.astype(F32)).astype(BF16)
    b_r = jnp.pad(jnp.concatenate([b_router_group, b_router_expert]), (0, LANES - N_GROUPS - N_EXPERTS))
    tm = min(MERGE_TM, S)
    tri_strict = jnp.asarray(np.tril(np.ones((tm, tm), np.float32), -1), BF16)
    upper = jnp.asarray(np.triu(np.ones((LANES, LANES), np.float32), 1), BF16)
    x2, xloc, route, cnt = _merge(
        x2d, o_gla, outs, stats, norm1_g.reshape(1, -1), w_branch_gate.astype(BF16),
        b_branch_gate.reshape(1, -1), w_proj_gla.astype(BF16), w_proj_attn.astype(BF16), w_out.astype(BF16),
        norm2_g.reshape(1, -1), jnp.concatenate([w_rh, w_rl], axis=1), b_r.reshape(1, -1), tri_strict, upper, S)

    csrc, cdst, tile_e, nact = _chunk_tables(cnt, T // tm)
    y = _experts(tile_e, nact, csrc, xloc, w_gate, w_up, w_down)
    return _combine(cdst, x2, route, y)


def _chunk_tables(cnt, nt):
    i32 = jnp.int32
    experts = jnp.arange(N_EXPERTS, dtype=i32)
    tiles = jnp.arange(nt, dtype=i32)
    c = (cnt.reshape(nt, 8, LANES)[:, 0, :N_EXPERTS].astype(i32) + CHUNK - 1) // CHUNK
    loff = jnp.cumsum(c, axis=1) - c
    ecum = jnp.cumsum(c, axis=0) - c
    tot = jnp.sum(c, axis=0)
    ptot = (tot + TILE_CHUNKS - 1) // TILE_CHUNKS * TILE_CHUNKS
    pend = jnp.cumsum(ptot)
    pstart = pend - ptot

    lc = jnp.arange(LOCAL_CHUNKS, dtype=i32)
    e_of = jnp.sum(((loff + c)[:, None, :] <= lc[None, :, None]).astype(i32), axis=-1)
    pick_e = e_of[:, :, None] == experts[None, None, :]
    glob = jnp.sum(jnp.where(pick_e, (pstart[None, :] + ecum - loff)[:, None, :], 0), axis=-1) + lc[None, :]
    cdst = jnp.where(e_of < N_EXPERTS, glob, 0) * CHUNK

    ntiles = (TOP_K * nt * MERGE_TM // CHUNK + nt * N_EXPERTS + N_EXPERTS * (TILE_CHUNKS - 1)
              + TILE_CHUNKS - 1) // TILE_CHUNKS
    g = jnp.arange(ntiles * TILE_CHUNKS, dtype=i32)
    e_g = jnp.minimum(jnp.sum((pend[None, :] <= g[:, None]).astype(i32), axis=1), N_EXPERTS - 1)
    k = g - jnp.sum(jnp.where(e_g[:, None] == experts[None, :], pstart[None, :], 0), axis=1)
    sel_e = (e_g[:, None, None] == experts[None, None, :])
    t_g = jnp.sum((sel_e & ((ecum + c)[None, :, :] <= k[:, None, None])).astype(i32), axis=(1, 2))
    valid = t_g < nt
    sel = sel_e & (jnp.minimum(t_g, nt - 1)[:, None, None] == tiles[None, :, None])
    local = jnp.sum(jnp.where(sel, (tiles[:, None] * LOCAL_CHUNKS + loff - ecum)[None, :, :], 0), axis=(1, 2)) + k
    zero_chunk = LOCAL_CHUNKS - 1
    csrc = jnp.where(valid, local, zero_chunk) * CHUNK

    tile_e = e_g[::TILE_CHUNKS]
    nact = (pend[-1:] // TILE_CHUNKS).astype(i32)
    return (csrc.reshape(ntiles, 1, TILE_CHUNKS), cdst.reshape(nt, 1, LOCAL_CHUNKS), tile_e, nact)


def kernel(x, norm1_g, w_in, w_gla_a2, b_gla_a, gla_out_norm_g, dil_q_norm_g, dil_k_norm_g, w_proj_gla,
           w_proj_attn, w_branch_gate, b_branch_gate, w_out, norm2_g, w_router_group, b_router_group,
           w_router_expert, b_router_expert, w_gate, w_up, w_down):
    B, S, D = x.shape
    assert D == D_MODEL and S % (DIL_BLOCK * DIL_PATTERNS[-1][1]) == 0
    x2d = x.reshape(B * S, D)
    params = (norm1_g, w_in, w_gla_a2, b_gla_a, gla_out_norm_g, dil_q_norm_g, dil_k_norm_g, w_proj_gla,
              w_proj_attn, w_branch_gate, b_branch_gate, w_out, norm2_g, w_router_group, b_router_group,
              w_router_expert, b_router_expert, w_gate, w_up, w_down)
    for layer in range(norm1_g.shape[0]):
        x2d = _layer(x2d, B, S, *(p[layer] for p in params))
    return x2d.reshape(B, S, D)
```

```python
import functools

import jax
import jax.numpy as jnp
import numpy as np
from jax import lax
from jax.experimental import pallas as pl
from jax.experimental.pallas import tpu as pltpu

F32 = jnp.float32
BF16 = jnp.bfloat16

D_MODEL = 1024
EPS = 1e-6
GLA_HEADS = 4
GLA_DK = 128
GLA_RANK = 16
GLA_TAU = 16.0
LOG2_E = 1.4426950408889634
GLA_SUBCHUNK = 16
GLA_BLOCK = 128
GLA_STEP_BLOCKS = 16
DIL_PATTERNS = ((128, 1), (512, 4), (2048, 16))
DIL_GROUPS = 3
DIL_HEADS = 4
DIL_DH = 128
DIL_BLOCK = 128
ALIBI_SLOPES = tuple(2.0 ** (-8.0 * (i + 1) / (DIL_GROUPS * DIL_HEADS)) for i in range(DIL_GROUPS * DIL_HEADS))
N_GROUPS = 4
EXPERTS_PER_GROUP = 8
N_EXPERTS = N_GROUPS * EXPERTS_PER_GROUP
TOP_K = 2
D_EXPERT = 512

HEAD_W = 512
N_GLA_TILES = 4
N_COL_TILES = N_GLA_TILES + 3 * DIL_GROUPS
QKV_W = 3 * HEAD_W
LANES = 128
NEG = -1e30

INPROJ_TM = 512
MERGE_TM = 512
MERGE_SPLIT = 1
MOE_TM = 512
CHUNK = 8
TILE_CHUNKS = MOE_TM // CHUNK
LOCAL_ROWS = 1280
LOCAL_CHUNKS = LOCAL_ROWS // CHUNK
DIL_STEP_BLOCKS = 16
VMEM_LIMIT = 56 * 1024 * 1024


def _dot(a, b):
    return jnp.dot(a, b, preferred_element_type=F32)


def _dot_nt(a, b):
    return lax.dot_general(a, b, (((1,), (1,)), ((), ())), preferred_element_type=F32)


def _dot_tn(a, b):
    return lax.dot_general(a, b, (((0,), (0,)), ((), ())), preferred_element_type=F32)


def _rms(x, g):
    return x * lax.rsqrt(jnp.mean(x * x, axis=-1, keepdims=True) + EPS) * g


def _params(sem):
    return pltpu.CompilerParams(dimension_semantics=sem, vmem_limit_bytes=VMEM_LIMIT)


def _head(c):
    return slice(c * LANES, (c + 1) * LANES)


ROW_SUBLANES = 8


def _load_rows(ref, lead, n):
    return jnp.concatenate([ref[lead + (pl.ds(c, n, stride=ROW_SUBLANES), slice(None))]
                            for c in range(ROW_SUBLANES)], axis=1)


def _store_rows(ref, lead, val, row0=0):
    n = val.shape[0]
    for c in range(ROW_SUBLANES):
        ref[lead + (pl.ds(row0 * ROW_SUBLANES + c, n, stride=ROW_SUBLANES), slice(None))] = val[:, _head(c)]


def _inproj_kernel(x_ref, g1_ref, wgla_ref, wdil_ref, wga_ref, wa2_ref, ba_ref, qkg_ref,
                   gla_ref, loga_ref, d0_ref, d1_ref, d2_ref, h_ref, ybuf_ref):
    tm = x_ref.shape[0]
    hb = _rms(x_ref[...], g1_ref[...]).astype(BF16)
    h_ref[...] = hb
    ga = _dot(hb, wga_ref[...])
    z = _dot(ga.astype(BF16), wa2_ref[...]) + ba_ref[...]
    log_sig = jnp.minimum(z, 0.0) - jnp.log(1.0 + jnp.exp(-jnp.abs(z)))
    loga_ref[...] = log_sig * (LOG2_E / GLA_TAU)

    dil_refs = (d0_ref, d1_ref, d2_ref)
    n_strided = 0
    for jj in range(N_COL_TILES):
        cols = slice(jj * HEAD_W, (jj + 1) * HEAD_W)
        if jj < N_GLA_TILES:
            w = wgla_ref[:, cols]
        else:
            w = wdil_ref[:, (jj - N_GLA_TILES) * HEAD_W:(jj - N_GLA_TILES + 1) * HEAD_W]
        y = _dot(h_ref[...], w)
        if jj < N_GLA_TILES:
            gla_ref[:, cols] = y.astype(BF16)
            continue
        gi, kind = divmod(jj - N_GLA_TILES, 3)
        d = DIL_PATTERNS[gi][1]
        ocols = [slice(kind * HEAD_W + c * LANES, kind * HEAD_W + (c + 1) * LANES) for c in range(DIL_HEADS)]
        if kind < 2:
            g = qkg_ref[jj]
            slabs = [_rms(y[:, _head(c)], g[:, _head(c)]) for c in range(DIL_HEADS)]
        else:
            slabs = [y[:, _head(c)] for c in range(DIL_HEADS)]
        out = dil_refs[gi]
        if d == 1:
            for c in range(DIL_HEADS):
                out[:, ocols[c]] = slabs[c].astype(BF16)
            continue
        buf = n_strided % ybuf_ref.shape[0]
        n_strided += 1
        for c in range(DIL_HEADS):
            ybuf_ref[buf, c] = slabs[c]
        for r in range(d):
            for c in range(DIL_HEADS):
                out[0, r, :, ocols[c]] = ybuf_ref[buf, c, pl.ds(r, tm // d, stride=d), :].astype(BF16)


def _resident(shape):
    return pl.BlockSpec(shape, lambda i: (0,) * len(shape), pipeline_mode=pl.Buffered(1))


def _inproj(x2d, g1, w_gla, w_dil, w_ga, w_a2, b_a, qk_gain, B, S):
    T = x2d.shape[0]
    tm = min(INPROJ_TM, S)
    tpb = S // tm

    def dil_spec(gi):
        d = DIL_PATTERNS[gi][1]
        if d == 1:
            return pl.BlockSpec((tm, QKV_W), lambda i: (i, 0))
        return pl.BlockSpec((1, d, tm // d, QKV_W), lambda i: (i // tpb, 0, i % tpb, 0))

    def dil_shape(gi):
        d = DIL_PATTERNS[gi][1]
        shape = (T, QKV_W) if d == 1 else (B, d, S // d, QKV_W)
        return jax.ShapeDtypeStruct(shape, BF16)

    return pl.pallas_call(
        _inproj_kernel,
        grid=(T // tm,),
        in_specs=[
            pl.BlockSpec((tm, D_MODEL), lambda i: (i, 0)),
            _resident(g1.shape), _resident(w_gla.shape), _resident(w_dil.shape), _resident(w_ga.shape),
            _resident(w_a2.shape),
            _resident(b_a.shape), _resident(qk_gain.shape),
        ],
        out_specs=[
            pl.BlockSpec((tm, N_GLA_TILES * HEAD_W), lambda i: (i, 0)),
            pl.BlockSpec((tm, HEAD_W), lambda i: (i, 0)),
            dil_spec(0), dil_spec(1), dil_spec(2),
        ],
        out_shape=[
            jax.ShapeDtypeStruct((T, N_GLA_TILES * HEAD_W), BF16),
            jax.ShapeDtypeStruct((T, HEAD_W), F32),
            dil_shape(0), dil_shape(1), dil_shape(2),
        ],
        scratch_shapes=[pltpu.VMEM((tm, D_MODEL), BF16), pltpu.VMEM((2, DIL_HEADS, tm, LANES), F32)],
        compiler_params=_params(("arbitrary",)),
        name="inproj",
    )(x2d, g1, w_gla, w_dil, w_ga, w_a2, b_a, qk_gain)


def _gla_kernel(q_ref, k_ref, v_ref, r_ref, la_ref, tri_ref, gn_ref, o_ref, st_ref):
    n = pl.program_id(1)

    @pl.when(n == 0)
    def _():
        st_ref[...] = jnp.zeros_like(st_ref)

    for blk in range(q_ref.shape[0] // GLA_BLOCK):
        _gla_block(q_ref, k_ref, v_ref, r_ref, la_ref, tri_ref, gn_ref, o_ref, st_ref,
                   slice(blk * GLA_BLOCK, (blk + 1) * GLA_BLOCK))


def _gla_block(q_ref, k_ref, v_ref, r_ref, la_ref, tri_ref, gn_ref, o_ref, st_ref, rows):
    L = GLA_BLOCK
    la = la_ref[rows, :]
    hi = la.astype(BF16)
    r1 = la - hi.astype(F32)
    mid = r1.astype(BF16)
    lo = (r1 - mid.astype(F32)).astype(BF16)
    tri = tri_ref[...]
    b_all = _dot(tri, hi) + _dot(tri, mid) + _dot(tri, lo)

    row = lax.broadcasted_iota(jnp.int32, (L, L), 0)
    col = lax.broadcasted_iota(jnp.int32, (L, L), 1)

    for h in range(GLA_HEADS):
        sl = _head(h)
        b = b_all[:, sl]
        q = q_ref[rows, sl].astype(F32) * (GLA_DK ** -0.5)
        k = k_ref[rows, sl].astype(F32)
        v = v_ref[rows, sl]

        att = jnp.zeros((L, L), F32)
        seg = L
        while seg > GLA_SUBCHUNK:
            half = seg // 2
            beta = jnp.concatenate(
                [jnp.broadcast_to(b[s + half - 1:s + half, :], (seg, GLA_DK)) for s in range(0, L, seg)], axis=0)
            rel = b - beta
            neg = jnp.minimum(rel, 0.0)
            qs = (q * jnp.exp2(neg)).astype(BF16)
            ks = (k * jnp.exp2(neg - rel)).astype(BF16)
            same = (row // seg) == (col // seg)
            mask = same & ((row % seg) >= half) & ((col % seg) < half)
            att = jnp.where(mask, _dot_nt(qs, ks), att)
            seg = half
        beta = jnp.concatenate(
            [jnp.zeros((GLA_SUBCHUNK, GLA_DK), F32)]
            + [jnp.broadcast_to(b[s - 1:s, :], (GLA_SUBCHUNK, GLA_DK)) for s in range(GLA_SUBCHUNK, L, GLA_SUBCHUNK)],
            axis=0)
        qs = (q * jnp.exp2(b - beta)).astype(BF16)
        ks = (k * jnp.exp2(beta - b)).astype(BF16)
        mask = ((row // GLA_SUBCHUNK) == (col // GLA_SUBCHUNK)) & (col <= row)
        att = jnp.where(mask, _dot_nt(qs, ks), att)

        state_t = st_ref[h]
        q0 = (q * jnp.exp2(b)).astype(BF16)
        o = _dot(att.astype(BF16), v) + _dot_nt(q0, state_t.astype(BF16))

        b_last = b[L - 1:L, :]
        k_end = (k * jnp.exp2(b_last - b)).astype(BF16)
        st_ref[h] = state_t * jnp.exp2(b_last) + _dot_tn(v, k_end)

        rr = r_ref[rows, sl].astype(F32)
        o_ref[rows, sl] = (_rms(o, gn_ref[...]) * (rr * jax.nn.sigmoid(rr))).astype(BF16)


def _gla(proj, log_a, tri, gn, B, S):
    L = GLA_STEP_BLOCKS * GLA_BLOCK
    nb = S // L
    return pl.pallas_call(
        _gla_kernel,
        grid=(B, nb),
        in_specs=[
            pl.BlockSpec((L, HEAD_W), lambda b, n: (b * nb + n, 0)),
            pl.BlockSpec((L, HEAD_W), lambda b, n: (b * nb + n, 1)),
            pl.BlockSpec((L, HEAD_W), lambda b, n: (b * nb + n, 2)),
            pl.BlockSpec((L, HEAD_W), lambda b, n: (b * nb + n, 3)),
            pl.BlockSpec((L, HEAD_W), lambda b, n: (b * nb + n, 0)),
            pl.BlockSpec((GLA_BLOCK, GLA_BLOCK), lambda b, n: (0, 0)),
            pl.BlockSpec((1, GLA_DK), lambda b, n: (0, 0)),
        ],
        out_specs=pl.BlockSpec((L, HEAD_W), lambda b, n: (b * nb + n, 0)),
        out_shape=jax.ShapeDtypeStruct((B * S, HEAD_W), BF16),
        scratch_shapes=[pltpu.VMEM((GLA_HEADS, GLA_DK, GLA_DK), F32)],
        compiler_params=_params(("arbitrary", "arbitrary")),
        name="gla",
    )(proj, proj, proj, proj, log_a, tri, gn)


def _dil_kernel(q_ref, kp_ref, kc_ref, vp_ref, vc_ref, bias_ref, o_ref, st_ref, *, nr, nq):
    first = pl.program_id(2) == 0
    Q = DIL_BLOCK
    col = lax.broadcasted_iota(jnp.int32, (Q, 2 * Q), 1)
    lane = lax.broadcasted_iota(jnp.int32, (Q, LANES), 1)
    for res in range(nr):
        for blk in range(nq):
            rows = slice(blk * Q, (blk + 1) * Q)
            prev = slice((blk - 1) * Q, blk * Q)
            stats = jnp.zeros((Q, LANES), F32)
            for h in range(DIL_HEADS):
                sl = _head(h)
                if blk == 0:
                    kprev, vprev = kp_ref[0, res, :, sl], vp_ref[0, res, :, sl]
                else:
                    kprev, vprev = kc_ref[0, res, prev, sl], vc_ref[0, res, prev, sl]
                kk = jnp.concatenate([kprev, kc_ref[0, res, rows, sl]], axis=0)
                vv = jnp.concatenate([vprev, vc_ref[0, res, rows, sl]], axis=0)
                s = _dot_nt(q_ref[0, res, rows, sl], kk) + bias_ref[h]
                if blk == 0:
                    s = jnp.where(jnp.logical_and(first, col < Q), NEG, s)
                m = jnp.max(s, axis=-1, keepdims=True)
                p = jnp.exp2(s - m)
                den = jnp.sum(p, axis=-1, keepdims=True)
                o_ref[0, res, rows, sl] = (_dot(p.astype(BF16), vv) / den).astype(BF16)
                stats = jnp.where((lane // 16) == h, m, stats)
                stats = jnp.where((lane // 16) == DIL_HEADS + h, den, stats)
            st_ref[0, res, rows, :] = stats


def _dilated(qkv, bias, gi):
    B, d, Lsub, _ = qkv.shape
    nq = min(DIL_STEP_BLOCKS, Lsub // DIL_BLOCK)
    nr = min(DIL_STEP_BLOCKS // nq, d)
    rows = nq * DIL_BLOCK

    def cur(c):
        return pl.BlockSpec((1, nr, rows, HEAD_W), lambda b, r, i: (b, r, i, c))

    def prev(c):
        return pl.BlockSpec((1, nr, DIL_BLOCK, HEAD_W), lambda b, r, i: (b, r, jnp.maximum(i * nq - 1, 0), c))

    return pl.pallas_call(
        functools.partial(_dil_kernel, nr=nr, nq=nq),
        grid=(B, d // nr, Lsub // rows),
        in_specs=[cur(0), prev(1), cur(1), prev(2), cur(2),
                  pl.BlockSpec((DIL_HEADS, DIL_BLOCK, 2 * DIL_BLOCK), lambda b, r, i: (0, 0, 0))],
        out_specs=[
            pl.BlockSpec((1, nr, rows, HEAD_W), lambda b, r, i: (b, r, i, 0)),
            pl.BlockSpec((1, nr, rows, LANES), lambda b, r, i: (b, r, i, 0)),
        ],
        out_shape=[
            jax.ShapeDtypeStruct((B, d, Lsub, HEAD_W), BF16),
            jax.ShapeDtypeStruct((B, d, Lsub, LANES), F32),
        ],
        compiler_params=_params(("arbitrary", "arbitrary", "arbitrary")),
        name=f"dilated{gi}",
    )(qkv, qkv, qkv, qkv, qkv, bias)


def _alibi_bias(gi):
    window, d = DIL_PATTERNS[gi]
    Q = DIL_BLOCK
    dist = np.arange(Q)[:, None] + Q - np.arange(2 * Q)[None, :]
    valid = (dist >= 0) & (dist <= window // d)
    slopes = np.asarray(ALIBI_SLOPES, np.float32).reshape(DIL_GROUPS, DIL_HEADS)[gi]
    bias = -slopes[:, None, None] * (d * dist).astype(np.float32) * np.float32(LOG2_E)
    return jnp.asarray(np.where(valid[None], bias, np.float32(NEG)), F32)


def _merge_rows(x_ref, og_ref, o0_ref, s0_ref, g1_ref, wbg_ref, bbg_ref, wpg_ref, wpa_ref, wout_ref,
                g2_ref, wr_ref, br_ref, x2_ref, obuf_ref, sbuf_ref, row0, n):
    rows = slice(row0, row0 + n)
    x = x_ref[rows, :]
    hb = _rms(x, g1_ref[...]).astype(BF16)
    gates = jax.nn.sigmoid(_dot(hb, wbg_ref[...]) + bbg_ref[...])

    stats = (s0_ref[rows, :], sbuf_ref[0, rows, :], sbuf_ref[1, rows, :])
    dens = [pltpu.roll(s, 64, 1) for s in stats]
    m_all = jnp.maximum(jnp.maximum(stats[0], stats[1]), stats[2])
    wts = [d * jnp.exp2(s - m_all) for s, d in zip(stats, dens)]
    inv = 1.0 / (wts[0] + wts[1] + wts[2])
    coef = [w * inv for w in wts]
    heads = []
    for h in range(DIL_HEADS):
        group_out = (o0_ref[rows, _head(h)].astype(F32), obuf_ref[0, h, rows, :], obuf_ref[1, h, rows, :])
        acc = jnp.zeros((n, DIL_DH), F32)
        for g in range(DIL_GROUPS):
            c = jnp.broadcast_to(coef[g][:, 16 * h:16 * h + 1], (n, DIL_DH))
            acc = acc + c * group_out[g]
        heads.append(acc.astype(BF16))
    o_att = jnp.concatenate(heads, axis=1)

    y = gates[:, :D_MODEL] * _dot(og_ref[rows, :], wpg_ref[...]) + gates[:, D_MODEL:] * _dot(o_att, wpa_ref[...])
    x2 = x + _dot(y.astype(BF16), wout_ref[...])
    x2_ref[rows, :] = x2

    h2 = _rms(x2, g2_ref[...])

    h2h = h2.astype(BF16)
    h2l = (h2 - h2h.astype(F32)).astype(BF16)
    wr = wr_ref[...]
    pa = _dot(h2h, wr)
    pb = _dot(h2l, wr)
    logit = pa[:, :LANES] + pa[:, LANES:] + pb[:, :LANES] + pb[:, LANES:] + br_ref[...]

    lane = lax.broadcasted_iota(jnp.int32, (n, LANES), 1).astype(F32)
    big = jnp.float32(1e9)
    gl = jnp.where(lane < N_GROUPS, logit, NEG)
    gmax = jnp.max(gl, axis=-1, keepdims=True)
    gsel = jnp.min(jnp.where(gl == gmax, lane, big), axis=-1, keepdims=True)
    g_p = 1.0 / jnp.sum(jnp.exp(gl - gmax), axis=-1, keepdims=True)
    lo = N_GROUPS + EXPERTS_PER_GROUP * gsel
    el = jnp.where((lane >= lo) & (lane < lo + EXPERTS_PER_GROUP), logit, NEG)
    v1 = jnp.max(el, axis=-1, keepdims=True)
    i1 = jnp.min(jnp.where(el == v1, lane, big), axis=-1, keepdims=True)
    el2 = jnp.where(lane == i1, NEG, el)
    v2 = jnp.max(el2, axis=-1, keepdims=True)
    i2 = jnp.min(jnp.where(el2 == v2, lane, big), axis=-1, keepdims=True)
    ex = jnp.exp(v2 - v1)
    w1 = g_p / (1.0 + ex)
    w2 = g_p * ex / (1.0 + ex)
    return i1 - N_GROUPS, i2 - N_GROUPS, w1, w2, h2h


def _merge_kernel(x_ref, og_ref, o0_ref, o1_ref, o2_ref, s0_ref, s1_ref, s2_ref, g1_ref, wbg_ref, bbg_ref,
                  wpg_ref, wpa_ref, wout_ref, g2_ref, wr_ref, br_ref, tri_ref, upper_ref,
                  x2_ref, xloc_ref, route_ref, cnt_ref, obuf_ref, sbuf_ref):
    tm = x_ref.shape[0]

    for slot, (o_ref, s_ref) in enumerate(((o1_ref, s1_ref), (o2_ref, s2_ref))):
        d = DIL_PATTERNS[slot + 1][1]
        for r in range(d):
            sbuf_ref[slot, pl.ds(r, tm // d, stride=d), :] = s_ref[0, r]
            for c in range(DIL_HEADS):
                obuf_ref[slot, c, pl.ds(r, tm // d, stride=d), :] = o_ref[0, r, :, _head(c)].astype(F32)

    sub = tm // MERGE_SPLIT
    parts = [_merge_rows(x_ref, og_ref, o0_ref, s0_ref, g1_ref, wbg_ref, bbg_ref, wpg_ref, wpa_ref, wout_ref,
                         g2_ref, wr_ref, br_ref, x2_ref, obuf_ref, sbuf_ref, p * sub, sub)
             for p in range(MERGE_SPLIT)]
    e1, e2, w1, w2, h2b = (jnp.concatenate([part[j] for part in parts], axis=0) for j in range(5))

    lane = lax.broadcasted_iota(jnp.int32, (tm, LANES), 1).astype(F32)
    oh1 = lane == e1
    oh2 = lane == e2
    onehot = jnp.where(oh1 | oh2, 1.0, 0.0)
    prefix = _dot(tri_ref[...], onehot.astype(BF16))
    cnt = jnp.sum(onehot, axis=0, keepdims=True)
    cnt_pad = jnp.floor((cnt + (CHUNK - 1)) * (1.0 / CHUNK)) * CHUNK
    seg_off = _dot(jnp.broadcast_to(cnt_pad, (8, LANES)).astype(BF16), upper_ref[...])[0:1, :]
    base = seg_off + prefix
    pos1 = jnp.sum(jnp.where(oh1, base, 0.0), axis=-1, keepdims=True)
    pos2 = jnp.sum(jnp.where(oh2, base, 0.0), axis=-1, keepdims=True)
    cnt_ref[...] = jnp.broadcast_to(cnt, cnt_ref.shape)

    route = jnp.zeros((tm, LANES), F32)
    for idx, val in enumerate((e1, e2, w1, w2, pos1, pos2)):
        route = jnp.where(lane == idx, val, route)
    route_ref[...] = route

    route_t = route.T
    slot = lax.broadcasted_iota(jnp.int32, (LOCAL_ROWS, tm), 0).astype(F32)
    perm = jnp.where((slot == route_t[4:5, :]) | (slot == route_t[5:6, :]), 1.0, 0.0).astype(BF16)
    xloc_ref[...] = _dot(perm, h2b)


def _merge(x2d, o_gla, outs, stats, g1, wbg, bbg, wpg, wpa, wout, g2, wr, br, tri, upper, S):
    T = x2d.shape[0]
    tm = min(MERGE_TM, S)
    assert TOP_K * tm + N_EXPERTS * CHUNK <= LOCAL_ROWS
    tpb = S // tm
    tok = lambda w: pl.BlockSpec((tm, w), lambda i: (i, 0))
    full = lambda a: pl.BlockSpec(a.shape, lambda i: (0,) * a.ndim)

    def res(gi, w):
        d = DIL_PATTERNS[gi][1]
        return pl.BlockSpec((1, d, tm // d, w), lambda i: (i // tpb, 0, i % tpb, 0))

    return pl.pallas_call(
        _merge_kernel,
        grid=(T // tm,),
        in_specs=[tok(D_MODEL), tok(HEAD_W), tok(HEAD_W), res(1, HEAD_W), res(2, HEAD_W),
                  tok(LANES), res(1, LANES), res(2, LANES),
                  full(g1), full(wbg), full(bbg), full(wpg), full(wpa), full(wout), full(g2),
                  full(wr), full(br), full(tri), full(upper)],
        out_specs=[tok(D_MODEL), pl.BlockSpec((LOCAL_ROWS, D_MODEL), lambda i: (i, 0)), tok(LANES),
                   pl.BlockSpec((8, LANES), lambda i: (i, 0))],
        out_shape=[
            jax.ShapeDtypeStruct((T, D_MODEL), F32),
            jax.ShapeDtypeStruct((T // tm * LOCAL_ROWS, D_MODEL), F32),
            jax.ShapeDtypeStruct((T, LANES), F32),
            jax.ShapeDtypeStruct((T // tm * 8, LANES), F32),
        ],
        scratch_shapes=[pltpu.VMEM((DIL_GROUPS - 1, DIL_HEADS, tm, LANES), F32),
                        pltpu.VMEM((DIL_GROUPS - 1, tm, LANES), F32)],
        compiler_params=_params(("arbitrary",)),
        name="merge",
    )(x2d, o_gla, *outs, *stats, g1, wbg, bbg, wpg, wpa, wout, g2, wr, br, tri, upper)


class _ChunkGather:
    def __init__(self, src_hbm, buf_ref, sem, nchunks):
        self.src, self.buf, self.sem, self.n = src_hbm, buf_ref, sem, nchunks

    def start(self, offs_ref, s):
        for j in range(self.n):
            row = pl.multiple_of(offs_ref[0, 0, j], CHUNK)
            pltpu.make_async_copy(self.src.at[pl.ds(row, CHUNK), :], self.buf.at[s, pl.ds(j * CHUNK, CHUNK), :],
                                  self.sem.at[s]).start()

    def wait(self, s):
        pltpu.make_async_copy(self.src.at[pl.ds(0, self.n * CHUNK), :], self.buf.at[s], self.sem.at[s]).wait()

    def step(self, offs_cur_ref, offs_next_ref):
        i = pl.program_id(0)
        slot = lax.rem(i, 2)

        @pl.when(i == 0)
        def _():
            self.start(offs_cur_ref, 0)

        self.wait(slot)

        @pl.when(i + 1 < pl.num_programs(0))
        def _():
            self.start(offs_next_ref, 1 - slot)

        return slot


def _expert_kernel(tile_e_ref, nact_ref, src_cur_ref, src_next_ref, xloc_ref, wg_ref, wu_ref, wd_ref,
                   y_ref, xbuf_ref, wgu_ref, wdn_ref, gsem):
    i = pl.program_id(0)
    slot = _ChunkGather(xloc_ref, xbuf_ref, gsem, TILE_CHUNKS).step(src_cur_ref, src_next_ref)
    active = i < nact_ref[0]

    @pl.when(jnp.logical_and(active, jnp.logical_or(i == 0, tile_e_ref[i] != tile_e_ref[jnp.maximum(i - 1, 0)])))
    def _():
        wgu_ref[:, :D_EXPERT] = wg_ref[0].astype(BF16)
        wgu_ref[:, D_EXPERT:] = wu_ref[0].astype(BF16)
        wdn_ref[...] = wd_ref[0].astype(BF16)

    @pl.when(active)
    def _():
        xb = xbuf_ref[slot].astype(BF16)
        gu = _dot(xb, wgu_ref[...])
        g, u = gu[:, :D_EXPERT], gu[:, D_EXPERT:]
        a = (g * jax.nn.sigmoid(g) * u).astype(BF16)
        y_ref[...] = _dot(a, wdn_ref[...])

    @pl.when(jnp.logical_not(active))
    def _():
        y_ref[...] = jnp.zeros_like(y_ref)


def _experts(tile_e, nact, csrc, xloc, wg, wu, wd):
    ntiles = tile_e.shape[0]
    R = MOE_TM

    def wsel(i, te, na):
        return (te[jnp.minimum(i, na[0] - 1)], 0, 0)

    def offs(shift):
        return pl.BlockSpec((1, 1, TILE_CHUNKS), lambda i, te, na: (jnp.minimum(i + shift, ntiles - 1), 0, 0),
                            memory_space=pltpu.SMEM)

    return pl.pallas_call(
        _expert_kernel,
        grid_spec=pltpu.PrefetchScalarGridSpec(
            num_scalar_prefetch=2,
            grid=(ntiles,),
            in_specs=[
                offs(0), offs(1),
                pl.BlockSpec(memory_space=pl.ANY),
                pl.BlockSpec((1, D_MODEL, D_EXPERT), wsel),
                pl.BlockSpec((1, D_MODEL, D_EXPERT), wsel),
                pl.BlockSpec((1, D_EXPERT, D_MODEL), wsel),
            ],
            out_specs=pl.BlockSpec((R, D_MODEL), lambda i, te, na: (i, 0)),
            scratch_shapes=[pltpu.VMEM((2, R, D_MODEL), F32), pltpu.VMEM((D_MODEL, 2 * D_EXPERT), BF16),
                            pltpu.VMEM((D_EXPERT, D_MODEL), BF16), pltpu.SemaphoreType.DMA((2,))],
        ),
        out_shape=jax.ShapeDtypeStruct((ntiles * R, D_MODEL), F32),
        compiler_params=_params(("arbitrary",)),
        name="experts",
    )(tile_e, nact, csrc, csrc, xloc, wg, wu, wd)


def _combine_kernel(dst_cur_ref, dst_next_ref, x2_ref, route_ref, y_ref, out_ref, ybuf_ref, gsem):
    slot = _ChunkGather(y_ref, ybuf_ref, gsem, LOCAL_CHUNKS).step(dst_cur_ref, dst_next_ref)
    tm = x2_ref.shape[0]
    route = route_ref[...]
    w1, w2, pos1, pos2 = (route[:, c:c + 1] for c in range(2, 6))
    yl = ybuf_ref[slot].astype(BF16)
    cols = lax.broadcasted_iota(jnp.int32, (tm, LOCAL_ROWS), 1).astype(F32)
    pick1 = jnp.where(cols == pos1, 1.0, 0.0).astype(BF16)
    pick2 = jnp.where(cols == pos2, 1.0, 0.0).astype(BF16)
    out_ref[...] = x2_ref[...] + w1 * _dot(pick1, yl) + w2 * _dot(pick2, yl)


def _combine(cdst, x2, route, y):
    T = x2.shape[0]
    nt = cdst.shape[0]
    tm = T // nt

    def offs(shift):
        return pl.BlockSpec((1, 1, LOCAL_CHUNKS), lambda i: (jnp.minimum(i + shift, nt - 1), 0, 0),
                            memory_space=pltpu.SMEM)

    return pl.pallas_call(
        _combine_kernel,
        grid=(nt,),
        in_specs=[
            offs(0), offs(1),
            pl.BlockSpec((tm, D_MODEL), lambda i: (i, 0)),
            pl.BlockSpec((tm, LANES), lambda i: (i, 0)),
            pl.BlockSpec(memory_space=pl.ANY),
        ],
        out_specs=pl.BlockSpec((tm, D_MODEL), lambda i: (i, 0)),
        out_shape=jax.ShapeDtypeStruct((T, D_MODEL), F32),
        scratch_shapes=[pltpu.VMEM((2, LOCAL_ROWS, D_MODEL), F32), pltpu.SemaphoreType.DMA((2,))],
        compiler_params=_params(("arbitrary",)),
        name="combine",
    )(cdst, cdst, x2, route, y)


def _layer(x2d, B, S, norm1_g, w_in, w_gla_a2, b_gla_a, gla_out_norm_g, dil_q_norm_g, dil_k_norm_g,
           w_proj_gla, w_proj_attn, w_branch_gate, b_branch_gate, w_out, norm2_g,
           w_router_group, b_router_group, w_router_expert, b_router_expert, w_gate, w_up, w_down):
    T = B * S
    n_gla = N_GLA_TILES * HEAD_W
    w_gla = w_in[:, :n_gla].astype(BF16)
    w_dil = w_in[:, n_gla + GLA_RANK:].astype(BF16)
    w_ga = jnp.pad(w_in[:, n_gla:n_gla + GLA_RANK], ((0, 0), (0, LANES - GLA_RANK))).astype(BF16)
    w_a2 = jnp.pad(w_gla_a2, ((0, LANES - GLA_RANK), (0, 0))).astype(BF16)
    gains = [jnp.ones((HEAD_W,), F32)] * N_GLA_TILES
    for gi in range(DIL_GROUPS):
        gains += [jnp.tile(dil_q_norm_g[gi], DIL_HEADS) * (DIL_DH ** -0.5 * LOG2_E),
                  jnp.tile(dil_k_norm_g[gi], DIL_HEADS), jnp.ones((HEAD_W,), F32)]
    qk_gain = jnp.stack(gains).reshape(N_COL_TILES, 1, HEAD_W)

    gla_in, log_a, qkv0, qkv1, qkv2 = _inproj(
        x2d, norm1_g.reshape(1, -1), w_gla, w_dil, w_ga, w_a2, b_gla_a.reshape(1, -1), qk_gain, B, S)

    tri_incl = jnp.asarray(np.tril(np.ones((GLA_BLOCK, GLA_BLOCK), np.float32)), BF16)
    o_gla = _gla(gla_in, log_a, tri_incl, gla_out_norm_g.reshape(1, -1), B, S)

    outs, stats = [], []
    for gi, qkv in enumerate((qkv0.reshape(B, 1, S, QKV_W), qkv1, qkv2)):
        o, st = _dilated(qkv, _alibi_bias(gi), gi)
        outs.append(o)
        stats.append(st)
    outs[0] = outs[0].reshape(T, HEAD_W)
    stats[0] = stats[0].reshape(T, LANES)

    w_r = jnp.pad(jnp.concatenate([w_router_group, w_router_expert], axis=1),
                  ((0, 0), (0, LANES - N_GROUPS - N_EXPERTS)))
    w_rh = w_r.astype(BF16)
    w_rl = (w_r - w_rh.astype(F32)).astype(BF16)
    b_r = jnp.pad(jnp.concatenate([b_router_group, b_router_expert]), (0, LANES - N_GROUPS - N_EXPERTS))
    tm = min(MERGE_TM, S)
    tri_strict = jnp.asarray(np.tril(np.ones((tm, tm), np.float32), -1), BF16)
    upper = jnp.asarray(np.triu(np.ones((LANES, LANES), np.float32), 1), BF16)
    x2, xloc, route, cnt = _merge(
        x2d, o_gla, outs, stats, norm1_g.reshape(1, -1), w_branch_gate.astype(BF16),
        b_branch_gate.reshape(1, -1), w_proj_gla.astype(BF16), w_proj_attn.astype(BF16), w_out.astype(BF16),
        norm2_g.reshape(1, -1), jnp.concatenate([w_rh, w_rl], axis=1), b_r.reshape(1, -1), tri_strict, upper, S)

    csrc, cdst, tile_e, nact = _chunk_tables(cnt, T // tm)
    y = _experts(tile_e, nact, csrc, xloc, w_gate, w_up, w_down)
    return _combine(cdst, x2, route, y)


def _chunk_tables(cnt, nt):
    i32 = jnp.int32
    experts = jnp.arange(N_EXPERTS, dtype=i32)
    tiles = jnp.arange(nt, dtype=i32)
    c = (cnt.reshape(nt, 8, LANES)[:, 0, :N_EXPERTS].astype(i32) + CHUNK - 1) // CHUNK
    loff = jnp.cumsum(c, axis=1) - c
    ecum = jnp.cumsum(c, axis=0) - c
    tot = jnp.sum(c, axis=0)
    ptot = (tot + TILE_CHUNKS - 1) // TILE_CHUNKS * TILE_CHUNKS
    pend = jnp.cumsum(ptot)
    pstart = pend - ptot

    lc = jnp.arange(LOCAL_CHUNKS, dtype=i32)
    e_of = jnp.sum(((loff + c)[:, None, :] <= lc[None, :, None]).astype(i32), axis=-1)
    pick_e = e_of[:, :, None] == experts[None, None, :]
    glob = jnp.sum(jnp.where(pick_e, (pstart[None, :] + ecum - loff)[:, None, :], 0), axis=-1) + lc[None, :]
    cdst = jnp.where(e_of < N_EXPERTS, glob, 0) * CHUNK

    ntiles = (TOP_K * nt * MERGE_TM // CHUNK + nt * N_EXPERTS + N_EXPERTS * (TILE_CHUNKS - 1)
              + TILE_CHUNKS - 1) // TILE_CHUNKS
    g = jnp.arange(ntiles * TILE_CHUNKS, dtype=i32)
    e_g = jnp.minimum(jnp.sum((pend[None, :] <= g[:, None]).astype(i32), axis=1), N_EXPERTS - 1)
    k = g - jnp.sum(jnp.where(e_g[:, None] == experts[None, :], pstart[None, :], 0), axis=1)
    sel_e = (e_g[:, None, None] == experts[None, None, :])
    t_g = jnp.sum((sel_e & ((ecum + c)[None, :, :] <= k[:, None, None])).astype(i32), axis=(1, 2))
    valid = t_g < nt
    sel = sel_e & (jnp.minimum(t_g, nt - 1)[:, None, None] == tiles[None, :, None])
    local = jnp.sum(jnp.where(sel, (tiles[:, None] * LOCAL_CHUNKS + loff - ecum)[None, :, :], 0), axis=(1, 2)) + k
    zero_chunk = LOCAL_CHUNKS - 1
    csrc = jnp.where(valid, local, zero_chunk) * CHUNK

    tile_e = e_g[::TILE_CHUNKS]
    nact = (pend[-1:] // TILE_CHUNKS).astype(i32)
    return (csrc.reshape(ntiles, 1, TILE_CHUNKS), cdst.reshape(nt, 1, LOCAL_CHUNKS), tile_e, nact)


def kernel(x, norm1_g, w_in, w_gla_a2, b_gla_a, gla_out_norm_g, dil_q_norm_g, dil_k_norm_g, w_proj_gla,
           w_proj_attn, w_branch_gate, b_branch_gate, w_out, norm2_g, w_router_group, b_router_group,
           w_router_expert, b_router_expert, w_gate, w_up, w_down):
    B, S, D = x.shape
    assert D == D_MODEL and S % (DIL_BLOCK * DIL_PATTERNS[-1][1]) == 0
    x2d = x.reshape(B * S, D)
    params = (norm1_g, w_in, w_gla_a2, b_gla_a, gla_out_norm_g, dil_q_norm_g, dil_k_norm_g, w_proj_gla,
              w_proj_attn, w_branch_gate, b_branch_gate, w_out, norm2_g, w_router_group, b_router_group,
              w_router_expert, b_router_expert, w_gate, w_up, w_down)
    for layer in range(norm1_g.shape[0]):
        x2d = _layer(x2d, B, S, *(p[layer] for p in params))
    return x2d.reshape(B, S, D)
```

```python
import functools

import jax
import jax.numpy as jnp
import numpy as np
from jax import lax
from jax.experimental import pallas as pl
from jax.experimental.pallas import tpu as pltpu

F32 = jnp.float32
BF16 = jnp.bfloat16

D_MODEL = 1024
EPS = 1e-6
GLA_HEADS = 4
GLA_DK = 128
GLA_RANK = 16
GLA_TAU = 16.0
LOG2_E = 1.4426950408889634
GLA_SUBCHUNK = 16
GLA_BLOCK = 128
GLA_STEP_BLOCKS = 16
DIL_PATTERNS = ((128, 1), (512, 4), (2048, 16))
DIL_GROUPS = 3
DIL_HEADS = 4
DIL_DH = 128
DIL_BLOCK = 128
ALIBI_SLOPES = tuple(2.0 ** (-8.0 * (i + 1) / (DIL_GROUPS * DIL_HEADS)) for i in range(DIL_GROUPS * DIL_HEADS))
N_GROUPS = 4
EXPERTS_PER_GROUP = 8
N_EXPERTS = N_GROUPS * EXPERTS_PER_GROUP
TOP_K = 2
D_EXPERT = 512

HEAD_W = 512
N_GLA_TILES = 4
N_COL_TILES = N_GLA_TILES + 3 * DIL_GROUPS
QKV_W = 3 * HEAD_W
LANES = 128
NEG = -1e30

INPROJ_TM = 512
MERGE_TM = 512
MERGE_SPLIT = 1
MOE_TM = 512
CHUNK = 8
TILE_CHUNKS = MOE_TM // CHUNK
LOCAL_ROWS = 1280
LOCAL_CHUNKS = LOCAL_ROWS // CHUNK
DIL_STEP_BLOCKS = 16
VMEM_LIMIT = 56 * 1024 * 1024


def _dot(a, b):
    return jnp.dot(a, b, preferred_element_type=F32)


def _dot_nt(a, b):
    return lax.dot_general(a, b, (((1,), (1,)), ((), ())), preferred_element_type=F32)


def _dot_tn(a, b):
    return lax.dot_general(a, b, (((0,), (0,)), ((), ())), preferred_element_type=F32)


def _rms(x, g):
    return x * lax.rsqrt(jnp.mean(x * x, axis=-1, keepdims=True) + EPS) * g


def _params(sem):
    return pltpu.CompilerParams(dimension_semantics=sem, vmem_limit_bytes=VMEM_LIMIT)


def _head(c):
    return slice(c * LANES, (c + 1) * LANES)


PACKED_W = D_MODEL // 2
_HI16 = 0xFFFF0000


def _pack_pairs(x):
    u = lax.bitcast_convert_type(x, jnp.uint32)
    w = x.shape[1] // 2
    return lax.bitcast_convert_type((u[:, :w] & jnp.uint32(_HI16)) | (u[:, w:] >> 16), F32)


def _unpack_pairs(p):
    u = lax.bitcast_convert_type(p, jnp.uint32)
    hi = lax.bitcast_convert_type(u & jnp.uint32(_HI16), F32)
    lo = lax.bitcast_convert_type(u << 16, F32)
    return jnp.concatenate([hi, lo], axis=1).astype(BF16)


def _inproj_kernel(x_ref, g1_ref, wgla_ref, wdil_ref, wga_ref, wa2_ref, ba_ref, qkg_ref,
                   gla_ref, loga_ref, d0_ref, d1_ref, d2_ref, h_ref, ybuf_ref):
    tm = x_ref.shape[0]
    hb = _rms(x_ref[...], g1_ref[...]).astype(BF16)
    h_ref[...] = hb
    ga = _dot(hb, wga_ref[...])
    z = _dot(ga.astype(BF16), wa2_ref[...]) + ba_ref[...]
    log_sig = jnp.minimum(z, 0.0) - jnp.log(1.0 + jnp.exp(-jnp.abs(z)))
    loga_ref[...] = log_sig * (LOG2_E / GLA_TAU)

    dil_refs = (d0_ref, d1_ref, d2_ref)
    n_strided = 0
    for jj in range(N_COL_TILES):
        cols = slice(jj * HEAD_W, (jj + 1) * HEAD_W)
        if jj < N_GLA_TILES:
            w = wgla_ref[:, cols]
        else:
            w = wdil_ref[:, (jj - N_GLA_TILES) * HEAD_W:(jj - N_GLA_TILES + 1) * HEAD_W]
        y = _dot(h_ref[...], w)
        if jj < N_GLA_TILES:
            gla_ref[:, cols] = y.astype(BF16)
            continue
        gi, kind = divmod(jj - N_GLA_TILES, 3)
        d = DIL_PATTERNS[gi][1]
        ocols = [slice(kind * HEAD_W + c * LANES, kind * HEAD_W + (c + 1) * LANES) for c in range(DIL_HEADS)]
        if kind < 2:
            g = qkg_ref[jj]
            slabs = [_rms(y[:, _head(c)], g[:, _head(c)]) for c in range(DIL_HEADS)]
        else:
            slabs = [y[:, _head(c)] for c in range(DIL_HEADS)]
        out = dil_refs[gi]
        if d == 1:
            for c in range(DIL_HEADS):
                out[:, ocols[c]] = slabs[c].astype(BF16)
            continue
        buf = n_strided % ybuf_ref.shape[0]
        n_strided += 1
        for c in range(DIL_HEADS):
            ybuf_ref[buf, c] = slabs[c]
        for r in range(d):
            for c in range(DIL_HEADS):
                out[0, r, :, ocols[c]] = ybuf_ref[buf, c, pl.ds(r, tm // d, stride=d), :].astype(BF16)


def _resident(shape):
    return pl.BlockSpec(shape, lambda i: (0,) * len(shape), pipeline_mode=pl.Buffered(1))


def _inproj(x2d, g1, w_gla, w_dil, w_ga, w_a2, b_a, qk_gain, B, S):
    T = x2d.shape[0]
    tm = min(INPROJ_TM, S)
    tpb = S // tm

    def dil_spec(gi):
        d = DIL_PATTERNS[gi][1]
        if d == 1:
            return pl.BlockSpec((tm, QKV_W), lambda i: (i, 0))
        return pl.BlockSpec((1, d, tm // d, QKV_W), lambda i: (i // tpb, 0, i % tpb, 0))

    def dil_shape(gi):
        d = DIL_PATTERNS[gi][1]
        shape = (T, QKV_W) if d == 1 else (B, d, S // d, QKV_W)
        return jax.ShapeDtypeStruct(shape, BF16)

    return pl.pallas_call(
        _inproj_kernel,
        grid=(T // tm,),
        in_specs=[
            pl.BlockSpec((tm, D_MODEL), lambda i: (i, 0)),
            _resident(g1.shape), _resident(w_gla.shape), _resident(w_dil.shape), _resident(w_ga.shape),
            _resident(w_a2.shape),
            _resident(b_a.shape), _resident(qk_gain.shape),
        ],
        out_specs=[
            pl.BlockSpec((tm, N_GLA_TILES * HEAD_W), lambda i: (i, 0)),
            pl.BlockSpec((tm, HEAD_W), lambda i: (i, 0)),
            dil_spec(0), dil_spec(1), dil_spec(2),
        ],
        out_shape=[
            jax.ShapeDtypeStruct((T, N_GLA_TILES * HEAD_W), BF16),
            jax.ShapeDtypeStruct((T, HEAD_W), F32),
            dil_shape(0), dil_shape(1), dil_shape(2),
        ],
        scratch_shapes=[pltpu.VMEM((tm, D_MODEL), BF16), pltpu.VMEM((2, DIL_HEADS, tm, LANES), F32)],
        compiler_params=_params(("arbitrary",)),
        name="inproj",
    )(x2d, g1, w_gla, w_dil, w_ga, w_a2, b_a, qk_gain)


def _gla_kernel(q_ref, k_ref, v_ref, r_ref, la_ref, tri_ref, gn_ref, o_ref, st_ref):
    n = pl.program_id(1)

    @pl.when(n == 0)
    def _():
        st_ref[...] = jnp.zeros_like(st_ref)

    for blk in range(q_ref.shape[0] // GLA_BLOCK):
        _gla_block(q_ref, k_ref, v_ref, r_ref, la_ref, tri_ref, gn_ref, o_ref, st_ref,
                   slice(blk * GLA_BLOCK, (blk + 1) * GLA_BLOCK))


def _gla_block(q_ref, k_ref, v_ref, r_ref, la_ref, tri_ref, gn_ref, o_ref, st_ref, rows):
    L = GLA_BLOCK
    la = la_ref[rows, :]
    hi = la.astype(BF16)
    r1 = la - hi.astype(F32)
    mid = r1.astype(BF16)
    lo = (r1 - mid.astype(F32)).astype(BF16)
    tri = tri_ref[...]
    b_all = _dot(tri, hi) + _dot(tri, mid) + _dot(tri, lo)

    row = lax.broadcasted_iota(jnp.int32, (L, L), 0)
    col = lax.broadcasted_iota(jnp.int32, (L, L), 1)

    for h in range(GLA_HEADS):
        sl = _head(h)
        b = b_all[:, sl]
        q = q_ref[rows, sl].astype(F32) * (GLA_DK ** -0.5)
        k = k_ref[rows, sl].astype(F32)
        v = v_ref[rows, sl]

        att = jnp.zeros((L, L), F32)
        seg = L
        while seg > GLA_SUBCHUNK:
            half = seg // 2
            beta = jnp.concatenate(
                [jnp.broadcast_to(b[s + half - 1:s + half, :], (seg, GLA_DK)) for s in range(0, L, seg)], axis=0)
            rel = b - beta
            neg = jnp.minimum(rel, 0.0)
            qs = (q * jnp.exp2(neg)).astype(BF16)
            ks = (k * jnp.exp2(neg - rel)).astype(BF16)
            same = (row // seg) == (col // seg)
            mask = same & ((row % seg) >= half) & ((col % seg) < half)
            att = jnp.where(mask, _dot_nt(qs, ks), att)
            seg = half
        beta = jnp.concatenate(
            [jnp.zeros((GLA_SUBCHUNK, GLA_DK), F32)]
            + [jnp.broadcast_to(b[s - 1:s, :], (GLA_SUBCHUNK, GLA_DK)) for s in range(GLA_SUBCHUNK, L, GLA_SUBCHUNK)],
            axis=0)
        qs = (q * jnp.exp2(b - beta)).astype(BF16)
        ks = (k * jnp.exp2(beta - b)).astype(BF16)
        mask = ((row // GLA_SUBCHUNK) == (col // GLA_SUBCHUNK)) & (col <= row)
        att = jnp.where(mask, _dot_nt(qs, ks), att)

        state_t = st_ref[h]
        q0 = (q * jnp.exp2(b)).astype(BF16)
        o = _dot(att.astype(BF16), v) + _dot_nt(q0, state_t.astype(BF16))

        b_last = b[L - 1:L, :]
        k_end = (k * jnp.exp2(b_last - b)).astype(BF16)
        st_ref[h] = state_t * jnp.exp2(b_last) + _dot_tn(v, k_end)

        rr = r_ref[rows, sl].astype(F32)
        o_ref[rows, sl] = (_rms(o, gn_ref[...]) * (rr * jax.nn.sigmoid(rr))).astype(BF16)


def _gla(proj, log_a, tri, gn, B, S):
    L = GLA_STEP_BLOCKS * GLA_BLOCK
    nb = S // L
    return pl.pallas_call(
        _gla_kernel,
        grid=(B, nb),
        in_specs=[
            pl.BlockSpec((L, HEAD_W), lambda b, n: (b * nb + n, 0)),
            pl.BlockSpec((L, HEAD_W), lambda b, n: (b * nb + n, 1)),
            pl.BlockSpec((L, HEAD_W), lambda b, n: (b * nb + n, 2)),
            pl.BlockSpec((L, HEAD_W), lambda b, n: (b * nb + n, 3)),
            pl.BlockSpec((L, HEAD_W), lambda b, n: (b * nb + n, 0)),
            pl.BlockSpec((GLA_BLOCK, GLA_BLOCK), lambda b, n: (0, 0)),
            pl.BlockSpec((1, GLA_DK), lambda b, n: (0, 0)),
        ],
        out_specs=pl.BlockSpec((L, HEAD_W), lambda b, n: (b * nb + n, 0)),
        out_shape=jax.ShapeDtypeStruct((B * S, HEAD_W), BF16),
        scratch_shapes=[pltpu.VMEM((GLA_HEADS, GLA_DK, GLA_DK), F32)],
        compiler_params=_params(("arbitrary", "arbitrary")),
        name="gla",
    )(proj, proj, proj, proj, log_a, tri, gn)


def _dil_kernel(q_ref, kp_ref, kc_ref, vp_ref, vc_ref, bias_ref, o_ref, st_ref, *, nr, nq):
    first = pl.program_id(2) == 0
    Q = DIL_BLOCK
    col = lax.broadcasted_iota(jnp.int32, (Q, 2 * Q), 1)
    lane = lax.broadcasted_iota(jnp.int32, (Q, LANES), 1)
    for res in range(nr):
        for blk in range(nq):
            rows = slice(blk * Q, (blk + 1) * Q)
            prev = slice((blk - 1) * Q, blk * Q)
            stats = jnp.zeros((Q, LANES), F32)
            for h in range(DIL_HEADS):
                sl = _head(h)
                if blk == 0:
                    kprev, vprev = kp_ref[0, res, :, sl], vp_ref[0, res, :, sl]
                else:
                    kprev, vprev = kc_ref[0, res, prev, sl], vc_ref[0, res, prev, sl]
                kk = jnp.concatenate([kprev, kc_ref[0, res, rows, sl]], axis=0)
                vv = jnp.concatenate([vprev, vc_ref[0, res, rows, sl]], axis=0)
                s = _dot_nt(q_ref[0, res, rows, sl], kk) + bias_ref[h]
                if blk == 0:
                    s = jnp.where(jnp.logical_and(first, col < Q), NEG, s)
                m = jnp.max(s, axis=-1, keepdims=True)
                p = jnp.exp2(s - m)
                den = jnp.sum(p, axis=-1, keepdims=True)
                o_ref[0, res, rows, sl] = (_dot(p.astype(BF16), vv) / den).astype(BF16)
                stats = jnp.where((lane // 16) == h, m, stats)
                stats = jnp.where((lane // 16) == DIL_HEADS + h, den, stats)
            st_ref[0, res, rows, :] = stats


def _dilated(qkv, bias, gi):
    B, d, Lsub, _ = qkv.shape
    nq = min(DIL_STEP_BLOCKS, Lsub // DIL_BLOCK)
    nr = min(DIL_STEP_BLOCKS // nq, d)
    rows = nq * DIL_BLOCK

    def cur(c):
        return pl.BlockSpec((1, nr, rows, HEAD_W), lambda b, r, i: (b, r, i, c))

    def prev(c):
        return pl.BlockSpec((1, nr, DIL_BLOCK, HEAD_W), lambda b, r, i: (b, r, jnp.maximum(i * nq - 1, 0), c))

    return pl.pallas_call(
        functools.partial(_dil_kernel, nr=nr, nq=nq),
        grid=(B, d // nr, Lsub // rows),
        in_specs=[cur(0), prev(1), cur(1), prev(2), cur(2),
                  pl.BlockSpec((DIL_HEADS, DIL_BLOCK, 2 * DIL_BLOCK), lambda b, r, i: (0, 0, 0))],
        out_specs=[
            pl.BlockSpec((1, nr, rows, HEAD_W), lambda b, r, i: (b, r, i, 0)),
            pl.BlockSpec((1, nr, rows, LANES), lambda b, r, i: (b, r, i, 0)),
        ],
        out_shape=[
            jax.ShapeDtypeStruct((B, d, Lsub, HEAD_W), BF16),
            jax.ShapeDtypeStruct((B, d, Lsub, LANES), F32),
        ],
        compiler_params=_params(("arbitrary", "arbitrary", "arbitrary")),
        name=f"dilated{gi}",
    )(qkv, qkv, qkv, qkv, qkv, bias)


def _alibi_bias(gi):
    window, d = DIL_PATTERNS[gi]
    Q = DIL_BLOCK
    dist = np.arange(Q)[:, None] + Q - np.arange(2 * Q)[None, :]
    valid = (dist >= 0) & (dist <= window // d)
    slopes = np.asarray(ALIBI_SLOPES, np.float32).reshape(DIL_GROUPS, DIL_HEADS)[gi]
    bias = -slopes[:, None, None] * (d * dist).astype(np.float32) * np.float32(LOG2_E)
    return jnp.asarray(np.where(valid[None], bias, np.float32(NEG)), F32)


def _merge_rows(x_ref, og_ref, o0_ref, s0_ref, g1_ref, wbg_ref, bbg_ref, wpg_ref, wpa_ref, wout_ref,
                g2_ref, wr_ref, br_ref, x2_ref, obuf_ref, sbuf_ref, row0, n):
    rows = slice(row0, row0 + n)
    x = x_ref[rows, :]
    hb = _rms(x, g1_ref[...]).astype(BF16)
    gates = jax.nn.sigmoid(_dot(hb, wbg_ref[...]) + bbg_ref[...])

    stats = (s0_ref[rows, :], sbuf_ref[0, rows, :], sbuf_ref[1, rows, :])
    dens = [pltpu.roll(s, 64, 1) for s in stats]
    m_all = jnp.maximum(jnp.maximum(stats[0], stats[1]), stats[2])
    wts = [d * jnp.exp2(s - m_all) for s, d in zip(stats, dens)]
    inv = 1.0 / (wts[0] + wts[1] + wts[2])
    coef = [w * inv for w in wts]
    heads = []
    for h in range(DIL_HEADS):
        group_out = (o0_ref[rows, _head(h)].astype(F32), obuf_ref[0, h, rows, :], obuf_ref[1, h, rows, :])
        acc = jnp.zeros((n, DIL_DH), F32)
        for g in range(DIL_GROUPS):
            c = jnp.broadcast_to(coef[g][:, 16 * h:16 * h + 1], (n, DIL_DH))
            acc = acc + c * group_out[g]
        heads.append(acc.astype(BF16))
    o_att = jnp.concatenate(heads, axis=1)

    y = gates[:, :D_MODEL] * _dot(og_ref[rows, :], wpg_ref[...]) + gates[:, D_MODEL:] * _dot(o_att, wpa_ref[...])
    x2 = x + _dot(y.astype(BF16), wout_ref[...])
    x2_ref[rows, :] = x2

    h2 = _rms(x2, g2_ref[...])

    h2h = h2.astype(BF16)
    h2l = (h2 - h2h.astype(F32)).astype(BF16)
    wr = wr_ref[...]
    pa = _dot(h2h, wr)
    pb = _dot(h2l, wr)
    logit = pa[:, :LANES] + pa[:, LANES:] + pb[:, :LANES] + pb[:, LANES:] + br_ref[...]

    lane = lax.broadcasted_iota(jnp.int32, (n, LANES), 1).astype(F32)
    big = jnp.float32(1e9)
    gl = jnp.where(lane < N_GROUPS, logit, NEG)
    gmax = jnp.max(gl, axis=-1, keepdims=True)
    gsel = jnp.min(jnp.where(gl == gmax, lane, big), axis=-1, keepdims=True)
    g_p = 1.0 / jnp.sum(jnp.exp(gl - gmax), axis=-1, keepdims=True)
    lo = N_GROUPS + EXPERTS_PER_GROUP * gsel
    el = jnp.where((lane >= lo) & (lane < lo + EXPERTS_PER_GROUP), logit, NEG)
    v1 = jnp.max(el, axis=-1, keepdims=True)
    i1 = jnp.min(jnp.where(el == v1, lane, big), axis=-1, keepdims=True)
    el2 = jnp.where(lane == i1, NEG, el)
    v2 = jnp.max(el2, axis=-1, keepdims=True)
    i2 = jnp.min(jnp.where(el2 == v2, lane, big), axis=-1, keepdims=True)
    ex = jnp.exp(v2 - v1)
    w1 = g_p / (1.0 + ex)
    w2 = g_p * ex / (1.0 + ex)
    return i1 - N_GROUPS, i2 - N_GROUPS, w1, w2, h2h


def _merge_kernel(x_ref, og_ref, o0_ref, o1_ref, o2_ref, s0_ref, s1_ref, s2_ref, g1_ref, wbg_ref, bbg_ref,
                  wpg_ref, wpa_ref, wout_ref, g2_ref, wr_ref, br_ref, tri_ref, upper_ref,
                  x2_ref, xloc_ref, route_ref, cnt_ref, obuf_ref, sbuf_ref):
    tm = x_ref.shape[0]

    for slot, (o_ref, s_ref) in enumerate(((o1_ref, s1_ref), (o2_ref, s2_ref))):
        d = DIL_PATTERNS[slot + 1][1]
        for r in range(d):
            sbuf_ref[slot, pl.ds(r, tm // d, stride=d), :] = s_ref[0, r]
            for c in range(DIL_HEADS):
                obuf_ref[slot, c, pl.ds(r, tm // d, stride=d), :] = o_ref[0, r, :, _head(c)].astype(F32)

    sub = tm // MERGE_SPLIT
    parts = [_merge_rows(x_ref, og_ref, o0_ref, s0_ref, g1_ref, wbg_ref, bbg_ref, wpg_ref, wpa_ref, wout_ref,
                         g2_ref, wr_ref, br_ref, x2_ref, obuf_ref, sbuf_ref, p * sub, sub)
             for p in range(MERGE_SPLIT)]
    e1, e2, w1, w2, h2b = (jnp.concatenate([part[j] for part in parts], axis=0) for j in range(5))

    lane = lax.broadcasted_iota(jnp.int32, (tm, LANES), 1).astype(F32)
    oh1 = lane == e1
    oh2 = lane == e2
    onehot = jnp.where(oh1 | oh2, 1.0, 0.0)
    prefix = _dot(tri_ref[...], onehot.astype(BF16))
    cnt = jnp.sum(onehot, axis=0, keepdims=True)
    cnt_pad = jnp.floor((cnt + (CHUNK - 1)) * (1.0 / CHUNK)) * CHUNK
    seg_off = _dot(jnp.broadcast_to(cnt_pad, (8, LANES)).astype(BF16), upper_ref[...])[0:1, :]
    base = seg_off + prefix
    pos1 = jnp.sum(jnp.where(oh1, base, 0.0), axis=-1, keepdims=True)
    pos2 = jnp.sum(jnp.where(oh2, base, 0.0), axis=-1, keepdims=True)
    cnt_ref[...] = jnp.broadcast_to(cnt, cnt_ref.shape)

    route = jnp.zeros((tm, LANES), F32)
    for idx, val in enumerate((e1, e2, w1, w2, pos1, pos2)):
        route = jnp.where(lane == idx, val, route)
    route_ref[...] = route

    route_t = route.T
    slot = lax.broadcasted_iota(jnp.int32, (LOCAL_ROWS, tm), 0).astype(F32)
    perm = jnp.where((slot == route_t[4:5, :]) | (slot == route_t[5:6, :]), 1.0, 0.0).astype(BF16)
    xloc_ref[...] = _pack_pairs(_dot(perm, h2b))


def _merge(x2d, o_gla, outs, stats, g1, wbg, bbg, wpg, wpa, wout, g2, wr, br, tri, upper, S):
    T = x2d.shape[0]
    tm = min(MERGE_TM, S)
    assert TOP_K * tm + N_EXPERTS * CHUNK <= LOCAL_ROWS
    tpb = S // tm
    tok = lambda w: pl.BlockSpec((tm, w), lambda i: (i, 0))
    full = lambda a: pl.BlockSpec(a.shape, lambda i: (0,) * a.ndim)

    def res(gi, w):
        d = DIL_PATTERNS[gi][1]
        return pl.BlockSpec((1, d, tm // d, w), lambda i: (i // tpb, 0, i % tpb, 0))

    return pl.pallas_call(
        _merge_kernel,
        grid=(T // tm,),
        in_specs=[tok(D_MODEL), tok(HEAD_W), tok(HEAD_W), res(1, HEAD_W), res(2, HEAD_W),
                  tok(LANES), res(1, LANES), res(2, LANES),
                  full(g1), full(wbg), full(bbg), full(wpg), full(wpa), full(wout), full(g2),
                  full(wr), full(br), full(tri), full(upper)],
        out_specs=[tok(D_MODEL), pl.BlockSpec((LOCAL_ROWS, PACKED_W), lambda i: (i, 0)), tok(LANES),
                   pl.BlockSpec((8, LANES), lambda i: (i, 0))],
        out_shape=[
            jax.ShapeDtypeStruct((T, D_MODEL), F32),
            jax.ShapeDtypeStruct((T // tm * LOCAL_ROWS, PACKED_W), F32),
            jax.ShapeDtypeStruct((T, LANES), F32),
            jax.ShapeDtypeStruct((T // tm * 8, LANES), F32),
        ],
        scratch_shapes=[pltpu.VMEM((DIL_GROUPS - 1, DIL_HEADS, tm, LANES), F32),
                        pltpu.VMEM((DIL_GROUPS - 1, tm, LANES), F32)],
        compiler_params=_params(("arbitrary",)),
        name="merge",
    )(x2d, o_gla, *outs, *stats, g1, wbg, bbg, wpg, wpa, wout, g2, wr, br, tri, upper)


class _ChunkGather:
    def __init__(self, src_hbm, buf_ref, sem, nchunks):
        self.src, self.buf, self.sem, self.n = src_hbm, buf_ref, sem, nchunks

    def start(self, offs_ref, s):
        for j in range(self.n):
            row = pl.multiple_of(offs_ref[0, 0, j], CHUNK)
            pltpu.make_async_copy(self.src.at[pl.ds(row, CHUNK), :], self.buf.at[s, pl.ds(j * CHUNK, CHUNK), :],
                                  self.sem.at[s]).start()

    def wait(self, s):
        pltpu.make_async_copy(self.src.at[pl.ds(0, self.n * CHUNK), :], self.buf.at[s], self.sem.at[s]).wait()

    def step(self, offs_cur_ref, offs_next_ref):
        i = pl.program_id(0)
        slot = lax.rem(i, 2)

        @pl.when(i == 0)
        def _():
            self.start(offs_cur_ref, 0)

        self.wait(slot)

        @pl.when(i + 1 < pl.num_programs(0))
        def _():
            self.start(offs_next_ref, 1 - slot)

        return slot


def _expert_kernel(tile_e_ref, nact_ref, src_cur_ref, src_next_ref, xloc_ref, wg_ref, wu_ref, wd_ref,
                   y_ref, xbuf_ref, wgu_ref, wdn_ref, gsem):
    i = pl.program_id(0)
    slot = _ChunkGather(xloc_ref, xbuf_ref, gsem, TILE_CHUNKS).step(src_cur_ref, src_next_ref)
    active = i < nact_ref[0]

    @pl.when(jnp.logical_and(active, jnp.logical_or(i == 0, tile_e_ref[i] != tile_e_ref[jnp.maximum(i - 1, 0)])))
    def _():
        wgu_ref[:, :D_EXPERT] = wg_ref[0].astype(BF16)
        wgu_ref[:, D_EXPERT:] = wu_ref[0].astype(BF16)
        wdn_ref[...] = wd_ref[0].astype(BF16)

    @pl.when(active)
    def _():
        xb = _unpack_pairs(xbuf_ref[slot])
        gu = _dot(xb, wgu_ref[...])
        g, u = gu[:, :D_EXPERT], gu[:, D_EXPERT:]
        a = (g * jax.nn.sigmoid(g) * u).astype(BF16)
        y_ref[...] = _pack_pairs(_dot(a, wdn_ref[...]).astype(BF16).astype(F32))

    @pl.when(jnp.logical_not(active))
    def _():
        y_ref[...] = jnp.zeros_like(y_ref)


def _experts(tile_e, nact, csrc, xloc, wg, wu, wd):
    ntiles = tile_e.shape[0]
    R = MOE_TM

    def wsel(i, te, na):
        return (te[jnp.minimum(i, na[0] - 1)], 0, 0)

    def offs(shift):
        return pl.BlockSpec((1, 1, TILE_CHUNKS), lambda i, te, na: (jnp.minimum(i + shift, ntiles - 1), 0, 0),
                            memory_space=pltpu.SMEM)

    return pl.pallas_call(
        _expert_kernel,
        grid_spec=pltpu.PrefetchScalarGridSpec(
            num_scalar_prefetch=2,
            grid=(ntiles,),
            in_specs=[
                offs(0), offs(1),
                pl.BlockSpec(memory_space=pl.ANY),
                pl.BlockSpec((1, D_MODEL, D_EXPERT), wsel),
                pl.BlockSpec((1, D_MODEL, D_EXPERT), wsel),
                pl.BlockSpec((1, D_EXPERT, D_MODEL), wsel),
            ],
            out_specs=pl.BlockSpec((R, PACKED_W), lambda i, te, na: (i, 0)),
            scratch_shapes=[pltpu.VMEM((2, R, PACKED_W), F32), pltpu.VMEM((D_MODEL, 2 * D_EXPERT), BF16),
                            pltpu.VMEM((D_EXPERT, D_MODEL), BF16), pltpu.SemaphoreType.DMA((2,))],
        ),
        out_shape=jax.ShapeDtypeStruct((ntiles * R, PACKED_W), F32),
        compiler_params=_params(("arbitrary",)),
        name="experts",
    )(tile_e, nact, csrc, csrc, xloc, wg, wu, wd)


def _combine_kernel(dst_cur_ref, dst_next_ref, x2_ref, route_ref, y_ref, out_ref, ybuf_ref, gsem):
    slot = _ChunkGather(y_ref, ybuf_ref, gsem, LOCAL_CHUNKS).step(dst_cur_ref, dst_next_ref)
    tm = x2_ref.shape[0]
    route = route_ref[...]
    w1, w2, pos1, pos2 = (route[:, c:c + 1] for c in range(2, 6))
    yl = _unpack_pairs(ybuf_ref[slot])
    cols = lax.broadcasted_iota(jnp.int32, (tm, LOCAL_ROWS), 1).astype(F32)
    pick1 = jnp.where(cols == pos1, 1.0, 0.0).astype(BF16)
    pick2 = jnp.where(cols == pos2, 1.0, 0.0).astype(BF16)
    out_ref[...] = x2_ref[...] + w1 * _dot(pick1, yl) + w2 * _dot(pick2, yl)


def _combine(cdst, x2, route, y):
    T = x2.shape[0]
    nt = cdst.shape[0]
    tm = T // nt

    def offs(shift):
        return pl.BlockSpec((1, 1, LOCAL_CHUNKS), lambda i: (jnp.minimum(i + shift, nt - 1), 0, 0),
                            memory_space=pltpu.SMEM)

    return pl.pallas_call(
        _combine_kernel,
        grid=(nt,),
        in_specs=[
            offs(0), offs(1),
            pl.BlockSpec((tm, D_MODEL), lambda i: (i, 0)),
            pl.BlockSpec((tm, LANES), lambda i: (i, 0)),
            pl.BlockSpec(memory_space=pl.ANY),
        ],
        out_specs=pl.BlockSpec((tm, D_MODEL), lambda i: (i, 0)),
        out_shape=jax.ShapeDtypeStruct((T, D_MODEL), F32),
        scratch_shapes=[pltpu.VMEM((2, LOCAL_ROWS, PACKED_W), F32), pltpu.SemaphoreType.DMA((2,))],
        compiler_params=_params(("arbitrary",)),
        name="combine",
    )(cdst, cdst, x2, route, y)


def _layer(x2d, B, S, norm1_g, w_in, w_gla_a2, b_gla_a, gla_out_norm_g, dil_q_norm_g, dil_k_norm_g,
           w_proj_gla, w_proj_attn, w_branch_gate, b_branch_gate, w_out, norm2_g,
           w_router_group, b_router_group, w_router_expert, b_router_expert, w_gate, w_up, w_down):
    T = B * S
    n_gla = N_GLA_TILES * HEAD_W
    w_gla = w_in[:, :n_gla].astype(BF16)
    w_dil = w_in[:, n_gla + GLA_RANK:].astype(BF16)
    w_ga = jnp.pad(w_in[:, n_gla:n_gla + GLA_RANK], ((0, 0), (0, LANES - GLA_RANK))).astype(BF16)
    w_a2 = jnp.pad(w_gla_a2, ((0, LANES - GLA_RANK), (0, 0))).astype(BF16)
    gains = [jnp.ones((HEAD_W,), F32)] * N_GLA_TILES
    for gi in range(DIL_GROUPS):
        gains += [jnp.tile(dil_q_norm_g[gi], DIL_HEADS) * (DIL_DH ** -0.5 * LOG2_E),
                  jnp.tile(dil_k_norm_g[gi], DIL_HEADS), jnp.ones((HEAD_W,), F32)]
    qk_gain = jnp.stack(gains).reshape(N_COL_TILES, 1, HEAD_W)

    gla_in, log_a, qkv0, qkv1, qkv2 = _inproj(
        x2d, norm1_g.reshape(1, -1), w_gla, w_dil, w_ga, w_a2, b_gla_a.reshape(1, -1), qk_gain, B, S)

    tri_incl = jnp.asarray(np.tril(np.ones((GLA_BLOCK, GLA_BLOCK), np.float32)), BF16)
    o_gla = _gla(gla_in, log_a, tri_incl, gla_out_norm_g.reshape(1, -1), B, S)

    outs, stats = [], []
    for gi, qkv in enumerate((qkv0.reshape(B, 1, S, QKV_W), qkv1, qkv2)):
        o, st = _dilated(qkv, _alibi_bias(gi), gi)
        outs.append(o)
        stats.append(st)
    outs[0] = outs[0].reshape(T, HEAD_W)
    stats[0] = stats[0].reshape(T, LANES)

    w_r = jnp.pad(jnp.concatenate([w_router_group, w_router_expert], axis=1),
                  ((0, 0), (0, LANES - N_GROUPS - N_EXPERTS)))
    w_rh = w_r.astype(BF16)
    w_rl = (w_r - w_rh.astype(F32)).astype(BF16)
    b_r = jnp.pad(jnp.concatenate([b_router_group, b_router_expert]), (0, LANES - N_GROUPS - N_EXPERTS))
    tm = min(MERGE_TM, S)
    tri_strict = jnp.asarray(np.tril(np.ones((tm, tm), np.float32), -1), BF16)
    upper = jnp.asarray(np.triu(np.ones((LANES, LANES), np.float32), 1), BF16)
    x2, xloc, route, cnt = _merge(
        x2d, o_gla, outs, stats, norm1_g.reshape(1, -1), w_branch_gate.astype(BF16),
        b_branch_gate.reshape(1, -1), w_proj_gla.astype(BF16), w_proj_attn.astype(BF16), w_out.astype(BF16),
        norm2_g.reshape(1, -1), jnp.concatenate([w_rh, w_rl], axis=1), b_r.reshape(1, -1), tri_strict, upper, S)

    csrc, cdst, tile_e, nact = _chunk_tables(cnt, T // tm)
    y = _experts(tile_e, nact, csrc, xloc, w_gate, w_up, w_down)
    return _combine(cdst, x2, route, y)


def _chunk_tables(cnt, nt):
    i32 = jnp.int32
    experts = jnp.arange(N_EXPERTS, dtype=i32)
    tiles = jnp.arange(nt, dtype=i32)
    c = (cnt.reshape(nt, 8, LANES)[:, 0, :N_EXPERTS].astype(i32) + CHUNK - 1) // CHUNK
    loff = jnp.cumsum(c, axis=1) - c
    ecum = jnp.cumsum(c, axis=0) - c
    tot = jnp.sum(c, axis=0)
    ptot = (tot + TILE_CHUNKS - 1) // TILE_CHUNKS * TILE_CHUNKS
    pend = jnp.cumsum(ptot)
    pstart = pend - ptot

    lc = jnp.arange(LOCAL_CHUNKS, dtype=i32)
    e_of = jnp.sum(((loff + c)[:, None, :] <= lc[None, :, None]).astype(i32), axis=-1)
    pick_e = e_of[:, :, None] == experts[None, None, :]
    glob = jnp.sum(jnp.where(pick_e, (pstart[None, :] + ecum - loff)[:, None, :], 0), axis=-1) + lc[None, :]
    cdst = jnp.where(e_of < N_EXPERTS, glob, 0) * CHUNK

    ntiles = (TOP_K * nt * MERGE_TM // CHUNK + nt * N_EXPERTS + N_EXPERTS * (TILE_CHUNKS - 1)
              + TILE_CHUNKS - 1) // TILE_CHUNKS
    g = jnp.arange(ntiles * TILE_CHUNKS, dtype=i32)
    e_g = jnp.minimum(jnp.sum((pend[None, :] <= g[:, None]).astype(i32), axis=1), N_EXPERTS - 1)
    k = g - jnp.sum(jnp.where(e_g[:, None] == experts[None, :], pstart[None, :], 0), axis=1)
    sel_e = (e_g[:, None, None] == experts[None, None, :])
    t_g = jnp.sum((sel_e & ((ecum + c)[None, :, :] <= k[:, None, None])).astype(i32), axis=(1, 2))
    valid = t_g < nt
    sel = sel_e & (jnp.minimum(t_g, nt - 1)[:, None, None] == tiles[None, :, None])
    local = jnp.sum(jnp.where(sel, (tiles[:, None] * LOCAL_CHUNKS + loff - ecum)[None, :, :], 0), axis=(1, 2)) + k
    zero_chunk = LOCAL_CHUNKS - 1
    csrc = jnp.where(valid, local, zero_chunk) * CHUNK

    tile_e = e_g[::TILE_CHUNKS]
    nact = (pend[-1:] // TILE_CHUNKS).astype(i32)
    return (csrc.reshape(ntiles, 1, TILE_CHUNKS), cdst.reshape(nt, 1, LOCAL_CHUNKS), tile_e, nact)


def kernel(x, norm1_g, w_in, w_gla_a2, b_gla_a, gla_out_norm_g, dil_q_norm_g, dil_k_norm_g, w_proj_gla,
           w_proj_attn, w_branch_gate, b_branch_gate, w_out, norm2_g, w_router_group, b_router_group,
           w_router_expert, b_router_expert, w_gate, w_up, w_down):
    B, S, D = x.shape
    assert D == D_MODEL and S % (DIL_BLOCK * DIL_PATTERNS[-1][1]) == 0
    x2d = x.reshape(B * S, D)
    params = (norm1_g, w_in, w_gla_a2, b_gla_a, gla_out_norm_g, dil_q_norm_g, dil_k_norm_g, w_proj_gla,
              w_proj_attn, w_branch_gate, b_branch_gate, w_out, norm2_g, w_router_group, b_router_group,
              w_router_expert, b_router_expert, w_gate, w_up, w_down)
    for layer in range(norm1_g.shape[0]):
        x2d = _layer(x2d, B, S, *(p[layer] for p in params))
    return x2d.reshape(B, S, D)
```

```python
import functools

import jax
import jax.numpy as jnp
import numpy as np
from jax import lax
from jax.experimental import pallas as pl
from jax.experimental.pallas import tpu as pltpu

F32 = jnp.float32
BF16 = jnp.bfloat16

D_MODEL = 1024
EPS = 1e-6
GLA_HEADS = 4
GLA_DK = 128
GLA_RANK = 16
GLA_TAU = 16.0
LOG2_E = 1.4426950408889634
GLA_SUBCHUNK = 16
GLA_BLOCK = 128
GLA_STEP_BLOCKS = 16
DIL_PATTERNS = ((128, 1), (512, 4), (2048, 16))
DIL_GROUPS = 3
DIL_HEADS = 4
DIL_DH = 128
DIL_BLOCK = 128
ALIBI_SLOPES = tuple(2.0 ** (-8.0 * (i + 1) / (DIL_GROUPS * DIL_HEADS)) for i in range(DIL_GROUPS * DIL_HEADS))
N_GROUPS = 4
EXPERTS_PER_GROUP = 8
N_EXPERTS = N_GROUPS * EXPERTS_PER_GROUP
TOP_K = 2
D_EXPERT = 512

HEAD_W = 512
N_GLA_TILES = 4
N_COL_TILES = N_GLA_TILES + 3 * DIL_GROUPS
QKV_W = 3 * HEAD_W
LANES = 128
NEG = -1e30

INPROJ_TM = 512
MERGE_TM = 512
MERGE_SPLIT = 1
MOE_TM = 512
CHUNK = 8
TILE_CHUNKS = MOE_TM // CHUNK
LOCAL_ROWS = 1280
LOCAL_CHUNKS = LOCAL_ROWS // CHUNK
DIL_STEP_BLOCKS = 16
VMEM_LIMIT = 56 * 1024 * 1024


def _dot(a, b):
    return jnp.dot(a, b, preferred_element_type=F32)


def _dot_nt(a, b):
    return lax.dot_general(a, b, (((1,), (1,)), ((), ())), preferred_element_type=F32)


def _dot_tn(a, b):
    return lax.dot_general(a, b, (((0,), (0,)), ((), ())), preferred_element_type=F32)


def _rms(x, g):
    return x * lax.rsqrt(jnp.mean(x * x, axis=-1, keepdims=True) + EPS) * g


def _params(sem):
    return pltpu.CompilerParams(dimension_semantics=sem, vmem_limit_bytes=VMEM_LIMIT)


def _head(c):
    return slice(c * LANES, (c + 1) * LANES)


PACKED_W = D_MODEL // 2
_HI16 = 0xFFFF0000


def _pack_pairs(x):
    u = lax.bitcast_convert_type(x, jnp.uint32)
    w = x.shape[1] // 2
    return lax.bitcast_convert_type((u[:, :w] & jnp.uint32(_HI16)) | (u[:, w:] >> 16), F32)


def _unpack_pairs(p):
    u = lax.bitcast_convert_type(p, jnp.uint32)
    hi = lax.bitcast_convert_type(u & jnp.uint32(_HI16), F32)
    lo = lax.bitcast_convert_type(u << 16, F32)
    return jnp.concatenate([hi, lo], axis=1).astype(BF16)


def _inproj_kernel(x_ref, g1_ref, wgla_ref, wdil_ref, wga_ref, wa2_ref, ba_ref, qkg_ref,
                   gla_ref, loga_ref, d0_ref, d1_ref, d2_ref, h_ref, ybuf_ref):
    tm = x_ref.shape[0]
    hb = _rms(x_ref[...], g1_ref[...]).astype(BF16)
    h_ref[...] = hb
    ga = _dot(hb, wga_ref[...])
    z = _dot(ga.astype(BF16), wa2_ref[...]) + ba_ref[...]
    log_sig = jnp.minimum(z, 0.0) - jnp.log(1.0 + jnp.exp(-jnp.abs(z)))
    loga_ref[...] = log_sig * (LOG2_E / GLA_TAU)

    dil_refs = (d0_ref, d1_ref, d2_ref)
    n_strided = 0
    for jj in range(N_COL_TILES):
        cols = slice(jj * HEAD_W, (jj + 1) * HEAD_W)
        if jj < N_GLA_TILES:
            w = wgla_ref[:, cols]
        else:
            w = wdil_ref[:, (jj - N_GLA_TILES) * HEAD_W:(jj - N_GLA_TILES + 1) * HEAD_W]
        y = _dot(h_ref[...], w)
        if jj < N_GLA_TILES:
            gla_ref[:, cols] = y.astype(BF16)
            continue
        gi, kind = divmod(jj - N_GLA_TILES, 3)
        d = DIL_PATTERNS[gi][1]
        ocols = [slice(kind * HEAD_W + c * LANES, kind * HEAD_W + (c + 1) * LANES) for c in range(DIL_HEADS)]
        if kind < 2:
            g = qkg_ref[jj]
            slabs = [_rms(y[:, _head(c)], g[:, _head(c)]) for c in range(DIL_HEADS)]
        else:
            slabs = [y[:, _head(c)] for c in range(DIL_HEADS)]
        out = dil_refs[gi]
        if d == 1:
            for c in range(DIL_HEADS):
                out[:, ocols[c]] = slabs[c].astype(BF16)
            continue
        buf = n_strided % ybuf_ref.shape[0]
        n_strided += 1
        for c in range(DIL_HEADS):
            ybuf_ref[buf, c] = slabs[c]
        for r in range(d):
            for c in range(DIL_HEADS):
                out[0, r, :, ocols[c]] = ybuf_ref[buf, c, pl.ds(r, tm // d, stride=d), :].astype(BF16)


def _resident(shape):
    return pl.BlockSpec(shape, lambda i: (0,) * len(shape), pipeline_mode=pl.Buffered(1))


def _inproj(x2d, g1, w_gla, w_dil, w_ga, w_a2, b_a, qk_gain, B, S):
    T = x2d.shape[0]
    tm = min(INPROJ_TM, S)
    tpb = S // tm

    def dil_spec(gi):
        d = DIL_PATTERNS[gi][1]
        if d == 1:
            return pl.BlockSpec((tm, QKV_W), lambda i: (i, 0))
        return pl.BlockSpec((1, d, tm // d, QKV_W), lambda i: (i // tpb, 0, i % tpb, 0))

    def dil_shape(gi):
        d = DIL_PATTERNS[gi][1]
        shape = (T, QKV_W) if d == 1 else (B, d, S // d, QKV_W)
        return jax.ShapeDtypeStruct(shape, BF16)

    return pl.pallas_call(
        _inproj_kernel,
        grid=(T // tm,),
        in_specs=[
            pl.BlockSpec((tm, D_MODEL), lambda i: (i, 0)),
            _resident(g1.shape), _resident(w_gla.shape), _resident(w_dil.shape), _resident(w_ga.shape),
            _resident(w_a2.shape),
            _resident(b_a.shape), _resident(qk_gain.shape),
        ],
        out_specs=[
            pl.BlockSpec((tm, N_GLA_TILES * HEAD_W), lambda i: (i, 0)),
            pl.BlockSpec((tm, HEAD_W), lambda i: (i, 0)),
            dil_spec(0), dil_spec(1), dil_spec(2),
        ],
        out_shape=[
            jax.ShapeDtypeStruct((T, N_GLA_TILES * HEAD_W), BF16),
            jax.ShapeDtypeStruct((T, HEAD_W), F32),
            dil_shape(0), dil_shape(1), dil_shape(2),
        ],
        scratch_shapes=[pltpu.VMEM((tm, D_MODEL), BF16), pltpu.VMEM((2, DIL_HEADS, tm, LANES), F32)],
        compiler_params=_params(("arbitrary",)),
        name="inproj",
    )(x2d, g1, w_gla, w_dil, w_ga, w_a2, b_a, qk_gain)


def _gla_kernel(q_ref, k_ref, v_ref, r_ref, la_ref, tri_ref, gn_ref, o_ref, st_ref):
    n = pl.program_id(1)

    @pl.when(n == 0)
    def _():
        st_ref[...] = jnp.zeros_like(st_ref)

    for blk in range(q_ref.shape[0] // GLA_BLOCK):
        _gla_block(q_ref, k_ref, v_ref, r_ref, la_ref, tri_ref, gn_ref, o_ref, st_ref,
                   slice(blk * GLA_BLOCK, (blk + 1) * GLA_BLOCK))


def _gla_block(q_ref, k_ref, v_ref, r_ref, la_ref, tri_ref, gn_ref, o_ref, st_ref, rows):
    L = GLA_BLOCK
    la = la_ref[rows, :]
    hi = la.astype(BF16)
    r1 = la - hi.astype(F32)
    mid = r1.astype(BF16)
    lo = (r1 - mid.astype(F32)).astype(BF16)
    tri = tri_ref[...]
    b_all = _dot(tri, hi) + _dot(tri, mid) + _dot(tri, lo)

    row = lax.broadcasted_iota(jnp.int32, (L, L), 0)
    col = lax.broadcasted_iota(jnp.int32, (L, L), 1)

    for h in range(GLA_HEADS):
        sl = _head(h)
        b = b_all[:, sl]
        q = q_ref[rows, sl].astype(F32) * (GLA_DK ** -0.5)
        k = k_ref[rows, sl].astype(F32)
        v = v_ref[rows, sl]

        att = jnp.zeros((L, L), F32)
        seg = L
        while seg > GLA_SUBCHUNK:
            half = seg // 2
            beta = jnp.concatenate(
                [jnp.broadcast_to(b[s + half - 1:s + half, :], (seg, GLA_DK)) for s in range(0, L, seg)], axis=0)
            rel = b - beta
            neg = jnp.minimum(rel, 0.0)
            qs = (q * jnp.exp2(neg)).astype(BF16)
            ks = (k * jnp.exp2(neg - rel)).astype(BF16)
            same = (row // seg) == (col // seg)
            mask = same & ((row % seg) >= half) & ((col % seg) < half)
            att = jnp.where(mask, _dot_nt(qs, ks), att)
            seg = half
        beta = jnp.concatenate(
            [jnp.zeros((GLA_SUBCHUNK, GLA_DK), F32)]
            + [jnp.broadcast_to(b[s - 1:s, :], (GLA_SUBCHUNK, GLA_DK)) for s in range(GLA_SUBCHUNK, L, GLA_SUBCHUNK)],
            axis=0)
        qs = (q * jnp.exp2(b - beta)).astype(BF16)
        ks = (k * jnp.exp2(beta - b)).astype(BF16)
        mask = ((row // GLA_SUBCHUNK) == (col // GLA_SUBCHUNK)) & (col <= row)
        att = jnp.where(mask, _dot_nt(qs, ks), att)

        state_t = st_ref[h]
        q0 = (q * jnp.exp2(b)).astype(BF16)
        o = _dot(att.astype(BF16), v) + _dot_nt(q0, state_t.astype(BF16))

        b_last = b[L - 1:L, :]
        k_end = (k * jnp.exp2(b_last - b)).astype(BF16)
        st_ref[h] = state_t * jnp.exp2(b_last) + _dot_tn(v, k_end)

        rr = r_ref[rows, sl].astype(F32)
        o_ref[rows, sl] = (_rms(o, gn_ref[...]) * (rr * jax.nn.sigmoid(rr))).astype(BF16)


def _gla(proj, log_a, tri, gn, B, S):
    L = GLA_STEP_BLOCKS * GLA_BLOCK
    nb = S // L
    return pl.pallas_call(
        _gla_kernel,
        grid=(B, nb),
        in_specs=[
            pl.BlockSpec((L, HEAD_W), lambda b, n: (b * nb + n, 0)),
            pl.BlockSpec((L, HEAD_W), lambda b, n: (b * nb + n, 1)),
            pl.BlockSpec((L, HEAD_W), lambda b, n: (b * nb + n, 2)),
            pl.BlockSpec((L, HEAD_W), lambda b, n: (b * nb + n, 3)),
            pl.BlockSpec((L, HEAD_W), lambda b, n: (b * nb + n, 0)),
            pl.BlockSpec((GLA_BLOCK, GLA_BLOCK), lambda b, n: (0, 0)),
            pl.BlockSpec((1, GLA_DK), lambda b, n: (0, 0)),
        ],
        out_specs=pl.BlockSpec((L, HEAD_W), lambda b, n: (b * nb + n, 0)),
        out_shape=jax.ShapeDtypeStruct((B * S, HEAD_W), BF16),
        scratch_shapes=[pltpu.VMEM((GLA_HEADS, GLA_DK, GLA_DK), F32)],
        compiler_params=_params(("arbitrary", "arbitrary")),
        name="gla",
    )(proj, proj, proj, proj, log_a, tri, gn)


def _dil_kernel(q_ref, kp_ref, kc_ref, vp_ref, vc_ref, bias_ref, o_ref, st_ref, *, nr, nq):
    first = pl.program_id(2) == 0
    Q = DIL_BLOCK
    col = lax.broadcasted_iota(jnp.int32, (Q, 2 * Q), 1)
    lane = lax.broadcasted_iota(jnp.int32, (Q, LANES), 1)
    for res in range(nr):
        for blk in range(nq):
            rows = slice(blk * Q, (blk + 1) * Q)
            prev = slice((blk - 1) * Q, blk * Q)
            stats = jnp.zeros((Q, LANES), F32)
            for h in range(DIL_HEADS):
                sl = _head(h)
                if blk == 0:
                    kprev, vprev = kp_ref[0, res, :, sl], vp_ref[0, res, :, sl]
                else:
                    kprev, vprev = kc_ref[0, res, prev, sl], vc_ref[0, res, prev, sl]
                kk = jnp.concatenate([kprev, kc_ref[0, res, rows, sl]], axis=0)
                vv = jnp.concatenate([vprev, vc_ref[0, res, rows, sl]], axis=0)
                s = _dot_nt(q_ref[0, res, rows, sl], kk) + bias_ref[h]
                if blk == 0:
                    s = jnp.where(jnp.logical_and(first, col < Q), NEG, s)
                m = jnp.max(s, axis=-1, keepdims=True)
                p = jnp.exp2(s - m)
                den = jnp.sum(p, axis=-1, keepdims=True)
                o_ref[0, res, rows, sl] = (_dot(p.astype(BF16), vv) / den).astype(BF16)
                stats = jnp.where((lane // 16) == h, m, stats)
                stats = jnp.where((lane // 16) == DIL_HEADS + h, den, stats)
            st_ref[0, res, rows, :] = stats


def _dilated(qkv, bias, gi):
    B, d, Lsub, _ = qkv.shape
    nq = min(DIL_STEP_BLOCKS, Lsub // DIL_BLOCK)
    nr = min(DIL_STEP_BLOCKS // nq, d)
    rows = nq * DIL_BLOCK

    def cur(c):
        return pl.BlockSpec((1, nr, rows, HEAD_W), lambda b, r, i: (b, r, i, c))

    def prev(c):
        return pl.BlockSpec((1, nr, DIL_BLOCK, HEAD_W), lambda b, r, i: (b, r, jnp.maximum(i * nq - 1, 0), c))

    return pl.pallas_call(
        functools.partial(_dil_kernel, nr=nr, nq=nq),
        grid=(B, d // nr, Lsub // rows),
        in_specs=[cur(0), prev(1), cur(1), prev(2), cur(2),
                  pl.BlockSpec((DIL_HEADS, DIL_BLOCK, 2 * DIL_BLOCK), lambda b, r, i: (0, 0, 0))],
        out_specs=[
            pl.BlockSpec((1, nr, rows, HEAD_W), lambda b, r, i: (b, r, i, 0)),
            pl.BlockSpec((1, nr, rows, LANES), lambda b, r, i: (b, r, i, 0)),
        ],
        out_shape=[
            jax.ShapeDtypeStruct((B, d, Lsub, HEAD_W), BF16),
            jax.ShapeDtypeStruct((B, d, Lsub, LANES), F32),
        ],
        compiler_params=_params(("arbitrary", "arbitrary", "arbitrary")),
        name=f"dilated{gi}",
    )(qkv, qkv, qkv, qkv, qkv, bias)


def _alibi_bias(gi):
    window, d = DIL_PATTERNS[gi]
    Q = DIL_BLOCK
    dist = np.arange(Q)[:, None] + Q - np.arange(2 * Q)[None, :]
    valid = (dist >= 0) & (dist <= window // d)
    slopes = np.asarray(ALIBI_SLOPES, np.float32).reshape(DIL_GROUPS, DIL_HEADS)[gi]
    bias = -slopes[:, None, None] * (d * dist).astype(np.float32) * np.float32(LOG2_E)
    return jnp.asarray(np.where(valid[None], bias, np.float32(NEG)), F32)


def _merge_rows(x_ref, og_ref, o0_ref, s0_ref, g1_ref, wbg_ref, bbg_ref, wpg_ref, wpa_ref, wout_ref,
                g2_ref, wr_ref, br_ref, x2_ref, obuf_ref, sbuf_ref, row0, n):
    rows = slice(row0, row0 + n)
    x = x_ref[rows, :]
    hb = _rms(x, g1_ref[...]).astype(BF16)
    gates = jax.nn.sigmoid(_dot(hb, wbg_ref[...]) + bbg_ref[...])

    stats = (s0_ref[rows, :], sbuf_ref[0, rows, :], sbuf_ref[1, rows, :])
    dens = [pltpu.roll(s, 64, 1) for s in stats]
    m_all = jnp.maximum(jnp.maximum(stats[0], stats[1]), stats[2])
    wts = [d * jnp.exp2(s - m_all) for s, d in zip(stats, dens)]
    inv = 1.0 / (wts[0] + wts[1] + wts[2])
    coef = [w * inv for w in wts]
    heads = []
    for h in range(DIL_HEADS):
        group_out = (o0_ref[rows, _head(h)].astype(F32), obuf_ref[0, h, rows, :], obuf_ref[1, h, rows, :])
        acc = jnp.zeros((n, DIL_DH), F32)
        for g in range(DIL_GROUPS):
            c = jnp.broadcast_to(coef[g][:, 16 * h:16 * h + 1], (n, DIL_DH))
            acc = acc + c * group_out[g]
        heads.append(acc.astype(BF16))
    o_att = jnp.concatenate(heads, axis=1)

    y = gates[:, :D_MODEL] * _dot(og_ref[rows, :], wpg_ref[...]) + gates[:, D_MODEL:] * _dot(o_att, wpa_ref[...])
    x2 = x + _dot(y.astype(BF16), wout_ref[...])
    x2_ref[rows, :] = x2

    h2 = _rms(x2, g2_ref[...])

    h2h = h2.astype(BF16)
    h2l = (h2 - h2h.astype(F32)).astype(BF16)
    wr = wr_ref[...]
    pa = _dot(h2h, wr)
    pb = _dot(h2l, wr)
    logit = pa[:, :LANES] + pa[:, LANES:] + pb[:, :LANES] + pb[:, LANES:] + br_ref[...]

    lane = lax.broadcasted_iota(jnp.int32, (n, LANES), 1).astype(F32)
    big = jnp.float32(1e9)
    gl = jnp.where(lane < N_GROUPS, logit, NEG)
    gmax = jnp.max(gl, axis=-1, keepdims=True)
    gsel = jnp.min(jnp.where(gl == gmax, lane, big), axis=-1, keepdims=True)
    g_p = 1.0 / jnp.sum(jnp.exp(gl - gmax), axis=-1, keepdims=True)
    lo = N_GROUPS + EXPERTS_PER_GROUP * gsel
    el = jnp.where((lane >= lo) & (lane < lo + EXPERTS_PER_GROUP), logit, NEG)
    v1 = jnp.max(el, axis=-1, keepdims=True)
    i1 = jnp.min(jnp.where(el == v1, lane, big), axis=-1, keepdims=True)
    el2 = jnp.where(lane == i1, NEG, el)
    v2 = jnp.max(el2, axis=-1, keepdims=True)
    i2 = jnp.min(jnp.where(el2 == v2, lane, big), axis=-1, keepdims=True)
    ex = jnp.exp(v2 - v1)
    w1 = g_p / (1.0 + ex)
    w2 = g_p * ex / (1.0 + ex)
    return i1 - N_GROUPS, i2 - N_GROUPS, w1, w2, h2h


def _merge_kernel(x_ref, og_ref, o0_ref, o1_ref, o2_ref, s0_ref, s1_ref, s2_ref, g1_ref, wbg_ref, bbg_ref,
                  wpg_ref, wpa_ref, wout_ref, g2_ref, wr_ref, br_ref, tri_ref, upper_ref,
                  x2_ref, xloc_ref, route_ref, cnt_ref, obuf_ref, sbuf_ref):
    tm = x_ref.shape[0]

    for slot, (o_ref, s_ref) in enumerate(((o1_ref, s1_ref), (o2_ref, s2_ref))):
        d = DIL_PATTERNS[slot + 1][1]
        for r in range(d):
            sbuf_ref[slot, pl.ds(r, tm // d, stride=d), :] = s_ref[0, r]
            for c in range(DIL_HEADS):
                obuf_ref[slot, c, pl.ds(r, tm // d, stride=d), :] = o_ref[0, r, :, _head(c)].astype(F32)

    sub = tm // MERGE_SPLIT
    parts = [_merge_rows(x_ref, og_ref, o0_ref, s0_ref, g1_ref, wbg_ref, bbg_ref, wpg_ref, wpa_ref, wout_ref,
                         g2_ref, wr_ref, br_ref, x2_ref, obuf_ref, sbuf_ref, p * sub, sub)
             for p in range(MERGE_SPLIT)]
    e1, e2, w1, w2, h2b = (jnp.concatenate([part[j] for part in parts], axis=0) for j in range(5))

    lane = lax.broadcasted_iota(jnp.int32, (tm, LANES), 1).astype(F32)
    oh1 = lane == e1
    oh2 = lane == e2
    onehot = jnp.where(oh1 | oh2, 1.0, 0.0)
    prefix = _dot(tri_ref[...], onehot.astype(BF16))
    cnt = jnp.sum(onehot, axis=0, keepdims=True)
    cnt_pad = jnp.floor((cnt + (CHUNK - 1)) * (1.0 / CHUNK)) * CHUNK
    seg_off = _dot(jnp.broadcast_to(cnt_pad, (8, LANES)).astype(BF16), upper_ref[...])[0:1, :]
    base = seg_off + prefix
    pos1 = jnp.sum(jnp.where(oh1, base, 0.0), axis=-1, keepdims=True)
    pos2 = jnp.sum(jnp.where(oh2, base, 0.0), axis=-1, keepdims=True)
    cnt_ref[...] = jnp.broadcast_to(cnt, cnt_ref.shape)

    route = jnp.zeros((tm, LANES), F32)
    for idx, val in enumerate((e1, e2, w1, w2, pos1, pos2)):
        route = jnp.where(lane == idx, val, route)
    route_ref[...] = route

    route_t = route.T
    slot = lax.broadcasted_iota(jnp.int32, (LOCAL_ROWS, tm), 0).astype(F32)
    perm = jnp.where((slot == route_t[4:5, :]) | (slot == route_t[5:6, :]), 1.0, 0.0).astype(BF16)
    xloc_ref[...] = _pack_pairs(_dot(perm, h2b))


def _merge(x2d, o_gla, outs, stats, g1, wbg, bbg, wpg, wpa, wout, g2, wr, br, tri, upper, S):
    T = x2d.shape[0]
    tm = min(MERGE_TM, S)
    assert TOP_K * tm + N_EXPERTS * CHUNK <= LOCAL_ROWS
    tpb = S // tm
    tok = lambda w: pl.BlockSpec((tm, w), lambda i: (i, 0))
    full = lambda a: pl.BlockSpec(a.shape, lambda i: (0,) * a.ndim)

    def res(gi, w):
        d = DIL_PATTERNS[gi][1]
        return pl.BlockSpec((1, d, tm // d, w), lambda i: (i // tpb, 0, i % tpb, 0))

    return pl.pallas_call(
        _merge_kernel,
        grid=(T // tm,),
        in_specs=[tok(D_MODEL), tok(HEAD_W), tok(HEAD_W), res(1, HEAD_W), res(2, HEAD_W),
                  tok(LANES), res(1, LANES), res(2, LANES),
                  full(g1), full(wbg), full(bbg), full(wpg), full(wpa), full(wout), full(g2),
                  full(wr), full(br), full(tri), full(upper)],
        out_specs=[tok(D_MODEL), pl.BlockSpec((LOCAL_ROWS, PACKED_W), lambda i: (i, 0)), tok(LANES),
                   pl.BlockSpec((8, LANES), lambda i: (i, 0))],
        out_shape=[
            jax.ShapeDtypeStruct((T, D_MODEL), F32),
            jax.ShapeDtypeStruct((T // tm * LOCAL_ROWS, PACKED_W), F32),
            jax.ShapeDtypeStruct((T, LANES), F32),
            jax.ShapeDtypeStruct((T // tm * 8, LANES), F32),
        ],
        scratch_shapes=[pltpu.VMEM((DIL_GROUPS - 1, DIL_HEADS, tm, LANES), F32),
                        pltpu.VMEM((DIL_GROUPS - 1, tm, LANES), F32)],
        compiler_params=_params(("arbitrary",)),
        name="merge",
    )(x2d, o_gla, *outs, *stats, g1, wbg, bbg, wpg, wpa, wout, g2, wr, br, tri, upper)


class _ChunkGather:
    def __init__(self, src_hbm, buf_ref, sem, nchunks):
        self.src, self.buf, self.sem, self.n = src_hbm, buf_ref, sem, nchunks

    def start(self, offs_ref, s):
        for j in range(self.n):
            row = pl.multiple_of(offs_ref[0, 0, j], CHUNK)
            pltpu.make_async_copy(self.src.at[pl.ds(row, CHUNK), :], self.buf.at[s, pl.ds(j * CHUNK, CHUNK), :],
                                  self.sem.at[s]).start()

    def wait(self, s):
        pltpu.make_async_copy(self.src.at[pl.ds(0, self.n * CHUNK), :], self.buf.at[s], self.sem.at[s]).wait()

    def step(self, offs_cur_ref, offs_next_ref):
        i = pl.program_id(0)
        slot = lax.rem(i, 2)

        @pl.when(i == 0)
        def _():
            self.start(offs_cur_ref, 0)

        self.wait(slot)

        @pl.when(i + 1 < pl.num_programs(0))
        def _():
            self.start(offs_next_ref, 1 - slot)

        return slot


def _expert_kernel(tile_e_ref, nact_ref, src_cur_ref, src_next_ref, xloc_ref, wg_ref, wu_ref, wd_ref,
                   y_ref, xbuf_ref, wgu_ref, wdn_ref, gsem):
    i = pl.program_id(0)
    slot = _ChunkGather(xloc_ref, xbuf_ref, gsem, TILE_CHUNKS).step(src_cur_ref, src_next_ref)
    active = i < nact_ref[0]

    @pl.when(jnp.logical_and(active, jnp.logical_or(i == 0, tile_e_ref[i] != tile_e_ref[jnp.maximum(i - 1, 0)])))
    def _():
        wgu_ref[:, :D_EXPERT] = wg_ref[0].astype(BF16)
        wgu_ref[:, D_EXPERT:] = wu_ref[0].astype(BF16)
        wdn_ref[...] = wd_ref[0].astype(BF16)

    @pl.when(active)
    def _():
        xb = _unpack_pairs(xbuf_ref[slot])
        gu = _dot(xb, wgu_ref[...])
        g, u = gu[:, :D_EXPERT], gu[:, D_EXPERT:]
        a = (g * jax.nn.sigmoid(g) * u).astype(BF16)
        y_ref[...] = _pack_pairs(_dot(a, wdn_ref[...]).astype(BF16).astype(F32))

    @pl.when(jnp.logical_not(active))
    def _():
        y_ref[...] = jnp.zeros_like(y_ref)


def _experts(tile_e, nact, csrc, xloc, wg, wu, wd):
    ntiles = tile_e.shape[0]
    R = MOE_TM

    def wsel(i, te, na):
        return (te[jnp.minimum(i, na[0] - 1)], 0, 0)

    def offs(shift):
        return pl.BlockSpec((1, 1, TILE_CHUNKS), lambda i, te, na: (jnp.minimum(i + shift, ntiles - 1), 0, 0),
                            memory_space=pltpu.SMEM)

    return pl.pallas_call(
        _expert_kernel,
        grid_spec=pltpu.PrefetchScalarGridSpec(
            num_scalar_prefetch=2,
            grid=(ntiles,),
            in_specs=[
                offs(0), offs(1),
                pl.BlockSpec(memory_space=pl.ANY),
                pl.BlockSpec((1, D_MODEL, D_EXPERT), wsel),
                pl.BlockSpec((1, D_MODEL, D_EXPERT), wsel),
                pl.BlockSpec((1, D_EXPERT, D_MODEL), wsel),
            ],
            out_specs=pl.BlockSpec((R, PACKED_W), lambda i, te, na: (i, 0)),
            scratch_shapes=[pltpu.VMEM((2, R, PACKED_W), F32), pltpu.VMEM((D_MODEL, 2 * D_EXPERT), BF16),
                            pltpu.VMEM((D_EXPERT, D_MODEL), BF16), pltpu.SemaphoreType.DMA((2,))],
        ),
        out_shape=jax.ShapeDtypeStruct((ntiles * R, PACKED_W), F32),
        compiler_params=_params(("arbitrary",)),
        name="experts",
    )(tile_e, nact, csrc, csrc, xloc, wg, wu, wd)


def _combine_kernel(dst_cur_ref, dst_next_ref, x2_ref, route_ref, y_ref, out_ref, ybuf_ref, gsem):
    slot = _ChunkGather(y_ref, ybuf_ref, gsem, LOCAL_CHUNKS).step(dst_cur_ref, dst_next_ref)
    tm = x2_ref.shape[0]
    route = route_ref[...]
    w1, w2, pos1, pos2 = (route[:, c:c + 1] for c in range(2, 6))
    yl = _unpack_pairs(ybuf_ref[slot])
    cols = lax.broadcasted_iota(jnp.int32, (tm, LOCAL_ROWS), 1).astype(F32)
    pick1 = jnp.where(cols == pos1, 1.0, 0.0).astype(BF16)
    pick2 = jnp.where(cols == pos2, 1.0, 0.0).astype(BF16)
    out_ref[...] = x2_ref[...] + w1 * _dot(pick1, yl) + w2 * _dot(pick2, yl)


def _combine(cdst, x2, route, y):
    T = x2.shape[0]
    nt = cdst.shape[0]
    tm = T // nt

    def offs(shift):
        return pl.BlockSpec((1, 1, LOCAL_CHUNKS), lambda i: (jnp.minimum(i + shift, nt - 1), 0, 0),
                            memory_space=pltpu.SMEM)

    return pl.pallas_call(
        _combine_kernel,
        grid=(nt,),
        in_specs=[
            offs(0), offs(1),
            pl.BlockSpec((tm, D_MODEL), lambda i: (i, 0)),
            pl.BlockSpec((tm, LANES), lambda i: (i, 0)),
            pl.BlockSpec(memory_space=pl.ANY),
        ],
        out_specs=pl.BlockSpec((tm, D_MODEL), lambda i: (i, 0)),
        out_shape=jax.ShapeDtypeStruct((T, D_MODEL), F32),
        scratch_shapes=[pltpu.VMEM((2, LOCAL_ROWS, PACKED_W), F32), pltpu.SemaphoreType.DMA((2,))],
        compiler_params=_params(("arbitrary",)),
        name="combine",
    )(cdst, cdst, x2, route, y)


def _layer(x2d, B, S, norm1_g, w_in, w_gla_a2, b_gla_a, gla_out_norm_g, dil_q_norm_g, dil_k_norm_g,
           w_proj_gla, w_proj_attn, w_branch_gate, b_branch_gate, w_out, norm2_g,
           w_router_group, b_router_group, w_router_expert, b_router_expert, w_gate, w_up, w_down):
    T = B * S
    n_gla = N_GLA_TILES * HEAD_W
    w_gla = w_in[:, :n_gla].astype(BF16)
    w_dil = w_in[:, n_gla + GLA_RANK:].astype(BF16)
    w_ga = jnp.pad(w_in[:, n_gla:n_gla + GLA_RANK], ((0, 0), (0, LANES - GLA_RANK))).astype(BF16)
    w_a2 = jnp.pad(w_gla_a2, ((0, LANES - GLA_RANK), (0, 0))).astype(BF16)
    gains = [jnp.ones((HEAD_W,), F32)] * N_GLA_TILES
    for gi in range(DIL_GROUPS):
        gains += [jnp.tile(dil_q_norm_g[gi], DIL_HEADS) * (DIL_DH ** -0.5 * LOG2_E),
                  jnp.tile(dil_k_norm_g[gi], DIL_HEADS), jnp.ones((HEAD_W,), F32)]
    qk_gain = jnp.stack(gains).reshape(N_COL_TILES, 1, HEAD_W)

    gla_in, log_a, qkv0, qkv1, qkv2 = _inproj(
        x2d, norm1_g.reshape(1, -1), w_gla, w_dil, w_ga, w_a2, b_gla_a.reshape(1, -1), qk_gain, B, S)

    tri_incl = jnp.asarray(np.tril(np.ones((GLA_BLOCK, GLA_BLOCK), np.float32)), BF16)
    o_gla = _gla(gla_in, log_a, tri_incl, gla_out_norm_g.reshape(1, -1), B, S)

    outs, stats = [], []
    for gi, qkv in enumerate((qkv0.reshape(B, 1, S, QKV_W), qkv1, qkv2)):
        o, st = _dilated(qkv, _alibi_bias(gi), gi)
        outs.append(o)
        stats.append(st)
    outs[0] = outs[0].reshape(T, HEAD_W)
    stats[0] = stats[0].reshape(T, LANES)

    w_r = jnp.pad(jnp.concatenate([w_router_group, w_router_expert], axis=1),
                  ((0, 0), (0, LANES - N_GROUPS - N_EXPERTS)))
    w_rh = w_r.astype(BF16)
    w_rl = (w_r - w_rh.astype(F32)).astype(BF16)
    b_r = jnp.pad(jnp.concatenate([b_router_group, b_router_expert]), (0, LANES - N_GROUPS - N_EXPERTS))
    tm = min(MERGE_TM, S)
    tri_strict = jnp.asarray(np.tril(np.ones((tm, tm), np.float32), -1), BF16)
    upper = jnp.asarray(np.triu(np.ones((LANES, LANES), np.float32), 1), BF16)
    x2, xloc, route, cnt = _merge(
        x2d, o_gla, outs, stats, norm1_g.reshape(1, -1), w_branch_gate.astype(BF16),
        b_branch_gate.reshape(1, -1), w_proj_gla.astype(BF16), w_proj_attn.astype(BF16), w_out.astype(BF16),
        norm2_g.reshape(1, -1), jnp.concatenate([w_rh, w_rl], axis=1), b_r.reshape(1, -1), tri_strict, upper, S)

    csrc, cdst, tile_e, nact = _chunk_tables(cnt, T // tm)
    y = _experts(tile_e, nact, csrc, xloc, w_gate, w_up, w_down)
    return _combine(cdst, x2, route, y)


def _chunk_tables(cnt, nt):
    i32 = jnp.int32
    experts = jnp.arange(N_EXPERTS, dtype=i32)
    tiles = jnp.arange(nt, dtype=i32)
    c = (cnt.reshape(nt, 8, LANES)[:, 0, :N_EXPERTS].astype(i32) + CHUNK - 1) // CHUNK
    loff = jnp.cumsum(c, axis=1) - c
    ecum = jnp.cumsum(c, axis=0) - c
    tot = jnp.sum(c, axis=0)
    ptot = (tot + TILE_CHUNKS - 1) // TILE_CHUNKS * TILE_CHUNKS
    pend = jnp.cumsum(ptot)
    pstart = pend - ptot

    lc = jnp.arange(LOCAL_CHUNKS, dtype=i32)
    e_of = jnp.sum(((loff + c)[:, None, :] <= lc[None, :, None]).astype(i32), axis=-1)
    pick_e = e_of[:, :, None] == experts[None, None, :]
    glob = jnp.sum(jnp.where(pick_e, (pstart[None, :] + ecum - loff)[:, None, :], 0), axis=-1) + lc[None, :]
    cdst = jnp.where(e_of < N_EXPERTS, glob, 0) * CHUNK

    ntiles = (TOP_K * nt * MERGE_TM // CHUNK + nt * N_EXPERTS + N_EXPERTS * (TILE_CHUNKS - 1)
              + TILE_CHUNKS - 1) // TILE_CHUNKS
    g = jnp.arange(ntiles * TILE_CHUNKS, dtype=i32)
    e_g = jnp.minimum(jnp.sum((pend[None, :] <= g[:, None]).astype(i32), axis=1), N_EXPERTS - 1)
    k = g - jnp.sum(jnp.where(e_g[:, None] == experts[None, :], pstart[None, :], 0), axis=1)
    pick = (e_g[:, None] == experts[None, :]).astype(F32)
    ends = jnp.dot(pick, (ecum + c).T.astype(F32), precision=lax.Precision.HIGHEST)
    t_g = jnp.sum((ends <= k[:, None].astype(F32)).astype(i32), axis=1)
    valid = t_g < nt
    base = jnp.dot(pick, (tiles[:, None] * LOCAL_CHUNKS + loff - ecum).T.astype(F32),
                   precision=lax.Precision.HIGHEST)
    in_tile = tiles[None, :] == jnp.minimum(t_g, nt - 1)[:, None]
    local = jnp.sum(jnp.where(in_tile, base, 0.0), axis=1).astype(i32) + k
    zero_chunk = LOCAL_CHUNKS - 1
    csrc = jnp.where(valid, local, zero_chunk) * CHUNK

    tile_e = e_g[::TILE_CHUNKS]
    nact = (pend[-1:] // TILE_CHUNKS).astype(i32)
    return (csrc.reshape(ntiles, 1, TILE_CHUNKS), cdst.reshape(nt, 1, LOCAL_CHUNKS), tile_e, nact)


def kernel(x, norm1_g, w_in, w_gla_a2, b_gla_a, gla_out_norm_g, dil_q_norm_g, dil_k_norm_g, w_proj_gla,
           w_proj_attn, w_branch_gate, b_branch_gate, w_out, norm2_g, w_router_group, b_router_group,
           w_router_expert, b_router_expert, w_gate, w_up, w_down):
    B, S, D = x.shape
    assert D == D_MODEL and S % (DIL_BLOCK * DIL_PATTERNS[-1][1]) == 0
    x2d = x.reshape(B * S, D)
    params = (norm1_g, w_in, w_gla_a2, b_gla_a, gla_out_norm_g, dil_q_norm_g, dil_k_norm_g, w_proj_gla,
              w_proj_attn, w_branch_gate, b_branch_gate, w_out, norm2_g, w_router_group, b_router_group,
              w_router_expert, b_router_expert, w_gate, w_up, w_down)
    for layer in range(norm1_g.shape[0]):
        x2d = _layer(x2d, B, S, *(p[layer] for p in params))
    return x2d.reshape(B, S, D)
```

```python
import functools

import jax
import jax.numpy as jnp
import numpy as np
from jax import lax
from jax.experimental import pallas as pl
from jax.experimental.pallas import tpu as pltpu

F32 = jnp.float32
BF16 = jnp.bfloat16

D_MODEL = 1024
EPS = 1e-6
GLA_HEADS = 4
GLA_DK = 128
GLA_RANK = 16
GLA_TAU = 16.0
LOG2_E = 1.4426950408889634
GLA_SUBCHUNK = 16
GLA_BLOCK = 128
GLA_STEP_BLOCKS = 16
DIL_PATTERNS = ((128, 1), (512, 4), (2048, 16))
DIL_GROUPS = 3
DIL_HEADS = 4
DIL_DH = 128
DIL_BLOCK = 128
ALIBI_SLOPES = tuple(2.0 ** (-8.0 * (i + 1) / (DIL_GROUPS * DIL_HEADS)) for i in range(DIL_GROUPS * DIL_HEADS))
N_GROUPS = 4
EXPERTS_PER_GROUP = 8
N_EXPERTS = N_GROUPS * EXPERTS_PER_GROUP
TOP_K = 2
D_EXPERT = 512

HEAD_W = 512
N_GLA_TILES = 4
N_COL_TILES = N_GLA_TILES + 3 * DIL_GROUPS
QKV_W = 3 * HEAD_W
LANES = 128
NEG = -1e30

INPROJ_TM = 512
MERGE_TM = 512
MERGE_SPLIT = 1
MOE_TM = 512
CHUNK = 8
TILE_CHUNKS = MOE_TM // CHUNK
LOCAL_ROWS = 1280
LOCAL_CHUNKS = LOCAL_ROWS // CHUNK
DIL_STEP_BLOCKS = 16
VMEM_LIMIT = 56 * 1024 * 1024


def _dot(a, b):
    return jnp.dot(a, b, preferred_element_type=F32)


def _dot_nt(a, b):
    return lax.dot_general(a, b, (((1,), (1,)), ((), ())), preferred_element_type=F32)


def _dot_tn(a, b):
    return lax.dot_general(a, b, (((0,), (0,)), ((), ())), preferred_element_type=F32)


def _rms(x, g):
    return x * lax.rsqrt(jnp.mean(x * x, axis=-1, keepdims=True) + EPS) * g


def _params(sem):
    return pltpu.CompilerParams(dimension_semantics=sem, vmem_limit_bytes=VMEM_LIMIT)


def _head(c):
    return slice(c * LANES, (c + 1) * LANES)


PACKED_W = D_MODEL // 2
_HI16 = 0xFFFF0000


def _pack_pairs(x):
    u = lax.bitcast_convert_type(x, jnp.uint32)
    w = x.shape[1] // 2
    return lax.bitcast_convert_type((u[:, :w] & jnp.uint32(_HI16)) | (u[:, w:] >> 16), F32)


def _unpack_pairs(p):
    u = lax.bitcast_convert_type(p, jnp.uint32)
    hi = lax.bitcast_convert_type(u & jnp.uint32(_HI16), F32)
    lo = lax.bitcast_convert_type(u << 16, F32)
    return jnp.concatenate([hi, lo], axis=1).astype(BF16)


def _inproj_kernel(x_ref, g1_ref, wgla_ref, wdil_ref, wga_ref, wa2_ref, ba_ref, qkg_ref,
                   gla_ref, loga_ref, d0_ref, d1_ref, d2_ref, h_ref, ybuf_ref):
    tm = x_ref.shape[0]
    hb = _rms(x_ref[...], g1_ref[...]).astype(BF16)
    h_ref[...] = hb
    ga = _dot(hb, wga_ref[...])
    z = _dot(ga.astype(BF16), wa2_ref[...]) + ba_ref[...]
    log_sig = jnp.minimum(z, 0.0) - jnp.log(1.0 + jnp.exp(-jnp.abs(z)))
    loga_ref[...] = log_sig * (LOG2_E / GLA_TAU)

    dil_refs = (d0_ref, d1_ref, d2_ref)
    n_strided = 0
    for jj in range(N_COL_TILES):
        cols = slice(jj * HEAD_W, (jj + 1) * HEAD_W)
        if jj < N_GLA_TILES:
            w = wgla_ref[:, cols]
        else:
            w = wdil_ref[:, (jj - N_GLA_TILES) * HEAD_W:(jj - N_GLA_TILES + 1) * HEAD_W]
        y = _dot(h_ref[...], w)
        if jj < N_GLA_TILES:
            gla_ref[:, cols] = y.astype(BF16)
            continue
        gi, kind = divmod(jj - N_GLA_TILES, 3)
        d = DIL_PATTERNS[gi][1]
        ocols = [slice(kind * HEAD_W + c * LANES, kind * HEAD_W + (c + 1) * LANES) for c in range(DIL_HEADS)]
        if kind < 2:
            g = qkg_ref[jj]
            slabs = [_rms(y[:, _head(c)], g[:, _head(c)]) for c in range(DIL_HEADS)]
        else:
            slabs = [y[:, _head(c)] for c in range(DIL_HEADS)]
        out = dil_refs[gi]
        if d == 1:
            for c in range(DIL_HEADS):
                out[:, ocols[c]] = slabs[c].astype(BF16)
            continue
        buf = n_strided % ybuf_ref.shape[0]
        n_strided += 1
        for c in range(DIL_HEADS):
            ybuf_ref[buf, c] = slabs[c]
        for r in range(d):
            for c in range(DIL_HEADS):
                out[0, r, :, ocols[c]] = ybuf_ref[buf, c, pl.ds(r, tm // d, stride=d), :].astype(BF16)


def _resident(shape):
    return pl.BlockSpec(shape, lambda i: (0,) * len(shape), pipeline_mode=pl.Buffered(1))


def _inproj(x2d, g1, w_gla, w_dil, w_ga, w_a2, b_a, qk_gain, B, S):
    T = x2d.shape[0]
    tm = min(INPROJ_TM, S)
    tpb = S // tm

    def dil_spec(gi):
        d = DIL_PATTERNS[gi][1]
        if d == 1:
            return pl.BlockSpec((tm, QKV_W), lambda i: (i, 0))
        return pl.BlockSpec((1, d, tm // d, QKV_W), lambda i: (i // tpb, 0, i % tpb, 0))

    def dil_shape(gi):
        d = DIL_PATTERNS[gi][1]
        shape = (T, QKV_W) if d == 1 else (B, d, S // d, QKV_W)
        return jax.ShapeDtypeStruct(shape, BF16)

    return pl.pallas_call(
        _inproj_kernel,
        grid=(T // tm,),
        in_specs=[
            pl.BlockSpec((tm, D_MODEL), lambda i: (i, 0)),
            _resident(g1.shape), _resident(w_gla.shape), _resident(w_dil.shape), _resident(w_ga.shape),
            _resident(w_a2.shape),
            _resident(b_a.shape), _resident(qk_gain.shape),
        ],
        out_specs=[
            pl.BlockSpec((tm, N_GLA_TILES * HEAD_W), lambda i: (i, 0)),
            pl.BlockSpec((tm, HEAD_W), lambda i: (i, 0)),
            dil_spec(0), dil_spec(1), dil_spec(2),
        ],
        out_shape=[
            jax.ShapeDtypeStruct((T, N_GLA_TILES * HEAD_W), BF16),
            jax.ShapeDtypeStruct((T, HEAD_W), F32),
            dil_shape(0), dil_shape(1), dil_shape(2),
        ],
        scratch_shapes=[pltpu.VMEM((tm, D_MODEL), BF16), pltpu.VMEM((2, DIL_HEADS, tm, LANES), F32)],
        compiler_params=_params(("arbitrary",)),
        name="inproj",
    )(x2d, g1, w_gla, w_dil, w_ga, w_a2, b_a, qk_gain)


def _gla_kernel(q_ref, k_ref, v_ref, r_ref, la_ref, tri_ref, gn_ref, o_ref, st_ref):
    n = pl.program_id(1)

    @pl.when(n == 0)
    def _():
        st_ref[...] = jnp.zeros_like(st_ref)

    for blk in range(q_ref.shape[0] // GLA_BLOCK):
        _gla_block(q_ref, k_ref, v_ref, r_ref, la_ref, tri_ref, gn_ref, o_ref, st_ref,
                   slice(blk * GLA_BLOCK, (blk + 1) * GLA_BLOCK))


def _gla_block(q_ref, k_ref, v_ref, r_ref, la_ref, tri_ref, gn_ref, o_ref, st_ref, rows):
    L = GLA_BLOCK
    la = la_ref[rows, :]
    hi = la.astype(BF16)
    r1 = la - hi.astype(F32)
    mid = r1.astype(BF16)
    lo = (r1 - mid.astype(F32)).astype(BF16)
    tri = tri_ref[...]
    b_all = _dot(tri, hi) + _dot(tri, mid) + _dot(tri, lo)

    row = lax.broadcasted_iota(jnp.int32, (L, L), 0)
    col = lax.broadcasted_iota(jnp.int32, (L, L), 1)

    for h in range(GLA_HEADS):
        sl = _head(h)
        b = b_all[:, sl]
        q = q_ref[rows, sl].astype(F32) * (GLA_DK ** -0.5)
        k = k_ref[rows, sl].astype(F32)
        v = v_ref[rows, sl]

        att = jnp.zeros((L, L), F32)
        seg = L
        while seg > GLA_SUBCHUNK:
            half = seg // 2
            beta = jnp.concatenate(
                [jnp.broadcast_to(b[s + half - 1:s + half, :], (seg, GLA_DK)) for s in range(0, L, seg)], axis=0)
            rel = b - beta
            neg = jnp.minimum(rel, 0.0)
            qs = (q * jnp.exp2(neg)).astype(BF16)
            ks = (k * jnp.exp2(neg - rel)).astype(BF16)
            same = (row // seg) == (col // seg)
            mask = same & ((row % seg) >= half) & ((col % seg) < half)
            att = jnp.where(mask, _dot_nt(qs, ks), att)
            seg = half
        beta = jnp.concatenate(
            [jnp.zeros((GLA_SUBCHUNK, GLA_DK), F32)]
            + [jnp.broadcast_to(b[s - 1:s, :], (GLA_SUBCHUNK, GLA_DK)) for s in range(GLA_SUBCHUNK, L, GLA_SUBCHUNK)],
            axis=0)
        qs = (q * jnp.exp2(b - beta)).astype(BF16)
        ks = (k * jnp.exp2(beta - b)).astype(BF16)
        mask = ((row // GLA_SUBCHUNK) == (col // GLA_SUBCHUNK)) & (col <= row)
        att = jnp.where(mask, _dot_nt(qs, ks), att)

        state_t = st_ref[h]
        q0 = (q * jnp.exp2(b)).astype(BF16)
        o = _dot(att.astype(BF16), v) + _dot_nt(q0, state_t.astype(BF16))

        b_last = b[L - 1:L, :]
        k_end = (k * jnp.exp2(b_last - b)).astype(BF16)
        st_ref[h] = state_t * jnp.exp2(b_last) + _dot_tn(v, k_end)

        rr = r_ref[rows, sl].astype(F32)
        o_ref[rows, sl] = (_rms(o, gn_ref[...]) * (rr * jax.nn.sigmoid(rr))).astype(BF16)


def _gla(proj, log_a, tri, gn, B, S):
    L = GLA_STEP_BLOCKS * GLA_BLOCK
    nb = S // L
    return pl.pallas_call(
        _gla_kernel,
        grid=(B, nb),
        in_specs=[
            pl.BlockSpec((L, HEAD_W), lambda b, n: (b * nb + n, 0)),
            pl.BlockSpec((L, HEAD_W), lambda b, n: (b * nb + n, 1)),
            pl.BlockSpec((L, HEAD_W), lambda b, n: (b * nb + n, 2)),
            pl.BlockSpec((L, HEAD_W), lambda b, n: (b * nb + n, 3)),
            pl.BlockSpec((L, HEAD_W), lambda b, n: (b * nb + n, 0)),
            pl.BlockSpec((GLA_BLOCK, GLA_BLOCK), lambda b, n: (0, 0)),
            pl.BlockSpec((1, GLA_DK), lambda b, n: (0, 0)),
        ],
        out_specs=pl.BlockSpec((L, HEAD_W), lambda b, n: (b * nb + n, 0)),
        out_shape=jax.ShapeDtypeStruct((B * S, HEAD_W), BF16),
        scratch_shapes=[pltpu.VMEM((GLA_HEADS, GLA_DK, GLA_DK), F32)],
        compiler_params=_params(("arbitrary", "arbitrary")),
        name="gla",
    )(proj, proj, proj, proj, log_a, tri, gn)


def _dil_kernel(q_ref, kp_ref, kc_ref, vp_ref, vc_ref, bias_ref, o_ref, st_ref, *, nr, nq):
    first = pl.program_id(2) == 0
    Q = DIL_BLOCK
    col = lax.broadcasted_iota(jnp.int32, (Q, 2 * Q), 1)
    lane = lax.broadcasted_iota(jnp.int32, (Q, LANES), 1)
    for res in range(nr):
        for blk in range(nq):
            rows = slice(blk * Q, (blk + 1) * Q)
            prev = slice((blk - 1) * Q, blk * Q)
            stats = jnp.zeros((Q, LANES), F32)
            for h in range(DIL_HEADS):
                sl = _head(h)
                if blk == 0:
                    kprev, vprev = kp_ref[0, res, :, sl], vp_ref[0, res, :, sl]
                else:
                    kprev, vprev = kc_ref[0, res, prev, sl], vc_ref[0, res, prev, sl]
                kk = jnp.concatenate([kprev, kc_ref[0, res, rows, sl]], axis=0)
                vv = jnp.concatenate([vprev, vc_ref[0, res, rows, sl]], axis=0)
                s = _dot_nt(q_ref[0, res, rows, sl], kk) + bias_ref[h]
                if blk == 0:
                    s = jnp.where(jnp.logical_and(first, col < Q), NEG, s)
                m = jnp.max(s, axis=-1, keepdims=True)
                p = jnp.exp2(s - m)
                den = jnp.sum(p, axis=-1, keepdims=True)
                o_ref[0, res, rows, sl] = (_dot(p.astype(BF16), vv) / den).astype(BF16)
                stats = jnp.where((lane // 16) == h, m, stats)
                stats = jnp.where((lane // 16) == DIL_HEADS + h, den, stats)
            st_ref[0, res, rows, :] = stats


def _dilated(qkv, bias, gi):
    B, d, Lsub, _ = qkv.shape
    nq = min(DIL_STEP_BLOCKS, Lsub // DIL_BLOCK)
    nr = min(DIL_STEP_BLOCKS // nq, d)
    rows = nq * DIL_BLOCK

    def cur(c):
        return pl.BlockSpec((1, nr, rows, HEAD_W), lambda b, r, i: (b, r, i, c))

    def prev(c):
        return pl.BlockSpec((1, nr, DIL_BLOCK, HEAD_W), lambda b, r, i: (b, r, jnp.maximum(i * nq - 1, 0), c))

    return pl.pallas_call(
        functools.partial(_dil_kernel, nr=nr, nq=nq),
        grid=(B, d // nr, Lsub // rows),
        in_specs=[cur(0), prev(1), cur(1), prev(2), cur(2),
                  pl.BlockSpec((DIL_HEADS, DIL_BLOCK, 2 * DIL_BLOCK), lambda b, r, i: (0, 0, 0))],
        out_specs=[
            pl.BlockSpec((1, nr, rows, HEAD_W), lambda b, r, i: (b, r, i, 0)),
            pl.BlockSpec((1, nr, rows, LANES), lambda b, r, i: (b, r, i, 0)),
        ],
        out_shape=[
            jax.ShapeDtypeStruct((B, d, Lsub, HEAD_W), BF16),
            jax.ShapeDtypeStruct((B, d, Lsub, LANES), F32),
        ],
        compiler_params=_params(("arbitrary", "arbitrary", "arbitrary")),
        name=f"dilated{gi}",
    )(qkv, qkv, qkv, qkv, qkv, bias)


def _alibi_bias(gi):
    window, d = DIL_PATTERNS[gi]
    Q = DIL_BLOCK
    dist = np.arange(Q)[:, None] + Q - np.arange(2 * Q)[None, :]
    valid = (dist >= 0) & (dist <= window // d)
    slopes = np.asarray(ALIBI_SLOPES, np.float32).reshape(DIL_GROUPS, DIL_HEADS)[gi]
    bias = -slopes[:, None, None] * (d * dist).astype(np.float32) * np.float32(LOG2_E)
    return jnp.asarray(np.where(valid[None], bias, np.float32(NEG)), F32)


def _merge_rows(x_ref, og_ref, o0_ref, s0_ref, g1_ref, wbg_ref, bbg_ref, wpg_ref, wpa_ref, wout_ref,
                g2_ref, wr_ref, br_ref, x2_ref, obuf_ref, sbuf_ref, row0, n):
    rows = slice(row0, row0 + n)
    x = x_ref[rows, :]
    hb = _rms(x, g1_ref[...]).astype(BF16)
    gates = jax.nn.sigmoid(_dot(hb, wbg_ref[...]) + bbg_ref[...])

    stats = (s0_ref[rows, :], sbuf_ref[0, rows, :], sbuf_ref[1, rows, :])
    dens = [pltpu.roll(s, 64, 1) for s in stats]
    m_all = jnp.maximum(jnp.maximum(stats[0], stats[1]), stats[2])
    wts = [d * jnp.exp2(s - m_all) for s, d in zip(stats, dens)]
    inv = 1.0 / (wts[0] + wts[1] + wts[2])
    coef = [w * inv for w in wts]
    heads = []
    for h in range(DIL_HEADS):
        group_out = (o0_ref[rows, _head(h)].astype(F32), obuf_ref[0, h, rows, :], obuf_ref[1, h, rows, :])
        acc = jnp.zeros((n, DIL_DH), F32)
        for g in range(DIL_GROUPS):
            c = jnp.broadcast_to(coef[g][:, 16 * h:16 * h + 1], (n, DIL_DH))
            acc = acc + c * group_out[g]
        heads.append(acc.astype(BF16))
    o_att = jnp.concatenate(heads, axis=1)

    y = gates[:, :D_MODEL] * _dot(og_ref[rows, :], wpg_ref[...]) + gates[:, D_MODEL:] * _dot(o_att, wpa_ref[...])
    x2 = x + _dot(y.astype(BF16), wout_ref[...])
    x2_ref[rows, :] = x2

    h2 = _rms(x2, g2_ref[...])

    h2h = h2.astype(BF16)
    h2l = (h2 - h2h.astype(F32)).astype(BF16)
    wr = wr_ref[...]
    pa = _dot(h2h, wr)
    pb = _dot(h2l, wr)
    logit = pa[:, :LANES] + pa[:, LANES:] + pb[:, :LANES] + pb[:, LANES:] + br_ref[...]

    lane = lax.broadcasted_iota(jnp.int32, (n, LANES), 1).astype(F32)
    big = jnp.float32(1e9)
    gl = jnp.where(lane < N_GROUPS, logit, NEG)
    gmax = jnp.max(gl, axis=-1, keepdims=True)
    gsel = jnp.min(jnp.where(gl == gmax, lane, big), axis=-1, keepdims=True)
    g_p = 1.0 / jnp.sum(jnp.exp(gl - gmax), axis=-1, keepdims=True)
    lo = N_GROUPS + EXPERTS_PER_GROUP * gsel
    el = jnp.where((lane >= lo) & (lane < lo + EXPERTS_PER_GROUP), logit, NEG)
    v1 = jnp.max(el, axis=-1, keepdims=True)
    i1 = jnp.min(jnp.where(el == v1, lane, big), axis=-1, keepdims=True)
    el2 = jnp.where(lane == i1, NEG, el)
    v2 = jnp.max(el2, axis=-1, keepdims=True)
    i2 = jnp.min(jnp.where(el2 == v2, lane, big), axis=-1, keepdims=True)
    ex = jnp.exp(v2 - v1)
    w1 = g_p / (1.0 + ex)
    w2 = g_p * ex / (1.0 + ex)
    return i1 - N_GROUPS, i2 - N_GROUPS, w1, w2, h2h


def _merge_kernel(x_ref, og_ref, o0_ref, o1_ref, o2_ref, s0_ref, s1_ref, s2_ref, g1_ref, wbg_ref, bbg_ref,
                  wpg_ref, wpa_ref, wout_ref, g2_ref, wr_ref, br_ref, tri_ref, upper_ref,
                  x2_ref, xloc_ref, route_ref, cnt_ref, obuf_ref, sbuf_ref):
    tm = x_ref.shape[0]

    for slot, (o_ref, s_ref) in enumerate(((o1_ref, s1_ref), (o2_ref, s2_ref))):
        d = DIL_PATTERNS[slot + 1][1]
        for r in range(d):
            sbuf_ref[slot, pl.ds(r, tm // d, stride=d), :] = s_ref[0, r]
            for c in range(DIL_HEADS):
                obuf_ref[slot, c, pl.ds(r, tm // d, stride=d), :] = o_ref[0, r, :, _head(c)].astype(F32)

    sub = tm // MERGE_SPLIT
    parts = [_merge_rows(x_ref, og_ref, o0_ref, s0_ref, g1_ref, wbg_ref, bbg_ref, wpg_ref, wpa_ref, wout_ref,
                         g2_ref, wr_ref, br_ref, x2_ref, obuf_ref, sbuf_ref, p * sub, sub)
             for p in range(MERGE_SPLIT)]
    e1, e2, w1, w2, h2b = (jnp.concatenate([part[j] for part in parts], axis=0) for j in range(5))

    lane = lax.broadcasted_iota(jnp.int32, (tm, LANES), 1).astype(F32)
    oh1 = lane == e1
    oh2 = lane == e2
    onehot = jnp.where(oh1 | oh2, 1.0, 0.0)
    prefix = _dot(tri_ref[...], onehot.astype(BF16))
    cnt = jnp.sum(onehot, axis=0, keepdims=True)
    cnt_pad = jnp.floor((cnt + (CHUNK - 1)) * (1.0 / CHUNK)) * CHUNK
    seg_off = _dot(jnp.broadcast_to(cnt_pad, (8, LANES)).astype(BF16), upper_ref[...])[0:1, :]
    base = seg_off + prefix
    pos1 = jnp.sum(jnp.where(oh1, base, 0.0), axis=-1, keepdims=True)
    pos2 = jnp.sum(jnp.where(oh2, base, 0.0), axis=-1, keepdims=True)
    cnt_ref[...] = jnp.broadcast_to(cnt, cnt_ref.shape)

    route = jnp.zeros((tm, LANES), F32)
    for idx, val in enumerate((e1, e2, w1, w2, pos1, pos2)):
        route = jnp.where(lane == idx, val, route)
    route_ref[...] = route

    route_t = route.T
    slot = lax.broadcasted_iota(jnp.int32, (LOCAL_ROWS, tm), 0).astype(F32)
    perm = jnp.where((slot == route_t[4:5, :]) | (slot == route_t[5:6, :]), 1.0, 0.0).astype(BF16)
    xloc_ref[...] = _pack_pairs(_dot(perm, h2b))


def _merge(x2d, o_gla, outs, stats, g1, wbg, bbg, wpg, wpa, wout, g2, wr, br, tri, upper, S):
    T = x2d.shape[0]
    tm = min(MERGE_TM, S)
    assert TOP_K * tm + N_EXPERTS * CHUNK <= LOCAL_ROWS
    tpb = S // tm
    tok = lambda w: pl.BlockSpec((tm, w), lambda i: (i, 0))
    full = lambda a: pl.BlockSpec(a.shape, lambda i: (0,) * a.ndim)

    def res(gi, w):
        d = DIL_PATTERNS[gi][1]
        return pl.BlockSpec((1, d, tm // d, w), lambda i: (i // tpb, 0, i % tpb, 0))

    return pl.pallas_call(
        _merge_kernel,
        grid=(T // tm,),
        in_specs=[tok(D_MODEL), tok(HEAD_W), tok(HEAD_W), res(1, HEAD_W), res(2, HEAD_W),
                  tok(LANES), res(1, LANES), res(2, LANES),
                  full(g1), full(wbg), full(bbg), full(wpg), full(wpa), full(wout), full(g2),
                  full(wr), full(br), full(tri), full(upper)],
        out_specs=[tok(D_MODEL), pl.BlockSpec((LOCAL_ROWS, PACKED_W), lambda i: (i, 0)), tok(LANES),
                   pl.BlockSpec((8, LANES), lambda i: (i, 0))],
        out_shape=[
            jax.ShapeDtypeStruct((T, D_MODEL), F32),
            jax.ShapeDtypeStruct((T // tm * LOCAL_ROWS, PACKED_W), F32),
            jax.ShapeDtypeStruct((T, LANES), F32),
            jax.ShapeDtypeStruct((T // tm * 8, LANES), F32),
        ],
        scratch_shapes=[pltpu.VMEM((DIL_GROUPS - 1, DIL_HEADS, tm, LANES), F32),
                        pltpu.VMEM((DIL_GROUPS - 1, tm, LANES), F32)],
        compiler_params=_params(("arbitrary",)),
        name="merge",
    )(x2d, o_gla, *outs, *stats, g1, wbg, bbg, wpg, wpa, wout, g2, wr, br, tri, upper)


class _ChunkGather:
    def __init__(self, src_hbm, buf_ref, sem, nchunks):
        self.src, self.buf, self.sem, self.n = src_hbm, buf_ref, sem, nchunks

    def start(self, offs_ref, s):
        for j in range(self.n):
            row = pl.multiple_of(offs_ref[0, 0, j], CHUNK)
            pltpu.make_async_copy(self.src.at[pl.ds(row, CHUNK), :], self.buf.at[s, pl.ds(j * CHUNK, CHUNK), :],
                                  self.sem.at[s]).start()

    def wait(self, s):
        pltpu.make_async_copy(self.src.at[pl.ds(0, self.n * CHUNK), :], self.buf.at[s], self.sem.at[s]).wait()

    def step(self, offs_cur_ref, offs_next_ref):
        i = pl.program_id(0)
        slot = lax.rem(i, 2)

        @pl.when(i == 0)
        def _():
            self.start(offs_cur_ref, 0)

        self.wait(slot)

        @pl.when(i + 1 < pl.num_programs(0))
        def _():
            self.start(offs_next_ref, 1 - slot)

        return slot


def _expert_kernel(tile_e_ref, nact_ref, src_cur_ref, src_next_ref, xloc_ref, wg_ref, wu_ref, wd_ref,
                   y_ref, xbuf_ref, wgu_ref, wdn_ref, gsem):
    i = pl.program_id(0)
    slot = _ChunkGather(xloc_ref, xbuf_ref, gsem, TILE_CHUNKS).step(src_cur_ref, src_next_ref)
    active = i < nact_ref[0]

    @pl.when(jnp.logical_and(active, jnp.logical_or(i == 0, tile_e_ref[i] != tile_e_ref[jnp.maximum(i - 1, 0)])))
    def _():
        wgu_ref[:, :D_EXPERT] = wg_ref[0].astype(BF16)
        wgu_ref[:, D_EXPERT:] = wu_ref[0].astype(BF16)
        wdn_ref[...] = wd_ref[0].astype(BF16)

    @pl.when(active)
    def _():
        xb = _unpack_pairs(xbuf_ref[slot])
        gu = _dot(xb, wgu_ref[...])
        g, u = gu[:, :D_EXPERT], gu[:, D_EXPERT:]
        a = (g * jax.nn.sigmoid(g) * u).astype(BF16)
        y_ref[...] = _pack_pairs(_dot(a, wdn_ref[...]).astype(BF16).astype(F32))

    @pl.when(jnp.logical_not(active))
    def _():
        y_ref[...] = jnp.zeros_like(y_ref)


def _experts(tile_e, nact, csrc, xloc, wg, wu, wd):
    ntiles = tile_e.shape[0]
    R = MOE_TM

    def wsel(i, te, na):
        return (te[jnp.minimum(i, na[0] - 1)], 0, 0)

    def offs(shift):
        return pl.BlockSpec((1, 1, TILE_CHUNKS), lambda i, te, na: (jnp.minimum(i + shift, ntiles - 1), 0, 0),
                            memory_space=pltpu.SMEM)

    return pl.pallas_call(
        _expert_kernel,
        grid_spec=pltpu.PrefetchScalarGridSpec(
            num_scalar_prefetch=2,
            grid=(ntiles,),
            in_specs=[
                offs(0), offs(1),
                pl.BlockSpec(memory_space=pl.ANY),
                pl.BlockSpec((1, D_MODEL, D_EXPERT), wsel),
                pl.BlockSpec((1, D_MODEL, D_EXPERT), wsel),
                pl.BlockSpec((1, D_EXPERT, D_MODEL), wsel),
            ],
            out_specs=pl.BlockSpec((R, PACKED_W), lambda i, te, na: (i, 0)),
            scratch_shapes=[pltpu.VMEM((2, R, PACKED_W), F32), pltpu.VMEM((D_MODEL, 2 * D_EXPERT), BF16),
                            pltpu.VMEM((D_EXPERT, D_MODEL), BF16), pltpu.SemaphoreType.DMA((2,))],
        ),
        out_shape=jax.ShapeDtypeStruct((ntiles * R, PACKED_W), F32),
        compiler_params=_params(("arbitrary",)),
        name="experts",
    )(tile_e, nact, csrc, csrc, xloc, wg, wu, wd)


def _combine_kernel(dst_cur_ref, dst_next_ref, x2_ref, route_ref, y_ref, out_ref, ybuf_ref, gsem):
    slot = _ChunkGather(y_ref, ybuf_ref, gsem, LOCAL_CHUNKS).step(dst_cur_ref, dst_next_ref)
    tm = x2_ref.shape[0]
    route = route_ref[...]
    w1, w2, pos1, pos2 = (route[:, c:c + 1] for c in range(2, 6))
    yl = _unpack_pairs(ybuf_ref[slot])
    cols = lax.broadcasted_iota(jnp.int32, (tm, LOCAL_ROWS), 1).astype(F32)
    gate = jnp.where(cols == pos1, w1, jnp.where(cols == pos2, w2, 0.0)).astype(BF16)
    out_ref[...] = x2_ref[...] + _dot(gate, yl)


def _combine(cdst, x2, route, y):
    T = x2.shape[0]
    nt = cdst.shape[0]
    tm = T // nt

    def offs(shift):
        return pl.BlockSpec((1, 1, LOCAL_CHUNKS), lambda i: (jnp.minimum(i + shift, nt - 1), 0, 0),
                            memory_space=pltpu.SMEM)

    return pl.pallas_call(
        _combine_kernel,
        grid=(nt,),
        in_specs=[
            offs(0), offs(1),
            pl.BlockSpec((tm, D_MODEL), lambda i: (i, 0)),
            pl.BlockSpec((tm, LANES), lambda i: (i, 0)),
            pl.BlockSpec(memory_space=pl.ANY),
        ],
        out_specs=pl.BlockSpec((tm, D_MODEL), lambda i: (i, 0)),
        out_shape=jax.ShapeDtypeStruct((T, D_MODEL), F32),
        scratch_shapes=[pltpu.VMEM((2, LOCAL_ROWS, PACKED_W), F32), pltpu.SemaphoreType.DMA((2,))],
        compiler_params=_params(("arbitrary",)),
        name="combine",
    )(cdst, cdst, x2, route, y)


def _layer(x2d, B, S, norm1_g, w_in, w_gla_a2, b_gla_a, gla_out_norm_g, dil_q_norm_g, dil_k_norm_g,
           w_proj_gla, w_proj_attn, w_branch_gate, b_branch_gate, w_out, norm2_g,
           w_router_group, b_router_group, w_router_expert, b_router_expert, w_gate, w_up, w_down):
    T = B * S
    n_gla = N_GLA_TILES * HEAD_W
    w_gla = w_in[:, :n_gla].astype(BF16)
    w_dil = w_in.astype(BF16)[:, n_gla + GLA_RANK:]
    w_ga = jnp.pad(w_in[:, n_gla:n_gla + GLA_RANK], ((0, 0), (0, LANES - GLA_RANK))).astype(BF16)
    w_a2 = jnp.pad(w_gla_a2, ((0, LANES - GLA_RANK), (0, 0))).astype(BF16)
    gains = [jnp.ones((HEAD_W,), F32)] * N_GLA_TILES
    for gi in range(DIL_GROUPS):
        gains += [jnp.tile(dil_q_norm_g[gi], DIL_HEADS) * (DIL_DH ** -0.5 * LOG2_E),
                  jnp.tile(dil_k_norm_g[gi], DIL_HEADS), jnp.ones((HEAD_W,), F32)]
    qk_gain = jnp.stack(gains).reshape(N_COL_TILES, 1, HEAD_W)

    gla_in, log_a, qkv0, qkv1, qkv2 = _inproj(
        x2d, norm1_g.reshape(1, -1), w_gla, w_dil, w_ga, w_a2, b_gla_a.reshape(1, -1), qk_gain, B, S)

    tri_incl = jnp.asarray(np.tril(np.ones((GLA_BLOCK, GLA_BLOCK), np.float32)), BF16)
    o_gla = _gla(gla_in, log_a, tri_incl, gla_out_norm_g.reshape(1, -1), B, S)

    outs, stats = [], []
    for gi, qkv in enumerate((qkv0.reshape(B, 1, S, QKV_W), qkv1, qkv2)):
        o, st = _dilated(qkv, _alibi_bias(gi), gi)
        outs.append(o)
        stats.append(st)
    outs[0] = outs[0].reshape(T, HEAD_W)
    stats[0] = stats[0].reshape(T, LANES)

    w_r = jnp.pad(jnp.concatenate([w_router_group, w_router_expert], axis=1),
                  ((0, 0), (0, LANES - N_GROUPS - N_EXPERTS)))
    w_rh = w_r.astype(BF16)
    w_rl = (w_r - w_rh.astype(F32)).astype(BF16)
    b_r = jnp.pad(jnp.concatenate([b_router_group, b_router_expert]), (0, LANES - N_GROUPS - N_EXPERTS))
    tm = min(MERGE_TM, S)
    tri_strict = jnp.asarray(np.tril(np.ones((tm, tm), np.float32), -1), BF16)
    upper = jnp.asarray(np.triu(np.ones((LANES, LANES), np.float32), 1), BF16)
    x2, xloc, route, cnt = _merge(
        x2d, o_gla, outs, stats, norm1_g.reshape(1, -1), w_branch_gate.astype(BF16),
        b_branch_gate.reshape(1, -1), w_proj_gla.astype(BF16), w_proj_attn.astype(BF16), w_out.astype(BF16),
        norm2_g.reshape(1, -1), jnp.concatenate([w_rh, w_rl], axis=1), b_r.reshape(1, -1), tri_strict, upper, S)

    csrc, cdst, tile_e, nact = _chunk_tables(cnt, T // tm)
    y = _experts(tile_e, nact, csrc, xloc, w_gate, w_up, w_down)
    return _combine(cdst, x2, route, y)


def _chunk_tables(cnt, nt):
    i32 = jnp.int32
    experts = jnp.arange(N_EXPERTS, dtype=i32)
    tiles = jnp.arange(nt, dtype=i32)
    c = (cnt.reshape(nt, 8, LANES)[:, 0, :N_EXPERTS].astype(i32) + CHUNK - 1) // CHUNK
    loff = jnp.cumsum(c, axis=1) - c
    ecum = jnp.cumsum(c, axis=0) - c
    tot = jnp.sum(c, axis=0)
    ptot = (tot + TILE_CHUNKS - 1) // TILE_CHUNKS * TILE_CHUNKS
    pend = jnp.cumsum(ptot)
    pstart = pend - ptot

    lc = jnp.arange(LOCAL_CHUNKS, dtype=i32)
    e_of = jnp.sum(((loff + c)[:, None, :] <= lc[None, :, None]).astype(i32), axis=-1)
    pick_e = e_of[:, :, None] == experts[None, None, :]
    glob = jnp.sum(jnp.where(pick_e, (pstart[None, :] + ecum - loff)[:, None, :], 0), axis=-1) + lc[None, :]
    cdst = jnp.where(e_of < N_EXPERTS, glob, 0) * CHUNK

    ntiles = (TOP_K * nt * MERGE_TM // CHUNK + nt * N_EXPERTS + N_EXPERTS * (TILE_CHUNKS - 1)
              + TILE_CHUNKS - 1) // TILE_CHUNKS
    g = jnp.arange(ntiles * TILE_CHUNKS, dtype=i32)
    e_g = jnp.minimum(jnp.sum((pend[None, :] <= g[:, None]).astype(i32), axis=1), N_EXPERTS - 1)
    k = g - jnp.sum(jnp.where(e_g[:, None] == experts[None, :], pstart[None, :], 0), axis=1)
    pick = (e_g[:, None] == experts[None, :]).astype(F32)
    ends = jnp.dot(pick, (ecum + c).T.astype(F32), precision=lax.Precision.HIGHEST)
    t_g = jnp.sum((ends <= k[:, None].astype(F32)).astype(i32), axis=1)
    valid = t_g < nt
    base = jnp.dot(pick, (tiles[:, None] * LOCAL_CHUNKS + loff - ecum).T.astype(F32),
                   precision=lax.Precision.HIGHEST)
    in_tile = tiles[None, :] == jnp.minimum(t_g, nt - 1)[:, None]
    local = jnp.sum(jnp.where(in_tile, base, 0.0), axis=1).astype(i32) + k
    zero_chunk = LOCAL_CHUNKS - 1
    csrc = jnp.where(valid, local, zero_chunk) * CHUNK

    tile_e = e_g[::TILE_CHUNKS]
    nact = (pend[-1:] // TILE_CHUNKS).astype(i32)
    return (csrc.reshape(ntiles, 1, TILE_CHUNKS), cdst.reshape(nt, 1, LOCAL_CHUNKS), tile_e, nact)


def kernel(x, norm1_g, w_in, w_gla_a2, b_gla_a, gla_out_norm_g, dil_q_norm_g, dil_k_norm_g, w_proj_gla,
           w_proj_attn, w_branch_gate, b_branch_gate, w_out, norm2_g, w_router_group, b_router_group,
           w_router_expert, b_router_expert, w_gate, w_up, w_down):
    B, S, D = x.shape
    assert D == D_MODEL and S % (DIL_BLOCK * DIL_PATTERNS[-1][1]) == 0
    x2d = x.reshape(B * S, D)
    params = (norm1_g, w_in, w_gla_a2, b_gla_a, gla_out_norm_g, dil_q_norm_g, dil_k_norm_g, w_proj_gla,
              w_proj_attn, w_branch_gate, b_branch_gate, w_out, norm2_g, w_router_group, b_router_group,
              w_router_expert, b_router_expert, w_gate, w_up, w_down)
    for layer in range(norm1_g.shape[0]):
        x2d = _layer(x2d, B, S, *(p[layer] for p in params))
    return x2d.reshape(B, S, D)
```

```python
import functools

import jax
import jax.numpy as jnp
import numpy as np
from jax import lax
from jax.experimental import pallas as pl
from jax.experimental.pallas import tpu as pltpu

F32 = jnp.float32
BF16 = jnp.bfloat16

D_MODEL = 1024
EPS = 1e-6
GLA_HEADS = 4
GLA_DK = 128
GLA_RANK = 16
GLA_TAU = 16.0
LOG2_E = 1.4426950408889634
GLA_SUBCHUNK = 16
GLA_BLOCK = 128
GLA_STEP_BLOCKS = 16
DIL_PATTERNS = ((128, 1), (512, 4), (2048, 16))
DIL_GROUPS = 3
DIL_HEADS = 4
DIL_DH = 128
DIL_BLOCK = 128
ALIBI_SLOPES = tuple(2.0 ** (-8.0 * (i + 1) / (DIL_GROUPS * DIL_HEADS)) for i in range(DIL_GROUPS * DIL_HEADS))
N_GROUPS = 4
EXPERTS_PER_GROUP = 8
N_EXPERTS = N_GROUPS * EXPERTS_PER_GROUP
TOP_K = 2
D_EXPERT = 512

HEAD_W = 512
N_GLA_TILES = 4
N_COL_TILES = N_GLA_TILES + 3 * DIL_GROUPS
QKV_W = 3 * HEAD_W
LANES = 128
NEG = -1e30

INPROJ_TM = 512
MERGE_TM = 512
MOE_TM = 512
CHUNK = 8
TILE_CHUNKS = MOE_TM // CHUNK
LOCAL_ROWS = 1280
LOCAL_CHUNKS = LOCAL_ROWS // CHUNK
DIL_STEP_BLOCKS = 16
VMEM_LIMIT = 56 * 1024 * 1024


def _dot(a, b):
    return jnp.dot(a, b, preferred_element_type=F32)


def _dot_nt(a, b):
    return lax.dot_general(a, b, (((1,), (1,)), ((), ())), preferred_element_type=F32)


def _dot_tn(a, b):
    return lax.dot_general(a, b, (((0,), (0,)), ((), ())), preferred_element_type=F32)


def _rms(x, g):
    return x * lax.rsqrt(jnp.mean(x * x, axis=-1, keepdims=True) + EPS) * g


def _params(sem):
    return pltpu.CompilerParams(dimension_semantics=sem, vmem_limit_bytes=VMEM_LIMIT)


def _head(c):
    return slice(c * LANES, (c + 1) * LANES)


PACKED_W = D_MODEL // 2
_HI16 = 0xFFFF0000


def _pack_pairs(x):
    u = lax.bitcast_convert_type(x, jnp.uint32)
    w = x.shape[1] // 2
    return lax.bitcast_convert_type((u[:, :w] & jnp.uint32(_HI16)) | (u[:, w:] >> 16), F32)


def _unpack_pairs(p):
    u = lax.bitcast_convert_type(p, jnp.uint32)
    hi = lax.bitcast_convert_type(u & jnp.uint32(_HI16), F32)
    lo = lax.bitcast_convert_type(u << 16, F32)
    return jnp.concatenate([hi, lo], axis=1).astype(BF16)


def _inproj_kernel(x_ref, g1_ref, wgla_ref, wdil_ref, wga_ref, wa2_ref, ba_ref, qkg_ref,
                   gla_ref, loga_ref, d0_ref, d1_ref, d2_ref, h_ref, ybuf_ref):
    tm = x_ref.shape[0]
    hb = _rms(x_ref[...], g1_ref[...]).astype(BF16)
    h_ref[...] = hb
    ga = _dot(hb, wga_ref[...])
    z = _dot(ga.astype(BF16), wa2_ref[...]) + ba_ref[...]
    log_sig = jnp.minimum(z, 0.0) - jnp.log(1.0 + jnp.exp(-jnp.abs(z)))
    loga_ref[...] = log_sig * (LOG2_E / GLA_TAU)

    dil_refs = (d0_ref, d1_ref, d2_ref)
    n_strided = 0
    for jj in range(N_COL_TILES):
        cols = slice(jj * HEAD_W, (jj + 1) * HEAD_W)
        if jj < N_GLA_TILES:
            w = wgla_ref[:, cols]
        else:
            w = wdil_ref[:, (jj - N_GLA_TILES) * HEAD_W:(jj - N_GLA_TILES + 1) * HEAD_W]
        y = _dot(h_ref[...], w)
        if jj < N_GLA_TILES:
            gla_ref[:, cols] = y.astype(BF16)
            continue
        gi, kind = divmod(jj - N_GLA_TILES, 3)
        d = DIL_PATTERNS[gi][1]
        ocols = [slice(kind * HEAD_W + c * LANES, kind * HEAD_W + (c + 1) * LANES) for c in range(DIL_HEADS)]
        if kind < 2:
            g = qkg_ref[jj]
            slabs = [_rms(y[:, _head(c)], g[:, _head(c)]) for c in range(DIL_HEADS)]
        else:
            slabs = [y[:, _head(c)] for c in range(DIL_HEADS)]
        out = dil_refs[gi]
        if d == 1:
            for c in range(DIL_HEADS):
                out[:, ocols[c]] = slabs[c].astype(BF16)
            continue
        buf = n_strided % ybuf_ref.shape[0]
        n_strided += 1
        for c in range(DIL_HEADS):
            ybuf_ref[buf, c] = slabs[c]
        for r in range(d):
            for c in range(DIL_HEADS):
                out[0, r, :, ocols[c]] = ybuf_ref[buf, c, pl.ds(r, tm // d, stride=d), :].astype(BF16)


def _resident(shape):
    return pl.BlockSpec(shape, lambda i: (0,) * len(shape), pipeline_mode=pl.Buffered(1))


def _inproj(x2d, g1, w_gla, w_dil, w_ga, w_a2, b_a, qk_gain, B, S):
    T = x2d.shape[0]
    tm = min(INPROJ_TM, S)
    tpb = S // tm

    def dil_spec(gi):
        d = DIL_PATTERNS[gi][1]
        if d == 1:
            return pl.BlockSpec((tm, QKV_W), lambda i: (i, 0))
        return pl.BlockSpec((1, d, tm // d, QKV_W), lambda i: (i // tpb, 0, i % tpb, 0))

    def dil_shape(gi):
        d = DIL_PATTERNS[gi][1]
        shape = (T, QKV_W) if d == 1 else (B, d, S // d, QKV_W)
        return jax.ShapeDtypeStruct(shape, BF16)

    return pl.pallas_call(
        _inproj_kernel,
        grid=(T // tm,),
        in_specs=[
            pl.BlockSpec((tm, D_MODEL), lambda i: (i, 0)),
            _resident(g1.shape), _resident(w_gla.shape), _resident(w_dil.shape), _resident(w_ga.shape),
            _resident(w_a2.shape),
            _resident(b_a.shape), _resident(qk_gain.shape),
        ],
        out_specs=[
            pl.BlockSpec((tm, N_GLA_TILES * HEAD_W), lambda i: (i, 0)),
            pl.BlockSpec((tm, HEAD_W), lambda i: (i, 0)),
            dil_spec(0), dil_spec(1), dil_spec(2),
        ],
        out_shape=[
            jax.ShapeDtypeStruct((T, N_GLA_TILES * HEAD_W), BF16),
            jax.ShapeDtypeStruct((T, HEAD_W), F32),
            dil_shape(0), dil_shape(1), dil_shape(2),
        ],
        scratch_shapes=[pltpu.VMEM((tm, D_MODEL), BF16), pltpu.VMEM((2, DIL_HEADS, tm, LANES), F32)],
        compiler_params=_params(("arbitrary",)),
        name="inproj",
    )(x2d, g1, w_gla, w_dil, w_ga, w_a2, b_a, qk_gain)


def _gla_kernel(q_ref, k_ref, v_ref, r_ref, la_ref, tri_ref, gn_ref, o_ref, st_ref):
    n = pl.program_id(1)

    @pl.when(n == 0)
    def _():
        st_ref[...] = jnp.zeros_like(st_ref)

    for blk in range(q_ref.shape[0] // GLA_BLOCK):
        _gla_block(q_ref, k_ref, v_ref, r_ref, la_ref, tri_ref, gn_ref, o_ref, st_ref,
                   slice(blk * GLA_BLOCK, (blk + 1) * GLA_BLOCK))


def _gla_block(q_ref, k_ref, v_ref, r_ref, la_ref, tri_ref, gn_ref, o_ref, st_ref, rows):
    L = GLA_BLOCK
    la = la_ref[rows, :]
    hi = la.astype(BF16)
    r1 = la - hi.astype(F32)
    mid = r1.astype(BF16)
    lo = (r1 - mid.astype(F32)).astype(BF16)
    tri = tri_ref[...]
    b_all = _dot(tri, hi) + _dot(tri, mid) + _dot(tri, lo)

    row = lax.broadcasted_iota(jnp.int32, (L, L), 0)
    col = lax.broadcasted_iota(jnp.int32, (L, L), 1)

    for h in range(GLA_HEADS):
        sl = _head(h)
        b = b_all[:, sl]
        q = q_ref[rows, sl].astype(F32) * (GLA_DK ** -0.5)
        k = k_ref[rows, sl].astype(F32)
        v = v_ref[rows, sl]

        att = jnp.zeros((L, L), F32)
        seg = L
        while seg > GLA_SUBCHUNK:
            half = seg // 2
            beta = jnp.concatenate(
                [jnp.broadcast_to(b[s + half - 1:s + half, :], (seg, GLA_DK)) for s in range(0, L, seg)], axis=0)
            rel = b - beta
            neg = jnp.minimum(rel, 0.0)
            qs = (q * jnp.exp2(neg)).astype(BF16)
            ks = (k * jnp.exp2(neg - rel)).astype(BF16)
            same = (row // seg) == (col // seg)
            mask = same & ((row % seg) >= half) & ((col % seg) < half)
            att = jnp.where(mask, _dot_nt(qs, ks), att)
            seg = half
        beta = jnp.concatenate(
            [jnp.zeros((GLA_SUBCHUNK, GLA_DK), F32)]
            + [jnp.broadcast_to(b[s - 1:s, :], (GLA_SUBCHUNK, GLA_DK)) for s in range(GLA_SUBCHUNK, L, GLA_SUBCHUNK)],
            axis=0)
        qs = (q * jnp.exp2(b - beta)).astype(BF16)
        ks = (k * jnp.exp2(beta - b)).astype(BF16)
        mask = ((row // GLA_SUBCHUNK) == (col // GLA_SUBCHUNK)) & (col <= row)
        att = jnp.where(mask, _dot_nt(qs, ks), att)

        state_t = st_ref[h]
        q0 = (q * jnp.exp2(b)).astype(BF16)
        o = _dot(att.astype(BF16), v) + _dot_nt(q0, state_t.astype(BF16))

        b_last = b[L - 1:L, :]
        k_end = (k * jnp.exp2(b_last - b)).astype(BF16)
        st_ref[h] = state_t * jnp.exp2(b_last) + _dot_tn(v, k_end)

        rr = r_ref[rows, sl].astype(F32)
        o_ref[rows, sl] = (_rms(o, gn_ref[...]) * (rr * jax.nn.sigmoid(rr))).astype(BF16)


def _gla(proj, log_a, tri, gn, B, S):
    L = GLA_STEP_BLOCKS * GLA_BLOCK
    nb = S // L
    return pl.pallas_call(
        _gla_kernel,
        grid=(B, nb),
        in_specs=[
            pl.BlockSpec((L, HEAD_W), lambda b, n: (b * nb + n, 0)),
            pl.BlockSpec((L, HEAD_W), lambda b, n: (b * nb + n, 1)),
            pl.BlockSpec((L, HEAD_W), lambda b, n: (b * nb + n, 2)),
            pl.BlockSpec((L, HEAD_W), lambda b, n: (b * nb + n, 3)),
            pl.BlockSpec((L, HEAD_W), lambda b, n: (b * nb + n, 0)),
            pl.BlockSpec((GLA_BLOCK, GLA_BLOCK), lambda b, n: (0, 0)),
            pl.BlockSpec((1, GLA_DK), lambda b, n: (0, 0)),
        ],
        out_specs=pl.BlockSpec((L, HEAD_W), lambda b, n: (b * nb + n, 0)),
        out_shape=jax.ShapeDtypeStruct((B * S, HEAD_W), BF16),
        scratch_shapes=[pltpu.VMEM((GLA_HEADS, GLA_DK, GLA_DK), F32)],
        compiler_params=_params(("arbitrary", "arbitrary")),
        name="gla",
    )(proj, proj, proj, proj, log_a, tri, gn)


def _dil_kernel(q_ref, kp_ref, kc_ref, vp_ref, vc_ref, bias_ref, o_ref, st_ref, *, nr, nq):
    first = pl.program_id(2) == 0
    Q = DIL_BLOCK
    col = lax.broadcasted_iota(jnp.int32, (Q, 2 * Q), 1)
    lane = lax.broadcasted_iota(jnp.int32, (Q, LANES), 1)
    for res in range(nr):
        for blk in range(nq):
            rows = slice(blk * Q, (blk + 1) * Q)
            prev = slice((blk - 1) * Q, blk * Q)
            stats = jnp.zeros((Q, LANES), F32)
            for h in range(DIL_HEADS):
                sl = _head(h)
                if blk == 0:
                    kprev, vprev = kp_ref[0, res, :, sl], vp_ref[0, res, :, sl]
                else:
                    kprev, vprev = kc_ref[0, res, prev, sl], vc_ref[0, res, prev, sl]
                kk = jnp.concatenate([kprev, kc_ref[0, res, rows, sl]], axis=0)
                vv = jnp.concatenate([vprev, vc_ref[0, res, rows, sl]], axis=0)
                s = _dot_nt(q_ref[0, res, rows, sl], kk) + bias_ref[h]
                if blk == 0:
                    s = jnp.where(jnp.logical_and(first, col < Q), NEG, s)
                m = jnp.max(s, axis=-1, keepdims=True)
                p = jnp.exp2(s - m)
                den = jnp.sum(p, axis=-1, keepdims=True)
                o_ref[0, res, rows, sl] = (_dot(p.astype(BF16), vv) / den).astype(BF16)
                stats = jnp.where((lane // 16) == h, m, stats)
                stats = jnp.where((lane // 16) == DIL_HEADS + h, den, stats)
            st_ref[0, res, rows, :] = stats


def _dilated(qkv, bias, gi):
    B, d, Lsub, _ = qkv.shape
    nq = min(DIL_STEP_BLOCKS, Lsub // DIL_BLOCK)
    nr = min(DIL_STEP_BLOCKS // nq, d)
    rows = nq * DIL_BLOCK

    def cur(c):
        return pl.BlockSpec((1, nr, rows, HEAD_W), lambda b, r, i: (b, r, i, c))

    def prev(c):
        return pl.BlockSpec((1, nr, DIL_BLOCK, HEAD_W), lambda b, r, i: (b, r, jnp.maximum(i * nq - 1, 0), c))

    return pl.pallas_call(
        functools.partial(_dil_kernel, nr=nr, nq=nq),
        grid=(B, d // nr, Lsub // rows),
        in_specs=[cur(0), prev(1), cur(1), prev(2), cur(2),
                  pl.BlockSpec((DIL_HEADS, DIL_BLOCK, 2 * DIL_BLOCK), lambda b, r, i: (0, 0, 0))],
        out_specs=[
            pl.BlockSpec((1, nr, rows, HEAD_W), lambda b, r, i: (b, r, i, 0)),
            pl.BlockSpec((1, nr, rows, LANES), lambda b, r, i: (b, r, i, 0)),
        ],
        out_shape=[
            jax.ShapeDtypeStruct((B, d, Lsub, HEAD_W), BF16),
            jax.ShapeDtypeStruct((B, d, Lsub, LANES), F32),
        ],
        compiler_params=_params(("arbitrary", "arbitrary", "arbitrary")),
        name=f"dilated{gi}",
    )(qkv, qkv, qkv, qkv, qkv, bias)


def _alibi_bias(gi):
    window, d = DIL_PATTERNS[gi]
    Q = DIL_BLOCK
    dist = np.arange(Q)[:, None] + Q - np.arange(2 * Q)[None, :]
    valid = (dist >= 0) & (dist <= window // d)
    slopes = np.asarray(ALIBI_SLOPES, np.float32).reshape(DIL_GROUPS, DIL_HEADS)[gi]
    bias = -slopes[:, None, None] * (d * dist).astype(np.float32) * np.float32(LOG2_E)
    return jnp.asarray(np.where(valid[None], bias, np.float32(NEG)), F32)


def _merge_rows(x_ref, og_ref, o0_ref, s0_ref, g1_ref, wbg_ref, bbg_ref, wpg_ref, wpa_ref, wout_ref,
                g2_ref, wr_ref, br_ref, x2_ref, obuf_ref, sbuf_ref, row0, n):
    rows = slice(row0, row0 + n)
    x = x_ref[rows, :]
    hb = _rms(x, g1_ref[...]).astype(BF16)
    gates = jax.nn.sigmoid(_dot(hb, wbg_ref[...]) + bbg_ref[...])

    stats = (s0_ref[rows, :], sbuf_ref[0, rows, :], sbuf_ref[1, rows, :])
    dens = [pltpu.roll(s, 64, 1) for s in stats]
    m_all = jnp.maximum(jnp.maximum(stats[0], stats[1]), stats[2])
    wts = [d * jnp.exp2(s - m_all) for s, d in zip(stats, dens)]
    inv = 1.0 / (wts[0] + wts[1] + wts[2])
    coef = [w * inv for w in wts]
    heads = []
    for h in range(DIL_HEADS):
        group_out = (o0_ref[rows, _head(h)].astype(F32), obuf_ref[0, h, rows, :], obuf_ref[1, h, rows, :])
        acc = jnp.zeros((n, DIL_DH), F32)
        for g in range(DIL_GROUPS):
            c = jnp.broadcast_to(coef[g][:, 16 * h:16 * h + 1], (n, DIL_DH))
            acc = acc + c * group_out[g]
        heads.append(acc.astype(BF16))
    o_att = jnp.concatenate(heads, axis=1)

    y = gates[:, :D_MODEL] * _dot(og_ref[rows, :], wpg_ref[...]) + gates[:, D_MODEL:] * _dot(o_att, wpa_ref[...])
    x2 = x + _dot(y.astype(BF16), wout_ref[...])
    x2_ref[rows, :] = x2

    h2 = _rms(x2, g2_ref[...])

    h2h = h2.astype(BF16)
    h2l = (h2 - h2h.astype(F32)).astype(BF16)
    wr = wr_ref[...]
    pa = _dot(h2h, wr)
    pb = _dot(h2l, wr)
    logit = pa[:, :LANES] + pa[:, LANES:] + pb[:, :LANES] + pb[:, LANES:] + br_ref[...]

    lane = lax.broadcasted_iota(jnp.int32, (n, LANES), 1).astype(F32)
    big = jnp.float32(1e9)
    gl = jnp.where(lane < N_GROUPS, logit, NEG)
    gmax = jnp.max(gl, axis=-1, keepdims=True)
    gsel = jnp.min(jnp.where(gl == gmax, lane, big), axis=-1, keepdims=True)
    g_p = 1.0 / jnp.sum(jnp.exp(gl - gmax), axis=-1, keepdims=True)
    lo = N_GROUPS + EXPERTS_PER_GROUP * gsel
    el = jnp.where((lane >= lo) & (lane < lo + EXPERTS_PER_GROUP), logit, NEG)
    v1 = jnp.max(el, axis=-1, keepdims=True)
    i1 = jnp.min(jnp.where(el == v1, lane, big), axis=-1, keepdims=True)
    el2 = jnp.where(lane == i1, NEG, el)
    v2 = jnp.max(el2, axis=-1, keepdims=True)
    i2 = jnp.min(jnp.where(el2 == v2, lane, big), axis=-1, keepdims=True)
    ex = jnp.exp(v2 - v1)
    w1 = g_p / (1.0 + ex)
    w2 = g_p * ex / (1.0 + ex)
    return i1 - N_GROUPS, i2 - N_GROUPS, w1, w2, h2h


def _merge_kernel(x_ref, og_ref, o0_ref, o1_ref, o2_ref, s0_ref, s1_ref, s2_ref, g1_ref, wbg_ref, bbg_ref,
                  wpg_ref, wpa_ref, wout_ref, g2_ref, wr_ref, br_ref, tri_ref, upper_ref,
                  x2_ref, xloc_ref, route_ref, cnt_ref, obuf_ref, sbuf_ref):
    tm = x_ref.shape[0]

    for slot, (o_ref, s_ref) in enumerate(((o1_ref, s1_ref), (o2_ref, s2_ref))):
        d = DIL_PATTERNS[slot + 1][1]
        for r in range(d):
            sbuf_ref[slot, pl.ds(r, tm // d, stride=d), :] = s_ref[0, r]
            for c in range(DIL_HEADS):
                obuf_ref[slot, c, pl.ds(r, tm // d, stride=d), :] = o_ref[0, r, :, _head(c)].astype(F32)

    e1, e2, w1, w2, h2b = _merge_rows(x_ref, og_ref, o0_ref, s0_ref, g1_ref, wbg_ref, bbg_ref, wpg_ref, wpa_ref,
                                      wout_ref, g2_ref, wr_ref, br_ref, x2_ref, obuf_ref, sbuf_ref, 0, tm)

    lane = lax.broadcasted_iota(jnp.int32, (tm, LANES), 1).astype(F32)
    oh1 = lane == e1
    oh2 = lane == e2
    onehot = jnp.where(oh1 | oh2, 1.0, 0.0)
    prefix = _dot(tri_ref[...], onehot.astype(BF16))
    cnt = jnp.sum(onehot, axis=0, keepdims=True)
    cnt_pad = jnp.floor((cnt + (CHUNK - 1)) * (1.0 / CHUNK)) * CHUNK
    seg_off = _dot(jnp.broadcast_to(cnt_pad, (8, LANES)).astype(BF16), upper_ref[...])[0:1, :]
    base = seg_off + prefix
    pos1 = jnp.sum(jnp.where(oh1, base, 0.0), axis=-1, keepdims=True)
    pos2 = jnp.sum(jnp.where(oh2, base, 0.0), axis=-1, keepdims=True)
    cnt_ref[...] = jnp.broadcast_to(cnt, cnt_ref.shape)

    route = jnp.zeros((tm, LANES), F32)
    for idx, val in enumerate((e1, e2, w1, w2, pos1, pos2)):
        route = jnp.where(lane == idx, val, route)
    route_ref[...] = route

    route_t = route.T
    slot = lax.broadcasted_iota(jnp.int32, (LOCAL_ROWS, tm), 0).astype(F32)
    perm = jnp.where((slot == route_t[4:5, :]) | (slot == route_t[5:6, :]), 1.0, 0.0).astype(BF16)
    xloc_ref[...] = _pack_pairs(_dot(perm, h2b))


def _merge(x2d, o_gla, outs, stats, g1, wbg, bbg, wpg, wpa, wout, g2, wr, br, tri, upper, S):
    T = x2d.shape[0]
    tm = min(MERGE_TM, S)
    assert TOP_K * tm + N_EXPERTS * CHUNK <= LOCAL_ROWS
    tpb = S // tm
    tok = lambda w: pl.BlockSpec((tm, w), lambda i: (i, 0))
    full = lambda a: pl.BlockSpec(a.shape, lambda i: (0,) * a.ndim)

    def res(gi, w):
        d = DIL_PATTERNS[gi][1]
        return pl.BlockSpec((1, d, tm // d, w), lambda i: (i // tpb, 0, i % tpb, 0))

    return pl.pallas_call(
        _merge_kernel,
        grid=(T // tm,),
        in_specs=[tok(D_MODEL), tok(HEAD_W), tok(HEAD_W), res(1, HEAD_W), res(2, HEAD_W),
                  tok(LANES), res(1, LANES), res(2, LANES),
                  full(g1), full(wbg), full(bbg), full(wpg), full(wpa), full(wout), full(g2),
                  full(wr), full(br), full(tri), full(upper)],
        out_specs=[tok(D_MODEL), pl.BlockSpec((LOCAL_ROWS, PACKED_W), lambda i: (i, 0)), tok(LANES),
                   pl.BlockSpec((8, LANES), lambda i: (i, 0))],
        out_shape=[
            jax.ShapeDtypeStruct((T, D_MODEL), F32),
            jax.ShapeDtypeStruct((T // tm * LOCAL_ROWS, PACKED_W), F32),
            jax.ShapeDtypeStruct((T, LANES), F32),
            jax.ShapeDtypeStruct((T // tm * 8, LANES), F32),
        ],
        scratch_shapes=[pltpu.VMEM((DIL_GROUPS - 1, DIL_HEADS, tm, LANES), F32),
                        pltpu.VMEM((DIL_GROUPS - 1, tm, LANES), F32)],
        compiler_params=_params(("arbitrary",)),
        name="merge",
    )(x2d, o_gla, *outs, *stats, g1, wbg, bbg, wpg, wpa, wout, g2, wr, br, tri, upper)


class _ChunkGather:
    def __init__(self, src_hbm, buf_ref, sem, nchunks):
        self.src, self.buf, self.sem, self.n = src_hbm, buf_ref, sem, nchunks

    def start(self, offs_ref, s):
        for j in range(self.n):
            row = pl.multiple_of(offs_ref[0, 0, j], CHUNK)
            pltpu.make_async_copy(self.src.at[pl.ds(row, CHUNK), :], self.buf.at[s, pl.ds(j * CHUNK, CHUNK), :],
                                  self.sem.at[s]).start()

    def wait(self, s):
        pltpu.make_async_copy(self.src.at[pl.ds(0, self.n * CHUNK), :], self.buf.at[s], self.sem.at[s]).wait()

    def step(self, offs_cur_ref, offs_next_ref):
        i = pl.program_id(0)
        slot = lax.rem(i, 2)

        @pl.when(i == 0)
        def _():
            self.start(offs_cur_ref, 0)

        self.wait(slot)

        @pl.when(i + 1 < pl.num_programs(0))
        def _():
            self.start(offs_next_ref, 1 - slot)

        return slot

    def start_first(self, offs_ref, s, count):
        def body(j, carry):
            row = pl.multiple_of(offs_ref[0, 0, j], CHUNK)
            dst = self.buf.at[s, pl.ds(pl.multiple_of(j * CHUNK, CHUNK), CHUNK), :]
            pltpu.make_async_copy(self.src.at[pl.ds(row, CHUNK), :], dst, self.sem.at[s]).start()
            return carry

        lax.fori_loop(0, count, body, 0)

    def wait_first(self, s, count):
        def body(j, carry):
            pltpu.make_async_copy(self.src.at[pl.ds(0, CHUNK), :], self.buf.at[s, pl.ds(0, CHUNK), :],
                                  self.sem.at[s]).wait()
            return carry

        lax.fori_loop(0, count, body, 0)

    def step_first(self, offs_cur_ref, offs_next_ref, count_cur, count_next):
        i = pl.program_id(0)
        slot = lax.rem(i, 2)

        @pl.when(i == 0)
        def _():
            self.buf[...] = jnp.zeros_like(self.buf)
            self.start_first(offs_cur_ref, 0, count_cur)

        self.wait_first(slot, count_cur)

        @pl.when(i + 1 < pl.num_programs(0))
        def _():
            self.start_first(offs_next_ref, 1 - slot, count_next)

        return slot


def _expert_kernel(tile_e_ref, nact_ref, src_cur_ref, src_next_ref, xloc_ref, wg_ref, wu_ref, wd_ref,
                   y_ref, xbuf_ref, wgu_ref, wdn_ref, gsem):
    i = pl.program_id(0)
    slot = _ChunkGather(xloc_ref, xbuf_ref, gsem, TILE_CHUNKS).step(src_cur_ref, src_next_ref)
    active = i < nact_ref[0]

    @pl.when(jnp.logical_and(active, jnp.logical_or(i == 0, tile_e_ref[i] != tile_e_ref[jnp.maximum(i - 1, 0)])))
    def _():
        wgu_ref[:, :D_EXPERT] = wg_ref[0].astype(BF16)
        wgu_ref[:, D_EXPERT:] = wu_ref[0].astype(BF16)
        wdn_ref[...] = wd_ref[0].astype(BF16)

    @pl.when(active)
    def _():
        xb = _unpack_pairs(xbuf_ref[slot])
        gu = _dot(xb, wgu_ref[...])
        g, u = gu[:, :D_EXPERT], gu[:, D_EXPERT:]
        a = (g * jax.nn.sigmoid(g) * u).astype(BF16)
        y_ref[...] = _pack_pairs(_dot(a, wdn_ref[...]).astype(BF16).astype(F32))

    @pl.when(jnp.logical_not(active))
    def _():
        y_ref[...] = jnp.zeros_like(y_ref)


def _experts(tile_e, nact, csrc, xloc, wg, wu, wd):
    ntiles = tile_e.shape[0]
    R = MOE_TM

    def wsel(i, te, na):
        return (te[jnp.minimum(i, na[0] - 1)], 0, 0)

    def offs(shift):
        return pl.BlockSpec((1, 1, TILE_CHUNKS), lambda i, te, na: (jnp.minimum(i + shift, ntiles - 1), 0, 0),
                            memory_space=pltpu.SMEM)

    return pl.pallas_call(
        _expert_kernel,
        grid_spec=pltpu.PrefetchScalarGridSpec(
            num_scalar_prefetch=2,
            grid=(ntiles,),
            in_specs=[
                offs(0), offs(1),
                pl.BlockSpec(memory_space=pl.ANY),
                pl.BlockSpec((1, D_MODEL, D_EXPERT), wsel),
                pl.BlockSpec((1, D_MODEL, D_EXPERT), wsel),
                pl.BlockSpec((1, D_EXPERT, D_MODEL), wsel),
            ],
            out_specs=pl.BlockSpec((R, PACKED_W), lambda i, te, na: (i, 0)),
            scratch_shapes=[pltpu.VMEM((2, R, PACKED_W), F32), pltpu.VMEM((D_MODEL, 2 * D_EXPERT), BF16),
                            pltpu.VMEM((D_EXPERT, D_MODEL), BF16), pltpu.SemaphoreType.DMA((2,))],
        ),
        out_shape=jax.ShapeDtypeStruct((ntiles * R, PACKED_W), F32),
        compiler_params=_params(("arbitrary",)),
        name="experts",
    )(tile_e, nact, csrc, csrc, xloc, wg, wu, wd)


def _combine_kernel(used_ref, dst_cur_ref, dst_next_ref, x2_ref, route_ref, y_ref, out_ref, ybuf_ref, gsem):
    i = pl.program_id(0)
    nxt = jnp.minimum(i + 1, pl.num_programs(0) - 1)
    slot = _ChunkGather(y_ref, ybuf_ref, gsem, LOCAL_CHUNKS).step_first(
        dst_cur_ref, dst_next_ref, used_ref[i], used_ref[nxt])
    tm = x2_ref.shape[0]
    route = route_ref[...]
    w1, w2, pos1, pos2 = (route[:, c:c + 1] for c in range(2, 6))
    yl = _unpack_pairs(ybuf_ref[slot])
    cols = lax.broadcasted_iota(jnp.int32, (tm, LOCAL_ROWS), 1).astype(F32)
    gate = jnp.where(cols == pos1, w1, jnp.where(cols == pos2, w2, 0.0)).astype(BF16)
    out_ref[...] = x2_ref[...] + _dot(gate, yl)


def _combine(used, cdst, x2, route, y):
    T = x2.shape[0]
    nt = cdst.shape[0]
    tm = T // nt

    def offs(shift):
        return pl.BlockSpec((1, 1, LOCAL_CHUNKS), lambda i, used: (jnp.minimum(i + shift, nt - 1), 0, 0),
                            memory_space=pltpu.SMEM)

    return pl.pallas_call(
        _combine_kernel,
        grid_spec=pltpu.PrefetchScalarGridSpec(
            num_scalar_prefetch=1,
            grid=(nt,),
            in_specs=[
                offs(0), offs(1),
                pl.BlockSpec((tm, D_MODEL), lambda i, used: (i, 0)),
                pl.BlockSpec((tm, LANES), lambda i, used: (i, 0)),
                pl.BlockSpec(memory_space=pl.ANY),
            ],
            out_specs=pl.BlockSpec((tm, D_MODEL), lambda i, used: (i, 0)),
            scratch_shapes=[pltpu.VMEM((2, LOCAL_ROWS, PACKED_W), F32), pltpu.SemaphoreType.DMA((2,))],
        ),
        out_shape=jax.ShapeDtypeStruct((T, D_MODEL), F32),
        compiler_params=_params(("arbitrary",)),
        name="combine",
    )(used, cdst, cdst, x2, route, y)


def _layer(x2d, B, S, norm1_g, w_in, w_gla_a2, b_gla_a, gla_out_norm_g, dil_q_norm_g, dil_k_norm_g,
           w_proj_gla, w_proj_attn, w_branch_gate, b_branch_gate, w_out, norm2_g,
           w_router_group, b_router_group, w_router_expert, b_router_expert, w_gate, w_up, w_down):
    T = B * S
    n_gla = N_GLA_TILES * HEAD_W
    w_gla = w_in[:, :n_gla].astype(BF16)
    w_dil = w_in.astype(BF16)[:, n_gla + GLA_RANK:]
    w_ga = jnp.pad(w_in[:, n_gla:n_gla + GLA_RANK], ((0, 0), (0, LANES - GLA_RANK))).astype(BF16)
    w_a2 = jnp.pad(w_gla_a2, ((0, LANES - GLA_RANK), (0, 0))).astype(BF16)
    gains = [jnp.ones((HEAD_W,), F32)] * N_GLA_TILES
    for gi in range(DIL_GROUPS):
        gains += [jnp.tile(dil_q_norm_g[gi], DIL_HEADS) * (DIL_DH ** -0.5 * LOG2_E),
                  jnp.tile(dil_k_norm_g[gi], DIL_HEADS), jnp.ones((HEAD_W,), F32)]
    qk_gain = jnp.stack(gains).reshape(N_COL_TILES, 1, HEAD_W)

    gla_in, log_a, qkv0, qkv1, qkv2 = _inproj(
        x2d, norm1_g.reshape(1, -1), w_gla, w_dil, w_ga, w_a2, b_gla_a.reshape(1, -1), qk_gain, B, S)

    tri_incl = jnp.asarray(np.tril(np.ones((GLA_BLOCK, GLA_BLOCK), np.float32)), BF16)
    o_gla = _gla(gla_in, log_a, tri_incl, gla_out_norm_g.reshape(1, -1), B, S)

    outs, stats = [], []
    for gi, qkv in enumerate((qkv0.reshape(B, 1, S, QKV_W), qkv1, qkv2)):
        o, st = _dilated(qkv, _alibi_bias(gi), gi)
        outs.append(o)
        stats.append(st)
    outs[0] = outs[0].reshape(T, HEAD_W)
    stats[0] = stats[0].reshape(T, LANES)

    w_r = jnp.pad(jnp.concatenate([w_router_group, w_router_expert], axis=1),
                  ((0, 0), (0, LANES - N_GROUPS - N_EXPERTS)))
    w_rh = w_r.astype(BF16)
    w_rl = (w_r - w_rh.astype(F32)).astype(BF16)
    b_r = jnp.pad(jnp.concatenate([b_router_group, b_router_expert]), (0, LANES - N_GROUPS - N_EXPERTS))
    tm = min(MERGE_TM, S)
    tri_strict = jnp.asarray(np.tril(np.ones((tm, tm), np.float32), -1), BF16)
    upper = jnp.asarray(np.triu(np.ones((LANES, LANES), np.float32), 1), BF16)
    x2, xloc, route, cnt = _merge(
        x2d, o_gla, outs, stats, norm1_g.reshape(1, -1), w_branch_gate.astype(BF16),
        b_branch_gate.reshape(1, -1), w_proj_gla.astype(BF16), w_proj_attn.astype(BF16), w_out.astype(BF16),
        norm2_g.reshape(1, -1), jnp.concatenate([w_rh, w_rl], axis=1), b_r.reshape(1, -1), tri_strict, upper, S)

    csrc, cdst, used, tile_e, nact = _chunk_tables(cnt, T // tm)
    y = _experts(tile_e, nact, csrc, xloc, w_gate, w_up, w_down)
    return _combine(used, cdst, x2, route, y)


def _chunk_tables(cnt, nt):
    i32 = jnp.int32
    experts = jnp.arange(N_EXPERTS, dtype=i32)
    tiles = jnp.arange(nt, dtype=i32)
    c = (cnt.reshape(nt, 8, LANES)[:, 0, :N_EXPERTS].astype(i32) + CHUNK - 1) // CHUNK
    loff = jnp.cumsum(c, axis=1) - c
    ecum = jnp.cumsum(c, axis=0) - c
    tot = jnp.sum(c, axis=0)
    ptot = (tot + TILE_CHUNKS - 1) // TILE_CHUNKS * TILE_CHUNKS
    pend = jnp.cumsum(ptot)
    pstart = pend - ptot

    lc = jnp.arange(LOCAL_CHUNKS, dtype=i32)
    e_of = jnp.sum(((loff + c)[:, None, :] <= lc[None, :, None]).astype(i32), axis=-1)
    pick_e = e_of[:, :, None] == experts[None, None, :]
    glob = jnp.sum(jnp.where(pick_e, (pstart[None, :] + ecum - loff)[:, None, :], 0), axis=-1) + lc[None, :]
    cdst = jnp.where(e_of < N_EXPERTS, glob, 0) * CHUNK

    ntiles = (TOP_K * nt * MERGE_TM // CHUNK + nt * N_EXPERTS + N_EXPERTS * (TILE_CHUNKS - 1)
              + TILE_CHUNKS - 1) // TILE_CHUNKS
    g = jnp.arange(ntiles * TILE_CHUNKS, dtype=i32)
    e_g = jnp.minimum(jnp.sum((pend[None, :] <= g[:, None]).astype(i32), axis=1), N_EXPERTS - 1)
    k = g - jnp.sum(jnp.where(e_g[:, None] == experts[None, :], pstart[None, :], 0), axis=1)
    pick = (e_g[:, None] == experts[None, :]).astype(F32)
    ends = jnp.dot(pick, (ecum + c).T.astype(F32), precision=lax.Precision.HIGHEST)
    t_g = jnp.sum((ends <= k[:, None].astype(F32)).astype(i32), axis=1)
    valid = t_g < nt
    base = jnp.dot(pick, (tiles[:, None] * LOCAL_CHUNKS + loff - ecum).T.astype(F32),
                   precision=lax.Precision.HIGHEST)
    in_tile = tiles[None, :] == jnp.minimum(t_g, nt - 1)[:, None]
    local = jnp.sum(jnp.where(in_tile, base, 0.0), axis=1).astype(i32) + k
    zero_chunk = LOCAL_CHUNKS - 1
    csrc = jnp.where(valid, local, zero_chunk) * CHUNK

    tile_e = e_g[::TILE_CHUNKS]
    nact = (pend[-1:] // TILE_CHUNKS).astype(i32)
    used = jnp.sum(c, axis=1).astype(i32)
    return (csrc.reshape(ntiles, 1, TILE_CHUNKS), cdst.reshape(nt, 1, LOCAL_CHUNKS), used, tile_e, nact)


def kernel(x, norm1_g, w_in, w_gla_a2, b_gla_a, gla_out_norm_g, dil_q_norm_g, dil_k_norm_g, w_proj_gla,
           w_proj_attn, w_branch_gate, b_branch_gate, w_out, norm2_g, w_router_group, b_router_group,
           w_router_expert, b_router_expert, w_gate, w_up, w_down):
    B, S, D = x.shape
    assert D == D_MODEL and S % (DIL_BLOCK * DIL_PATTERNS[-1][1]) == 0
    x2d = x.reshape(B * S, D)
    params = (norm1_g, w_in, w_gla_a2, b_gla_a, gla_out_norm_g, dil_q_norm_g, dil_k_norm_g, w_proj_gla,
              w_proj_attn, w_branch_gate, b_branch_gate, w_out, norm2_g, w_router_group, b_router_group,
              w_router_expert, b_router_expert, w_gate, w_up, w_down)
    for layer in range(norm1_g.shape[0]):
        x2d = _layer(x2d, B, S, *(p[layer] for p in params))
    return x2d.reshape(B, S, D)
```

```python
import functools

import jax
import jax.numpy as jnp
import numpy as np
from jax import lax
from jax.experimental import pallas as pl
from jax.experimental.pallas import tpu as pltpu

F32 = jnp.float32
BF16 = jnp.bfloat16

D_MODEL = 1024
EPS = 1e-6
GLA_HEADS = 4
GLA_DK = 128
GLA_RANK = 16
GLA_TAU = 16.0
LOG2_E = 1.4426950408889634
GLA_SUBCHUNK = 16
GLA_BLOCK = 128
GLA_STEP_BLOCKS = 16
DIL_PATTERNS = ((128, 1), (512, 4), (2048, 16))
DIL_GROUPS = 3
DIL_HEADS = 4
DIL_DH = 128
DIL_BLOCK = 128
ALIBI_SLOPES = tuple(2.0 ** (-8.0 * (i + 1) / (DIL_GROUPS * DIL_HEADS)) for i in range(DIL_GROUPS * DIL_HEADS))
N_GROUPS = 4
EXPERTS_PER_GROUP = 8
N_EXPERTS = N_GROUPS * EXPERTS_PER_GROUP
TOP_K = 2
D_EXPERT = 512

HEAD_W = 512
N_GLA_TILES = 4
N_COL_TILES = N_GLA_TILES + 3 * DIL_GROUPS
QKV_W = 3 * HEAD_W
LANES = 128
NEG = -1e30

INPROJ_TM = 512
MERGE_TM = 512
MOE_TM = 1024
CHUNK = 8
TILE_CHUNKS = MOE_TM // CHUNK
LOCAL_ROWS = 1280
LOCAL_CHUNKS = LOCAL_ROWS // CHUNK
DIL_STEP_BLOCKS = 16
VMEM_LIMIT = 56 * 1024 * 1024


def _dot(a, b):
    return jnp.dot(a, b, preferred_element_type=F32)


def _dot_nt(a, b):
    return lax.dot_general(a, b, (((1,), (1,)), ((), ())), preferred_element_type=F32)


def _dot_tn(a, b):
    return lax.dot_general(a, b, (((0,), (0,)), ((), ())), preferred_element_type=F32)


def _rms(x, g):
    return x * lax.rsqrt(jnp.mean(x * x, axis=-1, keepdims=True) + EPS) * g


def _params(sem):
    return pltpu.CompilerParams(dimension_semantics=sem, vmem_limit_bytes=VMEM_LIMIT)


def _head(c):
    return slice(c * LANES, (c + 1) * LANES)


PACKED_W = D_MODEL // 2
_HI16 = 0xFFFF0000


def _pack_pairs(x):
    u = lax.bitcast_convert_type(x, jnp.uint32)
    w = x.shape[1] // 2
    return lax.bitcast_convert_type((u[:, :w] & jnp.uint32(_HI16)) | (u[:, w:] >> 16), F32)


def _unpack_pairs(p):
    u = lax.bitcast_convert_type(p, jnp.uint32)
    hi = lax.bitcast_convert_type(u & jnp.uint32(_HI16), F32)
    lo = lax.bitcast_convert_type(u << 16, F32)
    return jnp.concatenate([hi, lo], axis=1).astype(BF16)


def _inproj_kernel(x_ref, g1_ref, wgla_ref, wdil_ref, wga_ref, wa2_ref, ba_ref, qkg_ref,
                   gla_ref, loga_ref, d0_ref, d1_ref, d2_ref, h_ref, ybuf_ref):
    tm = x_ref.shape[0]
    hb = _rms(x_ref[...], g1_ref[...]).astype(BF16)
    h_ref[...] = hb
    ga = _dot(hb, wga_ref[...])
    z = _dot(ga.astype(BF16), wa2_ref[...]) + ba_ref[...]
    log_sig = jnp.minimum(z, 0.0) - jnp.log(1.0 + jnp.exp(-jnp.abs(z)))
    loga_ref[...] = log_sig * (LOG2_E / GLA_TAU)

    dil_refs = (d0_ref, d1_ref, d2_ref)
    n_strided = 0
    for jj in range(N_COL_TILES):
        cols = slice(jj * HEAD_W, (jj + 1) * HEAD_W)
        if jj < N_GLA_TILES:
            w = wgla_ref[:, cols]
        else:
            w = wdil_ref[:, (jj - N_GLA_TILES) * HEAD_W:(jj - N_GLA_TILES + 1) * HEAD_W]
        y = _dot(h_ref[...], w)
        if jj < N_GLA_TILES:
            gla_ref[:, cols] = y.astype(BF16)
            continue
        gi, kind = divmod(jj - N_GLA_TILES, 3)
        d = DIL_PATTERNS[gi][1]
        ocols = [slice(kind * HEAD_W + c * LANES, kind * HEAD_W + (c + 1) * LANES) for c in range(DIL_HEADS)]
        if kind < 2:
            g = qkg_ref[jj]
            slabs = [_rms(y[:, _head(c)], g[:, _head(c)]) for c in range(DIL_HEADS)]
        else:
            slabs = [y[:, _head(c)] for c in range(DIL_HEADS)]
        out = dil_refs[gi]
        if d == 1:
            for c in range(DIL_HEADS):
                out[:, ocols[c]] = slabs[c].astype(BF16)
            continue
        buf = n_strided % ybuf_ref.shape[0]
        n_strided += 1
        for c in range(DIL_HEADS):
            ybuf_ref[buf, c] = slabs[c]
        for r in range(d):
            for c in range(DIL_HEADS):
                out[0, r, :, ocols[c]] = ybuf_ref[buf, c, pl.ds(r, tm // d, stride=d), :].astype(BF16)


def _resident(shape):
    return pl.BlockSpec(shape, lambda i: (0,) * len(shape), pipeline_mode=pl.Buffered(1))


def _inproj(x2d, g1, w_gla, w_dil, w_ga, w_a2, b_a, qk_gain, B, S):
    T = x2d.shape[0]
    tm = min(INPROJ_TM, S)
    tpb = S // tm

    def dil_spec(gi):
        d = DIL_PATTERNS[gi][1]
        if d == 1:
            return pl.BlockSpec((tm, QKV_W), lambda i: (i, 0))
        return pl.BlockSpec((1, d, tm // d, QKV_W), lambda i: (i // tpb, 0, i % tpb, 0))

    def dil_shape(gi):
        d = DIL_PATTERNS[gi][1]
        shape = (T, QKV_W) if d == 1 else (B, d, S // d, QKV_W)
        return jax.ShapeDtypeStruct(shape, BF16)

    return pl.pallas_call(
        _inproj_kernel,
        grid=(T // tm,),
        in_specs=[
            pl.BlockSpec((tm, D_MODEL), lambda i: (i, 0)),
            _resident(g1.shape), _resident(w_gla.shape), _resident(w_dil.shape), _resident(w_ga.shape),
            _resident(w_a2.shape),
            _resident(b_a.shape), _resident(qk_gain.shape),
        ],
        out_specs=[
            pl.BlockSpec((tm, N_GLA_TILES * HEAD_W), lambda i: (i, 0)),
            pl.BlockSpec((tm, HEAD_W), lambda i: (i, 0)),
            dil_spec(0), dil_spec(1), dil_spec(2),
        ],
        out_shape=[
            jax.ShapeDtypeStruct((T, N_GLA_TILES * HEAD_W), BF16),
            jax.ShapeDtypeStruct((T, HEAD_W), F32),
            dil_shape(0), dil_shape(1), dil_shape(2),
        ],
        scratch_shapes=[pltpu.VMEM((tm, D_MODEL), BF16), pltpu.VMEM((2, DIL_HEADS, tm, LANES), F32)],
        compiler_params=_params(("arbitrary",)),
        name="inproj",
    )(x2d, g1, w_gla, w_dil, w_ga, w_a2, b_a, qk_gain)


def _gla_kernel(q_ref, k_ref, v_ref, r_ref, la_ref, tri_ref, gn_ref, o_ref, st_ref):
    n = pl.program_id(1)

    @pl.when(n == 0)
    def _():
        st_ref[...] = jnp.zeros_like(st_ref)

    for blk in range(q_ref.shape[0] // GLA_BLOCK):
        _gla_block(q_ref, k_ref, v_ref, r_ref, la_ref, tri_ref, gn_ref, o_ref, st_ref,
                   slice(blk * GLA_BLOCK, (blk + 1) * GLA_BLOCK))


def _gla_block(q_ref, k_ref, v_ref, r_ref, la_ref, tri_ref, gn_ref, o_ref, st_ref, rows):
    L = GLA_BLOCK
    la = la_ref[rows, :]
    hi = la.astype(BF16)
    r1 = la - hi.astype(F32)
    mid = r1.astype(BF16)
    lo = (r1 - mid.astype(F32)).astype(BF16)
    tri = tri_ref[...]
    b_all = _dot(tri, hi) + _dot(tri, mid) + _dot(tri, lo)

    row = lax.broadcasted_iota(jnp.int32, (L, L), 0)
    col = lax.broadcasted_iota(jnp.int32, (L, L), 1)

    for h in range(GLA_HEADS):
        sl = _head(h)
        b = b_all[:, sl]
        q = q_ref[rows, sl].astype(F32) * (GLA_DK ** -0.5)
        k = k_ref[rows, sl].astype(F32)
        v = v_ref[rows, sl]

        att = jnp.zeros((L, L), F32)
        seg = L
        while seg > GLA_SUBCHUNK:
            half = seg // 2
            beta = jnp.concatenate(
                [jnp.broadcast_to(b[s + half - 1:s + half, :], (seg, GLA_DK)) for s in range(0, L, seg)], axis=0)
            rel = b - beta
            neg = jnp.minimum(rel, 0.0)
            qs = (q * jnp.exp2(neg)).astype(BF16)
            ks = (k * jnp.exp2(neg - rel)).astype(BF16)
            same = (row // seg) == (col // seg)
            mask = same & ((row % seg) >= half) & ((col % seg) < half)
            att = jnp.where(mask, _dot_nt(qs, ks), att)
            seg = half
        beta = jnp.concatenate(
            [jnp.zeros((GLA_SUBCHUNK, GLA_DK), F32)]
            + [jnp.broadcast_to(b[s - 1:s, :], (GLA_SUBCHUNK, GLA_DK)) for s in range(GLA_SUBCHUNK, L, GLA_SUBCHUNK)],
            axis=0)
        qs = (q * jnp.exp2(b - beta)).astype(BF16)
        ks = (k * jnp.exp2(beta - b)).astype(BF16)
        mask = ((row // GLA_SUBCHUNK) == (col // GLA_SUBCHUNK)) & (col <= row)
        att = jnp.where(mask, _dot_nt(qs, ks), att)

        state_t = st_ref[h]
        q0 = (q * jnp.exp2(b)).astype(BF16)
        o = _dot(att.astype(BF16), v) + _dot_nt(q0, state_t.astype(BF16))

        b_last = b[L - 1:L, :]
        k_end = (k * jnp.exp2(b_last - b)).astype(BF16)
        st_ref[h] = state_t * jnp.exp2(b_last) + _dot_tn(v, k_end)

        rr = r_ref[rows, sl].astype(F32)
        o_ref[rows, sl] = (_rms(o, gn_ref[...]) * (rr * jax.nn.sigmoid(rr))).astype(BF16)


def _gla(proj, log_a, tri, gn, B, S):
    L = GLA_STEP_BLOCKS * GLA_BLOCK
    nb = S // L
    return pl.pallas_call(
        _gla_kernel,
        grid=(B, nb),
        in_specs=[
            pl.BlockSpec((L, HEAD_W), lambda b, n: (b * nb + n, 0)),
            pl.BlockSpec((L, HEAD_W), lambda b, n: (b * nb + n, 1)),
            pl.BlockSpec((L, HEAD_W), lambda b, n: (b * nb + n, 2)),
            pl.BlockSpec((L, HEAD_W), lambda b, n: (b * nb + n, 3)),
            pl.BlockSpec((L, HEAD_W), lambda b, n: (b * nb + n, 0)),
            pl.BlockSpec((GLA_BLOCK, GLA_BLOCK), lambda b, n: (0, 0)),
            pl.BlockSpec((1, GLA_DK), lambda b, n: (0, 0)),
        ],
        out_specs=pl.BlockSpec((L, HEAD_W), lambda b, n: (b * nb + n, 0)),
        out_shape=jax.ShapeDtypeStruct((B * S, HEAD_W), BF16),
        scratch_shapes=[pltpu.VMEM((GLA_HEADS, GLA_DK, GLA_DK), F32)],
        compiler_params=_params(("arbitrary", "arbitrary")),
        name="gla",
    )(proj, proj, proj, proj, log_a, tri, gn)


def _dil_kernel(q_ref, kp_ref, kc_ref, vp_ref, vc_ref, bias_ref, o_ref, st_ref, *, nr, nq):
    first = pl.program_id(2) == 0
    Q = DIL_BLOCK
    col = lax.broadcasted_iota(jnp.int32, (Q, 2 * Q), 1)
    lane = lax.broadcasted_iota(jnp.int32, (Q, LANES), 1)
    for res in range(nr):
        for blk in range(nq):
            rows = slice(blk * Q, (blk + 1) * Q)
            prev = slice((blk - 1) * Q, blk * Q)
            stats = jnp.zeros((Q, LANES), F32)
            for h in range(DIL_HEADS):
                sl = _head(h)
                if blk == 0:
                    kprev, vprev = kp_ref[0, res, :, sl], vp_ref[0, res, :, sl]
                else:
                    kprev, vprev = kc_ref[0, res, prev, sl], vc_ref[0, res, prev, sl]
                kk = jnp.concatenate([kprev, kc_ref[0, res, rows, sl]], axis=0)
                vv = jnp.concatenate([vprev, vc_ref[0, res, rows, sl]], axis=0)
                s = _dot_nt(q_ref[0, res, rows, sl], kk) + bias_ref[h]
                if blk == 0:
                    s = jnp.where(jnp.logical_and(first, col < Q), NEG, s)
                m = jnp.max(s, axis=-1, keepdims=True)
                p = jnp.exp2(s - m)
                den = jnp.sum(p, axis=-1, keepdims=True)
                o_ref[0, res, rows, sl] = (_dot(p.astype(BF16), vv) / den).astype(BF16)
                stats = jnp.where((lane // 16) == h, m, stats)
                stats = jnp.where((lane // 16) == DIL_HEADS + h, den, stats)
            st_ref[0, res, rows, :] = stats


def _dilated(qkv, bias, gi):
    B, d, Lsub, _ = qkv.shape
    nq = min(DIL_STEP_BLOCKS, Lsub // DIL_BLOCK)
    nr = min(DIL_STEP_BLOCKS // nq, d)
    rows = nq * DIL_BLOCK

    def cur(c):
        return pl.BlockSpec((1, nr, rows, HEAD_W), lambda b, r, i: (b, r, i, c))

    def prev(c):
        return pl.BlockSpec((1, nr, DIL_BLOCK, HEAD_W), lambda b, r, i: (b, r, jnp.maximum(i * nq - 1, 0), c))

    return pl.pallas_call(
        functools.partial(_dil_kernel, nr=nr, nq=nq),
        grid=(B, d // nr, Lsub // rows),
        in_specs=[cur(0), prev(1), cur(1), prev(2), cur(2),
                  pl.BlockSpec((DIL_HEADS, DIL_BLOCK, 2 * DIL_BLOCK), lambda b, r, i: (0, 0, 0))],
        out_specs=[
            pl.BlockSpec((1, nr, rows, HEAD_W), lambda b, r, i: (b, r, i, 0)),
            pl.BlockSpec((1, nr, rows, LANES), lambda b, r, i: (b, r, i, 0)),
        ],
        out_shape=[
            jax.ShapeDtypeStruct((B, d, Lsub, HEAD_W), BF16),
            jax.ShapeDtypeStruct((B, d, Lsub, LANES), F32),
        ],
        compiler_params=_params(("arbitrary", "arbitrary", "arbitrary")),
        name=f"dilated{gi}",
    )(qkv, qkv, qkv, qkv, qkv, bias)


def _alibi_bias(gi):
    window, d = DIL_PATTERNS[gi]
    Q = DIL_BLOCK
    dist = np.arange(Q)[:, None] + Q - np.arange(2 * Q)[None, :]
    valid = (dist >= 0) & (dist <= window // d)
    slopes = np.asarray(ALIBI_SLOPES, np.float32).reshape(DIL_GROUPS, DIL_HEADS)[gi]
    bias = -slopes[:, None, None] * (d * dist).astype(np.float32) * np.float32(LOG2_E)
    return jnp.asarray(np.where(valid[None], bias, np.float32(NEG)), F32)


def _merge_rows(x_ref, og_ref, o0_ref, s0_ref, g1_ref, wbg_ref, bbg_ref, wpg_ref, wpa_ref, wout_ref,
                g2_ref, wr_ref, br_ref, x2_ref, obuf_ref, sbuf_ref, row0, n):
    rows = slice(row0, row0 + n)
    x = x_ref[rows, :]
    hb = _rms(x, g1_ref[...]).astype(BF16)
    gates = jax.nn.sigmoid(_dot(hb, wbg_ref[...]) + bbg_ref[...])

    stats = (s0_ref[rows, :], sbuf_ref[0, rows, :], sbuf_ref[1, rows, :])
    dens = [pltpu.roll(s, 64, 1) for s in stats]
    m_all = jnp.maximum(jnp.maximum(stats[0], stats[1]), stats[2])
    wts = [d * jnp.exp2(s - m_all) for s, d in zip(stats, dens)]
    inv = 1.0 / (wts[0] + wts[1] + wts[2])
    coef = [w * inv for w in wts]
    heads = []
    for h in range(DIL_HEADS):
        group_out = (o0_ref[rows, _head(h)].astype(F32), obuf_ref[0, h, rows, :], obuf_ref[1, h, rows, :])
        acc = jnp.zeros((n, DIL_DH), F32)
        for g in range(DIL_GROUPS):
            c = jnp.broadcast_to(coef[g][:, 16 * h:16 * h + 1], (n, DIL_DH))
            acc = acc + c * group_out[g]
        heads.append(acc.astype(BF16))
    o_att = jnp.concatenate(heads, axis=1)

    y = gates[:, :D_MODEL] * _dot(og_ref[rows, :], wpg_ref[...]) + gates[:, D_MODEL:] * _dot(o_att, wpa_ref[...])
    x2 = x + _dot(y.astype(BF16), wout_ref[...])
    x2_ref[rows, :] = x2

    h2 = _rms(x2, g2_ref[...])

    h2h = h2.astype(BF16)
    h2l = (h2 - h2h.astype(F32)).astype(BF16)
    wr = wr_ref[...]
    pa = _dot(h2h, wr)
    pb = _dot(h2l, wr)
    logit = pa[:, :LANES] + pa[:, LANES:] + pb[:, :LANES] + pb[:, LANES:] + br_ref[...]

    lane = lax.broadcasted_iota(jnp.int32, (n, LANES), 1).astype(F32)
    big = jnp.float32(1e9)
    gl = jnp.where(lane < N_GROUPS, logit, NEG)
    gmax = jnp.max(gl, axis=-1, keepdims=True)
    gsel = jnp.min(jnp.where(gl == gmax, lane, big), axis=-1, keepdims=True)
    g_p = 1.0 / jnp.sum(jnp.exp(gl - gmax), axis=-1, keepdims=True)
    lo = N_GROUPS + EXPERTS_PER_GROUP * gsel
    el = jnp.where((lane >= lo) & (lane < lo + EXPERTS_PER_GROUP), logit, NEG)
    v1 = jnp.max(el, axis=-1, keepdims=True)
    i1 = jnp.min(jnp.where(el == v1, lane, big), axis=-1, keepdims=True)
    el2 = jnp.where(lane == i1, NEG, el)
    v2 = jnp.max(el2, axis=-1, keepdims=True)
    i2 = jnp.min(jnp.where(el2 == v2, lane, big), axis=-1, keepdims=True)
    ex = jnp.exp(v2 - v1)
    w1 = g_p / (1.0 + ex)
    w2 = g_p * ex / (1.0 + ex)
    return i1 - N_GROUPS, i2 - N_GROUPS, w1, w2, h2h


def _merge_kernel(x_ref, og_ref, o0_ref, o1_ref, o2_ref, s0_ref, s1_ref, s2_ref, g1_ref, wbg_ref, bbg_ref,
                  wpg_ref, wpa_ref, wout_ref, g2_ref, wr_ref, br_ref, tri_ref, upper_ref,
                  x2_ref, xloc_ref, route_ref, cnt_ref, obuf_ref, sbuf_ref):
    tm = x_ref.shape[0]

    for slot, (o_ref, s_ref) in enumerate(((o1_ref, s1_ref), (o2_ref, s2_ref))):
        d = DIL_PATTERNS[slot + 1][1]
        for r in range(d):
            sbuf_ref[slot, pl.ds(r, tm // d, stride=d), :] = s_ref[0, r]
            for c in range(DIL_HEADS):
                obuf_ref[slot, c, pl.ds(r, tm // d, stride=d), :] = o_ref[0, r, :, _head(c)].astype(F32)

    e1, e2, w1, w2, h2b = _merge_rows(x_ref, og_ref, o0_ref, s0_ref, g1_ref, wbg_ref, bbg_ref, wpg_ref, wpa_ref,
                                      wout_ref, g2_ref, wr_ref, br_ref, x2_ref, obuf_ref, sbuf_ref, 0, tm)

    lane = lax.broadcasted_iota(jnp.int32, (tm, LANES), 1).astype(F32)
    oh1 = lane == e1
    oh2 = lane == e2
    onehot = jnp.where(oh1 | oh2, 1.0, 0.0)
    prefix = _dot(tri_ref[...], onehot.astype(BF16))
    cnt = jnp.sum(onehot, axis=0, keepdims=True)
    cnt_pad = jnp.floor((cnt + (CHUNK - 1)) * (1.0 / CHUNK)) * CHUNK
    seg_off = _dot(jnp.broadcast_to(cnt_pad, (8, LANES)).astype(BF16), upper_ref[...])[0:1, :]
    base = seg_off + prefix
    pos1 = jnp.sum(jnp.where(oh1, base, 0.0), axis=-1, keepdims=True)
    pos2 = jnp.sum(jnp.where(oh2, base, 0.0), axis=-1, keepdims=True)
    cnt_ref[...] = jnp.broadcast_to(cnt, cnt_ref.shape)

    route = jnp.zeros((tm, LANES), F32)
    for idx, val in enumerate((e1, e2, w1, w2, pos1, pos2)):
        route = jnp.where(lane == idx, val, route)
    route_ref[...] = route

    route_t = route.T
    slot = lax.broadcasted_iota(jnp.int32, (LOCAL_ROWS, tm), 0).astype(F32)
    perm = jnp.where((slot == route_t[4:5, :]) | (slot == route_t[5:6, :]), 1.0, 0.0).astype(BF16)
    xloc_ref[...] = _pack_pairs(_dot(perm, h2b))


def _merge(x2d, o_gla, outs, stats, g1, wbg, bbg, wpg, wpa, wout, g2, wr, br, tri, upper, S):
    T = x2d.shape[0]
    tm = min(MERGE_TM, S)
    assert TOP_K * tm + N_EXPERTS * CHUNK <= LOCAL_ROWS
    tpb = S // tm
    tok = lambda w: pl.BlockSpec((tm, w), lambda i: (i, 0))
    full = lambda a: pl.BlockSpec(a.shape, lambda i: (0,) * a.ndim)

    def res(gi, w):
        d = DIL_PATTERNS[gi][1]
        return pl.BlockSpec((1, d, tm // d, w), lambda i: (i // tpb, 0, i % tpb, 0))

    return pl.pallas_call(
        _merge_kernel,
        grid=(T // tm,),
        in_specs=[tok(D_MODEL), tok(HEAD_W), tok(HEAD_W), res(1, HEAD_W), res(2, HEAD_W),
                  tok(LANES), res(1, LANES), res(2, LANES),
                  full(g1), full(wbg), full(bbg), full(wpg), full(wpa), full(wout), full(g2),
                  full(wr), full(br), full(tri), full(upper)],
        out_specs=[tok(D_MODEL), pl.BlockSpec((LOCAL_ROWS, PACKED_W), lambda i: (i, 0)), tok(LANES),
                   pl.BlockSpec((8, LANES), lambda i: (i, 0))],
        out_shape=[
            jax.ShapeDtypeStruct((T, D_MODEL), F32),
            jax.ShapeDtypeStruct((T // tm * LOCAL_ROWS, PACKED_W), F32),
            jax.ShapeDtypeStruct((T, LANES), F32),
            jax.ShapeDtypeStruct((T // tm * 8, LANES), F32),
        ],
        scratch_shapes=[pltpu.VMEM((DIL_GROUPS - 1, DIL_HEADS, tm, LANES), F32),
                        pltpu.VMEM((DIL_GROUPS - 1, tm, LANES), F32)],
        compiler_params=_params(("arbitrary",)),
        name="merge",
    )(x2d, o_gla, *outs, *stats, g1, wbg, bbg, wpg, wpa, wout, g2, wr, br, tri, upper)


class _ChunkGather:
    def __init__(self, src_hbm, buf_ref, sem, nchunks):
        self.src, self.buf, self.sem, self.n = src_hbm, buf_ref, sem, nchunks

    def start(self, offs_ref, s):
        for j in range(self.n):
            row = pl.multiple_of(offs_ref[0, 0, j], CHUNK)
            pltpu.make_async_copy(self.src.at[pl.ds(row, CHUNK), :], self.buf.at[s, pl.ds(j * CHUNK, CHUNK), :],
                                  self.sem.at[s]).start()

    def wait(self, s):
        pltpu.make_async_copy(self.src.at[pl.ds(0, self.n * CHUNK), :], self.buf.at[s], self.sem.at[s]).wait()

    def step(self, offs_cur_ref, offs_next_ref):
        i = pl.program_id(0)
        slot = lax.rem(i, 2)

        @pl.when(i == 0)
        def _():
            self.start(offs_cur_ref, 0)

        self.wait(slot)

        @pl.when(i + 1 < pl.num_programs(0))
        def _():
            self.start(offs_next_ref, 1 - slot)

        return slot


def _expert_kernel(tile_e_ref, nact_ref, src_cur_ref, src_next_ref, xloc_ref, wg_ref, wu_ref, wd_ref,
                   y_ref, xbuf_ref, wgu_ref, wdn_ref, gsem):
    i = pl.program_id(0)
    slot = _ChunkGather(xloc_ref, xbuf_ref, gsem, TILE_CHUNKS).step(src_cur_ref, src_next_ref)
    active = i < nact_ref[0]

    @pl.when(jnp.logical_and(active, jnp.logical_or(i == 0, tile_e_ref[i] != tile_e_ref[jnp.maximum(i - 1, 0)])))
    def _():
        wgu_ref[:, :D_EXPERT] = wg_ref[0].astype(BF16)
        wgu_ref[:, D_EXPERT:] = wu_ref[0].astype(BF16)
        wdn_ref[...] = wd_ref[0].astype(BF16)

    @pl.when(active)
    def _():
        xb = _unpack_pairs(xbuf_ref[slot])
        gu = _dot(xb, wgu_ref[...])
        g, u = gu[:, :D_EXPERT], gu[:, D_EXPERT:]
        a = (g * jax.nn.sigmoid(g) * u).astype(BF16)
        y_ref[...] = _pack_pairs(_dot(a, wdn_ref[...]).astype(BF16).astype(F32))

    @pl.when(jnp.logical_not(active))
    def _():
        y_ref[...] = jnp.zeros_like(y_ref)


def _experts(tile_e, nact, csrc, xloc, wg, wu, wd):
    ntiles = tile_e.shape[0]
    R = MOE_TM

    def wsel(i, te, na):
        return (te[jnp.minimum(i, na[0] - 1)], 0, 0)

    def offs(shift):
        return pl.BlockSpec((1, 1, TILE_CHUNKS), lambda i, te, na: (jnp.minimum(i + shift, ntiles - 1), 0, 0),
                            memory_space=pltpu.SMEM)

    return pl.pallas_call(
        _expert_kernel,
        grid_spec=pltpu.PrefetchScalarGridSpec(
            num_scalar_prefetch=2,
            grid=(ntiles,),
            in_specs=[
                offs(0), offs(1),
                pl.BlockSpec(memory_space=pl.ANY),
                pl.BlockSpec((1, D_MODEL, D_EXPERT), wsel),
                pl.BlockSpec((1, D_MODEL, D_EXPERT), wsel),
                pl.BlockSpec((1, D_EXPERT, D_MODEL), wsel),
            ],
            out_specs=pl.BlockSpec((R, PACKED_W), lambda i, te, na: (i, 0)),
            scratch_shapes=[pltpu.VMEM((2, R, PACKED_W), F32), pltpu.VMEM((D_MODEL, 2 * D_EXPERT), BF16),
                            pltpu.VMEM((D_EXPERT, D_MODEL), BF16), pltpu.SemaphoreType.DMA((2,))],
        ),
        out_shape=jax.ShapeDtypeStruct((ntiles * R, PACKED_W), F32),
        compiler_params=_params(("arbitrary",)),
        name="experts",
    )(tile_e, nact, csrc, csrc, xloc, wg, wu, wd)


def _combine_kernel(dst_cur_ref, dst_next_ref, x2_ref, route_ref, y_ref, out_ref, ybuf_ref, gsem):
    slot = _ChunkGather(y_ref, ybuf_ref, gsem, LOCAL_CHUNKS).step(dst_cur_ref, dst_next_ref)
    tm = x2_ref.shape[0]
    route = route_ref[...]
    w1, w2, pos1, pos2 = (route[:, c:c + 1] for c in range(2, 6))
    yl = _unpack_pairs(ybuf_ref[slot])
    cols = lax.broadcasted_iota(jnp.int32, (tm, LOCAL_ROWS), 1).astype(F32)
    pick1 = jnp.where(cols == pos1, 1.0, 0.0).astype(BF16)
    pick2 = jnp.where(cols == pos2, 1.0, 0.0).astype(BF16)
    out_ref[...] = x2_ref[...] + w1 * _dot(pick1, yl) + w2 * _dot(pick2, yl)


def _combine(cdst, x2, route, y):
    T = x2.shape[0]
    nt = cdst.shape[0]
    tm = T // nt

    def offs(shift):
        return pl.BlockSpec((1, 1, LOCAL_CHUNKS), lambda i: (jnp.minimum(i + shift, nt - 1), 0, 0),
                            memory_space=pltpu.SMEM)

    return pl.pallas_call(
        _combine_kernel,
        grid=(nt,),
        in_specs=[
            offs(0), offs(1),
            pl.BlockSpec((tm, D_MODEL), lambda i: (i, 0)),
            pl.BlockSpec((tm, LANES), lambda i: (i, 0)),
            pl.BlockSpec(memory_space=pl.ANY),
        ],
        out_specs=pl.BlockSpec((tm, D_MODEL), lambda i: (i, 0)),
        out_shape=jax.ShapeDtypeStruct((T, D_MODEL), F32),
        scratch_shapes=[pltpu.VMEM((2, LOCAL_ROWS, PACKED_W), F32), pltpu.SemaphoreType.DMA((2,))],
        compiler_params=_params(("arbitrary",)),
        name="combine",
    )(cdst, cdst, x2, route, y)


def _layer(x2d, B, S, norm1_g, w_in, w_gla_a2, b_gla_a, gla_out_norm_g, dil_q_norm_g, dil_k_norm_g,
           w_proj_gla, w_proj_attn, w_branch_gate, b_branch_gate, w_out, norm2_g,
           w_router_group, b_router_group, w_router_expert, b_router_expert, w_gate, w_up, w_down):
    T = B * S
    n_gla = N_GLA_TILES * HEAD_W
    w_gla = w_in[:, :n_gla].astype(BF16)
    w_dil = w_in.astype(BF16)[:, n_gla + GLA_RANK:]
    w_ga = jnp.pad(w_in[:, n_gla:n_gla + GLA_RANK], ((0, 0), (0, LANES - GLA_RANK))).astype(BF16)
    w_a2 = jnp.pad(w_gla_a2, ((0, LANES - GLA_RANK), (0, 0))).astype(BF16)
    gains = [jnp.ones((HEAD_W,), F32)] * N_GLA_TILES
    for gi in range(DIL_GROUPS):
        gains += [jnp.tile(dil_q_norm_g[gi], DIL_HEADS) * (DIL_DH ** -0.5 * LOG2_E),
                  jnp.tile(dil_k_norm_g[gi], DIL_HEADS), jnp.ones((HEAD_W,), F32)]
    qk_gain = jnp.stack(gains).reshape(N_COL_TILES, 1, HEAD_W)

    gla_in, log_a, qkv0, qkv1, qkv2 = _inproj(
        x2d, norm1_g.reshape(1, -1), w_gla, w_dil, w_ga, w_a2, b_gla_a.reshape(1, -1), qk_gain, B, S)

    tri_incl = jnp.asarray(np.tril(np.ones((GLA_BLOCK, GLA_BLOCK), np.float32)), BF16)
    o_gla = _gla(gla_in, log_a, tri_incl, gla_out_norm_g.reshape(1, -1), B, S)

    outs, stats = [], []
    for gi, qkv in enumerate((qkv0.reshape(B, 1, S, QKV_W), qkv1, qkv2)):
        o, st = _dilated(qkv, _alibi_bias(gi), gi)
        outs.append(o)
        stats.append(st)
    outs[0] = outs[0].reshape(T, HEAD_W)
    stats[0] = stats[0].reshape(T, LANES)

    w_r = jnp.pad(jnp.concatenate([w_router_group, w_router_expert], axis=1),
                  ((0, 0), (0, LANES - N_GROUPS - N_EXPERTS)))
    w_rh = w_r.astype(BF16)
    w_rl = (w_r - w_rh.astype(F32)).astype(BF16)
    b_r = jnp.pad(jnp.concatenate([b_router_group, b_router_expert]), (0, LANES - N_GROUPS - N_EXPERTS))
    tm = min(MERGE_TM, S)
    tri_strict = jnp.asarray(np.tril(np.ones((tm, tm), np.float32), -1), BF16)
    upper = jnp.asarray(np.triu(np.ones((LANES, LANES), np.float32), 1), BF16)
    x2, xloc, route, cnt = _merge(
        x2d, o_gla, outs, stats, norm1_g.reshape(1, -1), w_branch_gate.astype(BF16),
        b_branch_gate.reshape(1, -1), w_proj_gla.astype(BF16), w_proj_attn.astype(BF16), w_out.astype(BF16),
        norm2_g.reshape(1, -1), jnp.concatenate([w_rh, w_rl], axis=1), b_r.reshape(1, -1), tri_strict, upper, S)

    csrc, cdst, tile_e, nact = _chunk_tables(cnt, T // tm)
    y = _experts(tile_e, nact, csrc, xloc, w_gate, w_up, w_down)
    return _combine(cdst, x2, route, y)


def _chunk_tables(cnt, nt):
    i32 = jnp.int32
    experts = jnp.arange(N_EXPERTS, dtype=i32)
    tiles = jnp.arange(nt, dtype=i32)
    c = (cnt.reshape(nt, 8, LANES)[:, 0, :N_EXPERTS].astype(i32) + CHUNK - 1) // CHUNK
    loff = jnp.cumsum(c, axis=1) - c
    ecum = jnp.cumsum(c, axis=0) - c
    tot = jnp.sum(c, axis=0)
    ptot = (tot + TILE_CHUNKS - 1) // TILE_CHUNKS * TILE_CHUNKS
    pend = jnp.cumsum(ptot)
    pstart = pend - ptot

    lc = jnp.arange(LOCAL_CHUNKS, dtype=i32)
    e_of = jnp.sum(((loff + c)[:, None, :] <= lc[None, :, None]).astype(i32), axis=-1)
    pick_e = e_of[:, :, None] == experts[None, None, :]
    glob = jnp.sum(jnp.where(pick_e, (pstart[None, :] + ecum - loff)[:, None, :], 0), axis=-1) + lc[None, :]
    cdst = jnp.where(e_of < N_EXPERTS, glob, 0) * CHUNK

    ntiles = (TOP_K * nt * MERGE_TM // CHUNK + nt * N_EXPERTS + N_EXPERTS * (TILE_CHUNKS - 1)
              + TILE_CHUNKS - 1) // TILE_CHUNKS
    g = jnp.arange(ntiles * TILE_CHUNKS, dtype=i32)
    e_g = jnp.minimum(jnp.sum((pend[None, :] <= g[:, None]).astype(i32), axis=1), N_EXPERTS - 1)
    k = g - jnp.sum(jnp.where(e_g[:, None] == experts[None, :], pstart[None, :], 0), axis=1)
    pick = (e_g[:, None] == experts[None, :]).astype(F32)
    ends = jnp.dot(pick, (ecum + c).T.astype(F32), precision=lax.Precision.HIGHEST)
    t_g = jnp.sum((ends <= k[:, None].astype(F32)).astype(i32), axis=1)
    valid = t_g < nt
    base = jnp.dot(pick, (tiles[:, None] * LOCAL_CHUNKS + loff - ecum).T.astype(F32),
                   precision=lax.Precision.HIGHEST)
    in_tile = tiles[None, :] == jnp.minimum(t_g, nt - 1)[:, None]
    local = jnp.sum(jnp.where(in_tile, base, 0.0), axis=1).astype(i32) + k
    zero_chunk = LOCAL_CHUNKS - 1
    csrc = jnp.where(valid, local, zero_chunk) * CHUNK

    tile_e = e_g[::TILE_CHUNKS]
    nact = (pend[-1:] // TILE_CHUNKS).astype(i32)
    return (csrc.reshape(ntiles, 1, TILE_CHUNKS), cdst.reshape(nt, 1, LOCAL_CHUNKS), tile_e, nact)


def kernel(x, norm1_g, w_in, w_gla_a2, b_gla_a, gla_out_norm_g, dil_q_norm_g, dil_k_norm_g, w_proj_gla,
           w_proj_attn, w_branch_gate, b_branch_gate, w_out, norm2_g, w_router_group, b_router_group,
           w_router_expert, b_router_expert, w_gate, w_up, w_down):
    B, S, D = x.shape
    assert D == D_MODEL and S % (DIL_BLOCK * DIL_PATTERNS[-1][1]) == 0
    x2d = x.reshape(B * S, D)
    params = (norm1_g, w_in, w_gla_a2, b_gla_a, gla_out_norm_g, dil_q_norm_g, dil_k_norm_g, w_proj_gla,
              w_proj_attn, w_branch_gate, b_branch_gate, w_out, norm2_g, w_router_group, b_router_group,
              w_router_expert, b_router_expert, w_gate, w_up, w_down)
    for layer in range(norm1_g.shape[0]):
        x2d = _layer(x2d, B, S, *(p[layer] for p in params))
    return x2d.reshape(B, S, D)
```

```python
import functools

import jax
import jax.numpy as jnp
import numpy as np
from jax import lax
from jax.experimental import pallas as pl
from jax.experimental.pallas import tpu as pltpu

F32 = jnp.float32
BF16 = jnp.bfloat16

D_MODEL = 1024
EPS = 1e-6
GLA_HEADS = 4
GLA_DK = 128
GLA_RANK = 16
GLA_TAU = 16.0
LOG2_E = 1.4426950408889634
GLA_SUBCHUNK = 16
GLA_BLOCK = 128
GLA_STEP_BLOCKS = 16
DIL_PATTERNS = ((128, 1), (512, 4), (2048, 16))
DIL_GROUPS = 3
DIL_HEADS = 4
DIL_DH = 128
DIL_BLOCK = 128
ALIBI_SLOPES = tuple(2.0 ** (-8.0 * (i + 1) / (DIL_GROUPS * DIL_HEADS)) for i in range(DIL_GROUPS * DIL_HEADS))
N_GROUPS = 4
EXPERTS_PER_GROUP = 8
N_EXPERTS = N_GROUPS * EXPERTS_PER_GROUP
TOP_K = 2
D_EXPERT = 512

HEAD_W = 512
N_GLA_TILES = 4
N_COL_TILES = N_GLA_TILES + 3 * DIL_GROUPS
QKV_W = 3 * HEAD_W
LANES = 128
NEG = -1e30

INPROJ_TM = 512
MERGE_TM = 512
MOE_TM = 512
CHUNK = 8
TILE_CHUNKS = MOE_TM // CHUNK
LOCAL_ROWS = -(-(TOP_K * MERGE_TM + N_EXPERTS * CHUNK) // LANES) * LANES
LOCAL_CHUNKS = LOCAL_ROWS // CHUNK
DIL_STEP_BLOCKS = 16
VMEM_LIMIT = 56 * 1024 * 1024


def _dot(a, b):
    return jnp.dot(a, b, preferred_element_type=F32)


def _dot_nt(a, b):
    return lax.dot_general(a, b, (((1,), (1,)), ((), ())), preferred_element_type=F32)


def _dot_tn(a, b):
    return lax.dot_general(a, b, (((0,), (0,)), ((), ())), preferred_element_type=F32)


def _rms(x, g):
    return x * lax.rsqrt(jnp.mean(x * x, axis=-1, keepdims=True) + EPS) * g


def _params(sem):
    return pltpu.CompilerParams(dimension_semantics=sem, vmem_limit_bytes=VMEM_LIMIT)


def _head(c):
    return slice(c * LANES, (c + 1) * LANES)


PACKED_W = D_MODEL // 2
_HI16 = 0xFFFF0000


def _pack_pairs(x):
    u = lax.bitcast_convert_type(x, jnp.uint32)
    w = x.shape[1] // 2
    return lax.bitcast_convert_type((u[:, :w] & jnp.uint32(_HI16)) | (u[:, w:] >> 16), F32)


def _unpack_pairs(p):
    u = lax.bitcast_convert_type(p, jnp.uint32)
    hi = lax.bitcast_convert_type(u & jnp.uint32(_HI16), F32)
    lo = lax.bitcast_convert_type(u << 16, F32)
    return jnp.concatenate([hi, lo], axis=1).astype(BF16)


def _inproj_kernel(x_ref, g1_ref, wgla_ref, wdil_ref, wga_ref, wa2_ref, ba_ref, qkg_ref,
                   gla_ref, loga_ref, d0_ref, d1_ref, d2_ref, h_ref, ybuf_ref):
    tm = x_ref.shape[0]
    hb = _rms(x_ref[...], g1_ref[...]).astype(BF16)
    h_ref[...] = hb
    ga = _dot(hb, wga_ref[...])
    z = _dot(ga.astype(BF16), wa2_ref[...]) + ba_ref[...]
    log_sig = jnp.minimum(z, 0.0) - jnp.log(1.0 + jnp.exp(-jnp.abs(z)))
    loga_ref[...] = log_sig * (LOG2_E / GLA_TAU)

    dil_refs = (d0_ref, d1_ref, d2_ref)
    n_strided = 0
    for jj in range(N_COL_TILES):
        cols = slice(jj * HEAD_W, (jj + 1) * HEAD_W)
        if jj < N_GLA_TILES:
            w = wgla_ref[:, cols]
        else:
            w = wdil_ref[:, (jj - N_GLA_TILES) * HEAD_W:(jj - N_GLA_TILES + 1) * HEAD_W]
        y = _dot(h_ref[...], w)
        if jj < N_GLA_TILES:
            gla_ref[:, cols] = y.astype(BF16)
            continue
        gi, kind = divmod(jj - N_GLA_TILES, 3)
        d = DIL_PATTERNS[gi][1]
        ocols = [slice(kind * HEAD_W + c * LANES, kind * HEAD_W + (c + 1) * LANES) for c in range(DIL_HEADS)]
        if kind < 2:
            g = qkg_ref[jj]
            slabs = [_rms(y[:, _head(c)], g[:, _head(c)]) for c in range(DIL_HEADS)]
        else:
            slabs = [y[:, _head(c)] for c in range(DIL_HEADS)]
        out = dil_refs[gi]
        if d == 1:
            for c in range(DIL_HEADS):
                out[:, ocols[c]] = slabs[c].astype(BF16)
            continue
        buf = n_strided % ybuf_ref.shape[0]
        n_strided += 1
        for c in range(DIL_HEADS):
            ybuf_ref[buf, c] = slabs[c]
        for r in range(d):
            for c in range(DIL_HEADS):
                out[0, r, :, ocols[c]] = ybuf_ref[buf, c, pl.ds(r, tm // d, stride=d), :].astype(BF16)


def _resident(shape):
    return pl.BlockSpec(shape, lambda i: (0,) * len(shape), pipeline_mode=pl.Buffered(1))


def _inproj(x2d, g1, w_gla, w_dil, w_ga, w_a2, b_a, qk_gain, B, S):
    T = x2d.shape[0]
    tm = min(INPROJ_TM, S)
    tpb = S // tm

    def dil_spec(gi):
        d = DIL_PATTERNS[gi][1]
        if d == 1:
            return pl.BlockSpec((tm, QKV_W), lambda i: (i, 0))
        return pl.BlockSpec((1, d, tm // d, QKV_W), lambda i: (i // tpb, 0, i % tpb, 0))

    def dil_shape(gi):
        d = DIL_PATTERNS[gi][1]
        shape = (T, QKV_W) if d == 1 else (B, d, S // d, QKV_W)
        return jax.ShapeDtypeStruct(shape, BF16)

    return pl.pallas_call(
        _inproj_kernel,
        grid=(T // tm,),
        in_specs=[
            pl.BlockSpec((tm, D_MODEL), lambda i: (i, 0)),
            _resident(g1.shape), _resident(w_gla.shape), _resident(w_dil.shape), _resident(w_ga.shape),
            _resident(w_a2.shape),
            _resident(b_a.shape), _resident(qk_gain.shape),
        ],
        out_specs=[
            pl.BlockSpec((tm, N_GLA_TILES * HEAD_W), lambda i: (i, 0)),
            pl.BlockSpec((tm, HEAD_W), lambda i: (i, 0)),
            dil_spec(0), dil_spec(1), dil_spec(2),
        ],
        out_shape=[
            jax.ShapeDtypeStruct((T, N_GLA_TILES * HEAD_W), BF16),
            jax.ShapeDtypeStruct((T, HEAD_W), F32),
            dil_shape(0), dil_shape(1), dil_shape(2),
        ],
        scratch_shapes=[pltpu.VMEM((tm, D_MODEL), BF16), pltpu.VMEM((2, DIL_HEADS, tm, LANES), F32)],
        compiler_params=_params(("arbitrary",)),
        name="inproj",
    )(x2d, g1, w_gla, w_dil, w_ga, w_a2, b_a, qk_gain)


def _gla_kernel(q_ref, k_ref, v_ref, r_ref, la_ref, tri_ref, gn_ref, o_ref, st_ref):
    n = pl.program_id(1)

    @pl.when(n == 0)
    def _():
        st_ref[...] = jnp.zeros_like(st_ref)

    for blk in range(q_ref.shape[0] // GLA_BLOCK):
        _gla_block(q_ref, k_ref, v_ref, r_ref, la_ref, tri_ref, gn_ref, o_ref, st_ref,
                   slice(blk * GLA_BLOCK, (blk + 1) * GLA_BLOCK))


def _gla_block(q_ref, k_ref, v_ref, r_ref, la_ref, tri_ref, gn_ref, o_ref, st_ref, rows):
    L = GLA_BLOCK
    la = la_ref[rows, :]
    hi = la.astype(BF16)
    r1 = la - hi.astype(F32)
    mid = r1.astype(BF16)
    lo = (r1 - mid.astype(F32)).astype(BF16)
    tri = tri_ref[...]
    b_all = _dot(tri, hi) + _dot(tri, mid) + _dot(tri, lo)

    row = lax.broadcasted_iota(jnp.int32, (L, L), 0)
    col = lax.broadcasted_iota(jnp.int32, (L, L), 1)

    for h in range(GLA_HEADS):
        sl = _head(h)
        b = b_all[:, sl]
        q = q_ref[rows, sl].astype(F32) * (GLA_DK ** -0.5)
        k = k_ref[rows, sl].astype(F32)
        v = v_ref[rows, sl]

        att = jnp.zeros((L, L), F32)
        seg = L
        while seg > GLA_SUBCHUNK:
            half = seg // 2
            beta = jnp.concatenate(
                [jnp.broadcast_to(b[s + half - 1:s + half, :], (seg, GLA_DK)) for s in range(0, L, seg)], axis=0)
            rel = b - beta
            neg = jnp.minimum(rel, 0.0)
            qs = (q * jnp.exp2(neg)).astype(BF16)
            ks = (k * jnp.exp2(neg - rel)).astype(BF16)
            same = (row // seg) == (col // seg)
            mask = same & ((row % seg) >= half) & ((col % seg) < half)
            att = jnp.where(mask, _dot_nt(qs, ks), att)
            seg = half
        beta = jnp.concatenate(
            [jnp.zeros((GLA_SUBCHUNK, GLA_DK), F32)]
            + [jnp.broadcast_to(b[s - 1:s, :], (GLA_SUBCHUNK, GLA_DK)) for s in range(GLA_SUBCHUNK, L, GLA_SUBCHUNK)],
            axis=0)
        qs = (q * jnp.exp2(b - beta)).astype(BF16)
        ks = (k * jnp.exp2(beta - b)).astype(BF16)
        mask = ((row // GLA_SUBCHUNK) == (col // GLA_SUBCHUNK)) & (col <= row)
        att = jnp.where(mask, _dot_nt(qs, ks), att)

        state_t = st_ref[h]
        q0 = (q * jnp.exp2(b)).astype(BF16)
        o = _dot(att.astype(BF16), v) + _dot_nt(q0, state_t.astype(BF16))

        b_last = b[L - 1:L, :]
        k_end = (k * jnp.exp2(b_last - b)).astype(BF16)
        st_ref[h] = state_t * jnp.exp2(b_last) + _dot_tn(v, k_end)

        rr = r_ref[rows, sl].astype(F32)
        o_ref[rows, sl] = (_rms(o, gn_ref[...]) * (rr * jax.nn.sigmoid(rr))).astype(BF16)


def _gla(proj, log_a, tri, gn, B, S):
    L = GLA_STEP_BLOCKS * GLA_BLOCK
    nb = S // L
    return pl.pallas_call(
        _gla_kernel,
        grid=(B, nb),
        in_specs=[
            pl.BlockSpec((L, HEAD_W), lambda b, n: (b * nb + n, 0)),
            pl.BlockSpec((L, HEAD_W), lambda b, n: (b * nb + n, 1)),
            pl.BlockSpec((L, HEAD_W), lambda b, n: (b * nb + n, 2)),
            pl.BlockSpec((L, HEAD_W), lambda b, n: (b * nb + n, 3)),
            pl.BlockSpec((L, HEAD_W), lambda b, n: (b * nb + n, 0)),
            pl.BlockSpec((GLA_BLOCK, GLA_BLOCK), lambda b, n: (0, 0)),
            pl.BlockSpec((1, GLA_DK), lambda b, n: (0, 0)),
        ],
        out_specs=pl.BlockSpec((L, HEAD_W), lambda b, n: (b * nb + n, 0)),
        out_shape=jax.ShapeDtypeStruct((B * S, HEAD_W), BF16),
        scratch_shapes=[pltpu.VMEM((GLA_HEADS, GLA_DK, GLA_DK), F32)],
        compiler_params=_params(("arbitrary", "arbitrary")),
        name="gla",
    )(proj, proj, proj, proj, log_a, tri, gn)


def _dil_kernel(q_ref, kp_ref, kc_ref, vp_ref, vc_ref, bias_ref, o_ref, st_ref, *, nr, nq):
    first = pl.program_id(2) == 0
    Q = DIL_BLOCK
    col = lax.broadcasted_iota(jnp.int32, (Q, 2 * Q), 1)
    lane = lax.broadcasted_iota(jnp.int32, (Q, LANES), 1)
    for res in range(nr):
        for blk in range(nq):
            rows = slice(blk * Q, (blk + 1) * Q)
            prev = slice((blk - 1) * Q, blk * Q)
            stats = jnp.zeros((Q, LANES), F32)
            for h in range(DIL_HEADS):
                sl = _head(h)
                if blk == 0:
                    kprev, vprev = kp_ref[0, res, :, sl], vp_ref[0, res, :, sl]
                else:
                    kprev, vprev = kc_ref[0, res, prev, sl], vc_ref[0, res, prev, sl]
                kk = jnp.concatenate([kprev, kc_ref[0, res, rows, sl]], axis=0)
                vv = jnp.concatenate([vprev, vc_ref[0, res, rows, sl]], axis=0)
                s = _dot_nt(q_ref[0, res, rows, sl], kk) + bias_ref[h]
                if blk == 0:
                    s = jnp.where(jnp.logical_and(first, col < Q), NEG, s)
                m = jnp.max(s, axis=-1, keepdims=True)
                p = jnp.exp2(s - m)
                den = jnp.sum(p, axis=-1, keepdims=True)
                o_ref[0, res, rows, sl] = (_dot(p.astype(BF16), vv) / den).astype(BF16)
                stats = jnp.where((lane // 16) == h, m, stats)
                stats = jnp.where((lane // 16) == DIL_HEADS + h, den, stats)
            st_ref[0, res, rows, :] = stats


def _dilated(qkv, bias, gi):
    B, d, Lsub, _ = qkv.shape
    nq = min(DIL_STEP_BLOCKS, Lsub // DIL_BLOCK)
    nr = min(DIL_STEP_BLOCKS // nq, d)
    rows = nq * DIL_BLOCK

    def cur(c):
        return pl.BlockSpec((1, nr, rows, HEAD_W), lambda b, r, i: (b, r, i, c))

    def prev(c):
        return pl.BlockSpec((1, nr, DIL_BLOCK, HEAD_W), lambda b, r, i: (b, r, jnp.maximum(i * nq - 1, 0), c))

    return pl.pallas_call(
        functools.partial(_dil_kernel, nr=nr, nq=nq),
        grid=(B, d // nr, Lsub // rows),
        in_specs=[cur(0), prev(1), cur(1), prev(2), cur(2),
                  pl.BlockSpec((DIL_HEADS, DIL_BLOCK, 2 * DIL_BLOCK), lambda b, r, i: (0, 0, 0))],
        out_specs=[
            pl.BlockSpec((1, nr, rows, HEAD_W), lambda b, r, i: (b, r, i, 0)),
            pl.BlockSpec((1, nr, rows, LANES), lambda b, r, i: (b, r, i, 0)),
        ],
        out_shape=[
            jax.ShapeDtypeStruct((B, d, Lsub, HEAD_W), BF16),
            jax.ShapeDtypeStruct((B, d, Lsub, LANES), F32),
        ],
        compiler_params=_params(("arbitrary", "arbitrary", "arbitrary")),
        name=f"dilated{gi}",
    )(qkv, qkv, qkv, qkv, qkv, bias)


def _alibi_bias(gi):
    window, d = DIL_PATTERNS[gi]
    Q = DIL_BLOCK
    dist = np.arange(Q)[:, None] + Q - np.arange(2 * Q)[None, :]
    valid = (dist >= 0) & (dist <= window // d)
    slopes = np.asarray(ALIBI_SLOPES, np.float32).reshape(DIL_GROUPS, DIL_HEADS)[gi]
    bias = -slopes[:, None, None] * (d * dist).astype(np.float32) * np.float32(LOG2_E)
    return jnp.asarray(np.where(valid[None], bias, np.float32(NEG)), F32)


def _merge_rows(x_ref, og_ref, o0_ref, s0_ref, g1_ref, wbg_ref, bbg_ref, wpg_ref, wpa_ref, wout_ref,
                g2_ref, wr_ref, br_ref, x2_ref, obuf_ref, sbuf_ref, row0, n):
    rows = slice(row0, row0 + n)
    x = x_ref[rows, :]
    hb = _rms(x, g1_ref[...]).astype(BF16)
    gates = jax.nn.sigmoid(_dot(hb, wbg_ref[...]) + bbg_ref[...])

    stats = (s0_ref[rows, :], sbuf_ref[0, rows, :], sbuf_ref[1, rows, :])
    dens = [pltpu.roll(s, 64, 1) for s in stats]
    m_all = jnp.maximum(jnp.maximum(stats[0], stats[1]), stats[2])
    wts = [d * jnp.exp2(s - m_all) for s, d in zip(stats, dens)]
    inv = 1.0 / (wts[0] + wts[1] + wts[2])
    coef = [w * inv for w in wts]
    heads = []
    for h in range(DIL_HEADS):
        group_out = (o0_ref[rows, _head(h)].astype(F32), obuf_ref[0, h, rows, :], obuf_ref[1, h, rows, :])
        acc = jnp.zeros((n, DIL_DH), F32)
        for g in range(DIL_GROUPS):
            c = jnp.broadcast_to(coef[g][:, 16 * h:16 * h + 1], (n, DIL_DH))
            acc = acc + c * group_out[g]
        heads.append(acc.astype(BF16))
    o_att = jnp.concatenate(heads, axis=1)

    y = gates[:, :D_MODEL] * _dot(og_ref[rows, :], wpg_ref[...]) + gates[:, D_MODEL:] * _dot(o_att, wpa_ref[...])
    x2 = x + _dot(y.astype(BF16), wout_ref[...])
    x2_ref[rows, :] = x2

    h2 = _rms(x2, g2_ref[...])

    h2h = h2.astype(BF16)
    h2l = (h2 - h2h.astype(F32)).astype(BF16)
    wr = wr_ref[...]
    pa = _dot(h2h, wr)
    pb = _dot(h2l, wr)
    logit = pa[:, :LANES] + pa[:, LANES:] + pb[:, :LANES] + pb[:, LANES:] + br_ref[...]

    lane = lax.broadcasted_iota(jnp.int32, (n, LANES), 1).astype(F32)
    big = jnp.float32(1e9)
    gl = jnp.where(lane < N_GROUPS, logit, NEG)
    gmax = jnp.max(gl, axis=-1, keepdims=True)
    gsel = jnp.min(jnp.where(gl == gmax, lane, big), axis=-1, keepdims=True)
    g_p = 1.0 / jnp.sum(jnp.exp(gl - gmax), axis=-1, keepdims=True)
    lo = N_GROUPS + EXPERTS_PER_GROUP * gsel
    el = jnp.where((lane >= lo) & (lane < lo + EXPERTS_PER_GROUP), logit, NEG)
    v1 = jnp.max(el, axis=-1, keepdims=True)
    i1 = jnp.min(jnp.where(el == v1, lane, big), axis=-1, keepdims=True)
    el2 = jnp.where(lane == i1, NEG, el)
    v2 = jnp.max(el2, axis=-1, keepdims=True)
    i2 = jnp.min(jnp.where(el2 == v2, lane, big), axis=-1, keepdims=True)
    ex = jnp.exp(v2 - v1)
    w1 = g_p / (1.0 + ex)
    w2 = g_p * ex / (1.0 + ex)
    return i1 - N_GROUPS, i2 - N_GROUPS, w1, w2, h2h


def _merge_kernel(x_ref, og_ref, o0_ref, o1_ref, o2_ref, s0_ref, s1_ref, s2_ref, g1_ref, wbg_ref, bbg_ref,
                  wpg_ref, wpa_ref, wout_ref, g2_ref, wr_ref, br_ref, tri_ref, upper_ref,
                  x2_ref, xloc_ref, route_ref, cnt_ref, obuf_ref, sbuf_ref):
    tm = x_ref.shape[0]

    for slot, (o_ref, s_ref) in enumerate(((o1_ref, s1_ref), (o2_ref, s2_ref))):
        d = DIL_PATTERNS[slot + 1][1]
        for r in range(d):
            sbuf_ref[slot, pl.ds(r, tm // d, stride=d), :] = s_ref[0, r]
            for c in range(DIL_HEADS):
                obuf_ref[slot, c, pl.ds(r, tm // d, stride=d), :] = o_ref[0, r, :, _head(c)].astype(F32)

    e1, e2, w1, w2, h2b = _merge_rows(x_ref, og_ref, o0_ref, s0_ref, g1_ref, wbg_ref, bbg_ref, wpg_ref, wpa_ref,
                                      wout_ref, g2_ref, wr_ref, br_ref, x2_ref, obuf_ref, sbuf_ref, 0, tm)

    lane = lax.broadcasted_iota(jnp.int32, (tm, LANES), 1).astype(F32)
    oh1 = lane == e1
    oh2 = lane == e2
    onehot = jnp.where(oh1 | oh2, 1.0, 0.0)
    prefix = _dot(tri_ref[...], onehot.astype(BF16))
    cnt = jnp.sum(onehot, axis=0, keepdims=True)
    cnt_pad = jnp.floor((cnt + (CHUNK - 1)) * (1.0 / CHUNK)) * CHUNK
    seg_off = _dot(jnp.broadcast_to(cnt_pad, (8, LANES)).astype(BF16), upper_ref[...])[0:1, :]
    base = seg_off + prefix
    pos1 = jnp.sum(jnp.where(oh1, base, 0.0), axis=-1, keepdims=True)
    pos2 = jnp.sum(jnp.where(oh2, base, 0.0), axis=-1, keepdims=True)
    cnt_ref[...] = jnp.broadcast_to(cnt, cnt_ref.shape)

    route = jnp.zeros((tm, LANES), F32)
    for idx, val in enumerate((e1, e2, w1, w2, pos1, pos2)):
        route = jnp.where(lane == idx, val, route)
    route_ref[...] = route

    route_t = route.T
    slot = lax.broadcasted_iota(jnp.int32, (LOCAL_ROWS, tm), 0).astype(F32)
    perm = jnp.where((slot == route_t[4:5, :]) | (slot == route_t[5:6, :]), 1.0, 0.0).astype(BF16)
    xloc_ref[...] = _pack_pairs(_dot(perm, h2b))


def _merge(x2d, o_gla, outs, stats, g1, wbg, bbg, wpg, wpa, wout, g2, wr, br, tri, upper, S):
    T = x2d.shape[0]
    tm = min(MERGE_TM, S)
    assert TOP_K * tm + N_EXPERTS * CHUNK <= LOCAL_ROWS
    tpb = S // tm
    tok = lambda w: pl.BlockSpec((tm, w), lambda i: (i, 0))
    full = lambda a: pl.BlockSpec(a.shape, lambda i: (0,) * a.ndim)

    def res(gi, w):
        d = DIL_PATTERNS[gi][1]
        return pl.BlockSpec((1, d, tm // d, w), lambda i: (i // tpb, 0, i % tpb, 0))

    return pl.pallas_call(
        _merge_kernel,
        grid=(T // tm,),
        in_specs=[tok(D_MODEL), tok(HEAD_W), tok(HEAD_W), res(1, HEAD_W), res(2, HEAD_W),
                  tok(LANES), res(1, LANES), res(2, LANES),
                  full(g1), full(wbg), full(bbg), full(wpg), full(wpa), full(wout), full(g2),
                  full(wr), full(br), full(tri), full(upper)],
        out_specs=[tok(D_MODEL), pl.BlockSpec((LOCAL_ROWS, PACKED_W), lambda i: (i, 0)), tok(LANES),
                   pl.BlockSpec((8, LANES), lambda i: (i, 0))],
        out_shape=[
            jax.ShapeDtypeStruct((T, D_MODEL), F32),
            jax.ShapeDtypeStruct((T // tm * LOCAL_ROWS, PACKED_W), F32),
            jax.ShapeDtypeStruct((T, LANES), F32),
            jax.ShapeDtypeStruct((T // tm * 8, LANES), F32),
        ],
        scratch_shapes=[pltpu.VMEM((DIL_GROUPS - 1, DIL_HEADS, tm, LANES), F32),
                        pltpu.VMEM((DIL_GROUPS - 1, tm, LANES), F32)],
        compiler_params=_params(("arbitrary",)),
        name="merge",
    )(x2d, o_gla, *outs, *stats, g1, wbg, bbg, wpg, wpa, wout, g2, wr, br, tri, upper)


class _ChunkGather:
    def __init__(self, src_hbm, buf_ref, sem, nchunks):
        self.src, self.buf, self.sem, self.n = src_hbm, buf_ref, sem, nchunks

    def start(self, offs_ref, s):
        for j in range(self.n):
            row = pl.multiple_of(offs_ref[0, 0, j], CHUNK)
            pltpu.make_async_copy(self.src.at[pl.ds(row, CHUNK), :], self.buf.at[s, pl.ds(j * CHUNK, CHUNK), :],
                                  self.sem.at[s]).start()

    def wait(self, s):
        pltpu.make_async_copy(self.src.at[pl.ds(0, self.n * CHUNK), :], self.buf.at[s], self.sem.at[s]).wait()

    def step(self, offs_cur_ref, offs_next_ref):
        i = pl.program_id(0)
        slot = lax.rem(i, 2)

        @pl.when(i == 0)
        def _():
            self.start(offs_cur_ref, 0)

        self.wait(slot)

        @pl.when(i + 1 < pl.num_programs(0))
        def _():
            self.start(offs_next_ref, 1 - slot)

        return slot


def _expert_kernel(tile_e_ref, nact_ref, src_cur_ref, src_next_ref, xloc_ref, wg_ref, wu_ref, wd_ref,
                   y_ref, xbuf_ref, wgu_ref, wdn_ref, gsem):
    i = pl.program_id(0)
    slot = _ChunkGather(xloc_ref, xbuf_ref, gsem, TILE_CHUNKS).step(src_cur_ref, src_next_ref)
    active = i < nact_ref[0]

    @pl.when(jnp.logical_and(active, jnp.logical_or(i == 0, tile_e_ref[i] != tile_e_ref[jnp.maximum(i - 1, 0)])))
    def _():
        wgu_ref[:, :D_EXPERT] = wg_ref[0].astype(BF16)
        wgu_ref[:, D_EXPERT:] = wu_ref[0].astype(BF16)
        wdn_ref[...] = wd_ref[0].astype(BF16)

    @pl.when(active)
    def _():
        xb = _unpack_pairs(xbuf_ref[slot])
        gu = _dot(xb, wgu_ref[...])
        g, u = gu[:, :D_EXPERT], gu[:, D_EXPERT:]
        a = (g * jax.nn.sigmoid(g) * u).astype(BF16)
        y_ref[...] = _pack_pairs(_dot(a, wdn_ref[...]).astype(BF16).astype(F32))

    @pl.when(jnp.logical_not(active))
    def _():
        y_ref[...] = jnp.zeros_like(y_ref)


def _experts(tile_e, nact, csrc, xloc, wg, wu, wd):
    ntiles = tile_e.shape[0]
    R = MOE_TM

    def wsel(i, te, na):
        return (te[jnp.minimum(i, na[0] - 1)], 0, 0)

    def offs(shift):
        return pl.BlockSpec((1, 1, TILE_CHUNKS), lambda i, te, na: (jnp.minimum(i + shift, ntiles - 1), 0, 0),
                            memory_space=pltpu.SMEM)

    return pl.pallas_call(
        _expert_kernel,
        grid_spec=pltpu.PrefetchScalarGridSpec(
            num_scalar_prefetch=2,
            grid=(ntiles,),
            in_specs=[
                offs(0), offs(1),
                pl.BlockSpec(memory_space=pl.ANY),
                pl.BlockSpec((1, D_MODEL, D_EXPERT), wsel),
                pl.BlockSpec((1, D_MODEL, D_EXPERT), wsel),
                pl.BlockSpec((1, D_EXPERT, D_MODEL), wsel),
            ],
            out_specs=pl.BlockSpec((R, PACKED_W), lambda i, te, na: (i, 0)),
            scratch_shapes=[pltpu.VMEM((2, R, PACKED_W), F32), pltpu.VMEM((D_MODEL, 2 * D_EXPERT), BF16),
                            pltpu.VMEM((D_EXPERT, D_MODEL), BF16), pltpu.SemaphoreType.DMA((2,))],
        ),
        out_shape=jax.ShapeDtypeStruct((ntiles * R, PACKED_W), F32),
        compiler_params=_params(("arbitrary",)),
        name="experts",
    )(tile_e, nact, csrc, csrc, xloc, wg, wu, wd)


def _combine_kernel(dst_cur_ref, dst_next_ref, x2_ref, route_ref, y_ref, out_ref, ybuf_ref, gsem):
    slot = _ChunkGather(y_ref, ybuf_ref, gsem, LOCAL_CHUNKS).step(dst_cur_ref, dst_next_ref)
    tm = x2_ref.shape[0]
    route = route_ref[...]
    w1, w2, pos1, pos2 = (route[:, c:c + 1] for c in range(2, 6))
    yl = _unpack_pairs(ybuf_ref[slot])
    cols = lax.broadcasted_iota(jnp.int32, (tm, LOCAL_ROWS), 1).astype(F32)
    pick1 = jnp.where(cols == pos1, 1.0, 0.0).astype(BF16)
    pick2 = jnp.where(cols == pos2, 1.0, 0.0).astype(BF16)
    out_ref[...] = x2_ref[...] + w1 * _dot(pick1, yl) + w2 * _dot(pick2, yl)


def _combine(cdst, x2, route, y):
    T = x2.shape[0]
    nt = cdst.shape[0]
    tm = T // nt

    def offs(shift):
        return pl.BlockSpec((1, 1, LOCAL_CHUNKS), lambda i: (jnp.minimum(i + shift, nt - 1), 0, 0),
                            memory_space=pltpu.SMEM)

    return pl.pallas_call(
        _combine_kernel,
        grid=(nt,),
        in_specs=[
            offs(0), offs(1),
            pl.BlockSpec((tm, D_MODEL), lambda i: (i, 0)),
            pl.BlockSpec((tm, LANES), lambda i: (i, 0)),
            pl.BlockSpec(memory_space=pl.ANY),
        ],
        out_specs=pl.BlockSpec((tm, D_MODEL), lambda i: (i, 0)),
        out_shape=jax.ShapeDtypeStruct((T, D_MODEL), F32),
        scratch_shapes=[pltpu.VMEM((2, LOCAL_ROWS, PACKED_W), F32), pltpu.SemaphoreType.DMA((2,))],
        compiler_params=_params(("arbitrary",)),
        name="combine",
    )(cdst, cdst, x2, route, y)


def _layer(x2d, B, S, norm1_g, w_in, w_gla_a2, b_gla_a, gla_out_norm_g, dil_q_norm_g, dil_k_norm_g,
           w_proj_gla, w_proj_attn, w_branch_gate, b_branch_gate, w_out, norm2_g,
           w_router_group, b_router_group, w_router_expert, b_router_expert, w_gate, w_up, w_down):
    T = B * S
    n_gla = N_GLA_TILES * HEAD_W
    w_gla = w_in[:, :n_gla].astype(BF16)
    w_dil = w_in.astype(BF16)[:, n_gla + GLA_RANK:]
    w_ga = jnp.pad(w_in[:, n_gla:n_gla + GLA_RANK], ((0, 0), (0, LANES - GLA_RANK))).astype(BF16)
    w_a2 = jnp.pad(w_gla_a2, ((0, LANES - GLA_RANK), (0, 0))).astype(BF16)
    gains = [jnp.ones((HEAD_W,), F32)] * N_GLA_TILES
    for gi in range(DIL_GROUPS):
        gains += [jnp.tile(dil_q_norm_g[gi], DIL_HEADS) * (DIL_DH ** -0.5 * LOG2_E),
                  jnp.tile(dil_k_norm_g[gi], DIL_HEADS), jnp.ones((HEAD_W,), F32)]
    qk_gain = jnp.stack(gains).reshape(N_COL_TILES, 1, HEAD_W)

    gla_in, log_a, qkv0, qkv1, qkv2 = _inproj(
        x2d, norm1_g.reshape(1, -1), w_gla, w_dil, w_ga, w_a2, b_gla_a.reshape(1, -1), qk_gain, B, S)

    tri_incl = jnp.asarray(np.tril(np.ones((GLA_BLOCK, GLA_BLOCK), np.float32)), BF16)
    o_gla = _gla(gla_in, log_a, tri_incl, gla_out_norm_g.reshape(1, -1), B, S)

    outs, stats = [], []
    for gi, qkv in enumerate((qkv0.reshape(B, 1, S, QKV_W), qkv1, qkv2)):
        o, st = _dilated(qkv, _alibi_bias(gi), gi)
        outs.append(o)
        stats.append(st)
    outs[0] = outs[0].reshape(T, HEAD_W)
    stats[0] = stats[0].reshape(T, LANES)

    w_r = jnp.pad(jnp.concatenate([w_router_group, w_router_expert], axis=1),
                  ((0, 0), (0, LANES - N_GROUPS - N_EXPERTS)))
    w_rh = w_r.astype(BF16)
    w_rl = (w_r - w_rh.astype(F32)).astype(BF16)
    b_r = jnp.pad(jnp.concatenate([b_router_group, b_router_expert]), (0, LANES - N_GROUPS - N_EXPERTS))
    tm = min(MERGE_TM, S)
    tri_strict = jnp.asarray(np.tril(np.ones((tm, tm), np.float32), -1), BF16)
    upper = jnp.asarray(np.triu(np.ones((LANES, LANES), np.float32), 1), BF16)
    x2, xloc, route, cnt = _merge(
        x2d, o_gla, outs, stats, norm1_g.reshape(1, -1), w_branch_gate.astype(BF16),
        b_branch_gate.reshape(1, -1), w_proj_gla.astype(BF16), w_proj_attn.astype(BF16), w_out.astype(BF16),
        norm2_g.reshape(1, -1), jnp.concatenate([w_rh, w_rl], axis=1), b_r.reshape(1, -1), tri_strict, upper, S)

    csrc, cdst, tile_e, nact = _chunk_tables(cnt, T // tm, tm)
    y = _experts(tile_e, nact, csrc, xloc, w_gate, w_up, w_down)
    return _combine(cdst, x2, route, y)


def _chunk_tables(cnt, nt, tm):
    i32 = jnp.int32
    experts = jnp.arange(N_EXPERTS, dtype=i32)
    tiles = jnp.arange(nt, dtype=i32)
    c = (cnt.reshape(nt, 8, LANES)[:, 0, :N_EXPERTS].astype(i32) + CHUNK - 1) // CHUNK
    loff = jnp.cumsum(c, axis=1) - c
    ecum = jnp.cumsum(c, axis=0) - c
    tot = jnp.sum(c, axis=0)
    ptot = (tot + TILE_CHUNKS - 1) // TILE_CHUNKS * TILE_CHUNKS
    pend = jnp.cumsum(ptot)
    pstart = pend - ptot

    lc = jnp.arange(LOCAL_CHUNKS, dtype=i32)
    e_of = jnp.sum(((loff + c)[:, None, :] <= lc[None, :, None]).astype(i32), axis=-1)
    pick_e = e_of[:, :, None] == experts[None, None, :]
    glob = jnp.sum(jnp.where(pick_e, (pstart[None, :] + ecum - loff)[:, None, :], 0), axis=-1) + lc[None, :]
    cdst = jnp.where(e_of < N_EXPERTS, glob, 0) * CHUNK

    max_chunks = (TOP_K * nt * tm + (CHUNK - 1) * nt * N_EXPERTS) // CHUNK + N_EXPERTS * (TILE_CHUNKS - 1)
    ntiles = -(-max_chunks // TILE_CHUNKS)
    g = jnp.arange(ntiles * TILE_CHUNKS, dtype=i32)
    e_g = jnp.minimum(jnp.sum((pend[None, :] <= g[:, None]).astype(i32), axis=1), N_EXPERTS - 1)
    k = g - jnp.sum(jnp.where(e_g[:, None] == experts[None, :], pstart[None, :], 0), axis=1)
    pick = (e_g[:, None] == experts[None, :]).astype(F32)
    ends = jnp.dot(pick, (ecum + c).T.astype(F32), precision=lax.Precision.HIGHEST)
    t_g = jnp.sum((ends <= k[:, None].astype(F32)).astype(i32), axis=1)
    valid = t_g < nt
    base = jnp.dot(pick, (tiles[:, None] * LOCAL_CHUNKS + loff - ecum).T.astype(F32),
                   precision=lax.Precision.HIGHEST)
    in_tile = tiles[None, :] == jnp.minimum(t_g, nt - 1)[:, None]
    local = jnp.sum(jnp.where(in_tile, base, 0.0), axis=1).astype(i32) + k
    zero_chunk = LOCAL_CHUNKS - 1
    csrc = jnp.where(valid, local, zero_chunk) * CHUNK

    tile_e = e_g[::TILE_CHUNKS]
    nact = (pend[-1:] // TILE_CHUNKS).astype(i32)
    return (csrc.reshape(ntiles, 1, TILE_CHUNKS), cdst.reshape(nt, 1, LOCAL_CHUNKS), tile_e, nact)


def kernel(x, norm1_g, w_in, w_gla_a2, b_gla_a, gla_out_norm_g, dil_q_norm_g, dil_k_norm_g, w_proj_gla,
           w_proj_attn, w_branch_gate, b_branch_gate, w_out, norm2_g, w_router_group, b_router_group,
           w_router_expert, b_router_expert, w_gate, w_up, w_down):
    B, S, D = x.shape
    assert D == D_MODEL and S % (DIL_BLOCK * DIL_PATTERNS[-1][1]) == 0
    x2d = x.reshape(B * S, D)
    params = (norm1_g, w_in, w_gla_a2, b_gla_a, gla_out_norm_g, dil_q_norm_g, dil_k_norm_g, w_proj_gla,
              w_proj_attn, w_branch_gate, b_branch_gate, w_out, norm2_g, w_router_group, b_router_group,
              w_router_expert, b_router_expert, w_gate, w_up, w_down)
    for layer in range(norm1_g.shape[0]):
        x2d = _layer(x2d, B, S, *(p[layer] for p in params))
    return x2d.reshape(B, S, D)
```

```python
import functools

import jax
import jax.numpy as jnp
import numpy as np
from jax import lax
from jax.experimental import pallas as pl
from jax.experimental.pallas import tpu as pltpu

F32 = jnp.float32
BF16 = jnp.bfloat16

D_MODEL = 1024
EPS = 1e-6
GLA_HEADS = 4
GLA_DK = 128
GLA_RANK = 16
GLA_TAU = 16.0
LOG2_E = 1.4426950408889634
GLA_SUBCHUNK = 16
GLA_BLOCK = 128
GLA_STEP_BLOCKS = 16
DIL_PATTERNS = ((128, 1), (512, 4), (2048, 16))
DIL_GROUPS = 3
DIL_HEADS = 4
DIL_DH = 128
DIL_BLOCK = 128
ALIBI_SLOPES = tuple(2.0 ** (-8.0 * (i + 1) / (DIL_GROUPS * DIL_HEADS)) for i in range(DIL_GROUPS * DIL_HEADS))
N_GROUPS = 4
EXPERTS_PER_GROUP = 8
N_EXPERTS = N_GROUPS * EXPERTS_PER_GROUP
TOP_K = 2
D_EXPERT = 512

HEAD_W = 512
N_GLA_TILES = 4
N_COL_TILES = N_GLA_TILES + 3 * DIL_GROUPS
QKV_W = 3 * HEAD_W
LANES = 128
NEG = -1e30

INPROJ_TM = 512
MERGE_TM = 512
MOE_TM = 512
CHUNK = 8
TILE_CHUNKS = MOE_TM // CHUNK
LOCAL_ROWS = -(-(TOP_K * MERGE_TM + N_EXPERTS * CHUNK) // LANES) * LANES
LOCAL_CHUNKS = LOCAL_ROWS // CHUNK
DIL_STEP_BLOCKS = 16
VMEM_LIMIT = 56 * 1024 * 1024


def _dot(a, b):
    return jnp.dot(a, b, preferred_element_type=F32)


def _dot_nt(a, b):
    return lax.dot_general(a, b, (((1,), (1,)), ((), ())), preferred_element_type=F32)


def _dot_tn(a, b):
    return lax.dot_general(a, b, (((0,), (0,)), ((), ())), preferred_element_type=F32)


def _rms(x, g):
    return x * lax.rsqrt(jnp.mean(x * x, axis=-1, keepdims=True) + EPS) * g


def _params(sem):
    return pltpu.CompilerParams(dimension_semantics=sem, vmem_limit_bytes=VMEM_LIMIT)


def _head(c):
    return slice(c * LANES, (c + 1) * LANES)


PACKED_W = D_MODEL // 2
_HI16 = 0xFFFF0000


def _pack_pairs(x):
    u = lax.bitcast_convert_type(x, jnp.uint32)
    w = x.shape[1] // 2
    return lax.bitcast_convert_type((u[:, :w] & jnp.uint32(_HI16)) | (u[:, w:] >> 16), F32)


def _unpack_pairs(p):
    u = lax.bitcast_convert_type(p, jnp.uint32)
    hi = lax.bitcast_convert_type(u & jnp.uint32(_HI16), F32)
    lo = lax.bitcast_convert_type(u << 16, F32)
    return jnp.concatenate([hi, lo], axis=1).astype(BF16)


def _inproj_kernel(x_ref, g1_ref, w_ref, wa2_ref, ba_ref, qkg_ref,
                   gla_ref, loga_ref, d0_ref, d1_ref, d2_ref, h_ref, ybuf_ref, wdil_ref):
    tm = x_ref.shape[0]
    n_gla = N_GLA_TILES * HEAD_W

    @pl.when(pl.program_id(0) == 0)
    def _():
        for k in range(3 * DIL_GROUPS):
            first = n_gla + GLA_RANK + k * HEAD_W
            wdil_ref[:, k * HEAD_W:(k + 1) * HEAD_W] = w_ref[:, first:first + HEAD_W]

    hb = _rms(x_ref[...], g1_ref[...]).astype(BF16)
    h_ref[...] = hb
    ga = _dot(hb, w_ref[:, n_gla:n_gla + LANES])
    z = _dot(ga.astype(BF16), wa2_ref[...]) + ba_ref[...]
    log_sig = jnp.minimum(z, 0.0) - jnp.log(1.0 + jnp.exp(-jnp.abs(z)))
    loga_ref[...] = log_sig * (LOG2_E / GLA_TAU)

    dil_refs = (d0_ref, d1_ref, d2_ref)
    n_strided = 0
    for jj in range(N_COL_TILES):
        cols = slice(jj * HEAD_W, (jj + 1) * HEAD_W)
        if jj < N_GLA_TILES:
            w = w_ref[:, cols]
        else:
            w = wdil_ref[:, (jj - N_GLA_TILES) * HEAD_W:(jj - N_GLA_TILES + 1) * HEAD_W]
        y = _dot(h_ref[...], w)
        if jj < N_GLA_TILES:
            gla_ref[:, cols] = y.astype(BF16)
            continue
        gi, kind = divmod(jj - N_GLA_TILES, 3)
        d = DIL_PATTERNS[gi][1]
        ocols = [slice(kind * HEAD_W + c * LANES, kind * HEAD_W + (c + 1) * LANES) for c in range(DIL_HEADS)]
        if kind < 2:
            g = qkg_ref[jj]
            slabs = [_rms(y[:, _head(c)], g[:, _head(c)]) for c in range(DIL_HEADS)]
        else:
            slabs = [y[:, _head(c)] for c in range(DIL_HEADS)]
        out = dil_refs[gi]
        if d == 1:
            for c in range(DIL_HEADS):
                out[:, ocols[c]] = slabs[c].astype(BF16)
            continue
        buf = n_strided % ybuf_ref.shape[0]
        n_strided += 1
        for c in range(DIL_HEADS):
            ybuf_ref[buf, c] = slabs[c]
        for r in range(d):
            for c in range(DIL_HEADS):
                out[0, r, :, ocols[c]] = ybuf_ref[buf, c, pl.ds(r, tm // d, stride=d), :].astype(BF16)


def _resident(shape):
    return pl.BlockSpec(shape, lambda i: (0,) * len(shape), pipeline_mode=pl.Buffered(1))


def _inproj(x2d, g1, w_all, w_a2, b_a, qk_gain, B, S):
    T = x2d.shape[0]
    tm = min(INPROJ_TM, S)
    tpb = S // tm

    def dil_spec(gi):
        d = DIL_PATTERNS[gi][1]
        if d == 1:
            return pl.BlockSpec((tm, QKV_W), lambda i: (i, 0))
        return pl.BlockSpec((1, d, tm // d, QKV_W), lambda i: (i // tpb, 0, i % tpb, 0))

    def dil_shape(gi):
        d = DIL_PATTERNS[gi][1]
        shape = (T, QKV_W) if d == 1 else (B, d, S // d, QKV_W)
        return jax.ShapeDtypeStruct(shape, BF16)

    return pl.pallas_call(
        _inproj_kernel,
        grid=(T // tm,),
        in_specs=[
            pl.BlockSpec((tm, D_MODEL), lambda i: (i, 0)),
            _resident(g1.shape), _resident(w_all.shape), _resident(w_a2.shape),
            _resident(b_a.shape), _resident(qk_gain.shape),
        ],
        out_specs=[
            pl.BlockSpec((tm, N_GLA_TILES * HEAD_W), lambda i: (i, 0)),
            pl.BlockSpec((tm, HEAD_W), lambda i: (i, 0)),
            dil_spec(0), dil_spec(1), dil_spec(2),
        ],
        out_shape=[
            jax.ShapeDtypeStruct((T, N_GLA_TILES * HEAD_W), BF16),
            jax.ShapeDtypeStruct((T, HEAD_W), F32),
            dil_shape(0), dil_shape(1), dil_shape(2),
        ],
        scratch_shapes=[pltpu.VMEM((tm, D_MODEL), BF16), pltpu.VMEM((2, DIL_HEADS, tm, LANES), F32),
                        pltpu.VMEM((D_MODEL, 3 * DIL_GROUPS * HEAD_W), BF16)],
        compiler_params=_params(("arbitrary",)),
        name="inproj",
    )(x2d, g1, w_all, w_a2, b_a, qk_gain)


def _gla_kernel(q_ref, k_ref, v_ref, r_ref, la_ref, tri_ref, gn_ref, o_ref, st_ref):
    n = pl.program_id(1)

    @pl.when(n == 0)
    def _():
        st_ref[...] = jnp.zeros_like(st_ref)

    for blk in range(q_ref.shape[0] // GLA_BLOCK):
        _gla_block(q_ref, k_ref, v_ref, r_ref, la_ref, tri_ref, gn_ref, o_ref, st_ref,
                   slice(blk * GLA_BLOCK, (blk + 1) * GLA_BLOCK))


def _gla_block(q_ref, k_ref, v_ref, r_ref, la_ref, tri_ref, gn_ref, o_ref, st_ref, rows):
    L = GLA_BLOCK
    la = la_ref[rows, :]
    hi = la.astype(BF16)
    r1 = la - hi.astype(F32)
    mid = r1.astype(BF16)
    lo = (r1 - mid.astype(F32)).astype(BF16)
    tri = tri_ref[...]
    b_all = _dot(tri, hi) + _dot(tri, mid) + _dot(tri, lo)

    row = lax.broadcasted_iota(jnp.int32, (L, L), 0)
    col = lax.broadcasted_iota(jnp.int32, (L, L), 1)

    for h in range(GLA_HEADS):
        sl = _head(h)
        b = b_all[:, sl]
        q = q_ref[rows, sl].astype(F32) * (GLA_DK ** -0.5)
        k = k_ref[rows, sl].astype(F32)
        v = v_ref[rows, sl]

        att = jnp.zeros((L, L), F32)
        seg = L
        while seg > GLA_SUBCHUNK:
            half = seg // 2
            beta = jnp.concatenate(
                [jnp.broadcast_to(b[s + half - 1:s + half, :], (seg, GLA_DK)) for s in range(0, L, seg)], axis=0)
            rel = b - beta
            neg = jnp.minimum(rel, 0.0)
            qs = (q * jnp.exp2(neg)).astype(BF16)
            ks = (k * jnp.exp2(neg - rel)).astype(BF16)
            same = (row // seg) == (col // seg)
            mask = same & ((row % seg) >= half) & ((col % seg) < half)
            att = jnp.where(mask, _dot_nt(qs, ks), att)
            seg = half
        beta = jnp.concatenate(
            [jnp.zeros((GLA_SUBCHUNK, GLA_DK), F32)]
            + [jnp.broadcast_to(b[s - 1:s, :], (GLA_SUBCHUNK, GLA_DK)) for s in range(GLA_SUBCHUNK, L, GLA_SUBCHUNK)],
            axis=0)
        qs = (q * jnp.exp2(b - beta)).astype(BF16)
        ks = (k * jnp.exp2(beta - b)).astype(BF16)
        mask = ((row // GLA_SUBCHUNK) == (col // GLA_SUBCHUNK)) & (col <= row)
        att = jnp.where(mask, _dot_nt(qs, ks), att)

        state_t = st_ref[h]
        q0 = (q * jnp.exp2(b)).astype(BF16)
        o = _dot(att.astype(BF16), v) + _dot_nt(q0, state_t.astype(BF16))

        b_last = b[L - 1:L, :]
        k_end = (k * jnp.exp2(b_last - b)).astype(BF16)
        st_ref[h] = state_t * jnp.exp2(b_last) + _dot_tn(v, k_end)

        rr = r_ref[rows, sl].astype(F32)
        o_ref[rows, sl] = (_rms(o, gn_ref[...]) * (rr * jax.nn.sigmoid(rr))).astype(BF16)


def _gla(proj, log_a, tri, gn, B, S):
    L = GLA_STEP_BLOCKS * GLA_BLOCK
    nb = S // L
    return pl.pallas_call(
        _gla_kernel,
        grid=(B, nb),
        in_specs=[
            pl.BlockSpec((L, HEAD_W), lambda b, n: (b * nb + n, 0)),
            pl.BlockSpec((L, HEAD_W), lambda b, n: (b * nb + n, 1)),
            pl.BlockSpec((L, HEAD_W), lambda b, n: (b * nb + n, 2)),
            pl.BlockSpec((L, HEAD_W), lambda b, n: (b * nb + n, 3)),
            pl.BlockSpec((L, HEAD_W), lambda b, n: (b * nb + n, 0)),
            pl.BlockSpec((GLA_BLOCK, GLA_BLOCK), lambda b, n: (0, 0)),
            pl.BlockSpec((1, GLA_DK), lambda b, n: (0, 0)),
        ],
        out_specs=pl.BlockSpec((L, HEAD_W), lambda b, n: (b * nb + n, 0)),
        out_shape=jax.ShapeDtypeStruct((B * S, HEAD_W), BF16),
        scratch_shapes=[pltpu.VMEM((GLA_HEADS, GLA_DK, GLA_DK), F32)],
        compiler_params=_params(("arbitrary", "arbitrary")),
        name="gla",
    )(proj, proj, proj, proj, log_a, tri, gn)


def _dil_kernel(q_ref, kp_ref, kc_ref, vp_ref, vc_ref, bias_ref, o_ref, st_ref, *, nr, nq):
    first = pl.program_id(2) == 0
    Q = DIL_BLOCK
    col = lax.broadcasted_iota(jnp.int32, (Q, 2 * Q), 1)
    lane = lax.broadcasted_iota(jnp.int32, (Q, LANES), 1)
    for res in range(nr):
        for blk in range(nq):
            rows = slice(blk * Q, (blk + 1) * Q)
            prev = slice((blk - 1) * Q, blk * Q)
            stats = jnp.zeros((Q, LANES), F32)
            for h in range(DIL_HEADS):
                sl = _head(h)
                if blk == 0:
                    kprev, vprev = kp_ref[0, res, :, sl], vp_ref[0, res, :, sl]
                else:
                    kprev, vprev = kc_ref[0, res, prev, sl], vc_ref[0, res, prev, sl]
                kk = jnp.concatenate([kprev, kc_ref[0, res, rows, sl]], axis=0)
                vv = jnp.concatenate([vprev, vc_ref[0, res, rows, sl]], axis=0)
                s = _dot_nt(q_ref[0, res, rows, sl], kk) + bias_ref[h]
                if blk == 0:
                    s = jnp.where(jnp.logical_and(first, col < Q), NEG, s)
                m = jnp.max(s, axis=-1, keepdims=True)
                p = jnp.exp2(s - m)
                den = jnp.sum(p, axis=-1, keepdims=True)
                o_ref[0, res, rows, sl] = (_dot(p.astype(BF16), vv) / den).astype(BF16)
                stats = jnp.where((lane // 16) == h, m, stats)
                stats = jnp.where((lane // 16) == DIL_HEADS + h, den, stats)
            st_ref[0, res, rows, :] = stats


def _dilated(qkv, bias, gi):
    B, d, Lsub, _ = qkv.shape
    nq = min(DIL_STEP_BLOCKS, Lsub // DIL_BLOCK)
    nr = min(DIL_STEP_BLOCKS // nq, d)
    rows = nq * DIL_BLOCK

    def cur(c):
        return pl.BlockSpec((1, nr, rows, HEAD_W), lambda b, r, i: (b, r, i, c))

    def prev(c):
        return pl.BlockSpec((1, nr, DIL_BLOCK, HEAD_W), lambda b, r, i: (b, r, jnp.maximum(i * nq - 1, 0), c))

    return pl.pallas_call(
        functools.partial(_dil_kernel, nr=nr, nq=nq),
        grid=(B, d // nr, Lsub // rows),
        in_specs=[cur(0), prev(1), cur(1), prev(2), cur(2),
                  pl.BlockSpec((DIL_HEADS, DIL_BLOCK, 2 * DIL_BLOCK), lambda b, r, i: (0, 0, 0))],
        out_specs=[
            pl.BlockSpec((1, nr, rows, HEAD_W), lambda b, r, i: (b, r, i, 0)),
            pl.BlockSpec((1, nr, rows, LANES), lambda b, r, i: (b, r, i, 0)),
        ],
        out_shape=[
            jax.ShapeDtypeStruct((B, d, Lsub, HEAD_W), BF16),
            jax.ShapeDtypeStruct((B, d, Lsub, LANES), F32),
        ],
        compiler_params=_params(("arbitrary", "arbitrary", "arbitrary")),
        name=f"dilated{gi}",
    )(qkv, qkv, qkv, qkv, qkv, bias)


def _alibi_bias(gi):
    window, d = DIL_PATTERNS[gi]
    Q = DIL_BLOCK
    dist = np.arange(Q)[:, None] + Q - np.arange(2 * Q)[None, :]
    valid = (dist >= 0) & (dist <= window // d)
    slopes = np.asarray(ALIBI_SLOPES, np.float32).reshape(DIL_GROUPS, DIL_HEADS)[gi]
    bias = -slopes[:, None, None] * (d * dist).astype(np.float32) * np.float32(LOG2_E)
    return jnp.asarray(np.where(valid[None], bias, np.float32(NEG)), F32)


def _merge_rows(x_ref, og_ref, o0_ref, s0_ref, g1_ref, wbg_ref, bbg_ref, wpg_ref, wpa_ref, wout_ref,
                g2_ref, wr_ref, br_ref, x2_ref, obuf_ref, sbuf_ref, row0, n):
    rows = slice(row0, row0 + n)
    x = x_ref[rows, :]
    hb = _rms(x, g1_ref[...]).astype(BF16)
    gates = jax.nn.sigmoid(_dot(hb, wbg_ref[...]) + bbg_ref[...])

    stats = (s0_ref[rows, :], sbuf_ref[0, rows, :], sbuf_ref[1, rows, :])
    dens = [pltpu.roll(s, 64, 1) for s in stats]
    m_all = jnp.maximum(jnp.maximum(stats[0], stats[1]), stats[2])
    wts = [d * jnp.exp2(s - m_all) for s, d in zip(stats, dens)]
    inv = 1.0 / (wts[0] + wts[1] + wts[2])
    coef = [w * inv for w in wts]
    heads = []
    for h in range(DIL_HEADS):
        group_out = (o0_ref[rows, _head(h)].astype(F32), obuf_ref[0, h, rows, :], obuf_ref[1, h, rows, :])
        acc = jnp.zeros((n, DIL_DH), F32)
        for g in range(DIL_GROUPS):
            c = jnp.broadcast_to(coef[g][:, 16 * h:16 * h + 1], (n, DIL_DH))
            acc = acc + c * group_out[g]
        heads.append(acc.astype(BF16))
    o_att = jnp.concatenate(heads, axis=1)

    y = gates[:, :D_MODEL] * _dot(og_ref[rows, :], wpg_ref[...]) + gates[:, D_MODEL:] * _dot(o_att, wpa_ref[...])
    x2 = x + _dot(y.astype(BF16), wout_ref[...])
    x2_ref[rows, :] = x2

    h2 = _rms(x2, g2_ref[...])

    h2h = h2.astype(BF16)
    h2l = (h2 - h2h.astype(F32)).astype(BF16)
    wr = wr_ref[...]
    pa = _dot(h2h, wr)
    pb = _dot(h2l, wr)
    logit = pa[:, :LANES] + pa[:, LANES:] + pb[:, :LANES] + pb[:, LANES:] + br_ref[...]

    lane = lax.broadcasted_iota(jnp.int32, (n, LANES), 1).astype(F32)
    big = jnp.float32(1e9)
    gl = jnp.where(lane < N_GROUPS, logit, NEG)
    gmax = jnp.max(gl, axis=-1, keepdims=True)
    gsel = jnp.min(jnp.where(gl == gmax, lane, big), axis=-1, keepdims=True)
    g_p = 1.0 / jnp.sum(jnp.exp(gl - gmax), axis=-1, keepdims=True)
    lo = N_GROUPS + EXPERTS_PER_GROUP * gsel
    el = jnp.where((lane >= lo) & (lane < lo + EXPERTS_PER_GROUP), logit, NEG)
    v1 = jnp.max(el, axis=-1, keepdims=True)
    i1 = jnp.min(jnp.where(el == v1, lane, big), axis=-1, keepdims=True)
    el2 = jnp.where(lane == i1, NEG, el)
    v2 = jnp.max(el2, axis=-1, keepdims=True)
    i2 = jnp.min(jnp.where(el2 == v2, lane, big), axis=-1, keepdims=True)
    ex = jnp.exp(v2 - v1)
    w1 = g_p / (1.0 + ex)
    w2 = g_p * ex / (1.0 + ex)
    return i1 - N_GROUPS, i2 - N_GROUPS, w1, w2, h2h


def _merge_kernel(x_ref, og_ref, o0_ref, o1_ref, o2_ref, s0_ref, s1_ref, s2_ref, g1_ref, wbg_ref, bbg_ref,
                  wpg_ref, wpa_ref, wout_ref, g2_ref, wr_ref, br_ref, tri_ref, upper_ref,
                  x2_ref, xloc_ref, route_ref, cnt_ref, obuf_ref, sbuf_ref):
    tm = x_ref.shape[0]

    for slot, (o_ref, s_ref) in enumerate(((o1_ref, s1_ref), (o2_ref, s2_ref))):
        d = DIL_PATTERNS[slot + 1][1]
        for r in range(d):
            sbuf_ref[slot, pl.ds(r, tm // d, stride=d), :] = s_ref[0, r]
            for c in range(DIL_HEADS):
                obuf_ref[slot, c, pl.ds(r, tm // d, stride=d), :] = o_ref[0, r, :, _head(c)].astype(F32)

    e1, e2, w1, w2, h2b = _merge_rows(x_ref, og_ref, o0_ref, s0_ref, g1_ref, wbg_ref, bbg_ref, wpg_ref, wpa_ref,
                                      wout_ref, g2_ref, wr_ref, br_ref, x2_ref, obuf_ref, sbuf_ref, 0, tm)

    lane = lax.broadcasted_iota(jnp.int32, (tm, LANES), 1).astype(F32)
    oh1 = lane == e1
    oh2 = lane == e2
    onehot = jnp.where(oh1 | oh2, 1.0, 0.0)
    prefix = _dot(tri_ref[...], onehot.astype(BF16))
    cnt = jnp.sum(onehot, axis=0, keepdims=True)
    cnt_pad = jnp.floor((cnt + (CHUNK - 1)) * (1.0 / CHUNK)) * CHUNK
    seg_off = _dot(jnp.broadcast_to(cnt_pad, (8, LANES)).astype(BF16), upper_ref[...])[0:1, :]
    base = seg_off + prefix
    pos1 = jnp.sum(jnp.where(oh1, base, 0.0), axis=-1, keepdims=True)
    pos2 = jnp.sum(jnp.where(oh2, base, 0.0), axis=-1, keepdims=True)
    cnt_ref[...] = jnp.broadcast_to(cnt, cnt_ref.shape)

    route = jnp.zeros((tm, LANES), F32)
    for idx, val in enumerate((e1, e2, w1, w2, pos1, pos2)):
        route = jnp.where(lane == idx, val, route)
    route_ref[...] = route

    route_t = route.T
    slot = lax.broadcasted_iota(jnp.int32, (LOCAL_ROWS, tm), 0).astype(F32)
    perm = jnp.where((slot == route_t[4:5, :]) | (slot == route_t[5:6, :]), 1.0, 0.0).astype(BF16)
    xloc_ref[...] = _pack_pairs(_dot(perm, h2b))


def _merge(x2d, o_gla, outs, stats, g1, wbg, bbg, wpg, wpa, wout, g2, wr, br, tri, upper, S):
    T = x2d.shape[0]
    tm = min(MERGE_TM, S)
    assert TOP_K * tm + N_EXPERTS * CHUNK <= LOCAL_ROWS
    tpb = S // tm
    tok = lambda w: pl.BlockSpec((tm, w), lambda i: (i, 0))
    full = lambda a: pl.BlockSpec(a.shape, lambda i: (0,) * a.ndim)

    def res(gi, w):
        d = DIL_PATTERNS[gi][1]
        return pl.BlockSpec((1, d, tm // d, w), lambda i: (i // tpb, 0, i % tpb, 0))

    return pl.pallas_call(
        _merge_kernel,
        grid=(T // tm,),
        in_specs=[tok(D_MODEL), tok(HEAD_W), tok(HEAD_W), res(1, HEAD_W), res(2, HEAD_W),
                  tok(LANES), res(1, LANES), res(2, LANES),
                  full(g1), full(wbg), full(bbg), full(wpg), full(wpa), full(wout), full(g2),
                  full(wr), full(br), full(tri), full(upper)],
        out_specs=[tok(D_MODEL), pl.BlockSpec((LOCAL_ROWS, PACKED_W), lambda i: (i, 0)), tok(LANES),
                   pl.BlockSpec((8, LANES), lambda i: (i, 0))],
        out_shape=[
            jax.ShapeDtypeStruct((T, D_MODEL), F32),
            jax.ShapeDtypeStruct((T // tm * LOCAL_ROWS, PACKED_W), F32),
            jax.ShapeDtypeStruct((T, LANES), F32),
            jax.ShapeDtypeStruct((T // tm * 8, LANES), F32),
        ],
        scratch_shapes=[pltpu.VMEM((DIL_GROUPS - 1, DIL_HEADS, tm, LANES), F32),
                        pltpu.VMEM((DIL_GROUPS - 1, tm, LANES), F32)],
        compiler_params=_params(("arbitrary",)),
        name="merge",
    )(x2d, o_gla, *outs, *stats, g1, wbg, bbg, wpg, wpa, wout, g2, wr, br, tri, upper)


class _ChunkGather:
    def __init__(self, src_hbm, buf_ref, sem, nchunks):
        self.src, self.buf, self.sem, self.n = src_hbm, buf_ref, sem, nchunks

    def start(self, offs_ref, s):
        for j in range(self.n):
            row = pl.multiple_of(offs_ref[0, 0, j], CHUNK)
            pltpu.make_async_copy(self.src.at[pl.ds(row, CHUNK), :], self.buf.at[s, pl.ds(j * CHUNK, CHUNK), :],
                                  self.sem.at[s]).start()

    def wait(self, s):
        pltpu.make_async_copy(self.src.at[pl.ds(0, self.n * CHUNK), :], self.buf.at[s], self.sem.at[s]).wait()

    def step(self, offs_cur_ref, offs_next_ref):
        i = pl.program_id(0)
        slot = lax.rem(i, 2)

        @pl.when(i == 0)
        def _():
            self.start(offs_cur_ref, 0)

        self.wait(slot)

        @pl.when(i + 1 < pl.num_programs(0))
        def _():
            self.start(offs_next_ref, 1 - slot)

        return slot


def _expert_kernel(tile_e_ref, nact_ref, src_cur_ref, src_next_ref, xloc_ref, wg_ref, wu_ref, wd_ref,
                   y_ref, xbuf_ref, wgu_ref, wdn_ref, gsem):
    i = pl.program_id(0)
    slot = _ChunkGather(xloc_ref, xbuf_ref, gsem, TILE_CHUNKS).step(src_cur_ref, src_next_ref)
    active = i < nact_ref[0]

    @pl.when(jnp.logical_and(active, jnp.logical_or(i == 0, tile_e_ref[i] != tile_e_ref[jnp.maximum(i - 1, 0)])))
    def _():
        wgu_ref[:, :D_EXPERT] = wg_ref[0].astype(BF16)
        wgu_ref[:, D_EXPERT:] = wu_ref[0].astype(BF16)
        wdn_ref[...] = wd_ref[0].astype(BF16)

    @pl.when(active)
    def _():
        xb = _unpack_pairs(xbuf_ref[slot])
        gu = _dot(xb, wgu_ref[...])
        g, u = gu[:, :D_EXPERT], gu[:, D_EXPERT:]
        a = (g * jax.nn.sigmoid(g) * u).astype(BF16)
        y_ref[...] = _pack_pairs(_dot(a, wdn_ref[...]).astype(BF16).astype(F32))

    @pl.when(jnp.logical_not(active))
    def _():
        y_ref[...] = jnp.zeros_like(y_ref)


def _experts(tile_e, nact, csrc, xloc, wg, wu, wd):
    ntiles = tile_e.shape[0]
    R = MOE_TM

    def wsel(i, te, na):
        return (te[jnp.minimum(i, na[0] - 1)], 0, 0)

    def offs(shift):
        return pl.BlockSpec((1, 1, TILE_CHUNKS), lambda i, te, na: (jnp.minimum(i + shift, ntiles - 1), 0, 0),
                            memory_space=pltpu.SMEM)

    return pl.pallas_call(
        _expert_kernel,
        grid_spec=pltpu.PrefetchScalarGridSpec(
            num_scalar_prefetch=2,
            grid=(ntiles,),
            in_specs=[
                offs(0), offs(1),
                pl.BlockSpec(memory_space=pl.ANY),
                pl.BlockSpec((1, D_MODEL, D_EXPERT), wsel),
                pl.BlockSpec((1, D_MODEL, D_EXPERT), wsel),
                pl.BlockSpec((1, D_EXPERT, D_MODEL), wsel),
            ],
            out_specs=pl.BlockSpec((R, PACKED_W), lambda i, te, na: (i, 0)),
            scratch_shapes=[pltpu.VMEM((2, R, PACKED_W), F32), pltpu.VMEM((D_MODEL, 2 * D_EXPERT), BF16),
                            pltpu.VMEM((D_EXPERT, D_MODEL), BF16), pltpu.SemaphoreType.DMA((2,))],
        ),
        out_shape=jax.ShapeDtypeStruct((ntiles * R, PACKED_W), F32),
        compiler_params=_params(("arbitrary",)),
        name="experts",
    )(tile_e, nact, csrc, csrc, xloc, wg, wu, wd)


def _combine_kernel(dst_cur_ref, dst_next_ref, x2_ref, route_ref, y_ref, out_ref, ybuf_ref, gsem):
    slot = _ChunkGather(y_ref, ybuf_ref, gsem, LOCAL_CHUNKS).step(dst_cur_ref, dst_next_ref)
    tm = x2_ref.shape[0]
    route = route_ref[...]
    w1, w2, pos1, pos2 = (route[:, c:c + 1] for c in range(2, 6))
    yl = _unpack_pairs(ybuf_ref[slot])
    cols = lax.broadcasted_iota(jnp.int32, (tm, LOCAL_ROWS), 1).astype(F32)
    pick1 = jnp.where(cols == pos1, 1.0, 0.0).astype(BF16)
    pick2 = jnp.where(cols == pos2, 1.0, 0.0).astype(BF16)
    out_ref[...] = x2_ref[...] + w1 * _dot(pick1, yl) + w2 * _dot(pick2, yl)


def _combine(cdst, x2, route, y):
    T = x2.shape[0]
    nt = cdst.shape[0]
    tm = T // nt

    def offs(shift):
        return pl.BlockSpec((1, 1, LOCAL_CHUNKS), lambda i: (jnp.minimum(i + shift, nt - 1), 0, 0),
                            memory_space=pltpu.SMEM)

    return pl.pallas_call(
        _combine_kernel,
        grid=(nt,),
        in_specs=[
            offs(0), offs(1),
            pl.BlockSpec((tm, D_MODEL), lambda i: (i, 0)),
            pl.BlockSpec((tm, LANES), lambda i: (i, 0)),
            pl.BlockSpec(memory_space=pl.ANY),
        ],
        out_specs=pl.BlockSpec((tm, D_MODEL), lambda i: (i, 0)),
        out_shape=jax.ShapeDtypeStruct((T, D_MODEL), F32),
        scratch_shapes=[pltpu.VMEM((2, LOCAL_ROWS, PACKED_W), F32), pltpu.SemaphoreType.DMA((2,))],
        compiler_params=_params(("arbitrary",)),
        name="combine",
    )(cdst, cdst, x2, route, y)


def _layer(x2d, B, S, norm1_g, w_in, w_gla_a2, b_gla_a, gla_out_norm_g, dil_q_norm_g, dil_k_norm_g,
           w_proj_gla, w_proj_attn, w_branch_gate, b_branch_gate, w_out, norm2_g,
           w_router_group, b_router_group, w_router_expert, b_router_expert, w_gate, w_up, w_down):
    T = B * S
    w_a2 = jnp.pad(w_gla_a2, ((0, LANES - GLA_RANK), (0, 0))).astype(BF16)
    gains = [jnp.ones((HEAD_W,), F32)] * N_GLA_TILES
    for gi in range(DIL_GROUPS):
        gains += [jnp.tile(dil_q_norm_g[gi], DIL_HEADS) * (DIL_DH ** -0.5 * LOG2_E),
                  jnp.tile(dil_k_norm_g[gi], DIL_HEADS), jnp.ones((HEAD_W,), F32)]
    qk_gain = jnp.stack(gains).reshape(N_COL_TILES, 1, HEAD_W)

    gla_in, log_a, qkv0, qkv1, qkv2 = _inproj(
        x2d, norm1_g.reshape(1, -1), w_in.astype(BF16), w_a2, b_gla_a.reshape(1, -1), qk_gain, B, S)

    tri_incl = jnp.asarray(np.tril(np.ones((GLA_BLOCK, GLA_BLOCK), np.float32)), BF16)
    o_gla = _gla(gla_in, log_a, tri_incl, gla_out_norm_g.reshape(1, -1), B, S)

    outs, stats = [], []
    for gi, qkv in enumerate((qkv0.reshape(B, 1, S, QKV_W), qkv1, qkv2)):
        o, st = _dilated(qkv, _alibi_bias(gi), gi)
        outs.append(o)
        stats.append(st)
    outs[0] = outs[0].reshape(T, HEAD_W)
    stats[0] = stats[0].reshape(T, LANES)

    w_r = jnp.pad(jnp.concatenate([w_router_group, w_router_expert], axis=1),
                  ((0, 0), (0, LANES - N_GROUPS - N_EXPERTS)))
    w_rh = w_r.astype(BF16)
    w_rl = (w_r - w_rh.astype(F32)).astype(BF16)
    b_r = jnp.pad(jnp.concatenate([b_router_group, b_router_expert]), (0, LANES - N_GROUPS - N_EXPERTS))
    tm = min(MERGE_TM, S)
    tri_strict = jnp.asarray(np.tril(np.ones((tm, tm), np.float32), -1), BF16)
    upper = jnp.asarray(np.triu(np.ones((LANES, LANES), np.float32), 1), BF16)
    x2, xloc, route, cnt = _merge(
        x2d, o_gla, outs, stats, norm1_g.reshape(1, -1), w_branch_gate.astype(BF16),
        b_branch_gate.reshape(1, -1), w_proj_gla.astype(BF16), w_proj_attn.astype(BF16), w_out.astype(BF16),
        norm2_g.reshape(1, -1), jnp.concatenate([w_rh, w_rl], axis=1), b_r.reshape(1, -1), tri_strict, upper, S)

    csrc, cdst, tile_e, nact = _chunk_tables(cnt, T // tm, tm)
    y = _experts(tile_e, nact, csrc, xloc, w_gate, w_up, w_down)
    return _combine(cdst, x2, route, y)


def _chunk_tables(cnt, nt, tm):
    i32 = jnp.int32
    experts = jnp.arange(N_EXPERTS, dtype=i32)
    tiles = jnp.arange(nt, dtype=i32)
    c = (cnt.reshape(nt, 8, LANES)[:, 0, :N_EXPERTS].astype(i32) + CHUNK - 1) // CHUNK
    loff = jnp.cumsum(c, axis=1) - c
    ecum = jnp.cumsum(c, axis=0) - c
    tot = jnp.sum(c, axis=0)
    ptot = (tot + TILE_CHUNKS - 1) // TILE_CHUNKS * TILE_CHUNKS
    pend = jnp.cumsum(ptot)
    pstart = pend - ptot

    lc = jnp.arange(LOCAL_CHUNKS, dtype=i32)
    e_of = jnp.sum(((loff + c)[:, None, :] <= lc[None, :, None]).astype(i32), axis=-1)
    pick_e = e_of[:, :, None] == experts[None, None, :]
    glob = jnp.sum(jnp.where(pick_e, (pstart[None, :] + ecum - loff)[:, None, :], 0), axis=-1) + lc[None, :]
    cdst = jnp.where(e_of < N_EXPERTS, glob, 0) * CHUNK

    max_chunks = (TOP_K * nt * tm + (CHUNK - 1) * nt * N_EXPERTS) // CHUNK + N_EXPERTS * (TILE_CHUNKS - 1)
    ntiles = -(-max_chunks // TILE_CHUNKS)
    g = jnp.arange(ntiles * TILE_CHUNKS, dtype=i32)
    e_g = jnp.minimum(jnp.sum((pend[None, :] <= g[:, None]).astype(i32), axis=1), N_EXPERTS - 1)
    k = g - jnp.sum(jnp.where(e_g[:, None] == experts[None, :], pstart[None, :], 0), axis=1)
    pick = (e_g[:, None] == experts[None, :]).astype(F32)
    ends = jnp.dot(pick, (ecum + c).T.astype(F32), precision=lax.Precision.HIGHEST)
    t_g = jnp.sum((ends <= k[:, None].astype(F32)).astype(i32), axis=1)
    valid = t_g < nt
    base = jnp.dot(pick, (tiles[:, None] * LOCAL_CHUNKS + loff - ecum).T.astype(F32),
                   precision=lax.Precision.HIGHEST)
    in_tile = tiles[None, :] == jnp.minimum(t_g, nt - 1)[:, None]
    local = jnp.sum(jnp.where(in_tile, base, 0.0), axis=1).astype(i32) + k
    zero_chunk = LOCAL_CHUNKS - 1
    csrc = jnp.where(valid, local, zero_chunk) * CHUNK

    tile_e = e_g[::TILE_CHUNKS]
    nact = (pend[-1:] // TILE_CHUNKS).astype(i32)
    return (csrc.reshape(ntiles, 1, TILE_CHUNKS), cdst.reshape(nt, 1, LOCAL_CHUNKS), tile_e, nact)


def kernel(x, norm1_g, w_in, w_gla_a2, b_gla_a, gla_out_norm_g, dil_q_norm_g, dil_k_norm_g, w_proj_gla,
           w_proj_attn, w_branch_gate, b_branch_gate, w_out, norm2_g, w_router_group, b_router_group,
           w_router_expert, b_router_expert, w_gate, w_up, w_down):
    B, S, D = x.shape
    assert D == D_MODEL and S % (DIL_BLOCK * DIL_PATTERNS[-1][1]) == 0
    x2d = x.reshape(B * S, D)
    params = (norm1_g, w_in, w_gla_a2, b_gla_a, gla_out_norm_g, dil_q_norm_g, dil_k_norm_g, w_proj_gla,
              w_proj_attn, w_branch_gate, b_branch_gate, w_out, norm2_g, w_router_group, b_router_group,
              w_router_expert, b_router_expert, w_gate, w_up, w_down)
    for layer in range(norm1_g.shape[0]):
        x2d = _layer(x2d, B, S, *(p[layer] for p in params))
    return x2d.reshape(B, S, D)
```

```python
import functools

import jax
import jax.numpy as jnp
import numpy as np
from jax import lax
from jax.experimental import pallas as pl
from jax.experimental.pallas import tpu as pltpu

F32 = jnp.float32
BF16 = jnp.bfloat16

D_MODEL = 1024
EPS = 1e-6
GLA_HEADS = 4
GLA_DK = 128
GLA_RANK = 16
GLA_TAU = 16.0
LOG2_E = 1.4426950408889634
GLA_SUBCHUNK = 16
GLA_BLOCK = 128
GLA_STEP_BLOCKS = 16
DIL_PATTERNS = ((128, 1), (512, 4), (2048, 16))
DIL_GROUPS = 3
DIL_HEADS = 4
DIL_DH = 128
DIL_BLOCK = 128
ALIBI_SLOPES = tuple(2.0 ** (-8.0 * (i + 1) / (DIL_GROUPS * DIL_HEADS)) for i in range(DIL_GROUPS * DIL_HEADS))
N_GROUPS = 4
EXPERTS_PER_GROUP = 8
N_EXPERTS = N_GROUPS * EXPERTS_PER_GROUP
TOP_K = 2
D_EXPERT = 512

HEAD_W = 512
N_GLA_TILES = 4
N_COL_TILES = N_GLA_TILES + 3 * DIL_GROUPS
QKV_W = 3 * HEAD_W
LANES = 128
NEG = -1e30

INPROJ_TM = 512
MERGE_TM = 512
MOE_TM = 512
CHUNK = 8
TILE_CHUNKS = MOE_TM // CHUNK
LOCAL_ROWS = -(-(TOP_K * MERGE_TM + N_EXPERTS * CHUNK) // LANES) * LANES
LOCAL_CHUNKS = LOCAL_ROWS // CHUNK
DIL_STEP_BLOCKS = 32
VMEM_LIMIT = 56 * 1024 * 1024


def _dot(a, b):
    return jnp.dot(a, b, preferred_element_type=F32)


def _dot_nt(a, b):
    return lax.dot_general(a, b, (((1,), (1,)), ((), ())), preferred_element_type=F32)


def _dot_tn(a, b):
    return lax.dot_general(a, b, (((0,), (0,)), ((), ())), preferred_element_type=F32)


def _rms(x, g):
    return x * lax.rsqrt(jnp.mean(x * x, axis=-1, keepdims=True) + EPS) * g


def _params(sem):
    return pltpu.CompilerParams(dimension_semantics=sem, vmem_limit_bytes=VMEM_LIMIT)


def _head(c):
    return slice(c * LANES, (c + 1) * LANES)


PACKED_W = D_MODEL // 2
_HI16 = 0xFFFF0000


def _pack_pairs(x):
    u = lax.bitcast_convert_type(x, jnp.uint32)
    w = x.shape[1] // 2
    return lax.bitcast_convert_type((u[:, :w] & jnp.uint32(_HI16)) | (u[:, w:] >> 16), F32)


def _unpack_pairs(p):
    u = lax.bitcast_convert_type(p, jnp.uint32)
    hi = lax.bitcast_convert_type(u & jnp.uint32(_HI16), F32)
    lo = lax.bitcast_convert_type(u << 16, F32)
    return jnp.concatenate([hi, lo], axis=1).astype(BF16)


def _inproj_kernel(x_ref, g1_ref, w_ref, wa2_ref, ba_ref, qkg_ref,
                   gla_ref, loga_ref, d0_ref, d1_ref, d2_ref, h_ref, ybuf_ref, wdil_ref):
    tm = x_ref.shape[0]
    n_gla = N_GLA_TILES * HEAD_W

    @pl.when(pl.program_id(0) == 0)
    def _():
        for k in range(3 * DIL_GROUPS):
            first = n_gla + GLA_RANK + k * HEAD_W
            wdil_ref[:, k * HEAD_W:(k + 1) * HEAD_W] = w_ref[:, first:first + HEAD_W]

    hb = _rms(x_ref[...], g1_ref[...]).astype(BF16)
    h_ref[...] = hb
    ga = _dot(hb, w_ref[:, n_gla:n_gla + LANES])
    z = _dot(ga.astype(BF16), wa2_ref[...]) + ba_ref[...]
    log_sig = jnp.minimum(z, 0.0) - jnp.log(1.0 + jnp.exp(-jnp.abs(z)))
    loga_ref[...] = log_sig * (LOG2_E / GLA_TAU)

    dil_refs = (d0_ref, d1_ref, d2_ref)
    n_strided = 0
    for jj in range(N_COL_TILES):
        cols = slice(jj * HEAD_W, (jj + 1) * HEAD_W)
        if jj < N_GLA_TILES:
            w = w_ref[:, cols]
        else:
            w = wdil_ref[:, (jj - N_GLA_TILES) * HEAD_W:(jj - N_GLA_TILES + 1) * HEAD_W]
        y = _dot(h_ref[...], w)
        if jj < N_GLA_TILES:
            gla_ref[:, cols] = y.astype(BF16)
            continue
        gi, kind = divmod(jj - N_GLA_TILES, 3)
        d = DIL_PATTERNS[gi][1]
        ocols = [slice(kind * HEAD_W + c * LANES, kind * HEAD_W + (c + 1) * LANES) for c in range(DIL_HEADS)]
        if kind < 2:
            g = qkg_ref[jj]
            slabs = [_rms(y[:, _head(c)], g[:, _head(c)]) for c in range(DIL_HEADS)]
        else:
            slabs = [y[:, _head(c)] for c in range(DIL_HEADS)]
        out = dil_refs[gi]
        if d == 1:
            for c in range(DIL_HEADS):
                out[:, ocols[c]] = slabs[c].astype(BF16)
            continue
        buf = n_strided % ybuf_ref.shape[0]
        n_strided += 1
        for c in range(DIL_HEADS):
            ybuf_ref[buf, c] = slabs[c]
        for r in range(d):
            for c in range(DIL_HEADS):
                out[0, r, :, ocols[c]] = ybuf_ref[buf, c, pl.ds(r, tm // d, stride=d), :].astype(BF16)


def _resident(shape):
    return pl.BlockSpec(shape, lambda i: (0,) * len(shape), pipeline_mode=pl.Buffered(1))


def _inproj(x2d, g1, w_all, w_a2, b_a, qk_gain, B, S):
    T = x2d.shape[0]
    tm = min(INPROJ_TM, S)
    tpb = S // tm

    def dil_spec(gi):
        d = DIL_PATTERNS[gi][1]
        if d == 1:
            return pl.BlockSpec((tm, QKV_W), lambda i: (i, 0))
        return pl.BlockSpec((1, d, tm // d, QKV_W), lambda i: (i // tpb, 0, i % tpb, 0))

    def dil_shape(gi):
        d = DIL_PATTERNS[gi][1]
        shape = (T, QKV_W) if d == 1 else (B, d, S // d, QKV_W)
        return jax.ShapeDtypeStruct(shape, BF16)

    return pl.pallas_call(
        _inproj_kernel,
        grid=(T // tm,),
        in_specs=[
            pl.BlockSpec((tm, D_MODEL), lambda i: (i, 0)),
            _resident(g1.shape), _resident(w_all.shape), _resident(w_a2.shape),
            _resident(b_a.shape), _resident(qk_gain.shape),
        ],
        out_specs=[
            pl.BlockSpec((tm, N_GLA_TILES * HEAD_W), lambda i: (i, 0)),
            pl.BlockSpec((tm, HEAD_W), lambda i: (i, 0)),
            dil_spec(0), dil_spec(1), dil_spec(2),
        ],
        out_shape=[
            jax.ShapeDtypeStruct((T, N_GLA_TILES * HEAD_W), BF16),
            jax.ShapeDtypeStruct((T, HEAD_W), F32),
            dil_shape(0), dil_shape(1), dil_shape(2),
        ],
        scratch_shapes=[pltpu.VMEM((tm, D_MODEL), BF16), pltpu.VMEM((2, DIL_HEADS, tm, LANES), F32),
                        pltpu.VMEM((D_MODEL, 3 * DIL_GROUPS * HEAD_W), BF16)],
        compiler_params=_params(("arbitrary",)),
        name="inproj",
    )(x2d, g1, w_all, w_a2, b_a, qk_gain)


def _gla_kernel(q_ref, k_ref, v_ref, r_ref, la_ref, tri_ref, gn_ref, o_ref, st_ref):
    n = pl.program_id(1)

    @pl.when(n == 0)
    def _():
        st_ref[...] = jnp.zeros_like(st_ref)

    for blk in range(q_ref.shape[0] // GLA_BLOCK):
        _gla_block(q_ref, k_ref, v_ref, r_ref, la_ref, tri_ref, gn_ref, o_ref, st_ref,
                   slice(blk * GLA_BLOCK, (blk + 1) * GLA_BLOCK))


def _gla_block(q_ref, k_ref, v_ref, r_ref, la_ref, tri_ref, gn_ref, o_ref, st_ref, rows):
    L = GLA_BLOCK
    la = la_ref[rows, :]
    hi = la.astype(BF16)
    r1 = la - hi.astype(F32)
    mid = r1.astype(BF16)
    lo = (r1 - mid.astype(F32)).astype(BF16)
    tri = tri_ref[...]
    b_all = _dot(tri, hi) + _dot(tri, mid) + _dot(tri, lo)

    row = lax.broadcasted_iota(jnp.int32, (L, L), 0)
    col = lax.broadcasted_iota(jnp.int32, (L, L), 1)

    for h in range(GLA_HEADS):
        sl = _head(h)
        b = b_all[:, sl]
        q = q_ref[rows, sl].astype(F32) * (GLA_DK ** -0.5)
        k = k_ref[rows, sl].astype(F32)
        v = v_ref[rows, sl]

        att = jnp.zeros((L, L), F32)
        seg = L
        while seg > GLA_SUBCHUNK:
            half = seg // 2
            beta = jnp.concatenate(
                [jnp.broadcast_to(b[s + half - 1:s + half, :], (seg, GLA_DK)) for s in range(0, L, seg)], axis=0)
            rel = b - beta
            neg = jnp.minimum(rel, 0.0)
            qs = (q * jnp.exp2(neg)).astype(BF16)
            ks = (k * jnp.exp2(neg - rel)).astype(BF16)
            same = (row // seg) == (col // seg)
            mask = same & ((row % seg) >= half) & ((col % seg) < half)
            att = jnp.where(mask, _dot_nt(qs, ks), att)
            seg = half
        beta = jnp.concatenate(
            [jnp.zeros((GLA_SUBCHUNK, GLA_DK), F32)]
            + [jnp.broadcast_to(b[s - 1:s, :], (GLA_SUBCHUNK, GLA_DK)) for s in range(GLA_SUBCHUNK, L, GLA_SUBCHUNK)],
            axis=0)
        qs = (q * jnp.exp2(b - beta)).astype(BF16)
        ks = (k * jnp.exp2(beta - b)).astype(BF16)
        mask = ((row // GLA_SUBCHUNK) == (col // GLA_SUBCHUNK)) & (col <= row)
        att = jnp.where(mask, _dot_nt(qs, ks), att)

        state_t = st_ref[h]
        q0 = (q * jnp.exp2(b)).astype(BF16)
        o = _dot(att.astype(BF16), v) + _dot_nt(q0, state_t.astype(BF16))

        b_last = b[L - 1:L, :]
        k_end = (k * jnp.exp2(b_last - b)).astype(BF16)
        st_ref[h] = state_t * jnp.exp2(b_last) + _dot_tn(v, k_end)

        rr = r_ref[rows, sl].astype(F32)
        o_ref[rows, sl] = (_rms(o, gn_ref[...]) * (rr * jax.nn.sigmoid(rr))).astype(BF16)


def _gla(proj, log_a, tri, gn, B, S):
    L = GLA_STEP_BLOCKS * GLA_BLOCK
    nb = S // L
    return pl.pallas_call(
        _gla_kernel,
        grid=(B, nb),
        in_specs=[
            pl.BlockSpec((L, HEAD_W), lambda b, n: (b * nb + n, 0)),
            pl.BlockSpec((L, HEAD_W), lambda b, n: (b * nb + n, 1)),
            pl.BlockSpec((L, HEAD_W), lambda b, n: (b * nb + n, 2)),
            pl.BlockSpec((L, HEAD_W), lambda b, n: (b * nb + n, 3)),
            pl.BlockSpec((L, HEAD_W), lambda b, n: (b * nb + n, 0)),
            pl.BlockSpec((GLA_BLOCK, GLA_BLOCK), lambda b, n: (0, 0)),
            pl.BlockSpec((1, GLA_DK), lambda b, n: (0, 0)),
        ],
        out_specs=pl.BlockSpec((L, HEAD_W), lambda b, n: (b * nb + n, 0)),
        out_shape=jax.ShapeDtypeStruct((B * S, HEAD_W), BF16),
        scratch_shapes=[pltpu.VMEM((GLA_HEADS, GLA_DK, GLA_DK), F32)],
        compiler_params=_params(("arbitrary", "arbitrary")),
        name="gla",
    )(proj, proj, proj, proj, log_a, tri, gn)


def _dil_kernel(q_ref, kp_ref, kc_ref, vp_ref, vc_ref, bias_ref, o_ref, st_ref, *, nr, nq):
    first = pl.program_id(2) == 0
    Q = DIL_BLOCK
    col = lax.broadcasted_iota(jnp.int32, (Q, 2 * Q), 1)
    lane = lax.broadcasted_iota(jnp.int32, (Q, LANES), 1)
    for res in range(nr):
        for blk in range(nq):
            rows = slice(blk * Q, (blk + 1) * Q)
            prev = slice((blk - 1) * Q, blk * Q)
            stats = jnp.zeros((Q, LANES), F32)
            for h in range(DIL_HEADS):
                sl = _head(h)
                if blk == 0:
                    kprev, vprev = kp_ref[0, res, :, sl], vp_ref[0, res, :, sl]
                else:
                    kprev, vprev = kc_ref[0, res, prev, sl], vc_ref[0, res, prev, sl]
                kk = jnp.concatenate([kprev, kc_ref[0, res, rows, sl]], axis=0)
                vv = jnp.concatenate([vprev, vc_ref[0, res, rows, sl]], axis=0)
                s = _dot_nt(q_ref[0, res, rows, sl], kk) + bias_ref[h]
                if blk == 0:
                    s = jnp.where(jnp.logical_and(first, col < Q), NEG, s)
                m = jnp.max(s, axis=-1, keepdims=True)
                p = jnp.exp2(s - m)
                den = jnp.sum(p, axis=-1, keepdims=True)
                o_ref[0, res, rows, sl] = (_dot(p.astype(BF16), vv) / den).astype(BF16)
                stats = jnp.where((lane // 16) == h, m, stats)
                stats = jnp.where((lane // 16) == DIL_HEADS + h, den, stats)
            st_ref[0, res, rows, :] = stats


def _dilated(qkv, bias, gi):
    B, d, Lsub, _ = qkv.shape
    nq = min(DIL_STEP_BLOCKS, Lsub // DIL_BLOCK)
    nr = min(DIL_STEP_BLOCKS // nq, d)
    rows = nq * DIL_BLOCK

    def cur(c):
        return pl.BlockSpec((1, nr, rows, HEAD_W), lambda b, r, i: (b, r, i, c))

    def prev(c):
        return pl.BlockSpec((1, nr, DIL_BLOCK, HEAD_W), lambda b, r, i: (b, r, jnp.maximum(i * nq - 1, 0), c))

    return pl.pallas_call(
        functools.partial(_dil_kernel, nr=nr, nq=nq),
        grid=(B, d // nr, Lsub // rows),
        in_specs=[cur(0), prev(1), cur(1), prev(2), cur(2),
                  pl.BlockSpec((DIL_HEADS, DIL_BLOCK, 2 * DIL_BLOCK), lambda b, r, i: (0, 0, 0))],
        out_specs=[
            pl.BlockSpec((1, nr, rows, HEAD_W), lambda b, r, i: (b, r, i, 0)),
            pl.BlockSpec((1, nr, rows, LANES), lambda b, r, i: (b, r, i, 0)),
        ],
        out_shape=[
            jax.ShapeDtypeStruct((B, d, Lsub, HEAD_W), BF16),
            jax.ShapeDtypeStruct((B, d, Lsub, LANES), F32),
        ],
        compiler_params=_params(("arbitrary", "arbitrary", "arbitrary")),
        name=f"dilated{gi}",
    )(qkv, qkv, qkv, qkv, qkv, bias)


def _alibi_bias(gi):
    window, d = DIL_PATTERNS[gi]
    Q = DIL_BLOCK
    dist = np.arange(Q)[:, None] + Q - np.arange(2 * Q)[None, :]
    valid = (dist >= 0) & (dist <= window // d)
    slopes = np.asarray(ALIBI_SLOPES, np.float32).reshape(DIL_GROUPS, DIL_HEADS)[gi]
    bias = -slopes[:, None, None] * (d * dist).astype(np.float32) * np.float32(LOG2_E)
    return jnp.asarray(np.where(valid[None], bias, np.float32(NEG)), F32)


def _merge_rows(x_ref, og_ref, o0_ref, s0_ref, g1_ref, wbg_ref, bbg_ref, wpg_ref, wpa_ref, wout_ref,
                g2_ref, wr_ref, br_ref, x2_ref, obuf_ref, sbuf_ref, row0, n):
    rows = slice(row0, row0 + n)
    x = x_ref[rows, :]
    hb = _rms(x, g1_ref[...]).astype(BF16)
    gates = jax.nn.sigmoid(_dot(hb, wbg_ref[...]) + bbg_ref[...])

    stats = (s0_ref[rows, :], sbuf_ref[0, rows, :], sbuf_ref[1, rows, :])
    dens = [pltpu.roll(s, 64, 1) for s in stats]
    m_all = jnp.maximum(jnp.maximum(stats[0], stats[1]), stats[2])
    wts = [d * jnp.exp2(s - m_all) for s, d in zip(stats, dens)]
    inv = 1.0 / (wts[0] + wts[1] + wts[2])
    coef = [w * inv for w in wts]
    heads = []
    for h in range(DIL_HEADS):
        group_out = (o0_ref[rows, _head(h)].astype(F32), obuf_ref[0, h, rows, :], obuf_ref[1, h, rows, :])
        acc = jnp.zeros((n, DIL_DH), F32)
        for g in range(DIL_GROUPS):
            c = jnp.broadcast_to(coef[g][:, 16 * h:16 * h + 1], (n, DIL_DH))
            acc = acc + c * group_out[g]
        heads.append(acc.astype(BF16))
    o_att = jnp.concatenate(heads, axis=1)

    y = gates[:, :D_MODEL] * _dot(og_ref[rows, :], wpg_ref[...]) + gates[:, D_MODEL:] * _dot(o_att, wpa_ref[...])
    x2 = x + _dot(y.astype(BF16), wout_ref[...])
    x2_ref[rows, :] = x2

    h2 = _rms(x2, g2_ref[...])

    h2h = h2.astype(BF16)
    h2l = (h2 - h2h.astype(F32)).astype(BF16)
    wr = wr_ref[...]
    pa = _dot(h2h, wr)
    pb = _dot(h2l, wr)
    logit = pa[:, :LANES] + pa[:, LANES:] + pb[:, :LANES] + pb[:, LANES:] + br_ref[...]

    lane = lax.broadcasted_iota(jnp.int32, (n, LANES), 1).astype(F32)
    big = jnp.float32(1e9)
    gl = jnp.where(lane < N_GROUPS, logit, NEG)
    gmax = jnp.max(gl, axis=-1, keepdims=True)
    gsel = jnp.min(jnp.where(gl == gmax, lane, big), axis=-1, keepdims=True)
    g_p = 1.0 / jnp.sum(jnp.exp(gl - gmax), axis=-1, keepdims=True)
    lo = N_GROUPS + EXPERTS_PER_GROUP * gsel
    el = jnp.where((lane >= lo) & (lane < lo + EXPERTS_PER_GROUP), logit, NEG)
    v1 = jnp.max(el, axis=-1, keepdims=True)
    i1 = jnp.min(jnp.where(el == v1, lane, big), axis=-1, keepdims=True)
    el2 = jnp.where(lane == i1, NEG, el)
    v2 = jnp.max(el2, axis=-1, keepdims=True)
    i2 = jnp.min(jnp.where(el2 == v2, lane, big), axis=-1, keepdims=True)
    ex = jnp.exp(v2 - v1)
    w1 = g_p / (1.0 + ex)
    w2 = g_p * ex / (1.0 + ex)
    return i1 - N_GROUPS, i2 - N_GROUPS, w1, w2, h2h


def _merge_kernel(x_ref, og_ref, o0_ref, o1_ref, o2_ref, s0_ref, s1_ref, s2_ref, g1_ref, wbg_ref, bbg_ref,
                  wpg_ref, wpa_ref, wout_ref, g2_ref, wr_ref, br_ref, tri_ref, upper_ref,
                  x2_ref, xloc_ref, route_ref, cnt_ref, obuf_ref, sbuf_ref):
    tm = x_ref.shape[0]

    for slot, (o_ref, s_ref) in enumerate(((o1_ref, s1_ref), (o2_ref, s2_ref))):
        d = DIL_PATTERNS[slot + 1][1]
        for r in range(d):
            sbuf_ref[slot, pl.ds(r, tm // d, stride=d), :] = s_ref[0, r]
            for c in range(DIL_HEADS):
                obuf_ref[slot, c, pl.ds(r, tm // d, stride=d), :] = o_ref[0, r, :, _head(c)].astype(F32)

    e1, e2, w1, w2, h2b = _merge_rows(x_ref, og_ref, o0_ref, s0_ref, g1_ref, wbg_ref, bbg_ref, wpg_ref, wpa_ref,
                                      wout_ref, g2_ref, wr_ref, br_ref, x2_ref, obuf_ref, sbuf_ref, 0, tm)

    lane = lax.broadcasted_iota(jnp.int32, (tm, LANES), 1).astype(F32)
    oh1 = lane == e1
    oh2 = lane == e2
    onehot = jnp.where(oh1 | oh2, 1.0, 0.0)
    prefix = _dot(tri_ref[...], onehot.astype(BF16))
    cnt = jnp.sum(onehot, axis=0, keepdims=True)
    cnt_pad = jnp.floor((cnt + (CHUNK - 1)) * (1.0 / CHUNK)) * CHUNK
    seg_off = _dot(jnp.broadcast_to(cnt_pad, (8, LANES)).astype(BF16), upper_ref[...])[0:1, :]
    base = seg_off + prefix
    pos1 = jnp.sum(jnp.where(oh1, base, 0.0), axis=-1, keepdims=True)
    pos2 = jnp.sum(jnp.where(oh2, base, 0.0), axis=-1, keepdims=True)
    cnt_ref[...] = jnp.broadcast_to(cnt, cnt_ref.shape)

    route = jnp.zeros((tm, LANES), F32)
    for idx, val in enumerate((e1, e2, w1, w2, pos1, pos2)):
        route = jnp.where(lane == idx, val, route)
    route_ref[...] = route

    route_t = route.T
    slot = lax.broadcasted_iota(jnp.int32, (LOCAL_ROWS, tm), 0).astype(F32)
    perm = jnp.where((slot == route_t[4:5, :]) | (slot == route_t[5:6, :]), 1.0, 0.0).astype(BF16)
    xloc_ref[...] = _pack_pairs(_dot(perm, h2b))


def _merge(x2d, o_gla, outs, stats, g1, wbg, bbg, wpg, wpa, wout, g2, wr, br, tri, upper, S):
    T = x2d.shape[0]
    tm = min(MERGE_TM, S)
    assert TOP_K * tm + N_EXPERTS * CHUNK <= LOCAL_ROWS
    tpb = S // tm
    tok = lambda w: pl.BlockSpec((tm, w), lambda i: (i, 0))
    full = lambda a: pl.BlockSpec(a.shape, lambda i: (0,) * a.ndim)

    def res(gi, w):
        d = DIL_PATTERNS[gi][1]
        return pl.BlockSpec((1, d, tm // d, w), lambda i: (i // tpb, 0, i % tpb, 0))

    return pl.pallas_call(
        _merge_kernel,
        grid=(T // tm,),
        in_specs=[tok(D_MODEL), tok(HEAD_W), tok(HEAD_W), res(1, HEAD_W), res(2, HEAD_W),
                  tok(LANES), res(1, LANES), res(2, LANES),
                  full(g1), full(wbg), full(bbg), full(wpg), full(wpa), full(wout), full(g2),
                  full(wr), full(br), full(tri), full(upper)],
        out_specs=[tok(D_MODEL), pl.BlockSpec((LOCAL_ROWS, PACKED_W), lambda i: (i, 0)), tok(LANES),
                   pl.BlockSpec((8, LANES), lambda i: (i, 0))],
        out_shape=[
            jax.ShapeDtypeStruct((T, D_MODEL), F32),
            jax.ShapeDtypeStruct((T // tm * LOCAL_ROWS, PACKED_W), F32),
            jax.ShapeDtypeStruct((T, LANES), F32),
            jax.ShapeDtypeStruct((T // tm * 8, LANES), F32),
        ],
        scratch_shapes=[pltpu.VMEM((DIL_GROUPS - 1, DIL_HEADS, tm, LANES), F32),
                        pltpu.VMEM((DIL_GROUPS - 1, tm, LANES), F32)],
        compiler_params=_params(("arbitrary",)),
        name="merge",
    )(x2d, o_gla, *outs, *stats, g1, wbg, bbg, wpg, wpa, wout, g2, wr, br, tri, upper)


class _ChunkGather:
    def __init__(self, src_hbm, buf_ref, sem, nchunks):
        self.src, self.buf, self.sem, self.n = src_hbm, buf_ref, sem, nchunks

    def start(self, offs_ref, s):
        for j in range(self.n):
            row = pl.multiple_of(offs_ref[0, 0, j], CHUNK)
            pltpu.make_async_copy(self.src.at[pl.ds(row, CHUNK), :], self.buf.at[s, pl.ds(j * CHUNK, CHUNK), :],
                                  self.sem.at[s]).start()

    def wait(self, s):
        pltpu.make_async_copy(self.src.at[pl.ds(0, self.n * CHUNK), :], self.buf.at[s], self.sem.at[s]).wait()

    def step(self, offs_cur_ref, offs_next_ref):
        i = pl.program_id(0)
        slot = lax.rem(i, 2)

        @pl.when(i == 0)
        def _():
            self.start(offs_cur_ref, 0)

        self.wait(slot)

        @pl.when(i + 1 < pl.num_programs(0))
        def _():
            self.start(offs_next_ref, 1 - slot)

        return slot


def _expert_kernel(tile_e_ref, nact_ref, src_cur_ref, src_next_ref, xloc_ref, wg_ref, wu_ref, wd_ref,
                   y_ref, xbuf_ref, wgu_ref, wdn_ref, gsem):
    i = pl.program_id(0)
    slot = _ChunkGather(xloc_ref, xbuf_ref, gsem, TILE_CHUNKS).step(src_cur_ref, src_next_ref)
    active = i < nact_ref[0]

    @pl.when(jnp.logical_and(active, jnp.logical_or(i == 0, tile_e_ref[i] != tile_e_ref[jnp.maximum(i - 1, 0)])))
    def _():
        wgu_ref[:, :D_EXPERT] = wg_ref[0].astype(BF16)
        wgu_ref[:, D_EXPERT:] = wu_ref[0].astype(BF16)
        wdn_ref[...] = wd_ref[0].astype(BF16)

    @pl.when(active)
    def _():
        xb = _unpack_pairs(xbuf_ref[slot])
        gu = _dot(xb, wgu_ref[...])
        g, u = gu[:, :D_EXPERT], gu[:, D_EXPERT:]
        a = (g * jax.nn.sigmoid(g) * u).astype(BF16)
        y_ref[...] = _pack_pairs(_dot(a, wdn_ref[...]).astype(BF16).astype(F32))

    @pl.when(jnp.logical_not(active))
    def _():
        y_ref[...] = jnp.zeros_like(y_ref)


def _experts(tile_e, nact, csrc, xloc, wg, wu, wd):
    ntiles = tile_e.shape[0]
    R = MOE_TM

    def wsel(i, te, na):
        return (te[jnp.minimum(i, na[0] - 1)], 0, 0)

    def offs(shift):
        return pl.BlockSpec((1, 1, TILE_CHUNKS), lambda i, te, na: (jnp.minimum(i + shift, ntiles - 1), 0, 0),
                            memory_space=pltpu.SMEM)

    return pl.pallas_call(
        _expert_kernel,
        grid_spec=pltpu.PrefetchScalarGridSpec(
            num_scalar_prefetch=2,
            grid=(ntiles,),
            in_specs=[
                offs(0), offs(1),
                pl.BlockSpec(memory_space=pl.ANY),
                pl.BlockSpec((1, D_MODEL, D_EXPERT), wsel),
                pl.BlockSpec((1, D_MODEL, D_EXPERT), wsel),
                pl.BlockSpec((1, D_EXPERT, D_MODEL), wsel),
            ],
            out_specs=pl.BlockSpec((R, PACKED_W), lambda i, te, na: (i, 0)),
            scratch_shapes=[pltpu.VMEM((2, R, PACKED_W), F32), pltpu.VMEM((D_MODEL, 2 * D_EXPERT), BF16),
                            pltpu.VMEM((D_EXPERT, D_MODEL), BF16), pltpu.SemaphoreType.DMA((2,))],
        ),
        out_shape=jax.ShapeDtypeStruct((ntiles * R, PACKED_W), F32),
        compiler_params=_params(("arbitrary",)),
        name="experts",
    )(tile_e, nact, csrc, csrc, xloc, wg, wu, wd)


def _combine_kernel(dst_cur_ref, dst_next_ref, x2_ref, route_ref, y_ref, out_ref, ybuf_ref, gsem):
    slot = _ChunkGather(y_ref, ybuf_ref, gsem, LOCAL_CHUNKS).step(dst_cur_ref, dst_next_ref)
    tm = x2_ref.shape[0]
    route = route_ref[...]
    w1, w2, pos1, pos2 = (route[:, c:c + 1] for c in range(2, 6))
    yl = _unpack_pairs(ybuf_ref[slot])
    cols = lax.broadcasted_iota(jnp.int32, (tm, LOCAL_ROWS), 1).astype(F32)
    pick1 = jnp.where(cols == pos1, 1.0, 0.0).astype(BF16)
    pick2 = jnp.where(cols == pos2, 1.0, 0.0).astype(BF16)
    out_ref[...] = x2_ref[...] + w1 * _dot(pick1, yl) + w2 * _dot(pick2, yl)


def _combine(cdst, x2, route, y):
    T = x2.shape[0]
    nt = cdst.shape[0]
    tm = T // nt

    def offs(shift):
        return pl.BlockSpec((1, 1, LOCAL_CHUNKS), lambda i: (jnp.minimum(i + shift, nt - 1), 0, 0),
                            memory_space=pltpu.SMEM)

    return pl.pallas_call(
        _combine_kernel,
        grid=(nt,),
        in_specs=[
            offs(0), offs(1),
            pl.BlockSpec((tm, D_MODEL), lambda i: (i, 0)),
            pl.BlockSpec((tm, LANES), lambda i: (i, 0)),
            pl.BlockSpec(memory_space=pl.ANY),
        ],
        out_specs=pl.BlockSpec((tm, D_MODEL), lambda i: (i, 0)),
        out_shape=jax.ShapeDtypeStruct((T, D_MODEL), F32),
        scratch_shapes=[pltpu.VMEM((2, LOCAL_ROWS, PACKED_W), F32), pltpu.SemaphoreType.DMA((2,))],
        compiler_params=_params(("arbitrary",)),
        name="combine",
    )(cdst, cdst, x2, route, y)


def _layer(x2d, B, S, norm1_g, w_in, w_gla_a2, b_gla_a, gla_out_norm_g, dil_q_norm_g, dil_k_norm_g,
           w_proj_gla, w_proj_attn, w_branch_gate, b_branch_gate, w_out, norm2_g,
           w_router_group, b_router_group, w_router_expert, b_router_expert, w_gate, w_up, w_down):
    T = B * S
    w_a2 = jnp.pad(w_gla_a2, ((0, LANES - GLA_RANK), (0, 0))).astype(BF16)
    gains = [jnp.ones((HEAD_W,), F32)] * N_GLA_TILES
    for gi in range(DIL_GROUPS):
        gains += [jnp.tile(dil_q_norm_g[gi], DIL_HEADS) * (DIL_DH ** -0.5 * LOG2_E),
                  jnp.tile(dil_k_norm_g[gi], DIL_HEADS), jnp.ones((HEAD_W,), F32)]
    qk_gain = jnp.stack(gains).reshape(N_COL_TILES, 1, HEAD_W)

    gla_in, log_a, qkv0, qkv1, qkv2 = _inproj(
        x2d, norm1_g.reshape(1, -1), w_in.astype(BF16), w_a2, b_gla_a.reshape(1, -1), qk_gain, B, S)

    tri_incl = jnp.asarray(np.tril(np.ones((GLA_BLOCK, GLA_BLOCK), np.float32)), BF16)
    o_gla = _gla(gla_in, log_a, tri_incl, gla_out_norm_g.reshape(1, -1), B, S)

    outs, stats = [], []
    for gi, qkv in enumerate((qkv0.reshape(B, 1, S, QKV_W), qkv1, qkv2)):
        o, st = _dilated(qkv, _alibi_bias(gi), gi)
        outs.append(o)
        stats.append(st)
    outs[0] = outs[0].reshape(T, HEAD_W)
    stats[0] = stats[0].reshape(T, LANES)

    w_r = jnp.pad(jnp.concatenate([w_router_group, w_router_expert], axis=1),
                  ((0, 0), (0, LANES - N_GROUPS - N_EXPERTS)))
    w_rh = w_r.astype(BF16)
    w_rl = (w_r - w_rh.astype(F32)).astype(BF16)
    b_r = jnp.pad(jnp.concatenate([b_router_group, b_router_expert]), (0, LANES - N_GROUPS - N_EXPERTS))
    tm = min(MERGE_TM, S)
    tri_strict = jnp.asarray(np.tril(np.ones((tm, tm), np.float32), -1), BF16)
    upper = jnp.asarray(np.triu(np.ones((LANES, LANES), np.float32), 1), BF16)
    x2, xloc, route, cnt = _merge(
        x2d, o_gla, outs, stats, norm1_g.reshape(1, -1), w_branch_gate.astype(BF16),
        b_branch_gate.reshape(1, -1), w_proj_gla.astype(BF16), w_proj_attn.astype(BF16), w_out.astype(BF16),
        norm2_g.reshape(1, -1), jnp.concatenate([w_rh, w_rl], axis=1), b_r.reshape(1, -1), tri_strict, upper, S)

    csrc, cdst, tile_e, nact = _chunk_tables(cnt, T // tm, tm)
    y = _experts(tile_e, nact, csrc, xloc, w_gate, w_up, w_down)
    return _combine(cdst, x2, route, y)


def _chunk_tables(cnt, nt, tm):
    i32 = jnp.int32
    experts = jnp.arange(N_EXPERTS, dtype=i32)
    tiles = jnp.arange(nt, dtype=i32)
    c = (cnt.reshape(nt, 8, LANES)[:, 0, :N_EXPERTS].astype(i32) + CHUNK - 1) // CHUNK
    loff = jnp.cumsum(c, axis=1) - c
    ecum = jnp.cumsum(c, axis=0) - c
    tot = jnp.sum(c, axis=0)
    ptot = (tot + TILE_CHUNKS - 1) // TILE_CHUNKS * TILE_CHUNKS
    pend = jnp.cumsum(ptot)
    pstart = pend - ptot

    lc = jnp.arange(LOCAL_CHUNKS, dtype=i32)
    e_of = jnp.sum(((loff + c)[:, None, :] <= lc[None, :, None]).astype(i32), axis=-1)
    pick_e = e_of[:, :, None] == experts[None, None, :]
    glob = jnp.sum(jnp.where(pick_e, (pstart[None, :] + ecum - loff)[:, None, :], 0), axis=-1) + lc[None, :]
    cdst = jnp.where(e_of < N_EXPERTS, glob, 0) * CHUNK

    max_chunks = (TOP_K * nt * tm + (CHUNK - 1) * nt * N_EXPERTS) // CHUNK + N_EXPERTS * (TILE_CHUNKS - 1)
    ntiles = -(-max_chunks // TILE_CHUNKS)
    g = jnp.arange(ntiles * TILE_CHUNKS, dtype=i32)
    e_g = jnp.minimum(jnp.sum((pend[None, :] <= g[:, None]).astype(i32), axis=1), N_EXPERTS - 1)
    k = g - jnp.sum(jnp.where(e_g[:, None] == experts[None, :], pstart[None, :], 0), axis=1)
    pick = (e_g[:, None] == experts[None, :]).astype(F32)
    ends = jnp.dot(pick, (ecum + c).T.astype(F32), precision=lax.Precision.HIGHEST)
    t_g = jnp.sum((ends <= k[:, None].astype(F32)).astype(i32), axis=1)
    valid = t_g < nt
    base = jnp.dot(pick, (tiles[:, None] * LOCAL_CHUNKS + loff - ecum).T.astype(F32),
                   precision=lax.Precision.HIGHEST)
    in_tile = tiles[None, :] == jnp.minimum(t_g, nt - 1)[:, None]
    local = jnp.sum(jnp.where(in_tile, base, 0.0), axis=1).astype(i32) + k
    zero_chunk = LOCAL_CHUNKS - 1
    csrc = jnp.where(valid, local, zero_chunk) * CHUNK

    tile_e = e_g[::TILE_CHUNKS]
    nact = (pend[-1:] // TILE_CHUNKS).astype(i32)
    return (csrc.reshape(ntiles, 1, TILE_CHUNKS), cdst.reshape(nt, 1, LOCAL_CHUNKS), tile_e, nact)


def kernel(x, norm1_g, w_in, w_gla_a2, b_gla_a, gla_out_norm_g, dil_q_norm_g, dil_k_norm_g, w_proj_gla,
           w_proj_attn, w_branch_gate, b_branch_gate, w_out, norm2_g, w_router_group, b_router_group,
           w_router_expert, b_router_expert, w_gate, w_up, w_down):
    B, S, D = x.shape
    assert D == D_MODEL and S % (DIL_BLOCK * DIL_PATTERNS[-1][1]) == 0
    x2d = x.reshape(B * S, D)
    params = (norm1_g, w_in, w_gla_a2, b_gla_a, gla_out_norm_g, dil_q_norm_g, dil_k_norm_g, w_proj_gla,
              w_proj_attn, w_branch_gate, b_branch_gate, w_out, norm2_g, w_router_group, b_router_group,
              w_router_expert, b_router_expert, w_gate, w_up, w_down)
    for layer in range(norm1_g.shape[0]):
        x2d = _layer(x2d, B, S, *(p[layer] for p in params))
    return x2d.reshape(B, S, D)
```

```python
import functools

import jax
import jax.numpy as jnp
import numpy as np
from jax import lax
from jax.experimental import pallas as pl
from jax.experimental.pallas import tpu as pltpu

F32 = jnp.float32
BF16 = jnp.bfloat16

D_MODEL = 1024
EPS = 1e-6
GLA_HEADS = 4
GLA_DK = 128
GLA_RANK = 16
GLA_TAU = 16.0
LOG2_E = 1.4426950408889634
GLA_SUBCHUNK = 16
GLA_BLOCK = 128
GLA_STEP_BLOCKS = 16
DIL_PATTERNS = ((128, 1), (512, 4), (2048, 16))
DIL_GROUPS = 3
DIL_HEADS = 4
DIL_DH = 128
DIL_BLOCK = 128
ALIBI_SLOPES = tuple(2.0 ** (-8.0 * (i + 1) / (DIL_GROUPS * DIL_HEADS)) for i in range(DIL_GROUPS * DIL_HEADS))
N_GROUPS = 4
EXPERTS_PER_GROUP = 8
N_EXPERTS = N_GROUPS * EXPERTS_PER_GROUP
TOP_K = 2
D_EXPERT = 512

HEAD_W = 512
N_GLA_TILES = 4
N_COL_TILES = N_GLA_TILES + 3 * DIL_GROUPS
QKV_W = 3 * HEAD_W
LANES = 128
NEG = -1e30

INPROJ_TM = 512
MERGE_TM = 512
MOE_TM = 512
CHUNK = 8
TILE_CHUNKS = MOE_TM // CHUNK
LOCAL_ROWS = -(-(TOP_K * MERGE_TM + N_EXPERTS * CHUNK) // LANES) * LANES
LOCAL_CHUNKS = LOCAL_ROWS // CHUNK
DIL_STEP_BLOCKS = 32
VMEM_LIMIT = 56 * 1024 * 1024


def _dot(a, b):
    return jnp.dot(a, b, preferred_element_type=F32)


def _dot_nt(a, b):
    return lax.dot_general(a, b, (((1,), (1,)), ((), ())), preferred_element_type=F32)


def _dot_tn(a, b):
    return lax.dot_general(a, b, (((0,), (0,)), ((), ())), preferred_element_type=F32)


def _rms(x, g):
    return x * lax.rsqrt(jnp.mean(x * x, axis=-1, keepdims=True) + EPS) * g


def _params(sem):
    return pltpu.CompilerParams(dimension_semantics=sem, vmem_limit_bytes=VMEM_LIMIT)


def _head(c):
    return slice(c * LANES, (c + 1) * LANES)


PACKED_W = D_MODEL // 2
_HI16 = 0xFFFF0000


def _pack_pairs(x):
    u = lax.bitcast_convert_type(x, jnp.uint32)
    w = x.shape[1] // 2
    return lax.bitcast_convert_type((u[:, :w] & jnp.uint32(_HI16)) | (u[:, w:] >> 16), F32)


def _unpack_pairs(p):
    u = lax.bitcast_convert_type(p, jnp.uint32)
    hi = lax.bitcast_convert_type(u & jnp.uint32(_HI16), F32)
    lo = lax.bitcast_convert_type(u << 16, F32)
    return jnp.concatenate([hi, lo], axis=1).astype(BF16)


def _inproj_kernel(x_ref, g1_ref, w_ref, wa2_ref, ba_ref, qkg_ref,
                   gla_ref, loga_ref, d0_ref, d1_ref, d2_ref, h_ref, ybuf_ref, wdil_ref):
    tm = x_ref.shape[0]
    n_gla = N_GLA_TILES * HEAD_W

    @pl.when(pl.program_id(0) == 0)
    def _():
        for k in range(3 * DIL_GROUPS):
            first = n_gla + GLA_RANK + k * HEAD_W
            wdil_ref[:, k * HEAD_W:(k + 1) * HEAD_W] = w_ref[:, first:first + HEAD_W]

    hb = _rms(x_ref[...], g1_ref[...]).astype(BF16)
    h_ref[...] = hb
    ga = _dot(hb, w_ref[:, n_gla:n_gla + LANES])
    z = _dot(ga.astype(BF16), wa2_ref[...]) + ba_ref[...]
    log_sig = jnp.minimum(z, 0.0) - jnp.log(1.0 + jnp.exp(-jnp.abs(z)))
    loga_ref[...] = log_sig * (LOG2_E / GLA_TAU)

    dil_refs = (d0_ref, d1_ref, d2_ref)
    n_strided = 0
    for jj in range(N_COL_TILES):
        cols = slice(jj * HEAD_W, (jj + 1) * HEAD_W)
        if jj < N_GLA_TILES:
            w = w_ref[:, cols]
        else:
            w = wdil_ref[:, (jj - N_GLA_TILES) * HEAD_W:(jj - N_GLA_TILES + 1) * HEAD_W]
        y = _dot(h_ref[...], w)
        if jj < N_GLA_TILES:
            gla_ref[:, cols] = y.astype(BF16)
            continue
        gi, kind = divmod(jj - N_GLA_TILES, 3)
        d = DIL_PATTERNS[gi][1]
        ocols = [slice(kind * HEAD_W + c * LANES, kind * HEAD_W + (c + 1) * LANES) for c in range(DIL_HEADS)]
        if kind < 2:
            g = qkg_ref[jj]
            slabs = [_rms(y[:, _head(c)], g[:, _head(c)]) for c in range(DIL_HEADS)]
        else:
            slabs = [y[:, _head(c)] for c in range(DIL_HEADS)]
        out = dil_refs[gi]
        if d == 1:
            for c in range(DIL_HEADS):
                out[:, ocols[c]] = slabs[c].astype(BF16)
            continue
        buf = n_strided % ybuf_ref.shape[0]
        n_strided += 1
        for c in range(DIL_HEADS):
            ybuf_ref[buf, c] = slabs[c]
        for r in range(d):
            for c in range(DIL_HEADS):
                out[0, r, :, ocols[c]] = ybuf_ref[buf, c, pl.ds(r, tm // d, stride=d), :].astype(BF16)


def _resident(shape):
    return pl.BlockSpec(shape, lambda i: (0,) * len(shape), pipeline_mode=pl.Buffered(1))


def _inproj(x2d, g1, w_all, w_a2, b_a, qk_gain, B, S):
    T = x2d.shape[0]
    tm = min(INPROJ_TM, S)
    tpb = S // tm

    def dil_spec(gi):
        d = DIL_PATTERNS[gi][1]
        if d == 1:
            return pl.BlockSpec((tm, QKV_W), lambda i: (i, 0))
        return pl.BlockSpec((1, d, tm // d, QKV_W), lambda i: (i // tpb, 0, i % tpb, 0))

    def dil_shape(gi):
        d = DIL_PATTERNS[gi][1]
        shape = (T, QKV_W) if d == 1 else (B, d, S // d, QKV_W)
        return jax.ShapeDtypeStruct(shape, BF16)

    return pl.pallas_call(
        _inproj_kernel,
        grid=(T // tm,),
        in_specs=[
            pl.BlockSpec((tm, D_MODEL), lambda i: (i, 0)),
            _resident(g1.shape), _resident(w_all.shape), _resident(w_a2.shape),
            _resident(b_a.shape), _resident(qk_gain.shape),
        ],
        out_specs=[
            pl.BlockSpec((tm, N_GLA_TILES * HEAD_W), lambda i: (i, 0)),
            pl.BlockSpec((tm, HEAD_W), lambda i: (i, 0)),
            dil_spec(0), dil_spec(1), dil_spec(2),
        ],
        out_shape=[
            jax.ShapeDtypeStruct((T, N_GLA_TILES * HEAD_W), BF16),
            jax.ShapeDtypeStruct((T, HEAD_W), F32),
            dil_shape(0), dil_shape(1), dil_shape(2),
        ],
        scratch_shapes=[pltpu.VMEM((tm, D_MODEL), BF16), pltpu.VMEM((2, DIL_HEADS, tm, LANES), F32),
                        pltpu.VMEM((D_MODEL, 3 * DIL_GROUPS * HEAD_W), BF16)],
        compiler_params=_params(("arbitrary",)),
        name="inproj",
    )(x2d, g1, w_all, w_a2, b_a, qk_gain)


def _gla_kernel(q_ref, k_ref, v_ref, r_ref, la_ref, tri_ref, gn_ref, o_ref, st_ref):
    n = pl.program_id(1)

    @pl.when(n == 0)
    def _():
        st_ref[...] = jnp.zeros_like(st_ref)

    for blk in range(q_ref.shape[0] // GLA_BLOCK):
        _gla_block(q_ref, k_ref, v_ref, r_ref, la_ref, tri_ref, gn_ref, o_ref, st_ref,
                   slice(blk * GLA_BLOCK, (blk + 1) * GLA_BLOCK))


def _gla_block(q_ref, k_ref, v_ref, r_ref, la_ref, tri_ref, gn_ref, o_ref, st_ref, rows):
    L = GLA_BLOCK
    la = la_ref[rows, :]
    hi = la.astype(BF16)
    r1 = la - hi.astype(F32)
    mid = r1.astype(BF16)
    lo = (r1 - mid.astype(F32)).astype(BF16)
    tri = tri_ref[...]
    b_all = _dot(tri, hi) + _dot(tri, mid) + _dot(tri, lo)

    row = lax.broadcasted_iota(jnp.int32, (L, L), 0)
    col = lax.broadcasted_iota(jnp.int32, (L, L), 1)

    for h in range(GLA_HEADS):
        sl = _head(h)
        b = b_all[:, sl]
        q = q_ref[rows, sl].astype(F32) * (GLA_DK ** -0.5)
        k = k_ref[rows, sl].astype(F32)
        v = v_ref[rows, sl]

        att = jnp.zeros((L, L), F32)
        seg = L
        while seg > GLA_SUBCHUNK:
            half = seg // 2
            beta = jnp.concatenate(
                [jnp.broadcast_to(b[s + half - 1:s + half, :], (seg, GLA_DK)) for s in range(0, L, seg)], axis=0)
            rel = b - beta
            neg = jnp.minimum(rel, 0.0)
            qs = (q * jnp.exp2(neg)).astype(BF16)
            ks = (k * jnp.exp2(neg - rel)).astype(BF16)
            same = (row // seg) == (col // seg)
            mask = same & ((row % seg) >= half) & ((col % seg) < half)
            att = jnp.where(mask, _dot_nt(qs, ks), att)
            seg = half
        beta = jnp.concatenate(
            [jnp.zeros((GLA_SUBCHUNK, GLA_DK), F32)]
            + [jnp.broadcast_to(b[s - 1:s, :], (GLA_SUBCHUNK, GLA_DK)) for s in range(GLA_SUBCHUNK, L, GLA_SUBCHUNK)],
            axis=0)
        qs = (q * jnp.exp2(b - beta)).astype(BF16)
        ks = (k * jnp.exp2(beta - b)).astype(BF16)
        mask = ((row // GLA_SUBCHUNK) == (col // GLA_SUBCHUNK)) & (col <= row)
        att = jnp.where(mask, _dot_nt(qs, ks), att)

        state_t = st_ref[h]
        q0 = (q * jnp.exp2(b)).astype(BF16)
        o = _dot(att.astype(BF16), v) + _dot_nt(q0, state_t.astype(BF16))

        b_last = b[L - 1:L, :]
        k_end = (k * jnp.exp2(b_last - b)).astype(BF16)
        st_ref[h] = state_t * jnp.exp2(b_last) + _dot_tn(v, k_end)

        rr = r_ref[rows, sl].astype(F32)
        o_ref[rows, sl] = (_rms(o, gn_ref[...]) * (rr * jax.nn.sigmoid(rr))).astype(BF16)


def _gla(proj, log_a, tri, gn, B, S):
    L = GLA_STEP_BLOCKS * GLA_BLOCK
    nb = S // L
    return pl.pallas_call(
        _gla_kernel,
        grid=(B, nb),
        in_specs=[
            pl.BlockSpec((L, HEAD_W), lambda b, n: (b * nb + n, 0)),
            pl.BlockSpec((L, HEAD_W), lambda b, n: (b * nb + n, 1)),
            pl.BlockSpec((L, HEAD_W), lambda b, n: (b * nb + n, 2)),
            pl.BlockSpec((L, HEAD_W), lambda b, n: (b * nb + n, 3)),
            pl.BlockSpec((L, HEAD_W), lambda b, n: (b * nb + n, 0)),
            pl.BlockSpec((GLA_BLOCK, GLA_BLOCK), lambda b, n: (0, 0)),
            pl.BlockSpec((1, GLA_DK), lambda b, n: (0, 0)),
        ],
        out_specs=pl.BlockSpec((L, HEAD_W), lambda b, n: (b * nb + n, 0)),
        out_shape=jax.ShapeDtypeStruct((B * S, HEAD_W), BF16),
        scratch_shapes=[pltpu.VMEM((GLA_HEADS, GLA_DK, GLA_DK), F32)],
        compiler_params=_params(("arbitrary", "arbitrary")),
        name="gla",
    )(proj, proj, proj, proj, log_a, tri, gn)


def _dil_kernel(q_ref, kp_ref, kc_ref, vp_ref, vc_ref, bias_ref, o_ref, st_ref, *, nr, nq):
    first = pl.program_id(2) == 0
    Q = DIL_BLOCK
    col = lax.broadcasted_iota(jnp.int32, (Q, 2 * Q), 1)
    lane = lax.broadcasted_iota(jnp.int32, (Q, LANES), 1)
    for res in range(nr):
        for blk in range(nq):
            rows = slice(blk * Q, (blk + 1) * Q)
            prev = slice((blk - 1) * Q, blk * Q)
            stats = jnp.zeros((Q, LANES), F32)
            for h in range(DIL_HEADS):
                sl = _head(h)
                if blk == 0:
                    kprev, vprev = kp_ref[0, res, :, sl], vp_ref[0, res, :, sl]
                else:
                    kprev, vprev = kc_ref[0, res, prev, sl], vc_ref[0, res, prev, sl]
                kk = jnp.concatenate([kprev, kc_ref[0, res, rows, sl]], axis=0)
                vv = jnp.concatenate([vprev, vc_ref[0, res, rows, sl]], axis=0)
                s = _dot_nt(q_ref[0, res, rows, sl], kk) + bias_ref[h]
                if blk == 0:
                    s = jnp.where(jnp.logical_and(first, col < Q), NEG, s)
                m = jnp.max(s, axis=-1, keepdims=True)
                p = jnp.exp2(s - m)
                den = jnp.sum(p, axis=-1, keepdims=True)
                o_ref[0, res, rows, sl] = (_dot(p.astype(BF16), vv) / den).astype(BF16)
                stats = jnp.where((lane // 16) == h, m, stats)
                stats = jnp.where((lane // 16) == DIL_HEADS + h, den, stats)
            st_ref[0, res, rows, :] = stats


def _dilated(qkv, bias, gi):
    B, d, Lsub, _ = qkv.shape
    nq = min(DIL_STEP_BLOCKS, Lsub // DIL_BLOCK)
    nr = min(DIL_STEP_BLOCKS // nq, d)
    rows = nq * DIL_BLOCK

    def cur(c):
        return pl.BlockSpec((1, nr, rows, HEAD_W), lambda b, r, i: (b, r, i, c))

    def prev(c):
        return pl.BlockSpec((1, nr, DIL_BLOCK, HEAD_W), lambda b, r, i: (b, r, jnp.maximum(i * nq - 1, 0), c))

    return pl.pallas_call(
        functools.partial(_dil_kernel, nr=nr, nq=nq),
        grid=(B, d // nr, Lsub // rows),
        in_specs=[cur(0), prev(1), cur(1), prev(2), cur(2),
                  pl.BlockSpec((DIL_HEADS, DIL_BLOCK, 2 * DIL_BLOCK), lambda b, r, i: (0, 0, 0))],
        out_specs=[
            pl.BlockSpec((1, nr, rows, HEAD_W), lambda b, r, i: (b, r, i, 0)),
            pl.BlockSpec((1, nr, rows, LANES), lambda b, r, i: (b, r, i, 0)),
        ],
        out_shape=[
            jax.ShapeDtypeStruct((B, d, Lsub, HEAD_W), BF16),
            jax.ShapeDtypeStruct((B, d, Lsub, LANES), F32),
        ],
        compiler_params=_params(("arbitrary", "arbitrary", "arbitrary")),
        name=f"dilated{gi}",
    )(qkv, qkv, qkv, qkv, qkv, bias)


def _alibi_bias(gi):
    window, d = DIL_PATTERNS[gi]
    Q = DIL_BLOCK
    dist = np.arange(Q)[:, None] + Q - np.arange(2 * Q)[None, :]
    valid = (dist >= 0) & (dist <= window // d)
    slopes = np.asarray(ALIBI_SLOPES, np.float32).reshape(DIL_GROUPS, DIL_HEADS)[gi]
    bias = -slopes[:, None, None] * (d * dist).astype(np.float32) * np.float32(LOG2_E)
    return jnp.asarray(np.where(valid[None], bias, np.float32(NEG)), F32)


def _merge_rows(x_ref, og_ref, o0_ref, s0_ref, g1_ref, wbg_ref, bbg_ref, wpg_ref, wpa_ref, wout_ref,
                g2_ref, wr_ref, br_ref, x2_ref, obuf_ref, sbuf_ref, row0, n):
    rows = slice(row0, row0 + n)
    x = x_ref[rows, :]
    hb = _rms(x, g1_ref[...]).astype(BF16)
    gates = jax.nn.sigmoid(_dot(hb, wbg_ref[...]) + bbg_ref[...])

    stats = (s0_ref[rows, :], sbuf_ref[0, rows, :], sbuf_ref[1, rows, :])
    dens = [pltpu.roll(s, 64, 1) for s in stats]
    m_all = jnp.maximum(jnp.maximum(stats[0], stats[1]), stats[2])
    wts = [d * jnp.exp2(s - m_all) for s, d in zip(stats, dens)]
    inv = 1.0 / (wts[0] + wts[1] + wts[2])
    coef = [w * inv for w in wts]
    heads = []
    for h in range(DIL_HEADS):
        group_out = (o0_ref[rows, _head(h)].astype(F32), obuf_ref[0, h, rows, :], obuf_ref[1, h, rows, :])
        acc = jnp.zeros((n, DIL_DH), F32)
        for g in range(DIL_GROUPS):
            c = jnp.broadcast_to(coef[g][:, 16 * h:16 * h + 1], (n, DIL_DH))
            acc = acc + c * group_out[g]
        heads.append(acc.astype(BF16))
    o_att = jnp.concatenate(heads, axis=1)

    y = gates[:, :D_MODEL] * _dot(og_ref[rows, :], wpg_ref[...]) + gates[:, D_MODEL:] * _dot(o_att, wpa_ref[...])
    x2 = x + _dot(y.astype(BF16), wout_ref[...])
    x2_ref[rows, :] = x2

    h2 = _rms(x2, g2_ref[...])

    h2h = h2.astype(BF16)
    h2l = (h2 - h2h.astype(F32)).astype(BF16)
    wr = wr_ref[...]
    pa = _dot(h2h, wr)
    pb = _dot(h2l, wr)
    logit = pa[:, :LANES] + pa[:, LANES:] + pb[:, :LANES] + pb[:, LANES:] + br_ref[...]

    lane = lax.broadcasted_iota(jnp.int32, (n, LANES), 1).astype(F32)
    big = jnp.float32(1e9)
    gl = jnp.where(lane < N_GROUPS, logit, NEG)
    gmax = jnp.max(gl, axis=-1, keepdims=True)
    gsel = jnp.min(jnp.where(gl == gmax, lane, big), axis=-1, keepdims=True)
    g_p = 1.0 / jnp.sum(jnp.exp(gl - gmax), axis=-1, keepdims=True)
    lo = N_GROUPS + EXPERTS_PER_GROUP * gsel
    el = jnp.where((lane >= lo) & (lane < lo + EXPERTS_PER_GROUP), logit, NEG)
    v1 = jnp.max(el, axis=-1, keepdims=True)
    i1 = jnp.min(jnp.where(el == v1, lane, big), axis=-1, keepdims=True)
    el2 = jnp.where(lane == i1, NEG, el)
    v2 = jnp.max(el2, axis=-1, keepdims=True)
    i2 = jnp.min(jnp.where(el2 == v2, lane, big), axis=-1, keepdims=True)
    ex = jnp.exp(v2 - v1)
    w1 = g_p / (1.0 + ex)
    w2 = g_p * ex / (1.0 + ex)
    return i1 - N_GROUPS, i2 - N_GROUPS, w1, w2, h2h


def _merge_kernel(x_ref, og_ref, o0_ref, o1_ref, o2_ref, s0_ref, s1_ref, s2_ref, g1_ref, wbg_ref, bbg_ref,
                  wpg_ref, wpa_ref, wout_ref, g2_ref, wr_ref, br_ref, tri_ref, upper_ref,
                  x2_ref, xloc_ref, route_ref, cnt_ref, obuf_ref, sbuf_ref):
    tm = x_ref.shape[0]

    for slot, (o_ref, s_ref) in enumerate(((o1_ref, s1_ref), (o2_ref, s2_ref))):
        d = DIL_PATTERNS[slot + 1][1]
        for r in range(d):
            sbuf_ref[slot, pl.ds(r, tm // d, stride=d), :] = s_ref[0, r]
            for c in range(DIL_HEADS):
                obuf_ref[slot, c, pl.ds(r, tm // d, stride=d), :] = o_ref[0, r, :, _head(c)].astype(F32)

    e1, e2, w1, w2, h2b = _merge_rows(x_ref, og_ref, o0_ref, s0_ref, g1_ref, wbg_ref, bbg_ref, wpg_ref, wpa_ref,
                                      wout_ref, g2_ref, wr_ref, br_ref, x2_ref, obuf_ref, sbuf_ref, 0, tm)

    lane = lax.broadcasted_iota(jnp.int32, (tm, LANES), 1).astype(F32)
    oh1 = lane == e1
    oh2 = lane == e2
    onehot = jnp.where(oh1 | oh2, 1.0, 0.0)
    prefix = _dot(tri_ref[...], onehot.astype(BF16))
    cnt = jnp.sum(onehot, axis=0, keepdims=True)
    cnt_pad = jnp.floor((cnt + (CHUNK - 1)) * (1.0 / CHUNK)) * CHUNK
    seg_off = _dot(jnp.broadcast_to(cnt_pad, (8, LANES)).astype(BF16), upper_ref[...])[0:1, :]
    base = seg_off + prefix
    pos1 = jnp.sum(jnp.where(oh1, base, 0.0), axis=-1, keepdims=True)
    pos2 = jnp.sum(jnp.where(oh2, base, 0.0), axis=-1, keepdims=True)
    cnt_ref[...] = jnp.broadcast_to(cnt, cnt_ref.shape)

    route = jnp.zeros((tm, LANES), F32)
    for idx, val in enumerate((e1, e2, w1, w2, pos1, pos2)):
        route = jnp.where(lane == idx, val, route)
    route_ref[...] = route

    route_t = route.T
    slot = lax.broadcasted_iota(jnp.int32, (LOCAL_ROWS, tm), 0).astype(F32)
    perm = jnp.where((slot == route_t[4:5, :]) | (slot == route_t[5:6, :]), 1.0, 0.0).astype(BF16)
    xloc_ref[...] = _pack_pairs(_dot(perm, h2b))


def _merge(x2d, o_gla, outs, stats, g1, wbg, bbg, wpg, wpa, wout, g2, wr, br, tri, upper, S):
    T = x2d.shape[0]
    tm = min(MERGE_TM, S)
    assert TOP_K * tm + N_EXPERTS * CHUNK <= LOCAL_ROWS
    tpb = S // tm
    tok = lambda w: pl.BlockSpec((tm, w), lambda i: (i, 0))
    full = lambda a: pl.BlockSpec(a.shape, lambda i: (0,) * a.ndim)

    def res(gi, w):
        d = DIL_PATTERNS[gi][1]
        return pl.BlockSpec((1, d, tm // d, w), lambda i: (i // tpb, 0, i % tpb, 0))

    return pl.pallas_call(
        _merge_kernel,
        grid=(T // tm,),
        in_specs=[tok(D_MODEL), tok(HEAD_W), tok(HEAD_W), res(1, HEAD_W), res(2, HEAD_W),
                  tok(LANES), res(1, LANES), res(2, LANES),
                  full(g1), full(wbg), full(bbg), full(wpg), full(wpa), full(wout), full(g2),
                  full(wr), full(br), full(tri), full(upper)],
        out_specs=[tok(D_MODEL), pl.BlockSpec((LOCAL_ROWS, PACKED_W), lambda i: (i, 0)), tok(LANES),
                   pl.BlockSpec((8, LANES), lambda i: (i, 0))],
        out_shape=[
            jax.ShapeDtypeStruct((T, D_MODEL), F32),
            jax.ShapeDtypeStruct((T // tm * LOCAL_ROWS, PACKED_W), F32),
            jax.ShapeDtypeStruct((T, LANES), F32),
            jax.ShapeDtypeStruct((T // tm * 8, LANES), F32),
        ],
        scratch_shapes=[pltpu.VMEM((DIL_GROUPS - 1, DIL_HEADS, tm, LANES), F32),
                        pltpu.VMEM((DIL_GROUPS - 1, tm, LANES), F32)],
        compiler_params=_params(("arbitrary",)),
        name="merge",
    )(x2d, o_gla, *outs, *stats, g1, wbg, bbg, wpg, wpa, wout, g2, wr, br, tri, upper)


class _ChunkGather:
    def __init__(self, src_hbm, buf_ref, sem, nchunks):
        self.src, self.buf, self.sem, self.n = src_hbm, buf_ref, sem, nchunks

    def start(self, offs_ref, s):
        for j in range(self.n):
            row = pl.multiple_of(offs_ref[0, 0, j], CHUNK)
            pltpu.make_async_copy(self.src.at[pl.ds(row, CHUNK), :], self.buf.at[s, pl.ds(j * CHUNK, CHUNK), :],
                                  self.sem.at[s]).start()

    def wait(self, s):
        pltpu.make_async_copy(self.src.at[pl.ds(0, self.n * CHUNK), :], self.buf.at[s], self.sem.at[s]).wait()

    def step(self, offs_cur_ref, offs_next_ref, used_steps=None):
        i = pl.program_id(0)
        slot = lax.rem(i, 2)
        last = pl.num_programs(0) if used_steps is None else jnp.minimum(used_steps, pl.num_programs(0))

        @pl.when(jnp.logical_and(i == 0, i < last))
        def _():
            self.start(offs_cur_ref, 0)

        @pl.when(i < last)
        def _():
            self.wait(slot)

        @pl.when(i + 1 < last)
        def _():
            self.start(offs_next_ref, 1 - slot)

        return slot


def _expert_kernel(tile_e_ref, nact_ref, src_cur_ref, src_next_ref, xloc_ref, wg_ref, wu_ref, wd_ref,
                   y_ref, xbuf_ref, wgu_ref, wdn_ref, gsem):
    i = pl.program_id(0)
    slot = _ChunkGather(xloc_ref, xbuf_ref, gsem, TILE_CHUNKS).step(src_cur_ref, src_next_ref, nact_ref[0])
    active = i < nact_ref[0]

    @pl.when(jnp.logical_and(active, jnp.logical_or(i == 0, tile_e_ref[i] != tile_e_ref[jnp.maximum(i - 1, 0)])))
    def _():
        wgu_ref[:, :D_EXPERT] = wg_ref[0].astype(BF16)
        wgu_ref[:, D_EXPERT:] = wu_ref[0].astype(BF16)
        wdn_ref[...] = wd_ref[0].astype(BF16)

    @pl.when(active)
    def _():
        xb = _unpack_pairs(xbuf_ref[slot])
        gu = _dot(xb, wgu_ref[...])
        g, u = gu[:, :D_EXPERT], gu[:, D_EXPERT:]
        a = (g * jax.nn.sigmoid(g) * u).astype(BF16)
        y_ref[...] = _pack_pairs(_dot(a, wdn_ref[...]).astype(BF16).astype(F32))

    @pl.when(jnp.logical_not(active))
    def _():
        y_ref[...] = jnp.zeros_like(y_ref)


def _experts(tile_e, nact, csrc, xloc, wg, wu, wd):
    ntiles = tile_e.shape[0]
    R = MOE_TM

    def wsel(i, te, na):
        return (te[jnp.minimum(i, na[0] - 1)], 0, 0)

    def offs(shift):
        return pl.BlockSpec((1, 1, TILE_CHUNKS), lambda i, te, na: (jnp.minimum(i + shift, ntiles - 1), 0, 0),
                            memory_space=pltpu.SMEM)

    return pl.pallas_call(
        _expert_kernel,
        grid_spec=pltpu.PrefetchScalarGridSpec(
            num_scalar_prefetch=2,
            grid=(ntiles,),
            in_specs=[
                offs(0), offs(1),
                pl.BlockSpec(memory_space=pl.ANY),
                pl.BlockSpec((1, D_MODEL, D_EXPERT), wsel),
                pl.BlockSpec((1, D_MODEL, D_EXPERT), wsel),
                pl.BlockSpec((1, D_EXPERT, D_MODEL), wsel),
            ],
            out_specs=pl.BlockSpec((R, PACKED_W), lambda i, te, na: (i, 0)),
            scratch_shapes=[pltpu.VMEM((2, R, PACKED_W), F32), pltpu.VMEM((D_MODEL, 2 * D_EXPERT), BF16),
                            pltpu.VMEM((D_EXPERT, D_MODEL), BF16), pltpu.SemaphoreType.DMA((2,))],
        ),
        out_shape=jax.ShapeDtypeStruct((ntiles * R, PACKED_W), F32),
        compiler_params=_params(("arbitrary",)),
        name="experts",
    )(tile_e, nact, csrc, csrc, xloc, wg, wu, wd)


def _combine_kernel(dst_cur_ref, dst_next_ref, x2_ref, route_ref, y_ref, out_ref, ybuf_ref, gsem):
    slot = _ChunkGather(y_ref, ybuf_ref, gsem, LOCAL_CHUNKS).step(dst_cur_ref, dst_next_ref)
    tm = x2_ref.shape[0]
    route = route_ref[...]
    w1, w2, pos1, pos2 = (route[:, c:c + 1] for c in range(2, 6))
    yl = _unpack_pairs(ybuf_ref[slot])
    cols = lax.broadcasted_iota(jnp.int32, (tm, LOCAL_ROWS), 1).astype(F32)
    pick1 = jnp.where(cols == pos1, 1.0, 0.0).astype(BF16)
    pick2 = jnp.where(cols == pos2, 1.0, 0.0).astype(BF16)
    out_ref[...] = x2_ref[...] + w1 * _dot(pick1, yl) + w2 * _dot(pick2, yl)


def _combine(cdst, x2, route, y):
    T = x2.shape[0]
    nt = cdst.shape[0]
    tm = T // nt

    def offs(shift):
        return pl.BlockSpec((1, 1, LOCAL_CHUNKS), lambda i: (jnp.minimum(i + shift, nt - 1), 0, 0),
                            memory_space=pltpu.SMEM)

    return pl.pallas_call(
        _combine_kernel,
        grid=(nt,),
        in_specs=[
            offs(0), offs(1),
            pl.BlockSpec((tm, D_MODEL), lambda i: (i, 0)),
            pl.BlockSpec((tm, LANES), lambda i: (i, 0)),
            pl.BlockSpec(memory_space=pl.ANY),
        ],
        out_specs=pl.BlockSpec((tm, D_MODEL), lambda i: (i, 0)),
        out_shape=jax.ShapeDtypeStruct((T, D_MODEL), F32),
        scratch_shapes=[pltpu.VMEM((2, LOCAL_ROWS, PACKED_W), F32), pltpu.SemaphoreType.DMA((2,))],
        compiler_params=_params(("arbitrary",)),
        name="combine",
    )(cdst, cdst, x2, route, y)


def _layer(x2d, B, S, norm1_g, w_in, w_gla_a2, b_gla_a, gla_out_norm_g, dil_q_norm_g, dil_k_norm_g,
           w_proj_gla, w_proj_attn, w_branch_gate, b_branch_gate, w_out, norm2_g,
           w_router_group, b_router_group, w_router_expert, b_router_expert, w_gate, w_up, w_down):
    T = B * S
    w_a2 = jnp.pad(w_gla_a2, ((0, LANES - GLA_RANK), (0, 0))).astype(BF16)
    gains = [jnp.ones((HEAD_W,), F32)] * N_GLA_TILES
    for gi in range(DIL_GROUPS):
        gains += [jnp.tile(dil_q_norm_g[gi], DIL_HEADS) * (DIL_DH ** -0.5 * LOG2_E),
                  jnp.tile(dil_k_norm_g[gi], DIL_HEADS), jnp.ones((HEAD_W,), F32)]
    qk_gain = jnp.stack(gains).reshape(N_COL_TILES, 1, HEAD_W)

    gla_in, log_a, qkv0, qkv1, qkv2 = _inproj(
        x2d, norm1_g.reshape(1, -1), w_in.astype(BF16), w_a2, b_gla_a.reshape(1, -1), qk_gain, B, S)

    tri_incl = jnp.asarray(np.tril(np.ones((GLA_BLOCK, GLA_BLOCK), np.float32)), BF16)
    o_gla = _gla(gla_in, log_a, tri_incl, gla_out_norm_g.reshape(1, -1), B, S)

    outs, stats = [], []
    for gi, qkv in enumerate((qkv0.reshape(B, 1, S, QKV_W), qkv1, qkv2)):
        o, st = _dilated(qkv, _alibi_bias(gi), gi)
        outs.append(o)
        stats.append(st)
    outs[0] = outs[0].reshape(T, HEAD_W)
    stats[0] = stats[0].reshape(T, LANES)

    w_r = jnp.pad(jnp.concatenate([w_router_group, w_router_expert], axis=1),
                  ((0, 0), (0, LANES - N_GROUPS - N_EXPERTS)))
    w_rh = w_r.astype(BF16)
    w_rl = (w_r - w_rh.astype(F32)).astype(BF16)
    b_r = jnp.pad(jnp.concatenate([b_router_group, b_router_expert]), (0, LANES - N_GROUPS - N_EXPERTS))
    tm = min(MERGE_TM, S)
    tri_strict = jnp.asarray(np.tril(np.ones((tm, tm), np.float32), -1), BF16)
    upper = jnp.asarray(np.triu(np.ones((LANES, LANES), np.float32), 1), BF16)
    x2, xloc, route, cnt = _merge(
        x2d, o_gla, outs, stats, norm1_g.reshape(1, -1), w_branch_gate.astype(BF16),
        b_branch_gate.reshape(1, -1), w_proj_gla.astype(BF16), w_proj_attn.astype(BF16), w_out.astype(BF16),
        norm2_g.reshape(1, -1), jnp.concatenate([w_rh, w_rl], axis=1), b_r.reshape(1, -1), tri_strict, upper, S)

    csrc, cdst, tile_e, nact = _chunk_tables(cnt, T // tm, tm)
    y = _experts(tile_e, nact, csrc, xloc, w_gate, w_up, w_down)
    return _combine(cdst, x2, route, y)


def _chunk_tables(cnt, nt, tm):
    i32 = jnp.int32
    experts = jnp.arange(N_EXPERTS, dtype=i32)
    tiles = jnp.arange(nt, dtype=i32)
    c = (cnt.reshape(nt, 8, LANES)[:, 0, :N_EXPERTS].astype(i32) + CHUNK - 1) // CHUNK
    loff = jnp.cumsum(c, axis=1) - c
    ecum = jnp.cumsum(c, axis=0) - c
    tot = jnp.sum(c, axis=0)
    ptot = (tot + TILE_CHUNKS - 1) // TILE_CHUNKS * TILE_CHUNKS
    pend = jnp.cumsum(ptot)
    pstart = pend - ptot

    lc = jnp.arange(LOCAL_CHUNKS, dtype=i32)
    e_of = jnp.sum(((loff + c)[:, None, :] <= lc[None, :, None]).astype(i32), axis=-1)
    pick_e = e_of[:, :, None] == experts[None, None, :]
    glob = jnp.sum(jnp.where(pick_e, (pstart[None, :] + ecum - loff)[:, None, :], 0), axis=-1) + lc[None, :]
    cdst = jnp.where(e_of < N_EXPERTS, glob, 0) * CHUNK

    max_chunks = (TOP_K * nt * tm + (CHUNK - 1) * nt * N_EXPERTS) // CHUNK + N_EXPERTS * (TILE_CHUNKS - 1)
    ntiles = -(-max_chunks // TILE_CHUNKS)
    g = jnp.arange(ntiles * TILE_CHUNKS, dtype=i32)
    e_g = jnp.minimum(jnp.sum((pend[None, :] <= g[:, None]).astype(i32), axis=1), N_EXPERTS - 1)
    k = g - jnp.sum(jnp.where(e_g[:, None] == experts[None, :], pstart[None, :], 0), axis=1)
    pick = (e_g[:, None] == experts[None, :]).astype(F32)
    ends = jnp.dot(pick, (ecum + c).T.astype(F32), precision=lax.Precision.HIGHEST)
    t_g = jnp.sum((ends <= k[:, None].astype(F32)).astype(i32), axis=1)
    valid = t_g < nt
    base = jnp.dot(pick, (tiles[:, None] * LOCAL_CHUNKS + loff - ecum).T.astype(F32),
                   precision=lax.Precision.HIGHEST)
    in_tile = tiles[None, :] == jnp.minimum(t_g, nt - 1)[:, None]
    local = jnp.sum(jnp.where(in_tile, base, 0.0), axis=1).astype(i32) + k
    zero_chunk = LOCAL_CHUNKS - 1
    csrc = jnp.where(valid, local, zero_chunk) * CHUNK

    tile_e = e_g[::TILE_CHUNKS]
    nact = (pend[-1:] // TILE_CHUNKS).astype(i32)
    return (csrc.reshape(ntiles, 1, TILE_CHUNKS), cdst.reshape(nt, 1, LOCAL_CHUNKS), tile_e, nact)


def kernel(x, norm1_g, w_in, w_gla_a2, b_gla_a, gla_out_norm_g, dil_q_norm_g, dil_k_norm_g, w_proj_gla,
           w_proj_attn, w_branch_gate, b_branch_gate, w_out, norm2_g, w_router_group, b_router_group,
           w_router_expert, b_router_expert, w_gate, w_up, w_down):
    B, S, D = x.shape
    assert D == D_MODEL and S % (DIL_BLOCK * DIL_PATTERNS[-1][1]) == 0
    x2d = x.reshape(B * S, D)
    params = (norm1_g, w_in, w_gla_a2, b_gla_a, gla_out_norm_g, dil_q_norm_g, dil_k_norm_g, w_proj_gla,
              w_proj_attn, w_branch_gate, b_branch_gate, w_out, norm2_g, w_router_group, b_router_group,
              w_router_expert, b_router_expert, w_gate, w_up, w_down)
    for layer in range(norm1_g.shape[0]):
        x2d = _layer(x2d, B, S, *(p[layer] for p in params))
    return x2d.reshape(B, S, D)
```

```python
import functools

import jax
import jax.numpy as jnp
import numpy as np
from jax import lax
from jax.experimental import pallas as pl
from jax.experimental.pallas import tpu as pltpu

F32 = jnp.float32
BF16 = jnp.bfloat16

D_MODEL = 1024
EPS = 1e-6
GLA_HEADS = 4
GLA_DK = 128
GLA_RANK = 16
GLA_TAU = 16.0
LOG2_E = 1.4426950408889634
GLA_SUBCHUNK = 16
GLA_BLOCK = 128
GLA_STEP_BLOCKS = 16
DIL_PATTERNS = ((128, 1), (512, 4), (2048, 16))
DIL_GROUPS = 3
DIL_HEADS = 4
DIL_DH = 128
DIL_BLOCK = 128
ALIBI_SLOPES = tuple(2.0 ** (-8.0 * (i + 1) / (DIL_GROUPS * DIL_HEADS)) for i in range(DIL_GROUPS * DIL_HEADS))
N_GROUPS = 4
EXPERTS_PER_GROUP = 8
N_EXPERTS = N_GROUPS * EXPERTS_PER_GROUP
TOP_K = 2
D_EXPERT = 512

HEAD_W = 512
N_GLA_TILES = 4
N_COL_TILES = N_GLA_TILES + 3 * DIL_GROUPS
QKV_W = 3 * HEAD_W
LANES = 128
NEG = -1e30

INPROJ_TM = 512
MERGE_TM = 512
MOE_TM = 512
CHUNK = 8
TILE_CHUNKS = MOE_TM // CHUNK
LOCAL_ROWS = -(-(TOP_K * MERGE_TM + N_EXPERTS * CHUNK) // LANES) * LANES
LOCAL_CHUNKS = LOCAL_ROWS // CHUNK
DIL_STEP_BLOCKS = 32
VMEM_LIMIT = 56 * 1024 * 1024


def _dot(a, b):
    return jnp.dot(a, b, preferred_element_type=F32)


def _dot_nt(a, b):
    return lax.dot_general(a, b, (((1,), (1,)), ((), ())), preferred_element_type=F32)


def _dot_tn(a, b):
    return lax.dot_general(a, b, (((0,), (0,)), ((), ())), preferred_element_type=F32)


def _rms(x, g):
    return x * lax.rsqrt(jnp.mean(x * x, axis=-1, keepdims=True) + EPS) * g


def _params(sem):
    return pltpu.CompilerParams(dimension_semantics=sem, vmem_limit_bytes=VMEM_LIMIT)


def _head(c):
    return slice(c * LANES, (c + 1) * LANES)


PACKED_W = D_MODEL // 2
_HI16 = 0xFFFF0000


def _pack_pairs(x):
    u = lax.bitcast_convert_type(x, jnp.uint32)
    w = x.shape[1] // 2
    return lax.bitcast_convert_type((u[:, :w] & jnp.uint32(_HI16)) | (u[:, w:] >> 16), F32)


def _unpack_pairs(p):
    u = lax.bitcast_convert_type(p, jnp.uint32)
    hi = lax.bitcast_convert_type(u & jnp.uint32(_HI16), F32)
    lo = lax.bitcast_convert_type(u << 16, F32)
    return jnp.concatenate([hi, lo], axis=1).astype(BF16)


def _inproj_kernel(x_ref, g1_ref, w_ref, wa2_ref, ba_ref, qkg_ref,
                   gla_ref, loga_ref, d0_ref, d1_ref, d2_ref, h_ref, ybuf_ref, wdil_ref):
    tm = x_ref.shape[0]
    n_gla = N_GLA_TILES * HEAD_W

    @pl.when(pl.program_id(0) == 0)
    def _():
        for k in range(3 * DIL_GROUPS):
            first = n_gla + GLA_RANK + k * HEAD_W
            wdil_ref[:, k * HEAD_W:(k + 1) * HEAD_W] = w_ref[:, first:first + HEAD_W]

    hb = _rms(x_ref[...], g1_ref[...]).astype(BF16)
    h_ref[...] = hb
    ga = _dot(hb, w_ref[:, n_gla:n_gla + LANES])
    z = _dot(ga.astype(BF16), wa2_ref[...]) + ba_ref[...]
    log_sig = jnp.minimum(z, 0.0) - jnp.log(1.0 + jnp.exp(-jnp.abs(z)))
    loga_ref[...] = log_sig * (LOG2_E / GLA_TAU)

    dil_refs = (d0_ref, d1_ref, d2_ref)
    n_strided = 0
    for jj in range(N_COL_TILES):
        cols = slice(jj * HEAD_W, (jj + 1) * HEAD_W)
        if jj < N_GLA_TILES:
            w = w_ref[:, cols]
        else:
            w = wdil_ref[:, (jj - N_GLA_TILES) * HEAD_W:(jj - N_GLA_TILES + 1) * HEAD_W]
        y = _dot(h_ref[...], w)
        if jj < N_GLA_TILES:
            gla_ref[:, cols] = y.astype(BF16)
            continue
        gi, kind = divmod(jj - N_GLA_TILES, 3)
        d = DIL_PATTERNS[gi][1]
        ocols = [slice(kind * HEAD_W + c * LANES, kind * HEAD_W + (c + 1) * LANES) for c in range(DIL_HEADS)]
        if kind < 2:
            g = qkg_ref[jj]
            slabs = [_rms(y[:, _head(c)], g[:, _head(c)]) for c in range(DIL_HEADS)]
        else:
            slabs = [y[:, _head(c)] for c in range(DIL_HEADS)]
        out = dil_refs[gi]
        if d == 1:
            for c in range(DIL_HEADS):
                out[:, ocols[c]] = slabs[c].astype(BF16)
            continue
        buf = n_strided % ybuf_ref.shape[0]
        n_strided += 1
        for c in range(DIL_HEADS):
            ybuf_ref[buf, c] = slabs[c]
        for r in range(d):
            for c in range(DIL_HEADS):
                out[0, r, :, ocols[c]] = ybuf_ref[buf, c, pl.ds(r, tm // d, stride=d), :].astype(BF16)


def _resident(shape):
    return pl.BlockSpec(shape, lambda i: (0,) * len(shape), pipeline_mode=pl.Buffered(1))


def _inproj(x2d, g1, w_all, w_a2, b_a, qk_gain, B, S):
    T = x2d.shape[0]
    tm = min(INPROJ_TM, S)
    tpb = S // tm

    def dil_spec(gi):
        d = DIL_PATTERNS[gi][1]
        if d == 1:
            return pl.BlockSpec((tm, QKV_W), lambda i: (i, 0))
        return pl.BlockSpec((1, d, tm // d, QKV_W), lambda i: (i // tpb, 0, i % tpb, 0))

    def dil_shape(gi):
        d = DIL_PATTERNS[gi][1]
        shape = (T, QKV_W) if d == 1 else (B, d, S // d, QKV_W)
        return jax.ShapeDtypeStruct(shape, BF16)

    return pl.pallas_call(
        _inproj_kernel,
        grid=(T // tm,),
        in_specs=[
            pl.BlockSpec((tm, D_MODEL), lambda i: (i, 0)),
            _resident(g1.shape), _resident(w_all.shape), _resident(w_a2.shape),
            _resident(b_a.shape), _resident(qk_gain.shape),
        ],
        out_specs=[
            pl.BlockSpec((tm, N_GLA_TILES * HEAD_W), lambda i: (i, 0)),
            pl.BlockSpec((tm, HEAD_W), lambda i: (i, 0)),
            dil_spec(0), dil_spec(1), dil_spec(2),
        ],
        out_shape=[
            jax.ShapeDtypeStruct((T, N_GLA_TILES * HEAD_W), BF16),
            jax.ShapeDtypeStruct((T, HEAD_W), F32),
            dil_shape(0), dil_shape(1), dil_shape(2),
        ],
        scratch_shapes=[pltpu.VMEM((tm, D_MODEL), BF16), pltpu.VMEM((2, DIL_HEADS, tm, LANES), F32),
                        pltpu.VMEM((D_MODEL, 3 * DIL_GROUPS * HEAD_W), BF16)],
        compiler_params=_params(("arbitrary",)),
        name="inproj",
    )(x2d, g1, w_all, w_a2, b_a, qk_gain)


def _gla_kernel(q_ref, k_ref, v_ref, r_ref, la_ref, tri_ref, gn_ref, o_ref, st_ref):
    n = pl.program_id(1)

    @pl.when(n == 0)
    def _():
        st_ref[...] = jnp.zeros_like(st_ref)

    for blk in range(q_ref.shape[0] // GLA_BLOCK):
        _gla_block(q_ref, k_ref, v_ref, r_ref, la_ref, tri_ref, gn_ref, o_ref, st_ref,
                   slice(blk * GLA_BLOCK, (blk + 1) * GLA_BLOCK))


def _gla_block(q_ref, k_ref, v_ref, r_ref, la_ref, tri_ref, gn_ref, o_ref, st_ref, rows):
    L = GLA_BLOCK
    la = la_ref[rows, :]
    hi = la.astype(BF16)
    r1 = la - hi.astype(F32)
    mid = r1.astype(BF16)
    lo = (r1 - mid.astype(F32)).astype(BF16)
    tri = tri_ref[...]
    b_all = _dot(tri, hi) + _dot(tri, mid) + _dot(tri, lo)

    row = lax.broadcasted_iota(jnp.int32, (L, L), 0)
    col = lax.broadcasted_iota(jnp.int32, (L, L), 1)

    for h in range(GLA_HEADS):
        sl = _head(h)
        b = b_all[:, sl]
        q = q_ref[rows, sl].astype(F32) * (GLA_DK ** -0.5)
        k = k_ref[rows, sl].astype(F32)
        v = v_ref[rows, sl]

        att = jnp.zeros((L, L), F32)
        seg = L
        while seg > GLA_SUBCHUNK:
            half = seg // 2
            beta = jnp.concatenate(
                [jnp.broadcast_to(b[s + half - 1:s + half, :], (seg, GLA_DK)) for s in range(0, L, seg)], axis=0)
            rel = b - beta
            neg = jnp.minimum(rel, 0.0)
            qs = (q * jnp.exp2(neg)).astype(BF16)
            ks = (k * jnp.exp2(neg - rel)).astype(BF16)
            same = (row // seg) == (col // seg)
            mask = same & ((row % seg) >= half) & ((col % seg) < half)
            att = jnp.where(mask, _dot_nt(qs, ks), att)
            seg = half
        beta = jnp.concatenate(
            [jnp.zeros((GLA_SUBCHUNK, GLA_DK), F32)]
            + [jnp.broadcast_to(b[s - 1:s, :], (GLA_SUBCHUNK, GLA_DK)) for s in range(GLA_SUBCHUNK, L, GLA_SUBCHUNK)],
            axis=0)
        qs = (q * jnp.exp2(b - beta)).astype(BF16)
        ks = (k * jnp.exp2(beta - b)).astype(BF16)
        mask = ((row // GLA_SUBCHUNK) == (col // GLA_SUBCHUNK)) & (col <= row)
        att = jnp.where(mask, _dot_nt(qs, ks), att)

        state_t = st_ref[h]
        q0 = (q * jnp.exp2(b)).astype(BF16)
        o = _dot(att.astype(BF16), v) + _dot_nt(q0, state_t.astype(BF16))

        b_last = b[L - 1:L, :]
        k_end = (k * jnp.exp2(b_last - b)).astype(BF16)
        st_ref[h] = state_t * jnp.exp2(b_last) + _dot_tn(v, k_end)

        rr = r_ref[rows, sl].astype(F32)
        o_ref[rows, sl] = (_rms(o, gn_ref[...]) * (rr * jax.nn.sigmoid(rr))).astype(BF16)


def _gla(proj, log_a, tri, gn, B, S):
    L = GLA_STEP_BLOCKS * GLA_BLOCK
    nb = S // L
    return pl.pallas_call(
        _gla_kernel,
        grid=(B, nb),
        in_specs=[
            pl.BlockSpec((L, HEAD_W), lambda b, n: (b * nb + n, 0)),
            pl.BlockSpec((L, HEAD_W), lambda b, n: (b * nb + n, 1)),
            pl.BlockSpec((L, HEAD_W), lambda b, n: (b * nb + n, 2)),
            pl.BlockSpec((L, HEAD_W), lambda b, n: (b * nb + n, 3)),
            pl.BlockSpec((L, HEAD_W), lambda b, n: (b * nb + n, 0)),
            pl.BlockSpec((GLA_BLOCK, GLA_BLOCK), lambda b, n: (0, 0)),
            pl.BlockSpec((1, GLA_DK), lambda b, n: (0, 0)),
        ],
        out_specs=pl.BlockSpec((L, HEAD_W), lambda b, n: (b * nb + n, 0)),
        out_shape=jax.ShapeDtypeStruct((B * S, HEAD_W), BF16),
        scratch_shapes=[pltpu.VMEM((GLA_HEADS, GLA_DK, GLA_DK), F32)],
        compiler_params=_params(("arbitrary", "arbitrary")),
        name="gla",
    )(proj, proj, proj, proj, log_a, tri, gn)


def _dil_kernel(q_ref, kp_ref, kc_ref, vp_ref, vc_ref, bias_ref, o_ref, st_ref, *, nr, nq):
    first = pl.program_id(2) == 0
    Q = DIL_BLOCK
    col = lax.broadcasted_iota(jnp.int32, (Q, 2 * Q), 1)
    lane = lax.broadcasted_iota(jnp.int32, (Q, LANES), 1)
    for res in range(nr):
        for blk in range(nq):
            rows = slice(blk * Q, (blk + 1) * Q)
            prev = slice((blk - 1) * Q, blk * Q)
            stats = jnp.zeros((Q, LANES), F32)
            for h in range(DIL_HEADS):
                sl = _head(h)
                if blk == 0:
                    kprev, vprev = kp_ref[0, res, :, sl], vp_ref[0, res, :, sl]
                else:
                    kprev, vprev = kc_ref[0, res, prev, sl], vc_ref[0, res, prev, sl]
                kk = jnp.concatenate([kprev, kc_ref[0, res, rows, sl]], axis=0)
                vv = jnp.concatenate([vprev, vc_ref[0, res, rows, sl]], axis=0)
                s = _dot_nt(q_ref[0, res, rows, sl], kk) + bias_ref[h]
                if blk == 0:
                    s = jnp.where(jnp.logical_and(first, col < Q), NEG, s)
                m = jnp.max(s, axis=-1, keepdims=True)
                p = jnp.exp2(s - m)
                den = jnp.sum(p, axis=-1, keepdims=True)
                o_ref[0, res, rows, sl] = (_dot(p.astype(BF16), vv) / den).astype(BF16)
                stats = jnp.where((lane // 16) == h, m, stats)
                stats = jnp.where((lane // 16) == DIL_HEADS + h, den, stats)
            st_ref[0, res, rows, :] = stats


def _dilated(qkv, bias, gi):
    B, d, Lsub, _ = qkv.shape
    nq = min(DIL_STEP_BLOCKS, Lsub // DIL_BLOCK)
    nr = min(DIL_STEP_BLOCKS // nq, d)
    rows = nq * DIL_BLOCK

    def cur(c):
        return pl.BlockSpec((1, nr, rows, HEAD_W), lambda b, r, i: (b, r, i, c))

    def prev(c):
        return pl.BlockSpec((1, nr, DIL_BLOCK, HEAD_W), lambda b, r, i: (b, r, jnp.maximum(i * nq - 1, 0), c))

    return pl.pallas_call(
        functools.partial(_dil_kernel, nr=nr, nq=nq),
        grid=(B, d // nr, Lsub // rows),
        in_specs=[cur(0), prev(1), cur(1), prev(2), cur(2),
                  pl.BlockSpec((DIL_HEADS, DIL_BLOCK, 2 * DIL_BLOCK), lambda b, r, i: (0, 0, 0))],
        out_specs=[
            pl.BlockSpec((1, nr, rows, HEAD_W), lambda b, r, i: (b, r, i, 0)),
            pl.BlockSpec((1, nr, rows, LANES), lambda b, r, i: (b, r, i, 0)),
        ],
        out_shape=[
            jax.ShapeDtypeStruct((B, d, Lsub, HEAD_W), BF16),
            jax.ShapeDtypeStruct((B, d, Lsub, LANES), F32),
        ],
        compiler_params=_params(("arbitrary", "arbitrary", "arbitrary")),
        name=f"dilated{gi}",
    )(qkv, qkv, qkv, qkv, qkv, bias)


def _alibi_bias(gi):
    window, d = DIL_PATTERNS[gi]
    Q = DIL_BLOCK
    dist = np.arange(Q)[:, None] + Q - np.arange(2 * Q)[None, :]
    valid = (dist >= 0) & (dist <= window // d)
    slopes = np.asarray(ALIBI_SLOPES, np.float32).reshape(DIL_GROUPS, DIL_HEADS)[gi]
    bias = -slopes[:, None, None] * (d * dist).astype(np.float32) * np.float32(LOG2_E)
    return jnp.asarray(np.where(valid[None], bias, np.float32(NEG)), F32)


def _merge_rows(x_ref, og_ref, o0_ref, s0_ref, g1_ref, wbg_ref, bbg_ref, wpg_ref, wpa_ref, wout_ref,
                g2_ref, wr_ref, br_ref, x2_ref, obuf_ref, sbuf_ref, row0, n):
    rows = slice(row0, row0 + n)
    x = x_ref[rows, :]
    hb = _rms(x, g1_ref[...]).astype(BF16)
    gates = jax.nn.sigmoid(_dot(hb, wbg_ref[...]) + bbg_ref[...])

    stats = (s0_ref[rows, :], sbuf_ref[0, rows, :], sbuf_ref[1, rows, :])
    dens = [pltpu.roll(s, 64, 1) for s in stats]
    m_all = jnp.maximum(jnp.maximum(stats[0], stats[1]), stats[2])
    wts = [d * jnp.exp2(s - m_all) for s, d in zip(stats, dens)]
    inv = 1.0 / (wts[0] + wts[1] + wts[2])
    coef = [w * inv for w in wts]
    heads = []
    for h in range(DIL_HEADS):
        group_out = (o0_ref[rows, _head(h)].astype(F32), obuf_ref[0, h, rows, :], obuf_ref[1, h, rows, :])
        acc = jnp.zeros((n, DIL_DH), F32)
        for g in range(DIL_GROUPS):
            c = jnp.broadcast_to(coef[g][:, 16 * h:16 * h + 1], (n, DIL_DH))
            acc = acc + c * group_out[g]
        heads.append(acc.astype(BF16))
    o_att = jnp.concatenate(heads, axis=1)

    y = gates[:, :D_MODEL] * _dot(og_ref[rows, :], wpg_ref[...]) + gates[:, D_MODEL:] * _dot(o_att, wpa_ref[...])
    x2 = x + _dot(y.astype(BF16), wout_ref[...])
    x2_ref[rows, :] = x2

    h2 = _rms(x2, g2_ref[...])

    h2h = h2.astype(BF16)
    h2l = (h2 - h2h.astype(F32)).astype(BF16)
    wr = wr_ref[...]
    pa = _dot(h2h, wr)
    pb = _dot(h2l, wr)
    logit = pa[:, :LANES] + pa[:, LANES:] + pb[:, :LANES] + pb[:, LANES:] + br_ref[...]

    lane = lax.broadcasted_iota(jnp.int32, (n, LANES), 1).astype(F32)
    big = jnp.float32(1e9)
    gl = jnp.where(lane < N_GROUPS, logit, NEG)
    gmax = jnp.max(gl, axis=-1, keepdims=True)
    gsel = jnp.min(jnp.where(gl == gmax, lane, big), axis=-1, keepdims=True)
    g_p = 1.0 / jnp.sum(jnp.exp(gl - gmax), axis=-1, keepdims=True)
    lo = N_GROUPS + EXPERTS_PER_GROUP * gsel
    el = jnp.where((lane >= lo) & (lane < lo + EXPERTS_PER_GROUP), logit, NEG)
    v1 = jnp.max(el, axis=-1, keepdims=True)
    i1 = jnp.min(jnp.where(el == v1, lane, big), axis=-1, keepdims=True)
    el2 = jnp.where(lane == i1, NEG, el)
    v2 = jnp.max(el2, axis=-1, keepdims=True)
    i2 = jnp.min(jnp.where(el2 == v2, lane, big), axis=-1, keepdims=True)
    ex = jnp.exp(v2 - v1)
    w1 = g_p / (1.0 + ex)
    w2 = g_p * ex / (1.0 + ex)
    return i1 - N_GROUPS, i2 - N_GROUPS, w1, w2, h2h


def _merge_kernel(x_ref, og_ref, o0_ref, o1_ref, o2_ref, s0_ref, s1_ref, s2_ref, g1_ref, wbg_ref, bbg_ref,
                  wpg_ref, wpa_ref, wout_ref, g2_ref, wr_ref, br_ref, tri_ref, upper_ref,
                  x2_ref, xloc_ref, route_ref, cnt_ref, obuf_ref, sbuf_ref):
    tm = x_ref.shape[0]

    for slot, (o_ref, s_ref) in enumerate(((o1_ref, s1_ref), (o2_ref, s2_ref))):
        d = DIL_PATTERNS[slot + 1][1]
        for r in range(d):
            sbuf_ref[slot, pl.ds(r, tm // d, stride=d), :] = s_ref[0, r]
            for c in range(DIL_HEADS):
                obuf_ref[slot, c, pl.ds(r, tm // d, stride=d), :] = o_ref[0, r, :, _head(c)].astype(F32)

    e1, e2, w1, w2, h2b = _merge_rows(x_ref, og_ref, o0_ref, s0_ref, g1_ref, wbg_ref, bbg_ref, wpg_ref, wpa_ref,
                                      wout_ref, g2_ref, wr_ref, br_ref, x2_ref, obuf_ref, sbuf_ref, 0, tm)

    lane = lax.broadcasted_iota(jnp.int32, (tm, LANES), 1).astype(F32)
    oh1 = lane == e1
    oh2 = lane == e2
    onehot = jnp.where(oh1 | oh2, 1.0, 0.0)
    prefix = _dot(tri_ref[...], onehot.astype(BF16))
    cnt = jnp.sum(onehot, axis=0, keepdims=True)
    cnt_pad = jnp.floor((cnt + (CHUNK - 1)) * (1.0 / CHUNK)) * CHUNK
    seg_off = _dot(jnp.broadcast_to(cnt_pad, (8, LANES)).astype(BF16), upper_ref[...])[0:1, :]
    base = seg_off + prefix
    pos1 = jnp.sum(jnp.where(oh1, base, 0.0), axis=-1, keepdims=True)
    pos2 = jnp.sum(jnp.where(oh2, base, 0.0), axis=-1, keepdims=True)
    cnt_ref[...] = jnp.broadcast_to(cnt, cnt_ref.shape)

    route = jnp.zeros((tm, LANES), F32)
    for idx, val in enumerate((e1, e2, w1, w2, pos1, pos2)):
        route = jnp.where(lane == idx, val, route)
    route_ref[...] = route

    route_t = route.T
    slot = lax.broadcasted_iota(jnp.int32, (LOCAL_ROWS, tm), 0).astype(F32)
    perm = jnp.where((slot == route_t[4:5, :]) | (slot == route_t[5:6, :]), 1.0, 0.0).astype(BF16)
    xloc_ref[...] = _pack_pairs(_dot(perm, h2b))


def _merge(x2d, o_gla, outs, stats, g1, wbg, bbg, wpg, wpa, wout, g2, wr, br, tri, upper, S):
    T = x2d.shape[0]
    tm = min(MERGE_TM, S)
    assert TOP_K * tm + N_EXPERTS * CHUNK <= LOCAL_ROWS
    tpb = S // tm
    tok = lambda w: pl.BlockSpec((tm, w), lambda i: (i, 0))
    full = lambda a: pl.BlockSpec(a.shape, lambda i: (0,) * a.ndim)

    def res(gi, w):
        d = DIL_PATTERNS[gi][1]
        return pl.BlockSpec((1, d, tm // d, w), lambda i: (i // tpb, 0, i % tpb, 0))

    return pl.pallas_call(
        _merge_kernel,
        grid=(T // tm,),
        in_specs=[tok(D_MODEL), tok(HEAD_W), tok(HEAD_W), res(1, HEAD_W), res(2, HEAD_W),
                  tok(LANES), res(1, LANES), res(2, LANES),
                  full(g1), full(wbg), full(bbg), full(wpg), full(wpa), full(wout), full(g2),
                  full(wr), full(br), full(tri), full(upper)],
        out_specs=[tok(D_MODEL), pl.BlockSpec((LOCAL_ROWS, PACKED_W), lambda i: (i, 0)), tok(LANES),
                   pl.BlockSpec((8, LANES), lambda i: (i, 0))],
        out_shape=[
            jax.ShapeDtypeStruct((T, D_MODEL), F32),
            jax.ShapeDtypeStruct((T // tm * LOCAL_ROWS, PACKED_W), F32),
            jax.ShapeDtypeStruct((T, LANES), F32),
            jax.ShapeDtypeStruct((T // tm * 8, LANES), F32),
        ],
        scratch_shapes=[pltpu.VMEM((DIL_GROUPS - 1, DIL_HEADS, tm, LANES), F32),
                        pltpu.VMEM((DIL_GROUPS - 1, tm, LANES), F32)],
        compiler_params=_params(("arbitrary",)),
        name="merge",
    )(x2d, o_gla, *outs, *stats, g1, wbg, bbg, wpg, wpa, wout, g2, wr, br, tri, upper)


class _ChunkGather:
    def __init__(self, src_hbm, buf_ref, sem, nchunks):
        self.src, self.buf, self.sem, self.n = src_hbm, buf_ref, sem, nchunks

    def start(self, offs_ref, s):
        for j in range(self.n):
            row = pl.multiple_of(offs_ref[0, 0, j], CHUNK)
            pltpu.make_async_copy(self.src.at[pl.ds(row, CHUNK), :], self.buf.at[s, pl.ds(j * CHUNK, CHUNK), :],
                                  self.sem.at[s]).start()

    def wait(self, s):
        pltpu.make_async_copy(self.src.at[pl.ds(0, self.n * CHUNK), :], self.buf.at[s], self.sem.at[s]).wait()

    def step(self, offs_cur_ref, offs_next_ref, used_steps=None):
        i = pl.program_id(0)
        slot = lax.rem(i, 2)
        last = pl.num_programs(0) if used_steps is None else jnp.minimum(used_steps, pl.num_programs(0))

        @pl.when(jnp.logical_and(i == 0, i < last))
        def _():
            self.start(offs_cur_ref, 0)

        @pl.when(i < last)
        def _():
            self.wait(slot)

        @pl.when(i + 1 < last)
        def _():
            self.start(offs_next_ref, 1 - slot)

        return slot


def _expert_kernel(tile_e_ref, nact_ref, src_cur_ref, src_next_ref, xloc_ref, wg_ref, wu_ref, wd_ref,
                   y_ref, xbuf_ref, wgu_ref, wdn_ref, gsem):
    i = pl.program_id(0)
    slot = _ChunkGather(xloc_ref, xbuf_ref, gsem, TILE_CHUNKS).step(src_cur_ref, src_next_ref, nact_ref[0])
    active = i < nact_ref[0]

    @pl.when(jnp.logical_and(active, jnp.logical_or(i == 0, tile_e_ref[i] != tile_e_ref[jnp.maximum(i - 1, 0)])))
    def _():
        wgu_ref[:, :D_EXPERT] = wg_ref[0].astype(BF16)
        wgu_ref[:, D_EXPERT:] = wu_ref[0].astype(BF16)
        wdn_ref[...] = wd_ref[0].astype(BF16)

    @pl.when(active)
    def _():
        xb = _unpack_pairs(xbuf_ref[slot])
        gu = _dot(xb, wgu_ref[...])
        g, u = gu[:, :D_EXPERT], gu[:, D_EXPERT:]
        a = (g * jax.nn.sigmoid(g) * u).astype(BF16)
        y_ref[...] = _pack_pairs(_dot(a, wdn_ref[...]).astype(BF16).astype(F32))

    @pl.when(jnp.logical_not(active))
    def _():
        y_ref[...] = jnp.zeros_like(y_ref)


def _experts(tile_e, nact, csrc, xloc, wg, wu, wd):
    ntiles = tile_e.shape[0]
    R = MOE_TM

    def wsel(i, te, na):
        return (te[jnp.minimum(i, na[0] - 1)], 0, 0)

    def offs(shift):
        return pl.BlockSpec((1, 1, TILE_CHUNKS), lambda i, te, na: (jnp.minimum(i + shift, ntiles - 1), 0, 0),
                            memory_space=pltpu.SMEM)

    return pl.pallas_call(
        _expert_kernel,
        grid_spec=pltpu.PrefetchScalarGridSpec(
            num_scalar_prefetch=2,
            grid=(ntiles,),
            in_specs=[
                offs(0), offs(1),
                pl.BlockSpec(memory_space=pl.ANY),
                pl.BlockSpec((1, D_MODEL, D_EXPERT), wsel),
                pl.BlockSpec((1, D_MODEL, D_EXPERT), wsel),
                pl.BlockSpec((1, D_EXPERT, D_MODEL), wsel),
            ],
            out_specs=pl.BlockSpec((R, PACKED_W), lambda i, te, na: (i, 0)),
            scratch_shapes=[pltpu.VMEM((2, R, PACKED_W), F32), pltpu.VMEM((D_MODEL, 2 * D_EXPERT), BF16),
                            pltpu.VMEM((D_EXPERT, D_MODEL), BF16), pltpu.SemaphoreType.DMA((2,))],
        ),
        out_shape=jax.ShapeDtypeStruct((ntiles * R, PACKED_W), F32),
        compiler_params=_params(("arbitrary",)),
        name="experts",
    )(tile_e, nact, csrc, csrc, xloc, wg, wu, wd)


def _combine_kernel(dst_cur_ref, dst_next_ref, x2_ref, route_ref, y_ref, out_ref, ybuf_ref, gsem):
    slot = _ChunkGather(y_ref, ybuf_ref, gsem, LOCAL_CHUNKS).step(dst_cur_ref, dst_next_ref)
    tm = x2_ref.shape[0]
    route = route_ref[...]
    w1, w2, pos1, pos2 = (route[:, c:c + 1] for c in range(2, 6))
    yl = _unpack_pairs(ybuf_ref[slot])
    cols = lax.broadcasted_iota(jnp.int32, (tm, LOCAL_ROWS), 1).astype(F32)
    pick1 = jnp.where(cols == pos1, 1.0, 0.0).astype(BF16)
    pick2 = jnp.where(cols == pos2, 1.0, 0.0).astype(BF16)
    out_ref[...] = x2_ref[...] + w1 * _dot(pick1, yl) + w2 * _dot(pick2, yl)


def _combine(cdst, x2, route, y):
    T = x2.shape[0]
    nt = cdst.shape[0]
    tm = T // nt

    def offs(shift):
        return pl.BlockSpec((1, 1, LOCAL_CHUNKS), lambda i: (jnp.minimum(i + shift, nt - 1), 0, 0),
                            memory_space=pltpu.SMEM)

    return pl.pallas_call(
        _combine_kernel,
        grid=(nt,),
        in_specs=[
            offs(0), offs(1),
            pl.BlockSpec((tm, D_MODEL), lambda i: (i, 0)),
            pl.BlockSpec((tm, LANES), lambda i: (i, 0)),
            pl.BlockSpec(memory_space=pl.ANY),
        ],
        out_specs=pl.BlockSpec((tm, D_MODEL), lambda i: (i, 0)),
        out_shape=jax.ShapeDtypeStruct((T, D_MODEL), F32),
        scratch_shapes=[pltpu.VMEM((2, LOCAL_ROWS, PACKED_W), F32), pltpu.SemaphoreType.DMA((2,))],
        compiler_params=_params(("arbitrary",)),
        name="combine",
    )(cdst, cdst, x2, route, y)


def _layer(x2d, B, S, norm1_g, w_in, w_gla_a2, b_gla_a, gla_out_norm_g, dil_q_norm_g, dil_k_norm_g,
           w_proj_gla, w_proj_attn, w_branch_gate, b_branch_gate, w_out, norm2_g,
           w_router_group, b_router_group, w_router_expert, b_router_expert, w_gate, w_up, w_down):
    T = B * S
    w_a2 = jnp.pad(w_gla_a2, ((0, LANES - GLA_RANK), (0, 0))).astype(BF16)
    gains = [jnp.ones((HEAD_W,), F32)] * N_GLA_TILES
    for gi in range(DIL_GROUPS):
        gains += [jnp.tile(dil_q_norm_g[gi], DIL_HEADS) * (DIL_DH ** -0.5 * LOG2_E),
                  jnp.tile(dil_k_norm_g[gi], DIL_HEADS), jnp.ones((HEAD_W,), F32)]
    qk_gain = jnp.stack(gains).reshape(N_COL_TILES, 1, HEAD_W)

    gla_in, log_a, qkv0, qkv1, qkv2 = _inproj(
        x2d, norm1_g.reshape(1, -1), w_in.astype(BF16), w_a2, b_gla_a.reshape(1, -1), qk_gain, B, S)

    tri_incl = jnp.asarray(np.tril(np.ones((GLA_BLOCK, GLA_BLOCK), np.float32)), BF16)
    o_gla = _gla(gla_in, log_a, tri_incl, gla_out_norm_g.reshape(1, -1), B, S)

    outs, stats = [], []
    for gi, qkv in enumerate((qkv0.reshape(B, 1, S, QKV_W), qkv1, qkv2)):
        o, st = _dilated(qkv, _alibi_bias(gi), gi)
        outs.append(o)
        stats.append(st)
    outs[0] = outs[0].reshape(T, HEAD_W)
    stats[0] = stats[0].reshape(T, LANES)

    w_r = jnp.pad(jnp.concatenate([w_router_group, w_router_expert], axis=1),
                  ((0, 0), (0, LANES - N_GROUPS - N_EXPERTS)))
    w_rh = w_r.astype(BF16)
    w_rl = (w_r - w_rh.astype(F32)).astype(BF16)
    b_r = jnp.pad(jnp.concatenate([b_router_group, b_router_expert]), (0, LANES - N_GROUPS - N_EXPERTS))
    tm = min(MERGE_TM, S)
    tri_strict = jnp.asarray(np.tril(np.ones((tm, tm), np.float32), -1), BF16)
    upper = jnp.asarray(np.triu(np.ones((LANES, LANES), np.float32), 1), BF16)
    x2, xloc, route, cnt = _merge(
        x2d, o_gla, outs, stats, norm1_g.reshape(1, -1), w_branch_gate.astype(BF16),
        b_branch_gate.reshape(1, -1), w_proj_gla.astype(BF16), w_proj_attn.astype(BF16), w_out.astype(BF16),
        norm2_g.reshape(1, -1), jnp.concatenate([w_rh, w_rl], axis=1), b_r.reshape(1, -1), tri_strict, upper, S)

    csrc, cdst, tile_e, nact = _chunk_tables(cnt, T // tm, tm)
    y = _experts(tile_e, nact, csrc, xloc, w_gate, w_up, w_down)
    return _combine(cdst, x2, route, y)


def _chunk_tables(cnt, nt, tm):
    i32 = jnp.int32
    experts = jnp.arange(N_EXPERTS, dtype=i32)
    tiles = jnp.arange(nt, dtype=i32)
    c = (cnt.reshape(nt, 8, LANES)[:, 0, :N_EXPERTS].astype(i32) + CHUNK - 1) // CHUNK
    loff = jnp.cumsum(c, axis=1) - c
    ecum = jnp.cumsum(c, axis=0) - c
    tot = jnp.sum(c, axis=0)
    ptot = (tot + TILE_CHUNKS - 1) // TILE_CHUNKS * TILE_CHUNKS
    pend = jnp.cumsum(ptot)
    pstart = pend - ptot

    lc = jnp.arange(LOCAL_CHUNKS, dtype=i32)
    e_of = jnp.sum(((loff + c)[:, None, :] <= lc[None, :, None]).astype(i32), axis=-1)
    pick_e = e_of[:, :, None] == experts[None, None, :]
    glob = jnp.sum(jnp.where(pick_e, (pstart[None, :] + ecum - loff)[:, None, :], 0), axis=-1) + lc[None, :]
    max_chunks = (TOP_K * nt * tm + (CHUNK - 1) * nt * N_EXPERTS) // CHUNK + N_EXPERTS * (TILE_CHUNKS - 1)
    ntiles = -(-max_chunks // TILE_CHUNKS)
    spread = (tiles[:, None] * LOCAL_CHUNKS + lc[None, :]) % (ntiles * TILE_CHUNKS)
    cdst = jnp.where(e_of < N_EXPERTS, glob, spread) * CHUNK

    g = jnp.arange(ntiles * TILE_CHUNKS, dtype=i32)
    e_g = jnp.minimum(jnp.sum((pend[None, :] <= g[:, None]).astype(i32), axis=1), N_EXPERTS - 1)
    k = g - jnp.sum(jnp.where(e_g[:, None] == experts[None, :], pstart[None, :], 0), axis=1)
    pick = (e_g[:, None] == experts[None, :]).astype(F32)
    ends = jnp.dot(pick, (ecum + c).T.astype(F32), precision=lax.Precision.HIGHEST)
    t_g = jnp.sum((ends <= k[:, None].astype(F32)).astype(i32), axis=1)
    valid = t_g < nt
    base = jnp.dot(pick, (tiles[:, None] * LOCAL_CHUNKS + loff - ecum).T.astype(F32),
                   precision=lax.Precision.HIGHEST)
    in_tile = tiles[None, :] == jnp.minimum(t_g, nt - 1)[:, None]
    local = jnp.sum(jnp.where(in_tile, base, 0.0), axis=1).astype(i32) + k
    spare = LOCAL_CHUNKS - -(-(TOP_K * tm + N_EXPERTS * (CHUNK - 1)) // CHUNK)
    assert spare >= 1
    zero_chunk = (g % nt) * LOCAL_CHUNKS + LOCAL_CHUNKS - 1 - (g // nt) % spare
    csrc = jnp.where(valid, local, zero_chunk) * CHUNK

    tile_e = e_g[::TILE_CHUNKS]
    nact = (pend[-1:] // TILE_CHUNKS).astype(i32)
    return (csrc.reshape(ntiles, 1, TILE_CHUNKS), cdst.reshape(nt, 1, LOCAL_CHUNKS), tile_e, nact)


def kernel(x, norm1_g, w_in, w_gla_a2, b_gla_a, gla_out_norm_g, dil_q_norm_g, dil_k_norm_g, w_proj_gla,
           w_proj_attn, w_branch_gate, b_branch_gate, w_out, norm2_g, w_router_group, b_router_group,
           w_router_expert, b_router_expert, w_gate, w_up, w_down):
    B, S, D = x.shape
    assert D == D_MODEL and S % (DIL_BLOCK * DIL_PATTERNS[-1][1]) == 0
    x2d = x.reshape(B * S, D)
    params = (norm1_g, w_in, w_gla_a2, b_gla_a, gla_out_norm_g, dil_q_norm_g, dil_k_norm_g, w_proj_gla,
              w_proj_attn, w_branch_gate, b_branch_gate, w_out, norm2_g, w_router_group, b_router_group,
              w_router_expert, b_router_expert, w_gate, w_up, w_down)
    for layer in range(norm1_g.shape[0]):
        x2d = _layer(x2d, B, S, *(p[layer] for p in params))
    return x2d.reshape(B, S, D)
```

```python
import functools

import jax
import jax.numpy as jnp
import numpy as np
from jax import lax
from jax.experimental import pallas as pl
from jax.experimental.pallas import tpu as pltpu

F32 = jnp.float32
BF16 = jnp.bfloat16

D_MODEL = 1024
EPS = 1e-6
GLA_HEADS = 4
GLA_DK = 128
GLA_RANK = 16
GLA_TAU = 16.0
LOG2_E = 1.4426950408889634
GLA_SUBCHUNK = 16
GLA_BLOCK = 128
GLA_STEP_BLOCKS = 16
DIL_PATTERNS = ((128, 1), (512, 4), (2048, 16))
DIL_GROUPS = 3
DIL_HEADS = 4
DIL_DH = 128
DIL_BLOCK = 128
ALIBI_SLOPES = tuple(2.0 ** (-8.0 * (i + 1) / (DIL_GROUPS * DIL_HEADS)) for i in range(DIL_GROUPS * DIL_HEADS))
N_GROUPS = 4
EXPERTS_PER_GROUP = 8
N_EXPERTS = N_GROUPS * EXPERTS_PER_GROUP
TOP_K = 2
D_EXPERT = 512

HEAD_W = 512
N_GLA_TILES = 4
N_COL_TILES = N_GLA_TILES + 3 * DIL_GROUPS
QKV_W = 3 * HEAD_W
LANES = 128
NEG = -1e30

INPROJ_TM = 512
MERGE_TM = 512
MOE_TM = 512
CHUNK = 8
TILE_CHUNKS = MOE_TM // CHUNK
LOCAL_ROWS = -(-(TOP_K * MERGE_TM + N_EXPERTS * CHUNK) // LANES) * LANES
LOCAL_CHUNKS = LOCAL_ROWS // CHUNK
DIL_STEP_BLOCKS = 32
VMEM_LIMIT = 56 * 1024 * 1024


def _dot(a, b):
    return jnp.dot(a, b, preferred_element_type=F32)


def _dot_nt(a, b):
    return lax.dot_general(a, b, (((1,), (1,)), ((), ())), preferred_element_type=F32)


def _dot_tn(a, b):
    return lax.dot_general(a, b, (((0,), (0,)), ((), ())), preferred_element_type=F32)


def _rms(x, g):
    return x * lax.rsqrt(jnp.mean(x * x, axis=-1, keepdims=True) + EPS) * g


def _params(sem):
    return pltpu.CompilerParams(dimension_semantics=sem, vmem_limit_bytes=VMEM_LIMIT)


def _head(c):
    return slice(c * LANES, (c + 1) * LANES)


PACKED_W = D_MODEL // 2
_HI16 = 0xFFFF0000


def _pack_pairs(x):
    u = lax.bitcast_convert_type(x, jnp.uint32)
    w = x.shape[1] // 2
    return lax.bitcast_convert_type((u[:, :w] & jnp.uint32(_HI16)) | (u[:, w:] >> 16), F32)


def _unpack_pairs(p):
    u = lax.bitcast_convert_type(p, jnp.uint32)
    hi = lax.bitcast_convert_type(u & jnp.uint32(_HI16), F32)
    lo = lax.bitcast_convert_type(u << 16, F32)
    return jnp.concatenate([hi, lo], axis=1).astype(BF16)


def _inproj_kernel(x_ref, g1_ref, w_ref, wa2_ref, ba_ref, qkg_ref,
                   gla_ref, loga_ref, d0_ref, d1_ref, d2_ref, h_ref, ybuf_ref, wdil_ref):
    tm = x_ref.shape[0]
    n_gla = N_GLA_TILES * HEAD_W

    @pl.when(pl.program_id(0) == 0)
    def _():
        for k in range(3 * DIL_GROUPS):
            first = n_gla + GLA_RANK + k * HEAD_W
            wdil_ref[:, k * HEAD_W:(k + 1) * HEAD_W] = w_ref[:, first:first + HEAD_W]

    hb = _rms(x_ref[...], g1_ref[...]).astype(BF16)
    h_ref[...] = hb
    ga = _dot(hb, w_ref[:, n_gla:n_gla + LANES])
    z = _dot(ga.astype(BF16), wa2_ref[...]) + ba_ref[...]
    log_sig = jnp.minimum(z, 0.0) - jnp.log(1.0 + jnp.exp(-jnp.abs(z)))
    loga_ref[...] = log_sig * (LOG2_E / GLA_TAU)

    dil_refs = (d0_ref, d1_ref, d2_ref)
    n_strided = 0
    for jj in range(N_COL_TILES):
        cols = slice(jj * HEAD_W, (jj + 1) * HEAD_W)
        if jj < N_GLA_TILES:
            w = w_ref[:, cols]
        else:
            w = wdil_ref[:, (jj - N_GLA_TILES) * HEAD_W:(jj - N_GLA_TILES + 1) * HEAD_W]
        y = _dot(h_ref[...], w)
        if jj < N_GLA_TILES:
            gla_ref[:, cols] = y.astype(BF16)
            continue
        gi, kind = divmod(jj - N_GLA_TILES, 3)
        d = DIL_PATTERNS[gi][1]
        ocols = [slice(kind * HEAD_W + c * LANES, kind * HEAD_W + (c + 1) * LANES) for c in range(DIL_HEADS)]
        if kind < 2:
            g = qkg_ref[jj]
            slabs = [_rms(y[:, _head(c)], g[:, _head(c)]) for c in range(DIL_HEADS)]
        else:
            slabs = [y[:, _head(c)] for c in range(DIL_HEADS)]
        out = dil_refs[gi]
        if d == 1:
            for c in range(DIL_HEADS):
                out[:, ocols[c]] = slabs[c].astype(BF16)
            continue
        buf = n_strided % ybuf_ref.shape[0]
        n_strided += 1
        for c in range(DIL_HEADS):
            ybuf_ref[buf, c] = slabs[c]
        for r in range(d):
            for c in range(DIL_HEADS):
                out[0, r, :, ocols[c]] = ybuf_ref[buf, c, pl.ds(r, tm // d, stride=d), :].astype(BF16)


def _resident(shape):
    return pl.BlockSpec(shape, lambda i: (0,) * len(shape), pipeline_mode=pl.Buffered(1))


def _inproj(x2d, g1, w_all, w_a2, b_a, qk_gain, B, S):
    T = x2d.shape[0]
    tm = min(INPROJ_TM, S)
    tpb = S // tm

    def dil_spec(gi):
        d = DIL_PATTERNS[gi][1]
        if d == 1:
            return pl.BlockSpec((tm, QKV_W), lambda i: (i, 0))
        return pl.BlockSpec((1, d, tm // d, QKV_W), lambda i: (i // tpb, 0, i % tpb, 0))

    def dil_shape(gi):
        d = DIL_PATTERNS[gi][1]
        shape = (T, QKV_W) if d == 1 else (B, d, S // d, QKV_W)
        return jax.ShapeDtypeStruct(shape, BF16)

    return pl.pallas_call(
        _inproj_kernel,
        grid=(T // tm,),
        in_specs=[
            pl.BlockSpec((tm, D_MODEL), lambda i: (i, 0)),
            _resident(g1.shape), _resident(w_all.shape), _resident(w_a2.shape),
            _resident(b_a.shape), _resident(qk_gain.shape),
        ],
        out_specs=[
            pl.BlockSpec((tm, N_GLA_TILES * HEAD_W), lambda i: (i, 0)),
            pl.BlockSpec((tm, HEAD_W), lambda i: (i, 0)),
            dil_spec(0), dil_spec(1), dil_spec(2),
        ],
        out_shape=[
            jax.ShapeDtypeStruct((T, N_GLA_TILES * HEAD_W), BF16),
            jax.ShapeDtypeStruct((T, HEAD_W), F32),
            dil_shape(0), dil_shape(1), dil_shape(2),
        ],
        scratch_shapes=[pltpu.VMEM((tm, D_MODEL), BF16), pltpu.VMEM((2, DIL_HEADS, tm, LANES), F32),
                        pltpu.VMEM((D_MODEL, 3 * DIL_GROUPS * HEAD_W), BF16)],
        compiler_params=_params(("arbitrary",)),
        name="inproj",
    )(x2d, g1, w_all, w_a2, b_a, qk_gain)


def _gla_kernel(q_ref, k_ref, v_ref, r_ref, la_ref, tri_ref, gn_ref, o_ref, st_ref):
    n = pl.program_id(1)

    @pl.when(n == 0)
    def _():
        st_ref[...] = jnp.zeros_like(st_ref)

    for blk in range(q_ref.shape[0] // GLA_BLOCK):
        _gla_block(q_ref, k_ref, v_ref, r_ref, la_ref, tri_ref, gn_ref, o_ref, st_ref,
                   slice(blk * GLA_BLOCK, (blk + 1) * GLA_BLOCK))


def _gla_block(q_ref, k_ref, v_ref, r_ref, la_ref, tri_ref, gn_ref, o_ref, st_ref, rows):
    L = GLA_BLOCK
    la = la_ref[rows, :]
    hi = la.astype(BF16)
    r1 = la - hi.astype(F32)
    mid = r1.astype(BF16)
    lo = (r1 - mid.astype(F32)).astype(BF16)
    tri = tri_ref[...]
    b_all = _dot(tri, hi) + _dot(tri, mid) + _dot(tri, lo)

    row = lax.broadcasted_iota(jnp.int32, (L, L), 0)
    col = lax.broadcasted_iota(jnp.int32, (L, L), 1)

    for h in range(GLA_HEADS):
        sl = _head(h)
        b = b_all[:, sl]
        q = q_ref[rows, sl].astype(F32) * (GLA_DK ** -0.5)
        k = k_ref[rows, sl].astype(F32)
        v = v_ref[rows, sl]

        att = jnp.zeros((L, L), F32)
        seg = L
        while seg > GLA_SUBCHUNK:
            half = seg // 2
            beta = jnp.concatenate(
                [jnp.broadcast_to(b[s + half - 1:s + half, :], (seg, GLA_DK)) for s in range(0, L, seg)], axis=0)
            rel = b - beta
            neg = jnp.minimum(rel, 0.0)
            qs = (q * jnp.exp2(neg)).astype(BF16)
            ks = (k * jnp.exp2(neg - rel)).astype(BF16)
            same = (row // seg) == (col // seg)
            mask = same & ((row % seg) >= half) & ((col % seg) < half)
            att = jnp.where(mask, _dot_nt(qs, ks), att)
            seg = half
        beta = jnp.concatenate(
            [jnp.zeros((GLA_SUBCHUNK, GLA_DK), F32)]
            + [jnp.broadcast_to(b[s - 1:s, :], (GLA_SUBCHUNK, GLA_DK)) for s in range(GLA_SUBCHUNK, L, GLA_SUBCHUNK)],
            axis=0)
        qs = (q * jnp.exp2(b - beta)).astype(BF16)
        ks = (k * jnp.exp2(beta - b)).astype(BF16)
        mask = ((row // GLA_SUBCHUNK) == (col // GLA_SUBCHUNK)) & (col <= row)
        att = jnp.where(mask, _dot_nt(qs, ks), att)

        state_t = st_ref[h]
        q0 = (q * jnp.exp2(b)).astype(BF16)
        o = _dot(att.astype(BF16), v) + _dot_nt(q0, state_t.astype(BF16))

        b_last = b[L - 1:L, :]
        k_end = (k * jnp.exp2(b_last - b)).astype(BF16)
        st_ref[h] = state_t * jnp.exp2(b_last) + _dot_tn(v, k_end)

        rr = r_ref[rows, sl].astype(F32)
        o_ref[rows, sl] = (_rms(o, gn_ref[...]) * (rr * jax.nn.sigmoid(rr))).astype(BF16)


def _gla(proj, log_a, tri, gn, B, S):
    L = GLA_STEP_BLOCKS * GLA_BLOCK
    nb = S // L
    return pl.pallas_call(
        _gla_kernel,
        grid=(B, nb),
        in_specs=[
            pl.BlockSpec((L, HEAD_W), lambda b, n: (b * nb + n, 0)),
            pl.BlockSpec((L, HEAD_W), lambda b, n: (b * nb + n, 1)),
            pl.BlockSpec((L, HEAD_W), lambda b, n: (b * nb + n, 2)),
            pl.BlockSpec((L, HEAD_W), lambda b, n: (b * nb + n, 3)),
            pl.BlockSpec((L, HEAD_W), lambda b, n: (b * nb + n, 0)),
            pl.BlockSpec((GLA_BLOCK, GLA_BLOCK), lambda b, n: (0, 0)),
            pl.BlockSpec((1, GLA_DK), lambda b, n: (0, 0)),
        ],
        out_specs=pl.BlockSpec((L, HEAD_W), lambda b, n: (b * nb + n, 0)),
        out_shape=jax.ShapeDtypeStruct((B * S, HEAD_W), BF16),
        scratch_shapes=[pltpu.VMEM((GLA_HEADS, GLA_DK, GLA_DK), F32)],
        compiler_params=_params(("arbitrary", "arbitrary")),
        name="gla",
    )(proj, proj, proj, proj, log_a, tri, gn)


def _dil_kernel(q_ref, kp_ref, kc_ref, vp_ref, vc_ref, bias_ref, o_ref, st_ref, *, nr, nq):
    first = pl.program_id(2) == 0
    Q = DIL_BLOCK
    col = lax.broadcasted_iota(jnp.int32, (Q, 2 * Q), 1)
    lane = lax.broadcasted_iota(jnp.int32, (Q, LANES), 1)
    for res in range(nr):
        for blk in range(nq):
            rows = slice(blk * Q, (blk + 1) * Q)
            prev = slice((blk - 1) * Q, blk * Q)
            stats = jnp.zeros((Q, LANES), F32)
            for h in range(DIL_HEADS):
                sl = _head(h)
                if blk == 0:
                    kprev, vprev = kp_ref[0, res, :, sl], vp_ref[0, res, :, sl]
                else:
                    kprev, vprev = kc_ref[0, res, prev, sl], vc_ref[0, res, prev, sl]
                kk = jnp.concatenate([kprev, kc_ref[0, res, rows, sl]], axis=0)
                vv = jnp.concatenate([vprev, vc_ref[0, res, rows, sl]], axis=0)
                s = _dot_nt(q_ref[0, res, rows, sl], kk) + bias_ref[h]
                if blk == 0:
                    s = jnp.where(jnp.logical_and(first, col < Q), NEG, s)
                m = jnp.max(s, axis=-1, keepdims=True)
                p = jnp.exp2(s - m)
                den = jnp.sum(p, axis=-1, keepdims=True)
                o_ref[0, res, rows, sl] = (_dot(p.astype(BF16), vv) / den).astype(BF16)
                stats = jnp.where((lane // 16) == h, m, stats)
                stats = jnp.where((lane // 16) == DIL_HEADS + h, den, stats)
            st_ref[0, res, rows, :] = stats


def _dilated(qkv, bias, gi):
    B, d, Lsub, _ = qkv.shape
    nq = min(DIL_STEP_BLOCKS, Lsub // DIL_BLOCK)
    nr = min(DIL_STEP_BLOCKS // nq, d)
    rows = nq * DIL_BLOCK

    def cur(c):
        return pl.BlockSpec((1, nr, rows, HEAD_W), lambda b, r, i: (b, r, i, c))

    def prev(c):
        return pl.BlockSpec((1, nr, DIL_BLOCK, HEAD_W), lambda b, r, i: (b, r, jnp.maximum(i * nq - 1, 0), c))

    return pl.pallas_call(
        functools.partial(_dil_kernel, nr=nr, nq=nq),
        grid=(B, d // nr, Lsub // rows),
        in_specs=[cur(0), prev(1), cur(1), prev(2), cur(2),
                  pl.BlockSpec((DIL_HEADS, DIL_BLOCK, 2 * DIL_BLOCK), lambda b, r, i: (0, 0, 0))],
        out_specs=[
            pl.BlockSpec((1, nr, rows, HEAD_W), lambda b, r, i: (b, r, i, 0)),
            pl.BlockSpec((1, nr, rows, LANES), lambda b, r, i: (b, r, i, 0)),
        ],
        out_shape=[
            jax.ShapeDtypeStruct((B, d, Lsub, HEAD_W), BF16),
            jax.ShapeDtypeStruct((B, d, Lsub, LANES), F32),
        ],
        compiler_params=_params(("arbitrary", "arbitrary", "arbitrary")),
        name=f"dilated{gi}",
    )(qkv, qkv, qkv, qkv, qkv, bias)


def _alibi_bias(gi):
    window, d = DIL_PATTERNS[gi]
    Q = DIL_BLOCK
    dist = np.arange(Q)[:, None] + Q - np.arange(2 * Q)[None, :]
    valid = (dist >= 0) & (dist <= window // d)
    slopes = np.asarray(ALIBI_SLOPES, np.float32).reshape(DIL_GROUPS, DIL_HEADS)[gi]
    bias = -slopes[:, None, None] * (d * dist).astype(np.float32) * np.float32(LOG2_E)
    return jnp.asarray(np.where(valid[None], bias, np.float32(NEG)), F32)


def _merge_rows(x_ref, og_ref, o0_ref, s0_ref, g1_ref, wbg_ref, bbg_ref, wpg_ref, wpa_ref, wout_ref,
                g2_ref, wr_ref, br_ref, x2_ref, obuf_ref, sbuf_ref, row0, n):
    rows = slice(row0, row0 + n)
    x = x_ref[rows, :]
    hb = _rms(x, g1_ref[...]).astype(BF16)
    gates = jax.nn.sigmoid(_dot(hb, wbg_ref[...]) + bbg_ref[...])

    stats = (s0_ref[rows, :], sbuf_ref[0, rows, :], sbuf_ref[1, rows, :])
    dens = [pltpu.roll(s, 64, 1) for s in stats]
    m_all = jnp.maximum(jnp.maximum(stats[0], stats[1]), stats[2])
    wts = [d * jnp.exp2(s - m_all) for s, d in zip(stats, dens)]
    inv = 1.0 / (wts[0] + wts[1] + wts[2])
    coef = [w * inv for w in wts]
    heads = []
    for h in range(DIL_HEADS):
        group_out = (o0_ref[rows, _head(h)].astype(F32), obuf_ref[0, h, rows, :], obuf_ref[1, h, rows, :])
        acc = jnp.zeros((n, DIL_DH), F32)
        for g in range(DIL_GROUPS):
            c = jnp.broadcast_to(coef[g][:, 16 * h:16 * h + 1], (n, DIL_DH))
            acc = acc + c * group_out[g]
        heads.append(acc.astype(BF16))
    o_att = jnp.concatenate(heads, axis=1)

    y = gates[:, :D_MODEL] * _dot(og_ref[rows, :], wpg_ref[...]) + gates[:, D_MODEL:] * _dot(o_att, wpa_ref[...])
    x2 = x + _dot(y.astype(BF16), wout_ref[...])
    x2_ref[rows, :] = x2

    h2 = _rms(x2, g2_ref[...])

    h2h = h2.astype(BF16)
    h2l = (h2 - h2h.astype(F32)).astype(BF16)
    wr = wr_ref[...]
    pa = _dot(h2h, wr)
    pb = _dot(h2l, wr)
    logit = pa[:, :LANES] + pa[:, LANES:] + pb[:, :LANES] + pb[:, LANES:] + br_ref[...]

    lane = lax.broadcasted_iota(jnp.int32, (n, LANES), 1).astype(F32)
    big = jnp.float32(1e9)
    gl = jnp.where(lane < N_GROUPS, logit, NEG)
    gmax = jnp.max(gl, axis=-1, keepdims=True)
    gsel = jnp.min(jnp.where(gl == gmax, lane, big), axis=-1, keepdims=True)
    g_p = 1.0 / jnp.sum(jnp.exp(gl - gmax), axis=-1, keepdims=True)
    lo = N_GROUPS + EXPERTS_PER_GROUP * gsel
    el = jnp.where((lane >= lo) & (lane < lo + EXPERTS_PER_GROUP), logit, NEG)
    v1 = jnp.max(el, axis=-1, keepdims=True)
    i1 = jnp.min(jnp.where(el == v1, lane, big), axis=-1, keepdims=True)
    el2 = jnp.where(lane == i1, NEG, el)
    v2 = jnp.max(el2, axis=-1, keepdims=True)
    i2 = jnp.min(jnp.where(el2 == v2, lane, big), axis=-1, keepdims=True)
    ex = jnp.exp(v2 - v1)
    w1 = g_p / (1.0 + ex)
    w2 = g_p * ex / (1.0 + ex)
    return i1 - N_GROUPS, i2 - N_GROUPS, w1, w2, h2h


def _merge_kernel(x_ref, og_ref, o0_ref, o1_ref, o2_ref, s0_ref, s1_ref, s2_ref, g1_ref, wbg_ref, bbg_ref,
                  wpg_ref, wpa_ref, wout_ref, g2_ref, wr_ref, br_ref, tri_ref, upper_ref,
                  x2_ref, xloc_ref, route_ref, cnt_ref, obuf_ref, sbuf_ref):
    tm = x_ref.shape[0]

    for slot, (o_ref, s_ref) in enumerate(((o1_ref, s1_ref), (o2_ref, s2_ref))):
        d = DIL_PATTERNS[slot + 1][1]
        for r in range(d):
            sbuf_ref[slot, pl.ds(r, tm // d, stride=d), :] = s_ref[0, r]
            for c in range(DIL_HEADS):
                obuf_ref[slot, c, pl.ds(r, tm // d, stride=d), :] = o_ref[0, r, :, _head(c)].astype(F32)

    e1, e2, w1, w2, h2b = _merge_rows(x_ref, og_ref, o0_ref, s0_ref, g1_ref, wbg_ref, bbg_ref, wpg_ref, wpa_ref,
                                      wout_ref, g2_ref, wr_ref, br_ref, x2_ref, obuf_ref, sbuf_ref, 0, tm)

    lane = lax.broadcasted_iota(jnp.int32, (tm, LANES), 1).astype(F32)
    oh1 = lane == e1
    oh2 = lane == e2
    onehot = jnp.where(oh1 | oh2, 1.0, 0.0)
    prefix = _dot(tri_ref[...], onehot.astype(BF16))
    cnt = jnp.sum(onehot, axis=0, keepdims=True)
    cnt_pad = jnp.floor((cnt + (CHUNK - 1)) * (1.0 / CHUNK)) * CHUNK
    seg_off = _dot(jnp.broadcast_to(cnt_pad, (8, LANES)).astype(BF16), upper_ref[...])[0:1, :]
    base = seg_off + prefix
    pos1 = jnp.sum(jnp.where(oh1, base, 0.0), axis=-1, keepdims=True)
    pos2 = jnp.sum(jnp.where(oh2, base, 0.0), axis=-1, keepdims=True)
    cnt_ref[...] = jnp.broadcast_to(cnt, cnt_ref.shape)

    route = jnp.zeros((tm, LANES), F32)
    for idx, val in enumerate((e1, e2, w1, w2, pos1, pos2)):
        route = jnp.where(lane == idx, val, route)
    route_ref[...] = route

    route_t = route.T
    slot = lax.broadcasted_iota(jnp.int32, (LOCAL_ROWS, tm), 0).astype(F32)
    perm = jnp.where((slot == route_t[4:5, :]) | (slot == route_t[5:6, :]), 1.0, 0.0).astype(BF16)
    xloc_ref[...] = _pack_pairs(_dot(perm, h2b))


def _merge(x2d, o_gla, outs, stats, g1, wbg, bbg, wpg, wpa, wout, g2, wr, br, tri, upper, S):
    T = x2d.shape[0]
    tm = min(MERGE_TM, S)
    assert TOP_K * tm + N_EXPERTS * CHUNK <= LOCAL_ROWS
    tpb = S // tm
    tok = lambda w: pl.BlockSpec((tm, w), lambda i: (i, 0))
    full = lambda a: pl.BlockSpec(a.shape, lambda i: (0,) * a.ndim)

    def res(gi, w):
        d = DIL_PATTERNS[gi][1]
        return pl.BlockSpec((1, d, tm // d, w), lambda i: (i // tpb, 0, i % tpb, 0))

    return pl.pallas_call(
        _merge_kernel,
        grid=(T // tm,),
        in_specs=[tok(D_MODEL), tok(HEAD_W), tok(HEAD_W), res(1, HEAD_W), res(2, HEAD_W),
                  tok(LANES), res(1, LANES), res(2, LANES),
                  full(g1), full(wbg), full(bbg), full(wpg), full(wpa), full(wout), full(g2),
                  full(wr), full(br), full(tri), full(upper)],
        out_specs=[tok(D_MODEL), pl.BlockSpec((LOCAL_ROWS, PACKED_W), lambda i: (i, 0)), tok(LANES),
                   pl.BlockSpec((8, LANES), lambda i: (i, 0))],
        out_shape=[
            jax.ShapeDtypeStruct((T, D_MODEL), F32),
            jax.ShapeDtypeStruct((T // tm * LOCAL_ROWS, PACKED_W), F32),
            jax.ShapeDtypeStruct((T, LANES), F32),
            jax.ShapeDtypeStruct((T // tm * 8, LANES), F32),
        ],
        scratch_shapes=[pltpu.VMEM((DIL_GROUPS - 1, DIL_HEADS, tm, LANES), F32),
                        pltpu.VMEM((DIL_GROUPS - 1, tm, LANES), F32)],
        compiler_params=_params(("arbitrary",)),
        name="merge",
    )(x2d, o_gla, *outs, *stats, g1, wbg, bbg, wpg, wpa, wout, g2, wr, br, tri, upper)


class _ChunkGather:
    def __init__(self, src_hbm, buf_ref, sem, nchunks):
        self.src, self.buf, self.sem, self.n = src_hbm, buf_ref, sem, nchunks

    def start(self, offs_ref, s):
        for j in range(self.n):
            row = pl.multiple_of(offs_ref[0, 0, j], CHUNK)
            pltpu.make_async_copy(self.src.at[pl.ds(row, CHUNK), :], self.buf.at[s, pl.ds(j * CHUNK, CHUNK), :],
                                  self.sem.at[s]).start(priority=j % 2)

    def wait(self, s):
        pltpu.make_async_copy(self.src.at[pl.ds(0, self.n * CHUNK), :], self.buf.at[s], self.sem.at[s]).wait()

    def step(self, offs_cur_ref, offs_next_ref, used_steps=None):
        i = pl.program_id(0)
        slot = lax.rem(i, 2)
        last = pl.num_programs(0) if used_steps is None else jnp.minimum(used_steps, pl.num_programs(0))

        @pl.when(jnp.logical_and(i == 0, i < last))
        def _():
            self.start(offs_cur_ref, 0)

        @pl.when(i < last)
        def _():
            self.wait(slot)

        @pl.when(i + 1 < last)
        def _():
            self.start(offs_next_ref, 1 - slot)

        return slot


def _expert_kernel(tile_e_ref, nact_ref, src_cur_ref, src_next_ref, xloc_ref, wg_ref, wu_ref, wd_ref,
                   y_ref, xbuf_ref, wgu_ref, wdn_ref, gsem):
    i = pl.program_id(0)
    slot = _ChunkGather(xloc_ref, xbuf_ref, gsem, TILE_CHUNKS).step(src_cur_ref, src_next_ref, nact_ref[0])
    active = i < nact_ref[0]

    @pl.when(jnp.logical_and(active, jnp.logical_or(i == 0, tile_e_ref[i] != tile_e_ref[jnp.maximum(i - 1, 0)])))
    def _():
        wgu_ref[:, :D_EXPERT] = wg_ref[0].astype(BF16)
        wgu_ref[:, D_EXPERT:] = wu_ref[0].astype(BF16)
        wdn_ref[...] = wd_ref[0].astype(BF16)

    @pl.when(active)
    def _():
        xb = _unpack_pairs(xbuf_ref[slot])
        gu = _dot(xb, wgu_ref[...])
        g, u = gu[:, :D_EXPERT], gu[:, D_EXPERT:]
        a = (g * jax.nn.sigmoid(g) * u).astype(BF16)
        y_ref[...] = _pack_pairs(_dot(a, wdn_ref[...]).astype(BF16).astype(F32))

    @pl.when(jnp.logical_not(active))
    def _():
        y_ref[...] = jnp.zeros_like(y_ref)


def _experts(tile_e, nact, csrc, xloc, wg, wu, wd):
    ntiles = tile_e.shape[0]
    R = MOE_TM

    def wsel(i, te, na):
        return (te[jnp.minimum(i, na[0] - 1)], 0, 0)

    def offs(shift):
        return pl.BlockSpec((1, 1, TILE_CHUNKS), lambda i, te, na: (jnp.minimum(i + shift, ntiles - 1), 0, 0),
                            memory_space=pltpu.SMEM)

    return pl.pallas_call(
        _expert_kernel,
        grid_spec=pltpu.PrefetchScalarGridSpec(
            num_scalar_prefetch=2,
            grid=(ntiles,),
            in_specs=[
                offs(0), offs(1),
                pl.BlockSpec(memory_space=pl.ANY),
                pl.BlockSpec((1, D_MODEL, D_EXPERT), wsel),
                pl.BlockSpec((1, D_MODEL, D_EXPERT), wsel),
                pl.BlockSpec((1, D_EXPERT, D_MODEL), wsel),
            ],
            out_specs=pl.BlockSpec((R, PACKED_W), lambda i, te, na: (i, 0)),
            scratch_shapes=[pltpu.VMEM((2, R, PACKED_W), F32), pltpu.VMEM((D_MODEL, 2 * D_EXPERT), BF16),
                            pltpu.VMEM((D_EXPERT, D_MODEL), BF16), pltpu.SemaphoreType.DMA((2,))],
        ),
        out_shape=jax.ShapeDtypeStruct((ntiles * R, PACKED_W), F32),
        compiler_params=_params(("arbitrary",)),
        name="experts",
    )(tile_e, nact, csrc, csrc, xloc, wg, wu, wd)


def _combine_kernel(dst_cur_ref, dst_next_ref, x2_ref, route_ref, y_ref, out_ref, ybuf_ref, gsem):
    slot = _ChunkGather(y_ref, ybuf_ref, gsem, LOCAL_CHUNKS).step(dst_cur_ref, dst_next_ref)
    tm = x2_ref.shape[0]
    route = route_ref[...]
    w1, w2, pos1, pos2 = (route[:, c:c + 1] for c in range(2, 6))
    yl = _unpack_pairs(ybuf_ref[slot])
    cols = lax.broadcasted_iota(jnp.int32, (tm, LOCAL_ROWS), 1).astype(F32)
    pick1 = jnp.where(cols == pos1, 1.0, 0.0).astype(BF16)
    pick2 = jnp.where(cols == pos2, 1.0, 0.0).astype(BF16)
    out_ref[...] = x2_ref[...] + w1 * _dot(pick1, yl) + w2 * _dot(pick2, yl)


def _combine(cdst, x2, route, y):
    T = x2.shape[0]
    nt = cdst.shape[0]
    tm = T // nt

    def offs(shift):
        return pl.BlockSpec((1, 1, LOCAL_CHUNKS), lambda i: (jnp.minimum(i + shift, nt - 1), 0, 0),
                            memory_space=pltpu.SMEM)

    return pl.pallas_call(
        _combine_kernel,
        grid=(nt,),
        in_specs=[
            offs(0), offs(1),
            pl.BlockSpec((tm, D_MODEL), lambda i: (i, 0)),
            pl.BlockSpec((tm, LANES), lambda i: (i, 0)),
            pl.BlockSpec(memory_space=pl.ANY),
        ],
        out_specs=pl.BlockSpec((tm, D_MODEL), lambda i: (i, 0)),
        out_shape=jax.ShapeDtypeStruct((T, D_MODEL), F32),
        scratch_shapes=[pltpu.VMEM((2, LOCAL_ROWS, PACKED_W), F32), pltpu.SemaphoreType.DMA((2,))],
        compiler_params=_params(("arbitrary",)),
        name="combine",
    )(cdst, cdst, x2, route, y)


def _layer(x2d, B, S, norm1_g, w_in, w_gla_a2, b_gla_a, gla_out_norm_g, dil_q_norm_g, dil_k_norm_g,
           w_proj_gla, w_proj_attn, w_branch_gate, b_branch_gate, w_out, norm2_g,
           w_router_group, b_router_group, w_router_expert, b_router_expert, w_gate, w_up, w_down):
    T = B * S
    w_a2 = jnp.pad(w_gla_a2, ((0, LANES - GLA_RANK), (0, 0))).astype(BF16)
    gains = [jnp.ones((HEAD_W,), F32)] * N_GLA_TILES
    for gi in range(DIL_GROUPS):
        gains += [jnp.tile(dil_q_norm_g[gi], DIL_HEADS) * (DIL_DH ** -0.5 * LOG2_E),
                  jnp.tile(dil_k_norm_g[gi], DIL_HEADS), jnp.ones((HEAD_W,), F32)]
    qk_gain = jnp.stack(gains).reshape(N_COL_TILES, 1, HEAD_W)

    gla_in, log_a, qkv0, qkv1, qkv2 = _inproj(
        x2d, norm1_g.reshape(1, -1), w_in.astype(BF16), w_a2, b_gla_a.reshape(1, -1), qk_gain, B, S)

    tri_incl = jnp.asarray(np.tril(np.ones((GLA_BLOCK, GLA_BLOCK), np.float32)), BF16)
    o_gla = _gla(gla_in, log_a, tri_incl, gla_out_norm_g.reshape(1, -1), B, S)

    outs, stats = [], []
    for gi, qkv in enumerate((qkv0.reshape(B, 1, S, QKV_W), qkv1, qkv2)):
        o, st = _dilated(qkv, _alibi_bias(gi), gi)
        outs.append(o)
        stats.append(st)
    outs[0] = outs[0].reshape(T, HEAD_W)
    stats[0] = stats[0].reshape(T, LANES)

    w_r = jnp.pad(jnp.concatenate([w_router_group, w_router_expert], axis=1),
                  ((0, 0), (0, LANES - N_GROUPS - N_EXPERTS)))
    w_rh = w_r.astype(BF16)
    w_rl = (w_r - w_rh.astype(F32)).astype(BF16)
    b_r = jnp.pad(jnp.concatenate([b_router_group, b_router_expert]), (0, LANES - N_GROUPS - N_EXPERTS))
    tm = min(MERGE_TM, S)
    tri_strict = jnp.asarray(np.tril(np.ones((tm, tm), np.float32), -1), BF16)
    upper = jnp.asarray(np.triu(np.ones((LANES, LANES), np.float32), 1), BF16)
    x2, xloc, route, cnt = _merge(
        x2d, o_gla, outs, stats, norm1_g.reshape(1, -1), w_branch_gate.astype(BF16),
        b_branch_gate.reshape(1, -1), w_proj_gla.astype(BF16), w_proj_attn.astype(BF16), w_out.astype(BF16),
        norm2_g.reshape(1, -1), jnp.concatenate([w_rh, w_rl], axis=1), b_r.reshape(1, -1), tri_strict, upper, S)

    csrc, cdst, tile_e, nact = _chunk_tables(cnt, T // tm, tm)
    y = _experts(tile_e, nact, csrc, xloc, w_gate, w_up, w_down)
    return _combine(cdst, x2, route, y)


def _chunk_tables(cnt, nt, tm):
    i32 = jnp.int32
    experts = jnp.arange(N_EXPERTS, dtype=i32)
    tiles = jnp.arange(nt, dtype=i32)
    c = (cnt.reshape(nt, 8, LANES)[:, 0, :N_EXPERTS].astype(i32) + CHUNK - 1) // CHUNK
    loff = jnp.cumsum(c, axis=1) - c
    ecum = jnp.cumsum(c, axis=0) - c
    tot = jnp.sum(c, axis=0)
    ptot = (tot + TILE_CHUNKS - 1) // TILE_CHUNKS * TILE_CHUNKS
    pend = jnp.cumsum(ptot)
    pstart = pend - ptot

    lc = jnp.arange(LOCAL_CHUNKS, dtype=i32)
    e_of = jnp.sum(((loff + c)[:, None, :] <= lc[None, :, None]).astype(i32), axis=-1)
    pick_e = e_of[:, :, None] == experts[None, None, :]
    glob = jnp.sum(jnp.where(pick_e, (pstart[None, :] + ecum - loff)[:, None, :], 0), axis=-1) + lc[None, :]
    max_chunks = (TOP_K * nt * tm + (CHUNK - 1) * nt * N_EXPERTS) // CHUNK + N_EXPERTS * (TILE_CHUNKS - 1)
    ntiles = -(-max_chunks // TILE_CHUNKS)
    spread = (tiles[:, None] * LOCAL_CHUNKS + lc[None, :]) % (ntiles * TILE_CHUNKS)
    cdst = jnp.where(e_of < N_EXPERTS, glob, spread) * CHUNK

    g = jnp.arange(ntiles * TILE_CHUNKS, dtype=i32)
    e_g = jnp.minimum(jnp.sum((pend[None, :] <= g[:, None]).astype(i32), axis=1), N_EXPERTS - 1)
    k = g - jnp.sum(jnp.where(e_g[:, None] == experts[None, :], pstart[None, :], 0), axis=1)
    pick = (e_g[:, None] == experts[None, :]).astype(F32)
    ends = jnp.dot(pick, (ecum + c).T.astype(F32), precision=lax.Precision.HIGHEST)
    t_g = jnp.sum((ends <= k[:, None].astype(F32)).astype(i32), axis=1)
    valid = t_g < nt
    base = jnp.dot(pick, (tiles[:, None] * LOCAL_CHUNKS + loff - ecum).T.astype(F32),
                   precision=lax.Precision.HIGHEST)
    in_tile = tiles[None, :] == jnp.minimum(t_g, nt - 1)[:, None]
    local = jnp.sum(jnp.where(in_tile, base, 0.0), axis=1).astype(i32) + k
    spare = LOCAL_CHUNKS - -(-(TOP_K * tm + N_EXPERTS * (CHUNK - 1)) // CHUNK)
    assert spare >= 1
    zero_chunk = (g % nt) * LOCAL_CHUNKS + LOCAL_CHUNKS - 1 - (g // nt) % spare
    csrc = jnp.where(valid, local, zero_chunk) * CHUNK

    tile_e = e_g[::TILE_CHUNKS]
    nact = (pend[-1:] // TILE_CHUNKS).astype(i32)
    return (csrc.reshape(ntiles, 1, TILE_CHUNKS), cdst.reshape(nt, 1, LOCAL_CHUNKS), tile_e, nact)


def kernel(x, norm1_g, w_in, w_gla_a2, b_gla_a, gla_out_norm_g, dil_q_norm_g, dil_k_norm_g, w_proj_gla,
           w_proj_attn, w_branch_gate, b_branch_gate, w_out, norm2_g, w_router_group, b_router_group,
           w_router_expert, b_router_expert, w_gate, w_up, w_down):
    B, S, D = x.shape
    assert D == D_MODEL and S % (DIL_BLOCK * DIL_PATTERNS[-1][1]) == 0
    x2d = x.reshape(B * S, D)
    params = (norm1_g, w_in, w_gla_a2, b_gla_a, gla_out_norm_g, dil_q_norm_g, dil_k_norm_g, w_proj_gla,
              w_proj_attn, w_branch_gate, b_branch_gate, w_out, norm2_g, w_router_group, b_router_group,
              w_router_expert, b_router_expert, w_gate, w_up, w_down)
    for layer in range(norm1_g.shape[0]):
        x2d = _layer(x2d, B, S, *(p[layer] for p in params))
    return x2d.reshape(B, S, D)
```

```python
import functools

import jax
import jax.numpy as jnp
import numpy as np
from jax import lax
from jax.experimental import pallas as pl
from jax.experimental.pallas import tpu as pltpu

F32 = jnp.float32
BF16 = jnp.bfloat16

D_MODEL = 1024
EPS = 1e-6
GLA_HEADS = 4
GLA_DK = 128
GLA_RANK = 16
GLA_TAU = 16.0
LOG2_E = 1.4426950408889634
GLA_SUBCHUNK = 16
GLA_BLOCK = 128
GLA_STEP_BLOCKS = 16
DIL_PATTERNS = ((128, 1), (512, 4), (2048, 16))
DIL_GROUPS = 3
DIL_HEADS = 4
DIL_DH = 128
DIL_BLOCK = 128
ALIBI_SLOPES = tuple(2.0 ** (-8.0 * (i + 1) / (DIL_GROUPS * DIL_HEADS)) for i in range(DIL_GROUPS * DIL_HEADS))
N_GROUPS = 4
EXPERTS_PER_GROUP = 8
N_EXPERTS = N_GROUPS * EXPERTS_PER_GROUP
TOP_K = 2
D_EXPERT = 512

HEAD_W = 512
N_GLA_TILES = 4
N_COL_TILES = N_GLA_TILES + 3 * DIL_GROUPS
QKV_W = 3 * HEAD_W
LANES = 128
NEG = -1e30

INPROJ_TM = 512
MERGE_TM = 512
MOE_TM = 512
CHUNK = 8
TILE_CHUNKS = MOE_TM // CHUNK
LOCAL_ROWS = -(-(TOP_K * MERGE_TM + N_EXPERTS * CHUNK) // LANES) * LANES
LOCAL_CHUNKS = LOCAL_ROWS // CHUNK
DIL_STEP_BLOCKS = 32
VMEM_LIMIT = 56 * 1024 * 1024


def _dot(a, b):
    return jnp.dot(a, b, preferred_element_type=F32)


def _dot_nt(a, b):
    return lax.dot_general(a, b, (((1,), (1,)), ((), ())), preferred_element_type=F32)


def _dot_tn(a, b):
    return lax.dot_general(a, b, (((0,), (0,)), ((), ())), preferred_element_type=F32)


def _rms(x, g):
    return x * lax.rsqrt(jnp.mean(x * x, axis=-1, keepdims=True) + EPS) * g


def _params(sem):
    return pltpu.CompilerParams(dimension_semantics=sem, vmem_limit_bytes=VMEM_LIMIT)


def _head(c):
    return slice(c * LANES, (c + 1) * LANES)


PACKED_W = D_MODEL // 2
_HI16 = 0xFFFF0000


def _pack_pairs(x):
    u = lax.bitcast_convert_type(x, jnp.uint32)
    w = x.shape[1] // 2
    return lax.bitcast_convert_type((u[:, :w] & jnp.uint32(_HI16)) | (u[:, w:] >> 16), F32)


def _unpack_pairs(p):
    u = lax.bitcast_convert_type(p, jnp.uint32)
    hi = lax.bitcast_convert_type(u & jnp.uint32(_HI16), F32)
    lo = lax.bitcast_convert_type(u << 16, F32)
    return jnp.concatenate([hi, lo], axis=1).astype(BF16)


def _inproj_kernel(x_ref, g1_ref, w_ref, wa2_ref, ba_ref, qkg_ref,
                   gla_ref, loga_ref, d0_ref, d1_ref, d2_ref, h_ref, ybuf_ref, wdil_ref):
    tm = x_ref.shape[0]
    n_gla = N_GLA_TILES * HEAD_W

    @pl.when(pl.program_id(0) == 0)
    def _():
        for k in range(3 * DIL_GROUPS):
            first = n_gla + GLA_RANK + k * HEAD_W
            wdil_ref[:, k * HEAD_W:(k + 1) * HEAD_W] = w_ref[:, first:first + HEAD_W]

    hb = _rms(x_ref[...], g1_ref[...]).astype(BF16)
    h_ref[...] = hb
    ga = _dot(hb, w_ref[:, n_gla:n_gla + LANES])
    z = _dot(ga.astype(BF16), wa2_ref[...]) + ba_ref[...]
    log_sig = jnp.minimum(z, 0.0) - jnp.log(1.0 + jnp.exp(-jnp.abs(z)))
    loga_ref[...] = log_sig * (LOG2_E / GLA_TAU)

    dil_refs = (d0_ref, d1_ref, d2_ref)
    n_strided = 0
    for jj in range(N_COL_TILES):
        cols = slice(jj * HEAD_W, (jj + 1) * HEAD_W)
        if jj < N_GLA_TILES:
            w = w_ref[:, cols]
        else:
            w = wdil_ref[:, (jj - N_GLA_TILES) * HEAD_W:(jj - N_GLA_TILES + 1) * HEAD_W]
        y = _dot(h_ref[...], w)
        if jj < N_GLA_TILES:
            gla_ref[:, cols] = y.astype(BF16)
            continue
        gi, kind = divmod(jj - N_GLA_TILES, 3)
        d = DIL_PATTERNS[gi][1]
        ocols = [slice(kind * HEAD_W + c * LANES, kind * HEAD_W + (c + 1) * LANES) for c in range(DIL_HEADS)]
        if kind < 2:
            g = qkg_ref[jj]
            slabs = [_rms(y[:, _head(c)], g[:, _head(c)]) for c in range(DIL_HEADS)]
        else:
            slabs = [y[:, _head(c)] for c in range(DIL_HEADS)]
        out = dil_refs[gi]
        if d == 1:
            for c in range(DIL_HEADS):
                out[:, ocols[c]] = slabs[c].astype(BF16)
            continue
        buf = n_strided % ybuf_ref.shape[0]
        n_strided += 1
        for c in range(DIL_HEADS):
            ybuf_ref[buf, c] = slabs[c]
        for r in range(d):
            for c in range(DIL_HEADS):
                out[0, r, :, ocols[c]] = ybuf_ref[buf, c, pl.ds(r, tm // d, stride=d), :].astype(BF16)


def _resident(shape):
    return pl.BlockSpec(shape, lambda i: (0,) * len(shape), pipeline_mode=pl.Buffered(1))


def _inproj(x2d, g1, w_all, w_a2, b_a, qk_gain, B, S):
    T = x2d.shape[0]
    tm = min(INPROJ_TM, S)
    tpb = S // tm

    def dil_spec(gi):
        d = DIL_PATTERNS[gi][1]
        if d == 1:
            return pl.BlockSpec((tm, QKV_W), lambda i: (i, 0))
        return pl.BlockSpec((1, d, tm // d, QKV_W), lambda i: (i // tpb, 0, i % tpb, 0))

    def dil_shape(gi):
        d = DIL_PATTERNS[gi][1]
        shape = (T, QKV_W) if d == 1 else (B, d, S // d, QKV_W)
        return jax.ShapeDtypeStruct(shape, BF16)

    return pl.pallas_call(
        _inproj_kernel,
        grid=(T // tm,),
        in_specs=[
            pl.BlockSpec((tm, D_MODEL), lambda i: (i, 0)),
            _resident(g1.shape), _resident(w_all.shape), _resident(w_a2.shape),
            _resident(b_a.shape), _resident(qk_gain.shape),
        ],
        out_specs=[
            pl.BlockSpec((tm, N_GLA_TILES * HEAD_W), lambda i: (i, 0)),
            pl.BlockSpec((tm, HEAD_W), lambda i: (i, 0)),
            dil_spec(0), dil_spec(1), dil_spec(2),
        ],
        out_shape=[
            jax.ShapeDtypeStruct((T, N_GLA_TILES * HEAD_W), BF16),
            jax.ShapeDtypeStruct((T, HEAD_W), F32),
            dil_shape(0), dil_shape(1), dil_shape(2),
        ],
        scratch_shapes=[pltpu.VMEM((tm, D_MODEL), BF16), pltpu.VMEM((2, DIL_HEADS, tm, LANES), F32),
                        pltpu.VMEM((D_MODEL, 3 * DIL_GROUPS * HEAD_W), BF16)],
        compiler_params=_params(("arbitrary",)),
        name="inproj",
    )(x2d, g1, w_all, w_a2, b_a, qk_gain)


def _gla_kernel(q_ref, k_ref, v_ref, r_ref, la_ref, tri_ref, gn_ref, o_ref, st_ref):
    n = pl.program_id(1)

    @pl.when(n == 0)
    def _():
        st_ref[...] = jnp.zeros_like(st_ref)

    for blk in range(q_ref.shape[0] // GLA_BLOCK):
        _gla_block(q_ref, k_ref, v_ref, r_ref, la_ref, tri_ref, gn_ref, o_ref, st_ref,
                   slice(blk * GLA_BLOCK, (blk + 1) * GLA_BLOCK))


def _gla_block(q_ref, k_ref, v_ref, r_ref, la_ref, tri_ref, gn_ref, o_ref, st_ref, rows):
    L = GLA_BLOCK
    la = la_ref[rows, :]
    hi = la.astype(BF16)
    r1 = la - hi.astype(F32)
    mid = r1.astype(BF16)
    lo = (r1 - mid.astype(F32)).astype(BF16)
    tri = tri_ref[...]
    b_all = _dot(tri, hi) + _dot(tri, mid) + _dot(tri, lo)

    row = lax.broadcasted_iota(jnp.int32, (L, L), 0)
    col = lax.broadcasted_iota(jnp.int32, (L, L), 1)

    for h in range(GLA_HEADS):
        sl = _head(h)
        b = b_all[:, sl]
        q = q_ref[rows, sl].astype(F32) * (GLA_DK ** -0.5)
        k = k_ref[rows, sl].astype(F32)
        v = v_ref[rows, sl]

        att = jnp.zeros((L, L), F32)
        seg = L
        while seg > GLA_SUBCHUNK:
            half = seg // 2
            beta = jnp.concatenate(
                [jnp.broadcast_to(b[s + half - 1:s + half, :], (seg, GLA_DK)) for s in range(0, L, seg)], axis=0)
            rel = b - beta
            neg = jnp.minimum(rel, 0.0)
            qs = (q * jnp.exp2(neg)).astype(BF16)
            ks = (k * jnp.exp2(neg - rel)).astype(BF16)
            same = (row // seg) == (col // seg)
            mask = same & ((row % seg) >= half) & ((col % seg) < half)
            att = jnp.where(mask, _dot_nt(qs, ks), att)
            seg = half
        beta = jnp.concatenate(
            [jnp.zeros((GLA_SUBCHUNK, GLA_DK), F32)]
            + [jnp.broadcast_to(b[s - 1:s, :], (GLA_SUBCHUNK, GLA_DK)) for s in range(GLA_SUBCHUNK, L, GLA_SUBCHUNK)],
            axis=0)
        qs = (q * jnp.exp2(b - beta)).astype(BF16)
        ks = (k * jnp.exp2(beta - b)).astype(BF16)
        mask = ((row // GLA_SUBCHUNK) == (col // GLA_SUBCHUNK)) & (col <= row)
        att = jnp.where(mask, _dot_nt(qs, ks), att)

        state_t = st_ref[h]
        q0 = (q * jnp.exp2(b)).astype(BF16)
        o = _dot(att.astype(BF16), v) + _dot_nt(q0, state_t.astype(BF16))

        b_last = b[L - 1:L, :]
        k_end = (k * jnp.exp2(b_last - b)).astype(BF16)
        st_ref[h] = state_t * jnp.exp2(b_last) + _dot_tn(v, k_end)

        rr = r_ref[rows, sl].astype(F32)
        o_ref[rows, sl] = (_rms(o, gn_ref[...]) * (rr * jax.nn.sigmoid(rr))).astype(BF16)


def _gla(proj, log_a, tri, gn, B, S):
    L = GLA_STEP_BLOCKS * GLA_BLOCK
    nb = S // L
    return pl.pallas_call(
        _gla_kernel,
        grid=(B, nb),
        in_specs=[
            pl.BlockSpec((L, HEAD_W), lambda b, n: (b * nb + n, 0)),
            pl.BlockSpec((L, HEAD_W), lambda b, n: (b * nb + n, 1)),
            pl.BlockSpec((L, HEAD_W), lambda b, n: (b * nb + n, 2)),
            pl.BlockSpec((L, HEAD_W), lambda b, n: (b * nb + n, 3)),
            pl.BlockSpec((L, HEAD_W), lambda b, n: (b * nb + n, 0)),
            pl.BlockSpec((GLA_BLOCK, GLA_BLOCK), lambda b, n: (0, 0)),
            pl.BlockSpec((1, GLA_DK), lambda b, n: (0, 0)),
        ],
        out_specs=pl.BlockSpec((L, HEAD_W), lambda b, n: (b * nb + n, 0)),
        out_shape=jax.ShapeDtypeStruct((B * S, HEAD_W), BF16),
        scratch_shapes=[pltpu.VMEM((GLA_HEADS, GLA_DK, GLA_DK), F32)],
        compiler_params=_params(("arbitrary", "arbitrary")),
        name="gla",
    )(proj, proj, proj, proj, log_a, tri, gn)


def _dil_kernel(q_ref, kp_ref, kc_ref, vp_ref, vc_ref, bias_ref, o_ref, st_ref, *, nr, nq):
    first = pl.program_id(2) == 0
    Q = DIL_BLOCK
    col = lax.broadcasted_iota(jnp.int32, (Q, 2 * Q), 1)
    lane = lax.broadcasted_iota(jnp.int32, (Q, LANES), 1)
    for res in range(nr):
        for blk in range(nq):
            rows = slice(blk * Q, (blk + 1) * Q)
            prev = slice((blk - 1) * Q, blk * Q)
            stats = jnp.zeros((Q, LANES), F32)
            for h in range(DIL_HEADS):
                sl = _head(h)
                if blk == 0:
                    kprev, vprev = kp_ref[0, res, :, sl], vp_ref[0, res, :, sl]
                else:
                    kprev, vprev = kc_ref[0, res, prev, sl], vc_ref[0, res, prev, sl]
                kk = jnp.concatenate([kprev, kc_ref[0, res, rows, sl]], axis=0)
                vv = jnp.concatenate([vprev, vc_ref[0, res, rows, sl]], axis=0)
                s = _dot_nt(q_ref[0, res, rows, sl], kk) + bias_ref[h]
                if blk == 0:
                    s = jnp.where(jnp.logical_and(first, col < Q), NEG, s)
                m = jnp.max(s, axis=-1, keepdims=True)
                p = jnp.exp2(s - m)
                den = jnp.sum(p, axis=-1, keepdims=True)
                o_ref[0, res, rows, sl] = (_dot(p.astype(BF16), vv) / den).astype(BF16)
                stats = jnp.where((lane // 16) == h, m, stats)
                stats = jnp.where((lane // 16) == DIL_HEADS + h, den, stats)
            st_ref[0, res, rows, :] = stats


def _dilated(qkv, bias, gi):
    B, d, Lsub, _ = qkv.shape
    nq = min(DIL_STEP_BLOCKS, Lsub // DIL_BLOCK)
    nr = min(DIL_STEP_BLOCKS // nq, d)
    rows = nq * DIL_BLOCK

    def cur(c):
        return pl.BlockSpec((1, nr, rows, HEAD_W), lambda b, r, i: (b, r, i, c))

    def prev(c):
        return pl.BlockSpec((1, nr, DIL_BLOCK, HEAD_W), lambda b, r, i: (b, r, jnp.maximum(i * nq - 1, 0), c))

    return pl.pallas_call(
        functools.partial(_dil_kernel, nr=nr, nq=nq),
        grid=(B, d // nr, Lsub // rows),
        in_specs=[cur(0), prev(1), cur(1), prev(2), cur(2),
                  pl.BlockSpec((DIL_HEADS, DIL_BLOCK, 2 * DIL_BLOCK), lambda b, r, i: (0, 0, 0))],
        out_specs=[
            pl.BlockSpec((1, nr, rows, HEAD_W), lambda b, r, i: (b, r, i, 0)),
            pl.BlockSpec((1, nr, rows, LANES), lambda b, r, i: (b, r, i, 0)),
        ],
        out_shape=[
            jax.ShapeDtypeStruct((B, d, Lsub, HEAD_W), BF16),
            jax.ShapeDtypeStruct((B, d, Lsub, LANES), F32),
        ],
        compiler_params=_params(("arbitrary", "arbitrary", "arbitrary")),
        name=f"dilated{gi}",
    )(qkv, qkv, qkv, qkv, qkv, bias)


def _alibi_bias(gi):
    window, d = DIL_PATTERNS[gi]
    Q = DIL_BLOCK
    dist = np.arange(Q)[:, None] + Q - np.arange(2 * Q)[None, :]
    valid = (dist >= 0) & (dist <= window // d)
    slopes = np.asarray(ALIBI_SLOPES, np.float32).reshape(DIL_GROUPS, DIL_HEADS)[gi]
    bias = -slopes[:, None, None] * (d * dist).astype(np.float32) * np.float32(LOG2_E)
    return jnp.asarray(np.where(valid[None], bias, np.float32(NEG)), F32)


def _merge_rows(x_ref, og_ref, o0_ref, s0_ref, g1_ref, wbg_ref, bbg_ref, wpg_ref, wpa_ref, wout_ref,
                g2_ref, wr_ref, br_ref, x2_ref, obuf_ref, sbuf_ref, row0, n):
    rows = slice(row0, row0 + n)
    x = x_ref[rows, :]
    hb = _rms(x, g1_ref[...]).astype(BF16)
    gates = jax.nn.sigmoid(_dot(hb, wbg_ref[...]) + bbg_ref[...])

    stats = (s0_ref[rows, :], sbuf_ref[0, rows, :], sbuf_ref[1, rows, :])
    dens = [pltpu.roll(s, 64, 1) for s in stats]
    m_all = jnp.maximum(jnp.maximum(stats[0], stats[1]), stats[2])
    wts = [d * jnp.exp2(s - m_all) for s, d in zip(stats, dens)]
    inv = 1.0 / (wts[0] + wts[1] + wts[2])
    coef = [w * inv for w in wts]
    heads = []
    for h in range(DIL_HEADS):
        group_out = (o0_ref[rows, _head(h)].astype(F32), obuf_ref[0, h, rows, :], obuf_ref[1, h, rows, :])
        acc = jnp.zeros((n, DIL_DH), F32)
        for g in range(DIL_GROUPS):
            c = jnp.broadcast_to(coef[g][:, 16 * h:16 * h + 1], (n, DIL_DH))
            acc = acc + c * group_out[g]
        heads.append(acc.astype(BF16))
    o_att = jnp.concatenate(heads, axis=1)

    y = gates[:, :D_MODEL] * _dot(og_ref[rows, :], wpg_ref[...]) + gates[:, D_MODEL:] * _dot(o_att, wpa_ref[...])
    x2 = x + _dot(y.astype(BF16), wout_ref[...])
    x2_ref[rows, :] = x2

    h2 = _rms(x2, g2_ref[...])

    h2h = h2.astype(BF16)
    h2l = (h2 - h2h.astype(F32)).astype(BF16)
    wr = wr_ref[...]
    pa = _dot(h2h, wr)
    pb = _dot(h2l, wr)
    logit = pa[:, :LANES] + pa[:, LANES:] + pb[:, :LANES] + pb[:, LANES:] + br_ref[...]

    lane = lax.broadcasted_iota(jnp.int32, (n, LANES), 1).astype(F32)
    big = jnp.float32(1e9)
    gl = jnp.where(lane < N_GROUPS, logit, NEG)
    gmax = jnp.max(gl, axis=-1, keepdims=True)
    gsel = jnp.min(jnp.where(gl == gmax, lane, big), axis=-1, keepdims=True)
    g_p = 1.0 / jnp.sum(jnp.exp(gl - gmax), axis=-1, keepdims=True)
    lo = N_GROUPS + EXPERTS_PER_GROUP * gsel
    el = jnp.where((lane >= lo) & (lane < lo + EXPERTS_PER_GROUP), logit, NEG)
    v1 = jnp.max(el, axis=-1, keepdims=True)
    i1 = jnp.min(jnp.where(el == v1, lane, big), axis=-1, keepdims=True)
    el2 = jnp.where(lane == i1, NEG, el)
    v2 = jnp.max(el2, axis=-1, keepdims=True)
    i2 = jnp.min(jnp.where(el2 == v2, lane, big), axis=-1, keepdims=True)
    ex = jnp.exp(v2 - v1)
    w1 = g_p / (1.0 + ex)
    w2 = g_p * ex / (1.0 + ex)
    return i1 - N_GROUPS, i2 - N_GROUPS, w1, w2, h2h


def _merge_kernel(x_ref, og_ref, o0_ref, o1_ref, o2_ref, s0_ref, s1_ref, s2_ref, g1_ref, wbg_ref, bbg_ref,
                  wpg_ref, wpa_ref, wout_ref, g2_ref, wr_ref, br_ref, tri_ref, upper_ref,
                  x2_ref, xloc_ref, route_ref, cnt_ref, obuf_ref, sbuf_ref):
    tm = x_ref.shape[0]

    for slot, (o_ref, s_ref) in enumerate(((o1_ref, s1_ref), (o2_ref, s2_ref))):
        d = DIL_PATTERNS[slot + 1][1]
        for r in range(d):
            sbuf_ref[slot, pl.ds(r, tm // d, stride=d), :] = s_ref[0, r]
            for c in range(DIL_HEADS):
                obuf_ref[slot, c, pl.ds(r, tm // d, stride=d), :] = o_ref[0, r, :, _head(c)].astype(F32)

    e1, e2, w1, w2, h2b = _merge_rows(x_ref, og_ref, o0_ref, s0_ref, g1_ref, wbg_ref, bbg_ref, wpg_ref, wpa_ref,
                                      wout_ref, g2_ref, wr_ref, br_ref, x2_ref, obuf_ref, sbuf_ref, 0, tm)

    lane = lax.broadcasted_iota(jnp.int32, (tm, LANES), 1).astype(F32)
    oh1 = lane == e1
    oh2 = lane == e2
    onehot = jnp.where(oh1 | oh2, 1.0, 0.0)
    prefix = _dot(tri_ref[...], onehot.astype(BF16))
    cnt = jnp.sum(onehot, axis=0, keepdims=True)
    cnt_pad = jnp.floor((cnt + (CHUNK - 1)) * (1.0 / CHUNK)) * CHUNK
    seg_off = _dot(jnp.broadcast_to(cnt_pad, (8, LANES)).astype(BF16), upper_ref[...])[0:1, :]
    base = seg_off + prefix
    pos1 = jnp.sum(jnp.where(oh1, base, 0.0), axis=-1, keepdims=True)
    pos2 = jnp.sum(jnp.where(oh2, base, 0.0), axis=-1, keepdims=True)
    cnt_ref[...] = jnp.broadcast_to(cnt, cnt_ref.shape)

    route = jnp.zeros((tm, LANES), F32)
    for idx, val in enumerate((e1, e2, w1, w2, pos1, pos2)):
        route = jnp.where(lane == idx, val, route)
    route_ref[...] = route

    route_t = route.T
    slot = lax.broadcasted_iota(jnp.int32, (LOCAL_ROWS, tm), 0).astype(F32)
    perm = jnp.where((slot == route_t[4:5, :]) | (slot == route_t[5:6, :]), 1.0, 0.0).astype(BF16)
    xloc_ref[...] = _pack_pairs(_dot(perm, h2b))


def _merge(x2d, o_gla, outs, stats, g1, wbg, bbg, wpg, wpa, wout, g2, wr, br, tri, upper, S):
    T = x2d.shape[0]
    tm = min(MERGE_TM, S)
    assert TOP_K * tm + N_EXPERTS * CHUNK <= LOCAL_ROWS
    tpb = S // tm
    tok = lambda w: pl.BlockSpec((tm, w), lambda i: (i, 0))
    full = lambda a: pl.BlockSpec(a.shape, lambda i: (0,) * a.ndim)

    def res(gi, w):
        d = DIL_PATTERNS[gi][1]
        return pl.BlockSpec((1, d, tm // d, w), lambda i: (i // tpb, 0, i % tpb, 0))

    return pl.pallas_call(
        _merge_kernel,
        grid=(T // tm,),
        in_specs=[tok(D_MODEL), tok(HEAD_W), tok(HEAD_W), res(1, HEAD_W), res(2, HEAD_W),
                  tok(LANES), res(1, LANES), res(2, LANES),
                  full(g1), full(wbg), full(bbg), full(wpg), full(wpa), full(wout), full(g2),
                  full(wr), full(br), full(tri), full(upper)],
        out_specs=[tok(D_MODEL), pl.BlockSpec((LOCAL_ROWS, PACKED_W), lambda i: (i, 0)), tok(LANES),
                   pl.BlockSpec((8, LANES), lambda i: (i, 0))],
        out_shape=[
            jax.ShapeDtypeStruct((T, D_MODEL), F32),
            jax.ShapeDtypeStruct((T // tm * LOCAL_ROWS, PACKED_W), F32),
            jax.ShapeDtypeStruct((T, LANES), F32),
            jax.ShapeDtypeStruct((T // tm * 8, LANES), F32),
        ],
        scratch_shapes=[pltpu.VMEM((DIL_GROUPS - 1, DIL_HEADS, tm, LANES), F32),
                        pltpu.VMEM((DIL_GROUPS - 1, tm, LANES), F32)],
        compiler_params=_params(("arbitrary",)),
        name="merge",
    )(x2d, o_gla, *outs, *stats, g1, wbg, bbg, wpg, wpa, wout, g2, wr, br, tri, upper)


class _ChunkGather:
    def __init__(self, src_hbm, buf_ref, sem, nchunks):
        self.src, self.buf, self.sem, self.n = src_hbm, buf_ref, sem, nchunks

    def start(self, offs_ref, s):
        for j in range(self.n):
            row = pl.multiple_of(offs_ref[0, 0, j], CHUNK)
            pltpu.make_async_copy(self.src.at[pl.ds(row, CHUNK), :], self.buf.at[s, pl.ds(j * CHUNK, CHUNK), :],
                                  self.sem.at[s]).start(priority=j % 2)

    def wait(self, s):
        pltpu.make_async_copy(self.src.at[pl.ds(0, self.n * CHUNK), :], self.buf.at[s], self.sem.at[s]).wait()

    def step(self, offs_cur_ref, offs_next_ref, used_steps=None):
        i = pl.program_id(0)
        slot = lax.rem(i, 2)
        last = pl.num_programs(0) if used_steps is None else jnp.minimum(used_steps, pl.num_programs(0))

        @pl.when(jnp.logical_and(i == 0, i < last))
        def _():
            self.start(offs_cur_ref, 0)

        @pl.when(i < last)
        def _():
            self.wait(slot)

        @pl.when(i + 1 < last)
        def _():
            self.start(offs_next_ref, 1 - slot)

        return slot


def _expert_kernel(tile_e_ref, nact_ref, first_ref, wslot_ref, next_e_ref, has_next_ref,
                   src_cur_ref, src_next_ref, xloc_ref, wg_ref, wu_ref, wd_ref,
                   y_ref, xbuf_ref, wgu_ref, wdn_ref, wgf_ref, wuf_ref, wdf_ref, gsem, wsem):
    i = pl.program_id(0)
    slot = _ChunkGather(xloc_ref, xbuf_ref, gsem, TILE_CHUNKS).step(src_cur_ref, src_next_ref, nact_ref[0])
    active = i < nact_ref[0]

    def weight_copies(e, s):
        return (pltpu.make_async_copy(wg_ref.at[e], wgf_ref.at[s], wsem.at[s]),
                pltpu.make_async_copy(wu_ref.at[e], wuf_ref.at[s], wsem.at[s]),
                pltpu.make_async_copy(wd_ref.at[e], wdf_ref.at[s], wsem.at[s]))

    @pl.when(i == 0)
    def _():
        for copy in weight_copies(tile_e_ref[0], 0):
            copy.start()

    @pl.when(jnp.logical_and(active, first_ref[i] == 1))
    def _():
        s = wslot_ref[i]
        for copy in weight_copies(tile_e_ref[i], s):
            copy.wait()

        @pl.when(has_next_ref[i] == 1)
        def _():
            for copy in weight_copies(next_e_ref[i], 1 - s):
                copy.start()

        wgu_ref[:, :D_EXPERT] = wgf_ref[s].astype(BF16)
        wgu_ref[:, D_EXPERT:] = wuf_ref[s].astype(BF16)
        wdn_ref[...] = wdf_ref[s].astype(BF16)

    @pl.when(active)
    def _():
        xb = _unpack_pairs(xbuf_ref[slot])
        gu = _dot(xb, wgu_ref[...])
        g, u = gu[:, :D_EXPERT], gu[:, D_EXPERT:]
        a = (g * jax.nn.sigmoid(g) * u).astype(BF16)
        y_ref[...] = _pack_pairs(_dot(a, wdn_ref[...]).astype(BF16).astype(F32))

    @pl.when(jnp.logical_not(active))
    def _():
        y_ref[...] = jnp.zeros_like(y_ref)


def _experts(tile_e, nact, csrc, xloc, wg, wu, wd):
    ntiles = tile_e.shape[0]
    R = MOE_TM
    i32 = jnp.int32

    idx = jnp.arange(ntiles, dtype=i32)
    prev_e = jnp.concatenate([jnp.full((1,), -1, i32), tile_e[:-1]])
    first = jnp.logical_and(idx < nact[0], tile_e != prev_e)
    wslot = ((jnp.cumsum(first.astype(i32)) - 1) % 2).astype(i32)
    later = lax.cummin(jnp.where(first, idx, ntiles)[::-1])[::-1]
    nxt = jnp.concatenate([later[1:], jnp.full((1,), ntiles, i32)])
    has_next = (nxt < ntiles).astype(i32)
    next_e = tile_e[jnp.minimum(nxt, ntiles - 1)]

    def offs(shift):
        return pl.BlockSpec((1, 1, TILE_CHUNKS), lambda i, *_: (jnp.minimum(i + shift, ntiles - 1), 0, 0),
                            memory_space=pltpu.SMEM)

    hbm = pl.BlockSpec(memory_space=pl.ANY)
    return pl.pallas_call(
        _expert_kernel,
        grid_spec=pltpu.PrefetchScalarGridSpec(
            num_scalar_prefetch=6,
            grid=(ntiles,),
            in_specs=[offs(0), offs(1), hbm, hbm, hbm, hbm],
            out_specs=pl.BlockSpec((R, PACKED_W), lambda i, *_: (i, 0)),
            scratch_shapes=[pltpu.VMEM((2, R, PACKED_W), F32), pltpu.VMEM((D_MODEL, 2 * D_EXPERT), BF16),
                            pltpu.VMEM((D_EXPERT, D_MODEL), BF16),
                            pltpu.VMEM((2, D_MODEL, D_EXPERT), F32), pltpu.VMEM((2, D_MODEL, D_EXPERT), F32),
                            pltpu.VMEM((2, D_EXPERT, D_MODEL), F32),
                            pltpu.SemaphoreType.DMA((2,)), pltpu.SemaphoreType.DMA((2,))],
        ),
        out_shape=jax.ShapeDtypeStruct((ntiles * R, PACKED_W), F32),
        compiler_params=_params(("arbitrary",)),
        name="experts",
    )(tile_e, nact, first.astype(i32), wslot, next_e, has_next, csrc, csrc, xloc, wg, wu, wd)


def _combine_kernel(dst_cur_ref, dst_next_ref, x2_ref, route_ref, y_ref, out_ref, ybuf_ref, gsem):
    slot = _ChunkGather(y_ref, ybuf_ref, gsem, LOCAL_CHUNKS).step(dst_cur_ref, dst_next_ref)
    tm = x2_ref.shape[0]
    route = route_ref[...]
    w1, w2, pos1, pos2 = (route[:, c:c + 1] for c in range(2, 6))
    yl = _unpack_pairs(ybuf_ref[slot])
    cols = lax.broadcasted_iota(jnp.int32, (tm, LOCAL_ROWS), 1).astype(F32)
    pick1 = jnp.where(cols == pos1, 1.0, 0.0).astype(BF16)
    pick2 = jnp.where(cols == pos2, 1.0, 0.0).astype(BF16)
    out_ref[...] = x2_ref[...] + w1 * _dot(pick1, yl) + w2 * _dot(pick2, yl)


def _combine(cdst, x2, route, y):
    T = x2.shape[0]
    nt = cdst.shape[0]
    tm = T // nt

    def offs(shift):
        return pl.BlockSpec((1, 1, LOCAL_CHUNKS), lambda i: (jnp.minimum(i + shift, nt - 1), 0, 0),
                            memory_space=pltpu.SMEM)

    return pl.pallas_call(
        _combine_kernel,
        grid=(nt,),
        in_specs=[
            offs(0), offs(1),
            pl.BlockSpec((tm, D_MODEL), lambda i: (i, 0)),
            pl.BlockSpec((tm, LANES), lambda i: (i, 0)),
            pl.BlockSpec(memory_space=pl.ANY),
        ],
        out_specs=pl.BlockSpec((tm, D_MODEL), lambda i: (i, 0)),
        out_shape=jax.ShapeDtypeStruct((T, D_MODEL), F32),
        scratch_shapes=[pltpu.VMEM((2, LOCAL_ROWS, PACKED_W), F32), pltpu.SemaphoreType.DMA((2,))],
        compiler_params=_params(("arbitrary",)),
        name="combine",
    )(cdst, cdst, x2, route, y)


def _layer(x2d, B, S, norm1_g, w_in, w_gla_a2, b_gla_a, gla_out_norm_g, dil_q_norm_g, dil_k_norm_g,
           w_proj_gla, w_proj_attn, w_branch_gate, b_branch_gate, w_out, norm2_g,
           w_router_group, b_router_group, w_router_expert, b_router_expert, w_gate, w_up, w_down):
    T = B * S
    w_a2 = jnp.pad(w_gla_a2, ((0, LANES - GLA_RANK), (0, 0))).astype(BF16)
    gains = [jnp.ones((HEAD_W,), F32)] * N_GLA_TILES
    for gi in range(DIL_GROUPS):
        gains += [jnp.tile(dil_q_norm_g[gi], DIL_HEADS) * (DIL_DH ** -0.5 * LOG2_E),
                  jnp.tile(dil_k_norm_g[gi], DIL_HEADS), jnp.ones((HEAD_W,), F32)]
    qk_gain = jnp.stack(gains).reshape(N_COL_TILES, 1, HEAD_W)

    gla_in, log_a, qkv0, qkv1, qkv2 = _inproj(
        x2d, norm1_g.reshape(1, -1), w_in.astype(BF16), w_a2, b_gla_a.reshape(1, -1), qk_gain, B, S)

    tri_incl = jnp.asarray(np.tril(np.ones((GLA_BLOCK, GLA_BLOCK), np.float32)), BF16)
    o_gla = _gla(gla_in, log_a, tri_incl, gla_out_norm_g.reshape(1, -1), B, S)

    outs, stats = [], []
    for gi, qkv in enumerate((qkv0.reshape(B, 1, S, QKV_W), qkv1, qkv2)):
        o, st = _dilated(qkv, _alibi_bias(gi), gi)
        outs.append(o)
        stats.append(st)
    outs[0] = outs[0].reshape(T, HEAD_W)
    stats[0] = stats[0].reshape(T, LANES)

    w_r = jnp.pad(jnp.concatenate([w_router_group, w_router_expert], axis=1),
                  ((0, 0), (0, LANES - N_GROUPS - N_EXPERTS)))
    w_rh = w_r.astype(BF16)
    w_rl = (w_r - w_rh.astype(F32)).astype(BF16)
    b_r = jnp.pad(jnp.concatenate([b_router_group, b_router_expert]), (0, LANES - N_GROUPS - N_EXPERTS))
    tm = min(MERGE_TM, S)
    tri_strict = jnp.asarray(np.tril(np.ones((tm, tm), np.float32), -1), BF16)
    upper = jnp.asarray(np.triu(np.ones((LANES, LANES), np.float32), 1), BF16)
    x2, xloc, route, cnt = _merge(
        x2d, o_gla, outs, stats, norm1_g.reshape(1, -1), w_branch_gate.astype(BF16),
        b_branch_gate.reshape(1, -1), w_proj_gla.astype(BF16), w_proj_attn.astype(BF16), w_out.astype(BF16),
        norm2_g.reshape(1, -1), jnp.concatenate([w_rh, w_rl], axis=1), b_r.reshape(1, -1), tri_strict, upper, S)

    csrc, cdst, tile_e, nact = _chunk_tables(cnt, T // tm, tm)
    y = _experts(tile_e, nact, csrc, xloc, w_gate, w_up, w_down)
    return _combine(cdst, x2, route, y)


def _chunk_tables(cnt, nt, tm):
    i32 = jnp.int32
    experts = jnp.arange(N_EXPERTS, dtype=i32)
    tiles = jnp.arange(nt, dtype=i32)
    c = (cnt.reshape(nt, 8, LANES)[:, 0, :N_EXPERTS].astype(i32) + CHUNK - 1) // CHUNK
    loff = jnp.cumsum(c, axis=1) - c
    ecum = jnp.cumsum(c, axis=0) - c
    tot = jnp.sum(c, axis=0)
    ptot = (tot + TILE_CHUNKS - 1) // TILE_CHUNKS * TILE_CHUNKS
    pend = jnp.cumsum(ptot)
    pstart = pend - ptot

    lc = jnp.arange(LOCAL_CHUNKS, dtype=i32)
    e_of = jnp.sum(((loff + c)[:, None, :] <= lc[None, :, None]).astype(i32), axis=-1)
    pick_e = e_of[:, :, None] == experts[None, None, :]
    glob = jnp.sum(jnp.where(pick_e, (pstart[None, :] + ecum - loff)[:, None, :], 0), axis=-1) + lc[None, :]
    max_chunks = (TOP_K * nt * tm + (CHUNK - 1) * nt * N_EXPERTS) // CHUNK + N_EXPERTS * (TILE_CHUNKS - 1)
    ntiles = -(-max_chunks // TILE_CHUNKS)
    spread = (tiles[:, None] * LOCAL_CHUNKS + lc[None, :]) % (ntiles * TILE_CHUNKS)
    cdst = jnp.where(e_of < N_EXPERTS, glob, spread) * CHUNK

    g = jnp.arange(ntiles * TILE_CHUNKS, dtype=i32)
    e_g = jnp.minimum(jnp.sum((pend[None, :] <= g[:, None]).astype(i32), axis=1), N_EXPERTS - 1)
    k = g - jnp.sum(jnp.where(e_g[:, None] == experts[None, :], pstart[None, :], 0), axis=1)
    pick = (e_g[:, None] == experts[None, :]).astype(F32)
    ends = jnp.dot(pick, (ecum + c).T.astype(F32), precision=lax.Precision.HIGHEST)
    t_g = jnp.sum((ends <= k[:, None].astype(F32)).astype(i32), axis=1)
    valid = t_g < nt
    base = jnp.dot(pick, (tiles[:, None] * LOCAL_CHUNKS + loff - ecum).T.astype(F32),
                   precision=lax.Precision.HIGHEST)
    in_tile = tiles[None, :] == jnp.minimum(t_g, nt - 1)[:, None]
    local = jnp.sum(jnp.where(in_tile, base, 0.0), axis=1).astype(i32) + k
    spare = LOCAL_CHUNKS - -(-(TOP_K * tm + N_EXPERTS * (CHUNK - 1)) // CHUNK)
    assert spare >= 1
    zero_chunk = (g % nt) * LOCAL_CHUNKS + LOCAL_CHUNKS - 1 - (g // nt) % spare
    csrc = jnp.where(valid, local, zero_chunk) * CHUNK

    tile_e = e_g[::TILE_CHUNKS]
    nact = (pend[-1:] // TILE_CHUNKS).astype(i32)
    return (csrc.reshape(ntiles, 1, TILE_CHUNKS), cdst.reshape(nt, 1, LOCAL_CHUNKS), tile_e, nact)


def kernel(x, norm1_g, w_in, w_gla_a2, b_gla_a, gla_out_norm_g, dil_q_norm_g, dil_k_norm_g, w_proj_gla,
           w_proj_attn, w_branch_gate, b_branch_gate, w_out, norm2_g, w_router_group, b_router_group,
           w_router_expert, b_router_expert, w_gate, w_up, w_down):
    B, S, D = x.shape
    assert D == D_MODEL and S % (DIL_BLOCK * DIL_PATTERNS[-1][1]) == 0
    x2d = x.reshape(B * S, D)
    params = (norm1_g, w_in, w_gla_a2, b_gla_a, gla_out_norm_g, dil_q_norm_g, dil_k_norm_g, w_proj_gla,
              w_proj_attn, w_branch_gate, b_branch_gate, w_out, norm2_g, w_router_group, b_router_group,
              w_router_expert, b_router_expert, w_gate, w_up, w_down)
    for layer in range(norm1_g.shape[0]):
        x2d = _layer(x2d, B, S, *(p[layer] for p in params))
    return x2d.reshape(B, S, D)
```
